```python
import jax, jax.numpy as jnp
from jax import lax
import numpy as np

D_MODEL = 1024
BATCH = 4
SEQ = 8192
DEPTH = 1

PLE_DIM = 256
MIX_WIDTH = D_MODEL
MLA_WIDTH = MIX_WIDTH // 2
CONV_WIDTH = MIX_WIDTH - MLA_WIDTH
N_HEADS = 8
V_HEAD_DIM = MLA_WIDTH // N_HEADS
QK_NOPE_DIM = 64
QK_ROPE_DIM = 32
Q_LORA = 3 * D_MODEL // 8
KV_LORA = D_MODEL // 4
CONV_K = 3
CONV_GROUPS = 8
N_GROUPS = 4
EXPERTS_PER_GROUP = 8
N_EXPERTS = N_GROUPS * EXPERTS_PER_GROUP
TOP_K = 2
EXPERT_FF = D_MODEL // 2
ROPE_BASE = 10000.0
EPS = 1e-6
Q_BLOCK = 128
MOE_BLOCK = 128
IN_SPLITS = (Q_LORA, KV_LORA, QK_ROPE_DIM, CONV_WIDTH, CONV_WIDTH, CONV_WIDTH)
IN_COLS = sum(IN_SPLITS)

kernel_name = "hybrid_mla_shortconv_hmoe_ple"


def rmsnorm(x, g):
    xf = x.astype(jnp.float32)
    y = xf * lax.rsqrt(jnp.mean(xf * xf, axis=-1, keepdims=True) + EPS)
    return (y * g.astype(jnp.float32)).astype(x.dtype)


def rope_tables(positions):
    inv_freq = ROPE_BASE ** (-jnp.arange(0, QK_ROPE_DIM, 2, dtype=jnp.float32) / QK_ROPE_DIM)
    ang = positions.astype(jnp.float32)[..., None] * inv_freq
    return jnp.cos(ang)[:, :, None, :], jnp.sin(ang)[:, :, None, :]


def apply_rope(x, cos, sin):
    half = QK_ROPE_DIM // 2
    x1, x2 = x[..., :half], x[..., half:]
    c, s = cos.astype(x.dtype), sin.astype(x.dtype)
    return jnp.concatenate([x1 * c - x2 * s, x2 * c + x1 * s], axis=-1)


def mla_mixer(c_q, c_kv, k_rope_raw, q_norm_g, w_uq, kv_norm_g, w_ukv, cos, sin):
    B, S, _ = c_q.shape
    q = jnp.einsum('bsr,rhd->bshd', rmsnorm(c_q, q_norm_g), w_uq)
    q_nope = q[..., :QK_NOPE_DIM]
    q_rope = apply_rope(q[..., QK_NOPE_DIM:], cos, sin)
    kv = jnp.einsum('bsr,rhd->bshd', rmsnorm(c_kv, kv_norm_g), w_ukv)
    k_nope, v = kv[..., :QK_NOPE_DIM], kv[..., QK_NOPE_DIM:]
    k_rope = apply_rope(k_rope_raw[:, :, None, :], cos, sin)[:, :, 0, :]
    scale = (QK_NOPE_DIM + QK_ROPE_DIM) ** -0.5
    outs = []
    for i in range(S // Q_BLOCK):
        q0, q1 = i * Q_BLOCK, (i + 1) * Q_BLOCK
        s = (jnp.einsum('bqhd,bkhd->bhqk', q_nope[:, q0:q1], k_nope[:, :q1])
             + jnp.einsum('bqhr,bkr->bhqk', q_rope[:, q0:q1], k_rope[:, :q1]))
        s = s.astype(jnp.float32) * scale
        mask = jnp.arange(q0, q1)[:, None] >= jnp.arange(q1)[None, :]
        s = jnp.where(mask, s, -jnp.inf)
        pr = jax.nn.softmax(s, axis=-1).astype(v.dtype)
        outs.append(jnp.einsum('bhqk,bkhd->bqhd', pr, v[:, :q1]))
    o = jnp.concatenate(outs, axis=1)
    return o.reshape(B, S, MLA_WIDTH)


def short_conv_mixer(b_gate, c_gate, u, conv_w):
    cu = c_gate * u
    y = lax.conv_general_dilated(cu, conv_w.astype(cu.dtype), window_strides=(1,),
                                 padding=((CONV_K - 1, 0),),
                                 dimension_numbers=('NWC', 'WIO', 'NWC'),
                                 feature_group_count=CONV_WIDTH)
    return b_gate * y


def hier_moe(xn, w_group_router, b_group_router, w_expert_router, b_expert_router,
             w_gate, w_up, w_down):
    B, S, D = xn.shape
    N = B * S
    xt = xn.reshape(N, D)
    xf = xt.astype(jnp.float32)
    g_prob = jax.nn.softmax(xf @ w_group_router.astype(jnp.float32)
                            + b_group_router.astype(jnp.float32), axis=-1)
    g_p, g_idx = lax.top_k(g_prob, 1)
    e_logits = (xf @ w_expert_router.astype(jnp.float32)
                + b_expert_router.astype(jnp.float32)).reshape(N, N_GROUPS, EXPERTS_PER_GROUP)
    e_logits = jnp.take_along_axis(e_logits, g_idx[:, :, None], axis=1)[:, 0]
    e_p, e_i = lax.top_k(jax.nn.softmax(e_logits, axis=-1), TOP_K)
    weights = g_p * (e_p / jnp.sum(e_p, axis=-1, keepdims=True))
    expert_ids = g_idx * EXPERTS_PER_GROUP + e_i

    A = N * TOP_K
    flat_e = expert_ids.reshape(A).astype(jnp.int32)
    flat_tok = jnp.repeat(jnp.arange(N, dtype=jnp.int32), TOP_K)
    flat_w = weights.reshape(A)
    order = jnp.argsort(flat_e, stable=True)
    se, stok, sw = flat_e[order], flat_tok[order], flat_w[order]
    counts = jnp.zeros((N_EXPERTS,), jnp.int32).at[flat_e].add(1)
    starts = jnp.cumsum(counts) - counts
    padded = (counts + MOE_BLOCK - 1) // MOE_BLOCK * MOE_BLOCK
    pad_ends = jnp.cumsum(padded)
    pad_starts = pad_ends - padded
    dest = pad_starts[se] + jnp.arange(A, dtype=jnp.int32) - starts[se]
    n_blocks = A // MOE_BLOCK + N_EXPERTS
    P = n_blocks * MOE_BLOCK
    row_tok = jnp.full((P,), N, jnp.int32).at[dest].set(stok)
    row_w = jnp.zeros((P,), jnp.float32).at[dest].set(sw)
    blk_e = jnp.minimum(jnp.searchsorted(pad_ends, jnp.arange(n_blocks, dtype=jnp.int32) * MOE_BLOCK,
                                         side='right'), N_EXPERTS - 1).astype(jnp.int32)
    x_pad = jnp.concatenate([xt, jnp.zeros((1, D), xt.dtype)], axis=0)
    xb = x_pad[row_tok].reshape(n_blocks, MOE_BLOCK, D)

    def expert_block(args):
        xblk, e = args
        hdn = jax.nn.silu(xblk @ w_gate[e]) * (xblk @ w_up[e])
        return hdn @ w_down[e]

    yb = lax.map(expert_block, (xb, blk_e)).reshape(P, D)
    y = jnp.zeros((N + 1, D), yb.dtype).at[row_tok].add(yb * row_w[:, None].astype(yb.dtype))
    return y[:N].reshape(B, S, D)


def setup_inputs(seed: int = 0) -> dict:
    key = jax.random.key(seed)
    ks = jax.random.split(key, 24)
    f32 = jnp.float32

    def nrm(k, shape, scale):
        return jax.random.normal(k, shape, f32) * scale

    def gain(k, shape):
        return 1.0 + 0.01 * jax.random.normal(k, shape, f32)

    x = jax.random.normal(ks[0], (BATCH, SEQ, D_MODEL), f32)
    p = jax.random.normal(ks[1], (DEPTH, BATCH, SEQ, PLE_DIM), f32)
    offs = jax.random.randint(ks[2], (BATCH, 1), 0, 1024, dtype=jnp.int32)
    positions = offs + jnp.arange(SEQ, dtype=jnp.int32)[None, :]
    return {
        "x": x,
        "p": p,
        "positions": positions,
        "attn_norm_g": gain(ks[3], (DEPTH, D_MODEL)),
        "w_in": nrm(ks[4], (DEPTH, D_MODEL, IN_COLS), D_MODEL ** -0.5),
        "q_norm_g": gain(ks[5], (DEPTH, Q_LORA)),
        "w_uq": nrm(ks[6], (DEPTH, Q_LORA, N_HEADS, QK_NOPE_DIM + QK_ROPE_DIM), Q_LORA ** -0.5),
        "kv_norm_g": gain(ks[7], (DEPTH, KV_LORA)),
        "w_ukv": nrm(ks[8], (DEPTH, KV_LORA, N_HEADS, QK_NOPE_DIM + V_HEAD_DIM), KV_LORA ** -0.5),
        "conv_w": nrm(ks[9], (DEPTH, CONV_K, 1, CONV_WIDTH), CONV_K ** -0.5),
        "w_out": nrm(ks[10], (DEPTH, MIX_WIDTH, D_MODEL), MIX_WIDTH ** -0.5),
        "moe_norm_g": gain(ks[11], (DEPTH, D_MODEL)),
        "w_group_router": nrm(ks[12], (DEPTH, D_MODEL, N_GROUPS), D_MODEL ** -0.5),
        "b_group_router": nrm(ks[13], (DEPTH, N_GROUPS), 0.01),
        "w_expert_router": nrm(ks[14], (DEPTH, D_MODEL, N_EXPERTS), D_MODEL ** -0.5),
        "b_expert_router": nrm(ks[15], (DEPTH, N_EXPERTS), 0.01),
        "w_gate": nrm(ks[16], (DEPTH, N_EXPERTS, D_MODEL, EXPERT_FF), D_MODEL ** -0.5),
        "w_up": nrm(ks[17], (DEPTH, N_EXPERTS, D_MODEL, EXPERT_FF), D_MODEL ** -0.5),
        "w_down": nrm(ks[18], (DEPTH, N_EXPERTS, EXPERT_FF, D_MODEL), EXPERT_FF ** -0.5),
        "ple_norm_g": gain(ks[19], (DEPTH, D_MODEL)),
        "w_ple_gate": nrm(ks[20], (DEPTH, D_MODEL, D_MODEL), D_MODEL ** -0.5),
        "b_ple_gate": nrm(ks[21], (DEPTH, D_MODEL), 0.01),
        "w_ple_proj": nrm(ks[22], (DEPTH, PLE_DIM, D_MODEL), PLE_DIM ** -0.5),
        "final_norm_g": gain(ks[23], (D_MODEL,)),
    }


def reference(x, p, positions, attn_norm_g, w_in, q_norm_g, w_uq, kv_norm_g, w_ukv, conv_w,
              w_out, moe_norm_g, w_group_router, b_group_router, w_expert_router,
              b_expert_router, w_gate, w_up, w_down, ple_norm_g, w_ple_gate, b_ple_gate,
              w_ple_proj, final_norm_g):
    cos, sin = rope_tables(positions)
    split_pts = [int(v) for v in np.cumsum(IN_SPLITS)[:-1]]
    h = x
    for i in range(DEPTH):
        xn = rmsnorm(h, attn_norm_g[i])
        z = xn @ w_in[i]
        c_q, c_kv, k_r, b_g, c_g, u = jnp.split(z, split_pts, axis=-1)
        o_mla = mla_mixer(c_q, c_kv, k_r, q_norm_g[i], w_uq[i], kv_norm_g[i], w_ukv[i], cos, sin)
        o_conv = short_conv_mixer(b_g, c_g, u, conv_w[i])
        h = h + jnp.concatenate([o_mla, o_conv], axis=-1) @ w_out[i]
        h = h + hier_moe(rmsnorm(h, moe_norm_g[i]), w_group_router[i], b_group_router[i],
                         w_expert_router[i], b_expert_router[i], w_gate[i], w_up[i], w_down[i])
        gate = jax.nn.sigmoid(rmsnorm(h, ple_norm_g[i]) @ w_ple_gate[i] + b_ple_gate[i])
        h = h + gate * (p[i] @ w_ple_proj[i])
    return rmsnorm(h, final_norm_g)
```

```python
import functools
import math

import jax
import jax.numpy as jnp
from jax import lax
from jax.experimental import pallas as pl
from jax.experimental.pallas import tpu as pltpu

D_MODEL = 1024
PLE_DIM = 256
MLA_WIDTH = 512
CONV_WIDTH = 512
N_HEADS = 8
V_HEAD_DIM = 64
QK_NOPE_DIM = 64
QK_ROPE_DIM = 32
Q_LORA = 384
KV_LORA = 256
CONV_K = 3
N_GROUPS = 4
EXPERTS_PER_GROUP = 8
N_EXPERTS = N_GROUPS * EXPERTS_PER_GROUP
EXPERT_FF = 512
ROPE_BASE = 10000.0
EPS = 1e-6

HEAD_PAD = 128
QK_DIM = QK_NOPE_DIM + QK_ROPE_DIM
HALF_ROPE = QK_ROPE_DIM // 2

_C_Q = 0
_C_KV = _C_Q + Q_LORA
_C_BG = _C_KV + KV_LORA
_C_CG = _C_BG + CONV_WIDTH
_C_U = _C_CG + CONV_WIDTH
_C_KR = _C_U + CONV_WIDTH
IN_COLS_PAD = _C_KR + HEAD_PAD

TM = 256
TQ = 512
TK = 256
MOE_BLK = 256
NEG = -1e30
VMEM_LIMIT = 48 * 1024 * 1024

F32 = jnp.float32
BF16 = jnp.bfloat16


def _rms(x, g):
    return x * lax.rsqrt(jnp.mean(x * x, axis=-1, keepdims=True) + EPS) * g


def _dot(a, b):
    return jnp.dot(a, b, preferred_element_type=F32)


def _dot_nt(a, b):
    return lax.dot_general(a, b, (((1,), (1,)), ((), ())), preferred_element_type=F32)


def _dot_tn(a, b):
    return lax.dot_general(a, b, (((0,), (0,)), ((), ())), preferred_element_type=F32)


def _inproj_kernel(x_ref, g_ref, win_ref, qg_ref, wqa_ref, wqb_ref, kvg_ref, wuk_ref, pk_ref,
                   wuvt_ref, ct_ref, st_ref, ck_ref, convw_ref,
                   qt_ref, k_ref, vt_ref, oc_ref, carry_ref, ext_ref):
    tm = x_ref.shape[1]
    x = x_ref[0]
    xn = _rms(x, g_ref[...])
    z = _dot(xn.astype(BF16), win_ref[...])

    cqn = _rms(z[:, _C_Q:_C_Q + Q_LORA], qg_ref[...]).astype(BF16)
    qa = _dot_nt(wqa_ref[...], cqn)
    qb = _dot_nt(wqb_ref[...], cqn)
    ct = ct_ref[0]
    st = st_ref[0]
    for h in range(N_HEADS):
        r0 = h * HEAD_PAD
        qt_ref[0, r0:r0 + HEAD_PAD, :] = (qa[r0:r0 + HEAD_PAD] * ct
                                          + qb[r0:r0 + HEAD_PAD] * st).astype(BF16)

    kvn = _rms(z[:, _C_KV:_C_KV + KV_LORA], kvg_ref[...]).astype(BF16)
    krot = (z[:, _C_KR:_C_KR + HEAD_PAD] * ck_ref[0]).astype(BF16)
    k_ref[0] = (_dot(kvn, wuk_ref[...]) + _dot(krot, pk_ref[...])).astype(BF16)
    vt_ref[0] = _dot_nt(wuvt_ref[...], kvn).astype(BF16)

    @pl.when(pl.program_id(1) == 0)
    def _():
        carry_ref[...] = jnp.zeros_like(carry_ref)

    cu = z[:, _C_CG:_C_CG + CONV_WIDTH] * z[:, _C_U:_C_U + CONV_WIDTH]
    ext_ref[0:8, :] = carry_ref[...]
    ext_ref[8:8 + tm, :] = cu
    cu1 = ext_ref[7:7 + tm, :]
    cu2 = ext_ref[6:6 + tm, :]
    w = convw_ref[...]
    y = w[0:1] * cu2 + w[1:2] * cu1 + w[2:3] * cu
    oc_ref[0] = (z[:, _C_BG:_C_BG + CONV_WIDTH] * y).astype(BF16)
    carry_ref[...] = ext_ref[tm:tm + 8, :]


def _inproj(x, g, win, qg, wqa, wqb, kvg, wuk, pk, wuvt, ct, st, ck, convw):
    B, S, D = x.shape
    tm = TM
    full = lambda a: pl.BlockSpec(a.shape, lambda b, i: (0,) * a.ndim)
    return pl.pallas_call(
        _inproj_kernel,
        grid=(B, S // tm),
        in_specs=[
            pl.BlockSpec((1, tm, D), lambda b, i: (b, i, 0)),
            full(g), full(win), full(qg), full(wqa), full(wqb), full(kvg), full(wuk), full(pk),
            full(wuvt),
            pl.BlockSpec((1, HEAD_PAD, tm), lambda b, i: (b, 0, i)),
            pl.BlockSpec((1, HEAD_PAD, tm), lambda b, i: (b, 0, i)),
            pl.BlockSpec((1, tm, HEAD_PAD), lambda b, i: (b, i, 0)),
            full(convw),
        ],
        out_specs=[
            pl.BlockSpec((1, N_HEADS * HEAD_PAD, tm), lambda b, i: (b, 0, i)),
            pl.BlockSpec((1, tm, N_HEADS * HEAD_PAD), lambda b, i: (b, i, 0)),
            pl.BlockSpec((1, MLA_WIDTH, tm), lambda b, i: (b, 0, i)),
            pl.BlockSpec((1, tm, CONV_WIDTH), lambda b, i: (b, i, 0)),
        ],
        out_shape=[
            jax.ShapeDtypeStruct((B, N_HEADS * HEAD_PAD, S), BF16),
            jax.ShapeDtypeStruct((B, S, N_HEADS * HEAD_PAD), BF16),
            jax.ShapeDtypeStruct((B, MLA_WIDTH, S), BF16),
            jax.ShapeDtypeStruct((B, S, CONV_WIDTH), BF16),
        ],
        scratch_shapes=[pltpu.VMEM((8, CONV_WIDTH), F32), pltpu.VMEM((tm + 8, CONV_WIDTH), F32)],
        compiler_params=pltpu.CompilerParams(
            dimension_semantics=("arbitrary", "arbitrary"), vmem_limit_bytes=VMEM_LIMIT),
        name="inproj",
    )(x, g, win, qg, wqa, wqb, kvg, wuk, pk, wuvt, ct, st, ck, convw)


def _attn_kernel(qt_ref, k_ref, vt_ref, o_ref, m_ref, l_ref, acc_ref):
    tq = qt_ref.shape[2]
    tk = TK
    i = pl.program_id(2)
    qt = qt_ref[0]
    m_ref[...] = jnp.full_like(m_ref, NEG)
    l_ref[...] = jnp.zeros_like(l_ref)
    acc_ref[...] = jnp.zeros_like(acc_ref)

    def step(j, masked):
        k0 = pl.multiple_of(j * tk, tk)
        s = _dot(k_ref[0, pl.ds(k0, tk), :], qt)
        if masked:
            kidx = k0 + lax.broadcasted_iota(jnp.int32, (tk, tq), 0)
            qidx = i * tq + lax.broadcasted_iota(jnp.int32, (tk, tq), 1)
            s = jnp.where(kidx <= qidx, s, NEG)
        m_old = m_ref[...]
        m_new = jnp.maximum(m_old, jnp.max(s, axis=0, keepdims=True))
        alpha = jnp.exp2(m_old - m_new)
        p = jnp.exp2(s - m_new)
        l_ref[...] = alpha * l_ref[...] + jnp.sum(p, axis=0, keepdims=True)
        acc_ref[...] = alpha * acc_ref[...] + _dot(vt_ref[0, :, pl.ds(k0, tk)], p.astype(BF16))
        m_ref[...] = m_new

    n_full = i * (tq // tk)

    def body(j, c):
        step(j, False)
        return c

    lax.fori_loop(0, n_full, body, 0)
    for d in range(tq // tk):
        step(n_full + d, True)
    o_ref[0] = (acc_ref[...] / l_ref[...]).astype(o_ref.dtype)


def _attention(qt, k, vt):
    B, _, S = qt.shape
    return pl.pallas_call(
        _attn_kernel,
        grid=(B, N_HEADS, S // TQ),
        in_specs=[
            pl.BlockSpec((1, HEAD_PAD, TQ), lambda b, h, i: (b, h, i)),
            pl.BlockSpec((1, S, HEAD_PAD), lambda b, h, i: (b, 0, h)),
            pl.BlockSpec((1, V_HEAD_DIM, S), lambda b, h, i: (b, h, 0)),
        ],
        out_specs=pl.BlockSpec((1, V_HEAD_DIM, TQ), lambda b, h, i: (b, h, i)),
        out_shape=jax.ShapeDtypeStruct((B, MLA_WIDTH, S), BF16),
        scratch_shapes=[pltpu.VMEM((1, TQ), F32), pltpu.VMEM((1, TQ), F32),
                        pltpu.VMEM((V_HEAD_DIM, TQ), F32)],
        compiler_params=pltpu.CompilerParams(
            dimension_semantics=("arbitrary", "arbitrary", "arbitrary"),
            vmem_limit_bytes=VMEM_LIMIT),
        name="attn",
    )(qt, k, vt)


def _outproj_kernel(x_ref, ot_ref, oc_ref, wo_ref, g_ref, wrh_ref, wrl_ref, br_ref,
                    h1_ref, xn_ref, eid_ref, wt_ref, rank_ref, cnt_ref, carry_ref):
    tm = x_ref.shape[1]

    @pl.when((pl.program_id(0) == 0) & (pl.program_id(1) == 0))
    def _():
        carry_ref[...] = jnp.zeros_like(carry_ref)

    attn = (_dot_tn(ot_ref[0], wo_ref[0:MLA_WIDTH, :])
            + _dot(oc_ref[0], wo_ref[MLA_WIDTH:MLA_WIDTH + CONV_WIDTH, :]))
    h1 = x_ref[0] + attn
    h1_ref[0] = h1
    xn = _rms(h1, g_ref[...])
    xn_ref[0] = xn

    xh = xn.astype(BF16)
    xl = (xn - xh.astype(F32)).astype(BF16)
    logits = (_dot(xh, wrh_ref[...]) + _dot(xl, wrh_ref[...]) + _dot(xh, wrl_ref[...])
              + br_ref[...])
    lane = lax.broadcasted_iota(jnp.int32, logits.shape, 1)
    big = jnp.int32(1 << 20)

    glog = jnp.where(lane < N_GROUPS, logits, NEG)
    gmax = jnp.max(glog, axis=-1, keepdims=True)
    gsum = jnp.sum(jnp.exp(glog - gmax), axis=-1, keepdims=True)
    g_p = 1.0 / gsum
    g_idx = jnp.min(jnp.where(glog == gmax, lane, big), axis=-1, keepdims=True)

    e_lo = N_GROUPS + g_idx * EXPERTS_PER_GROUP
    elog = jnp.where((lane >= e_lo) & (lane < e_lo + EXPERTS_PER_GROUP), logits, NEG)
    emax = jnp.max(elog, axis=-1, keepdims=True)
    esum = jnp.sum(jnp.exp(elog - emax), axis=-1, keepdims=True)
    e1 = jnp.min(jnp.where(elog == emax, lane, big), axis=-1, keepdims=True)
    elog2 = jnp.where(lane == e1, NEG, elog)
    emax2 = jnp.max(elog2, axis=-1, keepdims=True)
    e2 = jnp.min(jnp.where(elog2 == emax2, lane, big), axis=-1, keepdims=True)
    p1 = 1.0 / esum
    p2 = jnp.exp(emax2 - emax) / esum
    psum = p1 + p2
    w1 = g_p * (p1 / psum)
    w2 = g_p * (p2 / psum)
    e1 = e1 - N_GROUPS
    e2 = e2 - N_GROUPS

    oh1 = lane == e1
    oh2 = lane == e2
    oh = jnp.where(oh1 | oh2, 1.0, 0.0)
    row = lax.broadcasted_iota(jnp.int32, (tm, tm), 0)
    col = lax.broadcasted_iota(jnp.int32, (tm, tm), 1)
    lower = jnp.where(row > col, 1.0, 0.0).astype(BF16)
    cum = _dot(lower, oh.astype(BF16)) + carry_ref[...]
    r1 = jnp.sum(jnp.where(oh1, cum, 0.0), axis=-1, keepdims=True)
    r2 = jnp.sum(jnp.where(oh2, cum, 0.0), axis=-1, keepdims=True)
    carry_ref[...] = carry_ref[...] + jnp.sum(oh, axis=0, keepdims=True)
    cnt_ref[...] = carry_ref[...]

    eid_ref[0] = jnp.where(lane == 0, e1, jnp.where(lane == 1, e2, 0))
    rank_ref[0] = jnp.where(lane == 0, r1, jnp.where(lane == 1, r2, 0.0)).astype(jnp.int32)
    wt_ref[0] = jnp.where(lane == 0, w1, jnp.where(lane == 1, w2, 0.0))


def _outproj(x, ot, oc, wo, g, wrh, wrl, br):
    B, S, D = x.shape
    tm = TM
    full = lambda a: pl.BlockSpec(a.shape, lambda b, i: (0,) * a.ndim)
    tile = lambda w: pl.BlockSpec((1, tm, w), lambda b, i: (b, i, 0))
    return pl.pallas_call(
        _outproj_kernel,
        grid=(B, S // tm),
        in_specs=[
            tile(D),
            pl.BlockSpec((1, MLA_WIDTH, tm), lambda b, i: (b, 0, i)),
            tile(CONV_WIDTH),
            full(wo), full(g), full(wrh), full(wrl), full(br),
        ],
        out_specs=[tile(D), tile(D), tile(HEAD_PAD), tile(HEAD_PAD), tile(HEAD_PAD),
                   pl.BlockSpec((1, HEAD_PAD), lambda b, i: (0, 0))],
        out_shape=[
            jax.ShapeDtypeStruct((B, S, D), F32),
            jax.ShapeDtypeStruct((B, S, D), F32),
            jax.ShapeDtypeStruct((B, S, HEAD_PAD), jnp.int32),
            jax.ShapeDtypeStruct((B, S, HEAD_PAD), F32),
            jax.ShapeDtypeStruct((B, S, HEAD_PAD), jnp.int32),
            jax.ShapeDtypeStruct((1, HEAD_PAD), F32),
        ],
        scratch_shapes=[pltpu.VMEM((1, HEAD_PAD), F32)],
        compiler_params=pltpu.CompilerParams(
            dimension_semantics=("arbitrary", "arbitrary"), vmem_limit_bytes=VMEM_LIMIT),
        name="outproj",
    )(x, ot, oc, wo, g, wrh, wrl, br)


def _dispatch_kernel(dest_ref, xn_ref, xb_in_ref, xb_ref, sem):
    del xb_in_ref
    tm = xn_ref.shape[0]
    base = pl.program_id(0) * tm

    def row_copy(r, d):
        return pltpu.make_async_copy(xn_ref.at[pl.ds(r, 1)], xb_ref.at[pl.ds(d, 1)], sem)

    def issue(r, c):
        t = (base + r) * 2
        row_copy(r, dest_ref[t]).start()
        row_copy(r, dest_ref[t + 1]).start()
        return c

    lax.fori_loop(0, tm, issue, 0, unroll=8)
    for _ in range(2):
        pltpu.make_async_copy(xn_ref, xb_ref.at[pl.ds(0, tm)], sem).wait()


def _dispatch(dest, xn, xb_init):
    N, D = xn.shape
    tm = TM
    return pl.pallas_call(
        _dispatch_kernel,
        grid_spec=pltpu.PrefetchScalarGridSpec(
            num_scalar_prefetch=1,
            grid=(N // tm,),
            in_specs=[pl.BlockSpec((tm, D), lambda i, dest: (i, 0)),
                      pl.BlockSpec(memory_space=pl.ANY)],
            out_specs=pl.BlockSpec(memory_space=pl.ANY),
            scratch_shapes=[pltpu.SemaphoreType.DMA],
        ),
        out_shape=jax.ShapeDtypeStruct(xb_init.shape, xb_init.dtype),
        input_output_aliases={2: 0},
        compiler_params=pltpu.CompilerParams(
            dimension_semantics=("arbitrary",), vmem_limit_bytes=VMEM_LIMIT),
        name="dispatch",
    )(dest, xn, xb_init)


def _expert_kernel(blke_ref, nused_ref, xb_ref, wg_ref, wu_ref, wd_ref, yb_ref):
    del blke_ref
    i = pl.program_id(0)

    @pl.when(i < nused_ref[0])
    def _():
        xb = xb_ref[...].astype(BF16)
        g = _dot(xb, wg_ref[0])
        u = _dot(xb, wu_ref[0])
        hdn = (g * jax.nn.sigmoid(g)) * u
        yb_ref[...] = _dot(hdn.astype(BF16), wd_ref[0])

    @pl.when(i >= nused_ref[0])
    def _():
        yb_ref[...] = jnp.zeros_like(yb_ref)


def _experts(blk_e, n_used, xb, wg, wu, wd):
    P, D = xb.shape
    blk = MOE_BLK
    return pl.pallas_call(
        _expert_kernel,
        grid_spec=pltpu.PrefetchScalarGridSpec(
            num_scalar_prefetch=2,
            grid=(P // blk,),
            in_specs=[
                pl.BlockSpec((blk, D), lambda i, be, nu: (i, 0)),
                pl.BlockSpec((1, D, EXPERT_FF), lambda i, be, nu: (be[i], 0, 0)),
                pl.BlockSpec((1, D, EXPERT_FF), lambda i, be, nu: (be[i], 0, 0)),
                pl.BlockSpec((1, EXPERT_FF, D), lambda i, be, nu: (be[i], 0, 0)),
            ],
            out_specs=pl.BlockSpec((blk, D), lambda i, be, nu: (i, 0)),
        ),
        out_shape=jax.ShapeDtypeStruct((P, D), F32),
        compiler_params=pltpu.CompilerParams(
            dimension_semantics=("arbitrary",), vmem_limit_bytes=VMEM_LIMIT),
        name="experts",
    )(blk_e, n_used, xb, wg, wu, wd)


def _final_kernel(dest_ref, h1_ref, p_ref, wt_ref, yb_ref, gp_ref, wg_ref, bg_ref, wp_ref,
                  gf_ref, out_ref, buf_ref, sem):
    tm = h1_ref.shape[0]
    base = pl.program_id(0) * tm

    def row_copy(k, r, d):
        return pltpu.make_async_copy(yb_ref.at[pl.ds(d, 1)], buf_ref.at[k, pl.ds(r, 1)], sem)

    def issue(r, c):
        t = (base + r) * 2
        row_copy(0, r, dest_ref[t]).start()
        row_copy(1, r, dest_ref[t + 1]).start()
        return c

    lax.fori_loop(0, tm, issue, 0, unroll=8)
    pe = _dot(p_ref[...].astype(BF16), wp_ref[...])
    for k in range(2):
        pltpu.make_async_copy(yb_ref.at[pl.ds(0, tm)], buf_ref.at[k], sem).wait()

    wt = wt_ref[...]
    h2 = h1_ref[...] + wt[:, 0:1] * buf_ref[0] + wt[:, 1:2] * buf_ref[1]
    n = _rms(h2, gp_ref[...]).astype(BF16)
    gate = jax.nn.sigmoid(_dot(n, wg_ref[...]) + bg_ref[...])
    h3 = h2 + gate * pe
    out_ref[...] = _rms(h3, gf_ref[...])


def _final(dest, h1, p, wt, yb, gp, wg, bg, wp, gf):
    N, D = h1.shape
    tm = TM
    full = lambda a: pl.BlockSpec(a.shape, lambda i, dest: (0,) * a.ndim)
    return pl.pallas_call(
        _final_kernel,
        grid_spec=pltpu.PrefetchScalarGridSpec(
            num_scalar_prefetch=1,
            grid=(N // tm,),
            in_specs=[
                pl.BlockSpec((tm, D), lambda i, dest: (i, 0)),
                pl.BlockSpec((tm, PLE_DIM), lambda i, dest: (i, 0)),
                pl.BlockSpec((tm, HEAD_PAD), lambda i, dest: (i, 0)),
                pl.BlockSpec(memory_space=pl.ANY),
                full(gp), full(wg), full(bg), full(wp), full(gf),
            ],
            out_specs=pl.BlockSpec((tm, D), lambda i, dest: (i, 0)),
            scratch_shapes=[pltpu.VMEM((2, tm, D), F32), pltpu.SemaphoreType.DMA],
        ),
        out_shape=jax.ShapeDtypeStruct((N, D), F32),
        compiler_params=pltpu.CompilerParams(
            dimension_semantics=("arbitrary",), vmem_limit_bytes=VMEM_LIMIT),
        name="final",
    )(dest, h1, p, wt, yb, gp, wg, bg, wp, gf)


def _prep_weights(w_in, w_uq, w_ukv):
    cq, ckv, kr, bg, cg, u = jnp.split(
        w_in, [Q_LORA, Q_LORA + KV_LORA, Q_LORA + KV_LORA + QK_ROPE_DIM,
               Q_LORA + KV_LORA + QK_ROPE_DIM + CONV_WIDTH,
               Q_LORA + KV_LORA + QK_ROPE_DIM + 2 * CONV_WIDTH], axis=1)
    kr_rot = jnp.concatenate([-kr[:, HALF_ROPE:], kr[:, :HALF_ROPE]], axis=1)
    pad = jnp.zeros((D_MODEL, HEAD_PAD - 2 * QK_ROPE_DIM), w_in.dtype)
    win = jnp.concatenate([cq, ckv, bg, cg, u, kr, kr_rot, pad], axis=1).astype(BF16)

    zq = jnp.zeros((Q_LORA, N_HEADS, HEAD_PAD - QK_DIM), w_uq.dtype)
    wqa = jnp.concatenate([w_uq, zq], axis=2)
    rope = w_uq[:, :, QK_NOPE_DIM:]
    rot = jnp.concatenate([-rope[:, :, HALF_ROPE:], rope[:, :, :HALF_ROPE]], axis=2)
    wqb = jnp.concatenate([jnp.zeros((Q_LORA, N_HEADS, QK_NOPE_DIM), w_uq.dtype), rot, zq], axis=2)
    wqa = wqa.reshape(Q_LORA, N_HEADS * HEAD_PAD).T.astype(BF16)
    wqb = wqb.reshape(Q_LORA, N_HEADS * HEAD_PAD).T.astype(BF16)

    zk = jnp.zeros((KV_LORA, N_HEADS, HEAD_PAD - QK_NOPE_DIM), w_ukv.dtype)
    wuk = jnp.concatenate([w_ukv[:, :, :QK_NOPE_DIM], zk], axis=2)
    wuk = wuk.reshape(KV_LORA, N_HEADS * HEAD_PAD).astype(BF16)
    wuvt = w_ukv[:, :, QK_NOPE_DIM:].reshape(KV_LORA, MLA_WIDTH).T.astype(BF16)

    src = jnp.arange(HEAD_PAD)[:, None]
    dst = jnp.arange(N_HEADS * HEAD_PAD)[None, :]
    dlane = dst % HEAD_PAD - QK_NOPE_DIM
    pk = ((src < 2 * QK_ROPE_DIM) & (dlane >= 0) & (dlane < QK_ROPE_DIM)
          & (src % QK_ROPE_DIM == dlane)).astype(BF16)
    return win, wqa, wqb, wuk, wuvt, pk


def _rope_tables(positions):
    inv_freq = ROPE_BASE ** (-jnp.arange(0, QK_ROPE_DIM, 2, dtype=F32) / QK_ROPE_DIM)
    ang = positions.astype(F32)[..., None] * inv_freq
    cos, sin = jnp.cos(ang), jnp.sin(ang)
    B, S, _ = cos.shape
    qscale = (QK_DIM ** -0.5) * math.log2(math.e)
    cos_t = jnp.swapaxes(cos, 1, 2) * qscale
    sin_t = jnp.swapaxes(sin, 1, 2) * qscale
    ones = jnp.full((B, QK_NOPE_DIM, S), qscale, F32)
    zpad = jnp.zeros((B, HEAD_PAD - QK_DIM, S), F32)
    ct = jnp.concatenate([ones, cos_t, cos_t, zpad], axis=1)
    st = jnp.concatenate([jnp.zeros((B, QK_NOPE_DIM, S), F32), sin_t, sin_t, zpad], axis=1)
    ck = jnp.concatenate([cos, cos, sin, sin,
                          jnp.zeros((B, S, HEAD_PAD - 2 * QK_ROPE_DIM), F32)], axis=2)
    return ct, st, ck


def kernel(x, p, positions, attn_norm_g, w_in, q_norm_g, w_uq, kv_norm_g, w_ukv, conv_w, w_out,
           moe_norm_g, w_group_router, b_group_router, w_expert_router, b_expert_router,
           w_gate, w_up, w_down, ple_norm_g, w_ple_gate, b_ple_gate, w_ple_proj, final_norm_g):
    B, S, D = x.shape
    N = B * S
    assert w_in.shape[0] == 1, "single-layer trunk: the final norm is fused into the layer"
    ct, st, ck = _rope_tables(positions)
    row = lambda v: v.reshape(1, -1)
    h = x
    for i in range(1):
        win, wqa, wqb, wuk, wuvt, pk = _prep_weights(w_in[i], w_uq[i], w_ukv[i])
        qt, k, vt, oc = _inproj(h, row(attn_norm_g[i]), win, row(q_norm_g[i]), wqa, wqb,
                                row(kv_norm_g[i]), wuk, pk, wuvt, ct, st, ck,
                                conv_w[i].reshape(CONV_K, CONV_WIDTH))
        ot = _attention(qt, k, vt)

        wr = jnp.concatenate(
            [w_group_router[i], w_expert_router[i],
             jnp.zeros((D, HEAD_PAD - N_GROUPS - N_EXPERTS), F32)], axis=1)
        wrh = wr.astype(BF16)
        wrl = (wr - wrh.astype(F32)).astype(BF16)
        br = jnp.concatenate([b_group_router[i], b_expert_router[i],
                              jnp.zeros((HEAD_PAD - N_GROUPS - N_EXPERTS,), F32)]).reshape(1, -1)
        h1, xn, eid, wt, rank, cnt = _outproj(h, ot, oc, w_out[i].astype(BF16),
                                              row(moe_norm_g[i]), wrh, wrl, br)

        n_blocks = (2 * N) // MOE_BLK + N_EXPERTS
        counts = cnt[0, :N_EXPERTS].astype(jnp.int32)
        padded = (counts + MOE_BLK - 1) // MOE_BLK * MOE_BLK
        pad_ends = jnp.cumsum(padded)
        pad_starts = pad_ends - padded
        dest = (pad_starts[eid[..., :2]] + rank[..., :2]).reshape(2 * N).astype(jnp.int32)
        blk_e = jnp.minimum(
            jnp.searchsorted(pad_ends, jnp.arange(n_blocks, dtype=jnp.int32) * MOE_BLK,
                             side='right'), N_EXPERTS - 1).astype(jnp.int32)
        n_used = (pad_ends[-1:] // MOE_BLK).astype(jnp.int32)

        xb = _dispatch(dest, xn.reshape(N, D), jnp.zeros((n_blocks * MOE_BLK, D), F32))
        yb = _experts(blk_e, n_used, xb, w_gate[i].astype(BF16), w_up[i].astype(BF16),
                      w_down[i].astype(BF16))
        out = _final(dest, h1.reshape(N, D), p[i].reshape(N, PLE_DIM), wt.reshape(N, HEAD_PAD), yb,
                     row(ple_norm_g[i]), w_ple_gate[i].astype(BF16), row(b_ple_gate[i]),
                     w_ple_proj[i].astype(BF16), row(final_norm_g))
        h = out.reshape(B, S, D)
    return h
```

```python
import functools
import math

import jax
import jax.numpy as jnp
from jax import lax
from jax.experimental import pallas as pl
from jax.experimental.pallas import tpu as pltpu

D_MODEL = 1024
PLE_DIM = 256
MLA_WIDTH = 512
CONV_WIDTH = 512
N_HEADS = 8
V_HEAD_DIM = 64
QK_NOPE_DIM = 64
QK_ROPE_DIM = 32
Q_LORA = 384
KV_LORA = 256
CONV_K = 3
N_GROUPS = 4
EXPERTS_PER_GROUP = 8
N_EXPERTS = N_GROUPS * EXPERTS_PER_GROUP
EXPERT_FF = 512
ROPE_BASE = 10000.0
EPS = 1e-6

HEAD_PAD = 128
QK_DIM = QK_NOPE_DIM + QK_ROPE_DIM
HALF_ROPE = QK_ROPE_DIM // 2

_C_Q = 0
_C_KV = _C_Q + Q_LORA
_C_BG = _C_KV + KV_LORA
_C_CG = _C_BG + CONV_WIDTH
_C_U = _C_CG + CONV_WIDTH
_C_KR = _C_U + CONV_WIDTH
IN_COLS_PAD = _C_KR + HEAD_PAD

TM = 256
TQ = 512
TK = 256
ATT_HG = 4
MOE_BLK = 256
NEG = -1e30
VMEM_LIMIT = 48 * 1024 * 1024

F32 = jnp.float32
BF16 = jnp.bfloat16


def _rms(x, g):
    return x * lax.rsqrt(jnp.mean(x * x, axis=-1, keepdims=True) + EPS) * g


def _dot(a, b):
    return jnp.dot(a, b, preferred_element_type=F32)


def _dot_nt(a, b):
    return lax.dot_general(a, b, (((1,), (1,)), ((), ())), preferred_element_type=F32)


def _dot_tn(a, b):
    return lax.dot_general(a, b, (((0,), (0,)), ((), ())), preferred_element_type=F32)


def _inproj_kernel(x_ref, g_ref, win_ref, qg_ref, wqa_ref, wqb_ref, kvg_ref, wuk_ref, pk_ref,
                   wuvt_ref, ct_ref, st_ref, ck_ref, convw_ref,
                   qt_ref, k_ref, vt_ref, oc_ref, carry_ref, ext_ref):
    tm = x_ref.shape[1]
    x = x_ref[0]
    xn = _rms(x, g_ref[...])
    z = _dot(xn.astype(BF16), win_ref[...])

    cqn = _rms(z[:, _C_Q:_C_Q + Q_LORA], qg_ref[...]).astype(BF16)
    qa = _dot_nt(wqa_ref[...], cqn)
    qb = _dot_nt(wqb_ref[...], cqn)
    ct = ct_ref[0]
    st = st_ref[0]
    for h in range(N_HEADS):
        r0 = h * HEAD_PAD
        qt_ref[0, r0:r0 + HEAD_PAD, :] = (qa[r0:r0 + HEAD_PAD] * ct
                                          + qb[r0:r0 + HEAD_PAD] * st).astype(BF16)

    kvn = _rms(z[:, _C_KV:_C_KV + KV_LORA], kvg_ref[...]).astype(BF16)
    krot = (z[:, _C_KR:_C_KR + HEAD_PAD] * ck_ref[0]).astype(BF16)
    k_ref[0] = (_dot(kvn, wuk_ref[...]) + _dot(krot, pk_ref[...])).astype(BF16)
    vt_ref[0] = _dot_nt(wuvt_ref[...], kvn).astype(BF16)

    @pl.when(pl.program_id(1) == 0)
    def _():
        carry_ref[...] = jnp.zeros_like(carry_ref)

    cu = z[:, _C_CG:_C_CG + CONV_WIDTH] * z[:, _C_U:_C_U + CONV_WIDTH]
    ext_ref[0:8, :] = carry_ref[...]
    ext_ref[8:8 + tm, :] = cu
    cu1 = ext_ref[7:7 + tm, :]
    cu2 = ext_ref[6:6 + tm, :]
    w = convw_ref[...]
    y = w[0:1] * cu2 + w[1:2] * cu1 + w[2:3] * cu
    oc_ref[0] = (z[:, _C_BG:_C_BG + CONV_WIDTH] * y).astype(BF16)
    carry_ref[...] = ext_ref[tm:tm + 8, :]


def _inproj(x, g, win, qg, wqa, wqb, kvg, wuk, pk, wuvt, ct, st, ck, convw):
    B, S, D = x.shape
    tm = TM
    full = lambda a: pl.BlockSpec(a.shape, lambda b, i: (0,) * a.ndim)
    return pl.pallas_call(
        _inproj_kernel,
        grid=(B, S // tm),
        in_specs=[
            pl.BlockSpec((1, tm, D), lambda b, i: (b, i, 0)),
            full(g), full(win), full(qg), full(wqa), full(wqb), full(kvg), full(wuk), full(pk),
            full(wuvt),
            pl.BlockSpec((1, HEAD_PAD, tm), lambda b, i: (b, 0, i)),
            pl.BlockSpec((1, HEAD_PAD, tm), lambda b, i: (b, 0, i)),
            pl.BlockSpec((1, tm, HEAD_PAD), lambda b, i: (b, i, 0)),
            full(convw),
        ],
        out_specs=[
            pl.BlockSpec((1, N_HEADS * HEAD_PAD, tm), lambda b, i: (b, 0, i)),
            pl.BlockSpec((1, tm, N_HEADS * HEAD_PAD), lambda b, i: (b, i, 0)),
            pl.BlockSpec((1, MLA_WIDTH, tm), lambda b, i: (b, 0, i)),
            pl.BlockSpec((1, tm, CONV_WIDTH), lambda b, i: (b, i, 0)),
        ],
        out_shape=[
            jax.ShapeDtypeStruct((B, N_HEADS * HEAD_PAD, S), BF16),
            jax.ShapeDtypeStruct((B, S, N_HEADS * HEAD_PAD), BF16),
            jax.ShapeDtypeStruct((B, MLA_WIDTH, S), BF16),
            jax.ShapeDtypeStruct((B, S, CONV_WIDTH), BF16),
        ],
        scratch_shapes=[pltpu.VMEM((8, CONV_WIDTH), F32), pltpu.VMEM((tm + 8, CONV_WIDTH), F32)],
        compiler_params=pltpu.CompilerParams(
            dimension_semantics=("arbitrary", "arbitrary"), vmem_limit_bytes=VMEM_LIMIT),
        name="inproj",
    )(x, g, win, qg, wqa, wqb, kvg, wuk, pk, wuvt, ct, st, ck, convw)


def _attn_kernel(qt_ref, k_ref, vt_ref, o_ref, m_ref, l_ref, acc_ref, sa_ref, sb_ref):
    tq = qt_ref.shape[2]
    tk = TK
    assert tq == 2 * tk
    i = pl.program_id(2)
    m_ref[...] = jnp.full_like(m_ref, NEG)
    l_ref[...] = jnp.zeros_like(l_ref)
    acc_ref[...] = jnp.zeros_like(acc_ref)

    def scores(j, s_ref):
        k0 = pl.multiple_of(j * tk, tk)
        for g in range(ATT_HG):
            qt = qt_ref[0, g * HEAD_PAD:(g + 1) * HEAD_PAD, :]
            s_ref[g] = _dot(k_ref[0, pl.ds(k0, tk), g * HEAD_PAD:(g + 1) * HEAD_PAD], qt)

    def softmax_pv(j, s_ref, masked):
        k0 = pl.multiple_of(j * tk, tk)
        if masked:
            kidx = k0 + lax.broadcasted_iota(jnp.int32, (tk, tq), 0)
            qidx = i * tq + lax.broadcasted_iota(jnp.int32, (tk, tq), 1)
            visible = kidx <= qidx
        for g in range(ATT_HG):
            s = s_ref[g]
            if masked:
                s = jnp.where(visible, s, NEG)
            m_old = m_ref[g]
            m_new = jnp.maximum(m_old, jnp.max(s, axis=0, keepdims=True))
            alpha = jnp.exp2(m_old - m_new)
            p = jnp.exp2(s - m_new)
            l_ref[g] = alpha * l_ref[g] + jnp.sum(p, axis=0, keepdims=True)
            vt = vt_ref[0, g * V_HEAD_DIM:(g + 1) * V_HEAD_DIM, pl.ds(k0, tk)]
            acc_ref[g] = alpha * acc_ref[g] + _dot(vt, p.astype(BF16))
            m_ref[g] = m_new

    scores(0, sa_ref)

    def body(t, c):
        j = 2 * t
        scores(j + 1, sb_ref)
        softmax_pv(j, sa_ref, False)
        scores(j + 2, sa_ref)
        softmax_pv(j + 1, sb_ref, False)
        return c

    lax.fori_loop(0, i, body, 0)
    n_full = 2 * i
    scores(n_full + 1, sb_ref)
    softmax_pv(n_full, sa_ref, True)
    softmax_pv(n_full + 1, sb_ref, True)
    for g in range(ATT_HG):
        o_ref[0, g * V_HEAD_DIM:(g + 1) * V_HEAD_DIM, :] = (
            acc_ref[g] / l_ref[g]).astype(o_ref.dtype)


def _attention(qt, k, vt):
    B, _, S = qt.shape
    hg = ATT_HG
    return pl.pallas_call(
        _attn_kernel,
        grid=(B, N_HEADS // hg, S // TQ),
        in_specs=[
            pl.BlockSpec((1, hg * HEAD_PAD, TQ), lambda b, h, i: (b, h, i)),
            pl.BlockSpec((1, S, hg * HEAD_PAD), lambda b, h, i: (b, 0, h)),
            pl.BlockSpec((1, hg * V_HEAD_DIM, S), lambda b, h, i: (b, h, 0)),
        ],
        out_specs=pl.BlockSpec((1, hg * V_HEAD_DIM, TQ), lambda b, h, i: (b, h, i)),
        out_shape=jax.ShapeDtypeStruct((B, MLA_WIDTH, S), BF16),
        scratch_shapes=[pltpu.VMEM((hg, 1, TQ), F32), pltpu.VMEM((hg, 1, TQ), F32),
                        pltpu.VMEM((hg, V_HEAD_DIM, TQ), F32),
                        pltpu.VMEM((hg, TK, TQ), F32), pltpu.VMEM((hg, TK, TQ), F32)],
        compiler_params=pltpu.CompilerParams(
            dimension_semantics=("arbitrary", "arbitrary", "arbitrary"),
            vmem_limit_bytes=VMEM_LIMIT),
        name="attn",
    )(qt, k, vt)


def _outproj_kernel(x_ref, ot_ref, oc_ref, wo_ref, g_ref, wrh_ref, wrl_ref, br_ref,
                    h1_ref, xn_ref, eid_ref, wt_ref, rank_ref, cnt_ref, carry_ref):
    tm = x_ref.shape[1]

    @pl.when((pl.program_id(0) == 0) & (pl.program_id(1) == 0))
    def _():
        carry_ref[...] = jnp.zeros_like(carry_ref)

    attn = (_dot_tn(ot_ref[0], wo_ref[0:MLA_WIDTH, :])
            + _dot(oc_ref[0], wo_ref[MLA_WIDTH:MLA_WIDTH + CONV_WIDTH, :]))
    h1 = x_ref[0] + attn
    h1_ref[0] = h1
    xn = _rms(h1, g_ref[...])
    xn_ref[0] = xn

    xh = xn.astype(BF16)
    xl = (xn - xh.astype(F32)).astype(BF16)
    logits = (_dot(xh, wrh_ref[...]) + _dot(xl, wrh_ref[...]) + _dot(xh, wrl_ref[...])
              + br_ref[...])
    lane = lax.broadcasted_iota(jnp.int32, logits.shape, 1)
    big = jnp.int32(1 << 20)

    glog = jnp.where(lane < N_GROUPS, logits, NEG)
    gmax = jnp.max(glog, axis=-1, keepdims=True)
    gsum = jnp.sum(jnp.exp(glog - gmax), axis=-1, keepdims=True)
    g_p = 1.0 / gsum
    g_idx = jnp.min(jnp.where(glog == gmax, lane, big), axis=-1, keepdims=True)

    e_lo = N_GROUPS + g_idx * EXPERTS_PER_GROUP
    elog = jnp.where((lane >= e_lo) & (lane < e_lo + EXPERTS_PER_GROUP), logits, NEG)
    emax = jnp.max(elog, axis=-1, keepdims=True)
    esum = jnp.sum(jnp.exp(elog - emax), axis=-1, keepdims=True)
    e1 = jnp.min(jnp.where(elog == emax, lane, big), axis=-1, keepdims=True)
    elog2 = jnp.where(lane == e1, NEG, elog)
    emax2 = jnp.max(elog2, axis=-1, keepdims=True)
    e2 = jnp.min(jnp.where(elog2 == emax2, lane, big), axis=-1, keepdims=True)
    p1 = 1.0 / esum
    p2 = jnp.exp(emax2 - emax) / esum
    psum = p1 + p2
    w1 = g_p * (p1 / psum)
    w2 = g_p * (p2 / psum)
    e1 = e1 - N_GROUPS
    e2 = e2 - N_GROUPS

    oh1 = lane == e1
    oh2 = lane == e2
    oh = jnp.where(oh1 | oh2, 1.0, 0.0)
    row = lax.broadcasted_iota(jnp.int32, (tm, tm), 0)
    col = lax.broadcasted_iota(jnp.int32, (tm, tm), 1)
    lower = jnp.where(row > col, 1.0, 0.0).astype(BF16)
    cum = _dot(lower, oh.astype(BF16)) + carry_ref[...]
    r1 = jnp.sum(jnp.where(oh1, cum, 0.0), axis=-1, keepdims=True)
    r2 = jnp.sum(jnp.where(oh2, cum, 0.0), axis=-1, keepdims=True)
    carry_ref[...] = carry_ref[...] + jnp.sum(oh, axis=0, keepdims=True)
    cnt_ref[...] = carry_ref[...]

    eid_ref[0] = jnp.where(lane == 0, e1, jnp.where(lane == 1, e2, 0))
    rank_ref[0] = jnp.where(lane == 0, r1, jnp.where(lane == 1, r2, 0.0)).astype(jnp.int32)
    wt_ref[0] = jnp.where(lane == 0, w1, jnp.where(lane == 1, w2, 0.0))


def _outproj(x, ot, oc, wo, g, wrh, wrl, br):
    B, S, D = x.shape
    tm = TM
    full = lambda a: pl.BlockSpec(a.shape, lambda b, i: (0,) * a.ndim)
    tile = lambda w: pl.BlockSpec((1, tm, w), lambda b, i: (b, i, 0))
    return pl.pallas_call(
        _outproj_kernel,
        grid=(B, S // tm),
        in_specs=[
            tile(D),
            pl.BlockSpec((1, MLA_WIDTH, tm), lambda b, i: (b, 0, i)),
            tile(CONV_WIDTH),
            full(wo), full(g), full(wrh), full(wrl), full(br),
        ],
        out_specs=[tile(D), tile(D), tile(HEAD_PAD), tile(HEAD_PAD), tile(HEAD_PAD),
                   pl.BlockSpec((1, HEAD_PAD), lambda b, i: (0, 0))],
        out_shape=[
            jax.ShapeDtypeStruct((B, S, D), F32),
            jax.ShapeDtypeStruct((B, S, D), F32),
            jax.ShapeDtypeStruct((B, S, HEAD_PAD), jnp.int32),
            jax.ShapeDtypeStruct((B, S, HEAD_PAD), F32),
            jax.ShapeDtypeStruct((B, S, HEAD_PAD), jnp.int32),
            jax.ShapeDtypeStruct((1, HEAD_PAD), F32),
        ],
        scratch_shapes=[pltpu.VMEM((1, HEAD_PAD), F32)],
        compiler_params=pltpu.CompilerParams(
            dimension_semantics=("arbitrary", "arbitrary"), vmem_limit_bytes=VMEM_LIMIT),
        name="outproj",
    )(x, ot, oc, wo, g, wrh, wrl, br)


def _dispatch_kernel(dest_ref, xn_ref, xb_in_ref, xb_ref, sem):
    del xb_in_ref
    tm = xn_ref.shape[0]
    base = pl.program_id(0) * tm

    def row_copy(r, d):
        return pltpu.make_async_copy(xn_ref.at[pl.ds(r, 1)], xb_ref.at[pl.ds(d, 1)], sem)

    def issue(r, c):
        t = (base + r) * 2
        row_copy(r, dest_ref[t]).start()
        row_copy(r, dest_ref[t + 1]).start()
        return c

    lax.fori_loop(0, tm, issue, 0, unroll=8)
    for _ in range(2):
        pltpu.make_async_copy(xn_ref, xb_ref.at[pl.ds(0, tm)], sem).wait()


def _dispatch(dest, xn, xb_init):
    N, D = xn.shape
    tm = TM
    return pl.pallas_call(
        _dispatch_kernel,
        grid_spec=pltpu.PrefetchScalarGridSpec(
            num_scalar_prefetch=1,
            grid=(N // tm,),
            in_specs=[pl.BlockSpec((tm, D), lambda i, dest: (i, 0)),
                      pl.BlockSpec(memory_space=pl.ANY)],
            out_specs=pl.BlockSpec(memory_space=pl.ANY),
            scratch_shapes=[pltpu.SemaphoreType.DMA],
        ),
        out_shape=jax.ShapeDtypeStruct(xb_init.shape, xb_init.dtype),
        input_output_aliases={2: 0},
        compiler_params=pltpu.CompilerParams(
            dimension_semantics=("arbitrary",), vmem_limit_bytes=VMEM_LIMIT),
        name="dispatch",
    )(dest, xn, xb_init)


def _expert_kernel(blke_ref, nused_ref, xb_ref, wg_ref, wu_ref, wd_ref, yb_ref):
    del blke_ref
    i = pl.program_id(0)

    @pl.when(i < nused_ref[0])
    def _():
        xb = xb_ref[...].astype(BF16)
        g = _dot(xb, wg_ref[0])
        u = _dot(xb, wu_ref[0])
        hdn = (g * jax.nn.sigmoid(g)) * u
        yb_ref[...] = _dot(hdn.astype(BF16), wd_ref[0])

    @pl.when(i >= nused_ref[0])
    def _():
        yb_ref[...] = jnp.zeros_like(yb_ref)


def _experts(blk_e, n_used, xb, wg, wu, wd):
    P, D = xb.shape
    blk = MOE_BLK
    return pl.pallas_call(
        _expert_kernel,
        grid_spec=pltpu.PrefetchScalarGridSpec(
            num_scalar_prefetch=2,
            grid=(P // blk,),
            in_specs=[
                pl.BlockSpec((blk, D), lambda i, be, nu: (i, 0)),
                pl.BlockSpec((1, D, EXPERT_FF), lambda i, be, nu: (be[i], 0, 0)),
                pl.BlockSpec((1, D, EXPERT_FF), lambda i, be, nu: (be[i], 0, 0)),
                pl.BlockSpec((1, EXPERT_FF, D), lambda i, be, nu: (be[i], 0, 0)),
            ],
            out_specs=pl.BlockSpec((blk, D), lambda i, be, nu: (i, 0)),
        ),
        out_shape=jax.ShapeDtypeStruct((P, D), F32),
        compiler_params=pltpu.CompilerParams(
            dimension_semantics=("arbitrary",), vmem_limit_bytes=VMEM_LIMIT),
        name="experts",
    )(blk_e, n_used, xb, wg, wu, wd)


def _final_kernel(dest_ref, h1_ref, p_ref, wt_ref, yb_ref, gp_ref, wg_ref, bg_ref, wp_ref,
                  gf_ref, out_ref, buf_ref, sem):
    tm = h1_ref.shape[0]
    base = pl.program_id(0) * tm

    def row_copy(k, r, d):
        return pltpu.make_async_copy(yb_ref.at[pl.ds(d, 1)], buf_ref.at[k, pl.ds(r, 1)], sem)

    def issue(r, c):
        t = (base + r) * 2
        row_copy(0, r, dest_ref[t]).start()
        row_copy(1, r, dest_ref[t + 1]).start()
        return c

    lax.fori_loop(0, tm, issue, 0, unroll=8)
    pe = _dot(p_ref[...].astype(BF16), wp_ref[...])
    for k in range(2):
        pltpu.make_async_copy(yb_ref.at[pl.ds(0, tm)], buf_ref.at[k], sem).wait()

    wt = wt_ref[...]
    h2 = h1_ref[...] + wt[:, 0:1] * buf_ref[0] + wt[:, 1:2] * buf_ref[1]
    n = _rms(h2, gp_ref[...]).astype(BF16)
    gate = jax.nn.sigmoid(_dot(n, wg_ref[...]) + bg_ref[...])
    h3 = h2 + gate * pe
    out_ref[...] = _rms(h3, gf_ref[...])


def _final(dest, h1, p, wt, yb, gp, wg, bg, wp, gf):
    N, D = h1.shape
    tm = TM
    full = lambda a: pl.BlockSpec(a.shape, lambda i, dest: (0,) * a.ndim)
    return pl.pallas_call(
        _final_kernel,
        grid_spec=pltpu.PrefetchScalarGridSpec(
            num_scalar_prefetch=1,
            grid=(N // tm,),
            in_specs=[
                pl.BlockSpec((tm, D), lambda i, dest: (i, 0)),
                pl.BlockSpec((tm, PLE_DIM), lambda i, dest: (i, 0)),
                pl.BlockSpec((tm, HEAD_PAD), lambda i, dest: (i, 0)),
                pl.BlockSpec(memory_space=pl.ANY),
                full(gp), full(wg), full(bg), full(wp), full(gf),
            ],
            out_specs=pl.BlockSpec((tm, D), lambda i, dest: (i, 0)),
            scratch_shapes=[pltpu.VMEM((2, tm, D), F32), pltpu.SemaphoreType.DMA],
        ),
        out_shape=jax.ShapeDtypeStruct((N, D), F32),
        compiler_params=pltpu.CompilerParams(
            dimension_semantics=("arbitrary",), vmem_limit_bytes=VMEM_LIMIT),
        name="final",
    )(dest, h1, p, wt, yb, gp, wg, bg, wp, gf)


def _prep_weights(w_in, w_uq, w_ukv):
    cq, ckv, kr, bg, cg, u = jnp.split(
        w_in, [Q_LORA, Q_LORA + KV_LORA, Q_LORA + KV_LORA + QK_ROPE_DIM,
               Q_LORA + KV_LORA + QK_ROPE_DIM + CONV_WIDTH,
               Q_LORA + KV_LORA + QK_ROPE_DIM + 2 * CONV_WIDTH], axis=1)
    kr_rot = jnp.concatenate([-kr[:, HALF_ROPE:], kr[:, :HALF_ROPE]], axis=1)
    pad = jnp.zeros((D_MODEL, HEAD_PAD - 2 * QK_ROPE_DIM), w_in.dtype)
    win = jnp.concatenate([cq, ckv, bg, cg, u, kr, kr_rot, pad], axis=1).astype(BF16)

    zq = jnp.zeros((Q_LORA, N_HEADS, HEAD_PAD - QK_DIM), w_uq.dtype)
    wqa = jnp.concatenate([w_uq, zq], axis=2)
    rope = w_uq[:, :, QK_NOPE_DIM:]
    rot = jnp.concatenate([-rope[:, :, HALF_ROPE:], rope[:, :, :HALF_ROPE]], axis=2)
    wqb = jnp.concatenate([jnp.zeros((Q_LORA, N_HEADS, QK_NOPE_DIM), w_uq.dtype), rot, zq], axis=2)
    wqa = wqa.reshape(Q_LORA, N_HEADS * HEAD_PAD).T.astype(BF16)
    wqb = wqb.reshape(Q_LORA, N_HEADS * HEAD_PAD).T.astype(BF16)

    zk = jnp.zeros((KV_LORA, N_HEADS, HEAD_PAD - QK_NOPE_DIM), w_ukv.dtype)
    wuk = jnp.concatenate([w_ukv[:, :, :QK_NOPE_DIM], zk], axis=2)
    wuk = wuk.reshape(KV_LORA, N_HEADS * HEAD_PAD).astype(BF16)
    wuvt = w_ukv[:, :, QK_NOPE_DIM:].reshape(KV_LORA, MLA_WIDTH).T.astype(BF16)

    src = jnp.arange(HEAD_PAD)[:, None]
    dst = jnp.arange(N_HEADS * HEAD_PAD)[None, :]
    dlane = dst % HEAD_PAD - QK_NOPE_DIM
    pk = ((src < 2 * QK_ROPE_DIM) & (dlane >= 0) & (dlane < QK_ROPE_DIM)
          & (src % QK_ROPE_DIM == dlane)).astype(BF16)
    return win, wqa, wqb, wuk, wuvt, pk


def _rope_tables(positions):
    inv_freq = ROPE_BASE ** (-jnp.arange(0, QK_ROPE_DIM, 2, dtype=F32) / QK_ROPE_DIM)
    ang = positions.astype(F32)[..., None] * inv_freq
    cos, sin = jnp.cos(ang), jnp.sin(ang)
    B, S, _ = cos.shape
    qscale = (QK_DIM ** -0.5) * math.log2(math.e)
    cos_t = jnp.swapaxes(cos, 1, 2) * qscale
    sin_t = jnp.swapaxes(sin, 1, 2) * qscale
    ones = jnp.full((B, QK_NOPE_DIM, S), qscale, F32)
    zpad = jnp.zeros((B, HEAD_PAD - QK_DIM, S), F32)
    ct = jnp.concatenate([ones, cos_t, cos_t, zpad], axis=1)
    st = jnp.concatenate([jnp.zeros((B, QK_NOPE_DIM, S), F32), sin_t, sin_t, zpad], axis=1)
    ck = jnp.concatenate([cos, cos, sin, sin,
                          jnp.zeros((B, S, HEAD_PAD - 2 * QK_ROPE_DIM), F32)], axis=2)
    return ct, st, ck


def kernel(x, p, positions, attn_norm_g, w_in, q_norm_g, w_uq, kv_norm_g, w_ukv, conv_w, w_out,
           moe_norm_g, w_group_router, b_group_router, w_expert_router, b_expert_router,
           w_gate, w_up, w_down, ple_norm_g, w_ple_gate, b_ple_gate, w_ple_proj, final_norm_g):
    B, S, D = x.shape
    N = B * S
    assert w_in.shape[0] == 1, "single-layer trunk: the final norm is fused into the layer"
    ct, st, ck = _rope_tables(positions)
    row = lambda v: v.reshape(1, -1)
    h = x
    for i in range(1):
        win, wqa, wqb, wuk, wuvt, pk = _prep_weights(w_in[i], w_uq[i], w_ukv[i])
        qt, k, vt, oc = _inproj(h, row(attn_norm_g[i]), win, row(q_norm_g[i]), wqa, wqb,
                                row(kv_norm_g[i]), wuk, pk, wuvt, ct, st, ck,
                                conv_w[i].reshape(CONV_K, CONV_WIDTH))
        ot = _attention(qt, k, vt)

        wr = jnp.concatenate(
            [w_group_router[i], w_expert_router[i],
             jnp.zeros((D, HEAD_PAD - N_GROUPS - N_EXPERTS), F32)], axis=1)
        wrh = wr.astype(BF16)
        wrl = (wr - wrh.astype(F32)).astype(BF16)
        br = jnp.concatenate([b_group_router[i], b_expert_router[i],
                              jnp.zeros((HEAD_PAD - N_GROUPS - N_EXPERTS,), F32)]).reshape(1, -1)
        h1, xn, eid, wt, rank, cnt = _outproj(h, ot, oc, w_out[i].astype(BF16),
                                              row(moe_norm_g[i]), wrh, wrl, br)

        n_blocks = (2 * N) // MOE_BLK + N_EXPERTS
        counts = cnt[0, :N_EXPERTS].astype(jnp.int32)
        padded = (counts + MOE_BLK - 1) // MOE_BLK * MOE_BLK
        pad_ends = jnp.cumsum(padded)
        pad_starts = pad_ends - padded
        sel = eid[..., :2, None] == jnp.arange(N_EXPERTS, dtype=jnp.int32)
        dest = (jnp.sum(jnp.where(sel, pad_starts, 0), axis=-1) + rank[..., :2]).reshape(2 * N)
        blk_start = jnp.arange(n_blocks, dtype=jnp.int32) * MOE_BLK
        blk_e = jnp.minimum(jnp.sum((pad_ends[None, :] <= blk_start[:, None]).astype(jnp.int32),
                                    axis=1), N_EXPERTS - 1)
        n_used = (pad_ends[-1:] // MOE_BLK).astype(jnp.int32)

        xb = _dispatch(dest, xn.reshape(N, D), jnp.zeros((n_blocks * MOE_BLK, D), F32))
        yb = _experts(blk_e, n_used, xb, w_gate[i].astype(BF16), w_up[i].astype(BF16),
                      w_down[i].astype(BF16))
        out = _final(dest, h1.reshape(N, D), p[i].reshape(N, PLE_DIM), wt.reshape(N, HEAD_PAD), yb,
                     row(ple_norm_g[i]), w_ple_gate[i].astype(BF16), row(b_ple_gate[i]),
                     w_ple_proj[i].astype(BF16), row(final_norm_g))
        h = out.reshape(B, S, D)
    return h
```

```python
import functools
import math

import jax
import jax.numpy as jnp
from jax import lax
from jax.experimental import pallas as pl
from jax.experimental.pallas import tpu as pltpu

D_MODEL = 1024
PLE_DIM = 256
MLA_WIDTH = 512
CONV_WIDTH = 512
N_HEADS = 8
V_HEAD_DIM = 64
QK_NOPE_DIM = 64
QK_ROPE_DIM = 32
Q_LORA = 384
KV_LORA = 256
CONV_K = 3
N_GROUPS = 4
EXPERTS_PER_GROUP = 8
N_EXPERTS = N_GROUPS * EXPERTS_PER_GROUP
EXPERT_FF = 512
ROPE_BASE = 10000.0
EPS = 1e-6

HEAD_PAD = 128
QK_DIM = QK_NOPE_DIM + QK_ROPE_DIM
HALF_ROPE = QK_ROPE_DIM // 2

_C_Q = 0
_C_KV = _C_Q + Q_LORA
_C_BG = _C_KV + KV_LORA
_C_CG = _C_BG + CONV_WIDTH
_C_U = _C_CG + CONV_WIDTH
_C_KR = _C_U + CONV_WIDTH
IN_COLS_PAD = _C_KR + HEAD_PAD

TM = 256
TQ = 512
TK = 256
ATT_HG = 4
MOE_BLK = 256
NEG = -1e30
Q_SCALE = (QK_DIM ** -0.5) * math.log2(math.e)
VMEM_LIMIT = 48 * 1024 * 1024

F32 = jnp.float32
BF16 = jnp.bfloat16


def _rms(x, g):
    return x * lax.rsqrt(jnp.mean(x * x, axis=-1, keepdims=True) + EPS) * g


def _dot(a, b):
    return jnp.dot(a, b, preferred_element_type=F32)


def _dot_nt(a, b):
    return lax.dot_general(a, b, (((1,), (1,)), ((), ())), preferred_element_type=F32)


def _dot_tn(a, b):
    return lax.dot_general(a, b, (((0,), (0,)), ((), ())), preferred_element_type=F32)


def _inproj_kernel(x_ref, pos_ref, invf_ref, g_ref, win_ref, qg_ref, wqa_ref, wqb_ref, kvg_ref,
                   wuk_ref, pk_ref, wuvt_ref, convw_ref,
                   qt_ref, k_ref, vt_ref, oc_ref, carry_ref, ext_ref):
    tm = x_ref.shape[1]
    x = x_ref[0]
    xn = _rms(x, g_ref[...])
    z = _dot(xn.astype(BF16), win_ref[...])

    ang = invf_ref[...] * pos_ref[0]
    cos = jnp.cos(ang)
    sin = jnp.sin(ang)
    cos2 = jnp.concatenate([cos, cos], axis=0)
    sin2 = jnp.concatenate([sin, sin], axis=0)

    cqn = _rms(z[:, _C_Q:_C_Q + Q_LORA], qg_ref[...]).astype(BF16)
    qa = _dot_nt(wqa_ref[...], cqn)
    qb = _dot_nt(wqb_ref[...], cqn)
    for h in range(N_HEADS):
        r0 = h * HEAD_PAD
        nope = qa[r0:r0 + QK_NOPE_DIM]
        rope = (qa[r0 + QK_NOPE_DIM:r0 + QK_DIM] * cos2
                + qb[h * QK_ROPE_DIM:(h + 1) * QK_ROPE_DIM] * sin2)
        qh = jnp.concatenate([nope, rope, qa[r0 + QK_DIM:r0 + HEAD_PAD]], axis=0) * Q_SCALE
        qt_ref[0, r0:r0 + HEAD_PAD, :] = qh.astype(BF16)

    kvn = _rms(z[:, _C_KV:_C_KV + KV_LORA], kvg_ref[...]).astype(BF16)
    krt = z[:, _C_KR:_C_KR + HEAD_PAD].T
    krot = krt[0:QK_ROPE_DIM] * cos2 + krt[QK_ROPE_DIM:2 * QK_ROPE_DIM] * sin2
    k_ref[0] = (_dot(kvn, wuk_ref[...]) + _dot_tn(krot.astype(BF16), pk_ref[...])).astype(BF16)
    vt_ref[0] = _dot_nt(wuvt_ref[...], kvn).astype(BF16)

    @pl.when(pl.program_id(1) == 0)
    def _():
        carry_ref[...] = jnp.zeros_like(carry_ref)

    cu = z[:, _C_CG:_C_CG + CONV_WIDTH] * z[:, _C_U:_C_U + CONV_WIDTH]
    ext_ref[0:8, :] = carry_ref[...]
    ext_ref[8:8 + tm, :] = cu
    cu1 = ext_ref[7:7 + tm, :]
    cu2 = ext_ref[6:6 + tm, :]
    w = convw_ref[...]
    y = w[0:1] * cu2 + w[1:2] * cu1 + w[2:3] * cu
    oc_ref[0] = (z[:, _C_BG:_C_BG + CONV_WIDTH] * y).astype(BF16)
    carry_ref[...] = ext_ref[tm:tm + 8, :]


def _inproj(x, pos, invf, g, win, qg, wqa, wqb, kvg, wuk, pk, wuvt, convw):
    B, S, D = x.shape
    tm = TM
    full = lambda a: pl.BlockSpec(a.shape, lambda b, i: (0,) * a.ndim)
    return pl.pallas_call(
        _inproj_kernel,
        grid=(B, S // tm),
        in_specs=[
            pl.BlockSpec((1, tm, D), lambda b, i: (b, i, 0)),
            pl.BlockSpec((1, 1, tm), lambda b, i: (b, 0, i)),
            full(invf), full(g), full(win), full(qg), full(wqa), full(wqb), full(kvg), full(wuk),
            full(pk), full(wuvt), full(convw),
        ],
        out_specs=[
            pl.BlockSpec((1, N_HEADS * HEAD_PAD, tm), lambda b, i: (b, 0, i)),
            pl.BlockSpec((1, tm, N_HEADS * HEAD_PAD), lambda b, i: (b, i, 0)),
            pl.BlockSpec((1, MLA_WIDTH, tm), lambda b, i: (b, 0, i)),
            pl.BlockSpec((1, tm, CONV_WIDTH), lambda b, i: (b, i, 0)),
        ],
        out_shape=[
            jax.ShapeDtypeStruct((B, N_HEADS * HEAD_PAD, S), BF16),
            jax.ShapeDtypeStruct((B, S, N_HEADS * HEAD_PAD), BF16),
            jax.ShapeDtypeStruct((B, MLA_WIDTH, S), BF16),
            jax.ShapeDtypeStruct((B, S, CONV_WIDTH), BF16),
        ],
        scratch_shapes=[pltpu.VMEM((8, CONV_WIDTH), F32), pltpu.VMEM((tm + 8, CONV_WIDTH), F32)],
        compiler_params=pltpu.CompilerParams(
            dimension_semantics=("arbitrary", "arbitrary"), vmem_limit_bytes=VMEM_LIMIT),
        name="inproj",
    )(x, pos, invf, g, win, qg, wqa, wqb, kvg, wuk, pk, wuvt, convw)


def _attn_kernel(qt_ref, k_ref, vt_ref, o_ref, m_ref, l_ref, acc_ref, sa_ref, sb_ref):
    tq = qt_ref.shape[2]
    tk = TK
    assert tq == 2 * tk
    i = pl.program_id(2)
    m_ref[...] = jnp.full_like(m_ref, NEG)
    l_ref[...] = jnp.zeros_like(l_ref)
    acc_ref[...] = jnp.zeros_like(acc_ref)

    def scores(j, s_ref):
        k0 = pl.multiple_of(j * tk, tk)
        for g in range(ATT_HG):
            qt = qt_ref[0, g * HEAD_PAD:(g + 1) * HEAD_PAD, :]
            s_ref[g] = _dot(k_ref[0, pl.ds(k0, tk), g * HEAD_PAD:(g + 1) * HEAD_PAD], qt)

    def softmax_pv(j, s_ref, masked):
        k0 = pl.multiple_of(j * tk, tk)
        if masked:
            kidx = k0 + lax.broadcasted_iota(jnp.int32, (tk, tq), 0)
            qidx = i * tq + lax.broadcasted_iota(jnp.int32, (tk, tq), 1)
            visible = kidx <= qidx
        for g in range(ATT_HG):
            s = s_ref[g]
            if masked:
                s = jnp.where(visible, s, NEG)
            m_old = m_ref[g]
            m_new = jnp.maximum(m_old, jnp.max(s, axis=0, keepdims=True))
            alpha = jnp.exp2(m_old - m_new)
            p = jnp.exp2(s - m_new)
            l_ref[g] = alpha * l_ref[g] + jnp.sum(p, axis=0, keepdims=True)
            vt = vt_ref[0, g * V_HEAD_DIM:(g + 1) * V_HEAD_DIM, pl.ds(k0, tk)]
            acc_ref[g] = alpha * acc_ref[g] + _dot(vt, p.astype(BF16))
            m_ref[g] = m_new

    scores(0, sa_ref)

    def body(t, c):
        j = 2 * t
        scores(j + 1, sb_ref)
        softmax_pv(j, sa_ref, False)
        scores(j + 2, sa_ref)
        softmax_pv(j + 1, sb_ref, False)
        return c

    lax.fori_loop(0, i, body, 0)
    n_full = 2 * i
    scores(n_full + 1, sb_ref)
    softmax_pv(n_full, sa_ref, True)
    softmax_pv(n_full + 1, sb_ref, True)
    for g in range(ATT_HG):
        o_ref[0, g * V_HEAD_DIM:(g + 1) * V_HEAD_DIM, :] = (
            acc_ref[g] / l_ref[g]).astype(o_ref.dtype)


def _attention(qt, k, vt):
    B, _, S = qt.shape
    hg = ATT_HG
    return pl.pallas_call(
        _attn_kernel,
        grid=(B, N_HEADS // hg, S // TQ),
        in_specs=[
            pl.BlockSpec((1, hg * HEAD_PAD, TQ), lambda b, h, i: (b, h, i)),
            pl.BlockSpec((1, S, hg * HEAD_PAD), lambda b, h, i: (b, 0, h)),
            pl.BlockSpec((1, hg * V_HEAD_DIM, S), lambda b, h, i: (b, h, 0)),
        ],
        out_specs=pl.BlockSpec((1, hg * V_HEAD_DIM, TQ), lambda b, h, i: (b, h, i)),
        out_shape=jax.ShapeDtypeStruct((B, MLA_WIDTH, S), BF16),
        scratch_shapes=[pltpu.VMEM((hg, 1, TQ), F32), pltpu.VMEM((hg, 1, TQ), F32),
                        pltpu.VMEM((hg, V_HEAD_DIM, TQ), F32),
                        pltpu.VMEM((hg, TK, TQ), F32), pltpu.VMEM((hg, TK, TQ), F32)],
        compiler_params=pltpu.CompilerParams(
            dimension_semantics=("arbitrary", "arbitrary", "arbitrary"),
            vmem_limit_bytes=VMEM_LIMIT),
        name="attn",
    )(qt, k, vt)


def _outproj_kernel(x_ref, ot_ref, oc_ref, wo_ref, g_ref, wrh_ref, wrl_ref, br_ref,
                    h1_ref, xn_ref, eid_ref, wt_ref, rank_ref, cnt_ref, carry_ref):
    tm = x_ref.shape[1]

    @pl.when((pl.program_id(0) == 0) & (pl.program_id(1) == 0))
    def _():
        carry_ref[...] = jnp.zeros_like(carry_ref)

    attn = (_dot_tn(ot_ref[0], wo_ref[0:MLA_WIDTH, :])
            + _dot(oc_ref[0], wo_ref[MLA_WIDTH:MLA_WIDTH + CONV_WIDTH, :]))
    h1 = x_ref[0] + attn
    h1_ref[0] = h1
    xn = _rms(h1, g_ref[...])
    xn_ref[0] = xn

    xh = xn.astype(BF16)
    xl = (xn - xh.astype(F32)).astype(BF16)
    logits = (_dot(xh, wrh_ref[...]) + _dot(xl, wrh_ref[...]) + _dot(xh, wrl_ref[...])
              + br_ref[...])
    lane = lax.broadcasted_iota(jnp.int32, logits.shape, 1)
    big = jnp.int32(1 << 20)

    glog = jnp.where(lane < N_GROUPS, logits, NEG)
    gmax = jnp.max(glog, axis=-1, keepdims=True)
    gsum = jnp.sum(jnp.exp(glog - gmax), axis=-1, keepdims=True)
    g_p = 1.0 / gsum
    g_idx = jnp.min(jnp.where(glog == gmax, lane, big), axis=-1, keepdims=True)

    e_lo = N_GROUPS + g_idx * EXPERTS_PER_GROUP
    elog = jnp.where((lane >= e_lo) & (lane < e_lo + EXPERTS_PER_GROUP), logits, NEG)
    emax = jnp.max(elog, axis=-1, keepdims=True)
    esum = jnp.sum(jnp.exp(elog - emax), axis=-1, keepdims=True)
    e1 = jnp.min(jnp.where(elog == emax, lane, big), axis=-1, keepdims=True)
    elog2 = jnp.where(lane == e1, NEG, elog)
    emax2 = jnp.max(elog2, axis=-1, keepdims=True)
    e2 = jnp.min(jnp.where(elog2 == emax2, lane, big), axis=-1, keepdims=True)
    p1 = 1.0 / esum
    p2 = jnp.exp(emax2 - emax) / esum
    psum = p1 + p2
    w1 = g_p * (p1 / psum)
    w2 = g_p * (p2 / psum)
    e1 = e1 - N_GROUPS
    e2 = e2 - N_GROUPS

    oh1 = lane == e1
    oh2 = lane == e2
    oh = jnp.where(oh1 | oh2, 1.0, 0.0)
    row = lax.broadcasted_iota(jnp.int32, (tm, tm), 0)
    col = lax.broadcasted_iota(jnp.int32, (tm, tm), 1)
    lower = jnp.where(row > col, 1.0, 0.0).astype(BF16)
    cum = _dot(lower, oh.astype(BF16)) + carry_ref[...]
    r1 = jnp.sum(jnp.where(oh1, cum, 0.0), axis=-1, keepdims=True)
    r2 = jnp.sum(jnp.where(oh2, cum, 0.0), axis=-1, keepdims=True)
    carry_ref[...] = carry_ref[...] + jnp.sum(oh, axis=0, keepdims=True)
    cnt_ref[...] = carry_ref[...]

    eid_ref[0] = jnp.where(lane == 0, e1, jnp.where(lane == 1, e2, 0))
    rank_ref[0] = jnp.where(lane == 0, r1, jnp.where(lane == 1, r2, 0.0)).astype(jnp.int32)
    wt_ref[0] = jnp.where(lane == 0, w1, jnp.where(lane == 1, w2, 0.0))


def _outproj(x, ot, oc, wo, g, wrh, wrl, br):
    B, S, D = x.shape
    tm = TM
    full = lambda a: pl.BlockSpec(a.shape, lambda b, i: (0,) * a.ndim)
    tile = lambda w: pl.BlockSpec((1, tm, w), lambda b, i: (b, i, 0))
    return pl.pallas_call(
        _outproj_kernel,
        grid=(B, S // tm),
        in_specs=[
            tile(D),
            pl.BlockSpec((1, MLA_WIDTH, tm), lambda b, i: (b, 0, i)),
            tile(CONV_WIDTH),
            full(wo), full(g), full(wrh), full(wrl), full(br),
        ],
        out_specs=[tile(D), tile(D), tile(HEAD_PAD), tile(HEAD_PAD), tile(HEAD_PAD),
                   pl.BlockSpec((1, HEAD_PAD), lambda b, i: (0, 0))],
        out_shape=[
            jax.ShapeDtypeStruct((B, S, D), F32),
            jax.ShapeDtypeStruct((B, S, D), F32),
            jax.ShapeDtypeStruct((B, S, HEAD_PAD), jnp.int32),
            jax.ShapeDtypeStruct((B, S, HEAD_PAD), F32),
            jax.ShapeDtypeStruct((B, S, HEAD_PAD), jnp.int32),
            jax.ShapeDtypeStruct((1, HEAD_PAD), F32),
        ],
        scratch_shapes=[pltpu.VMEM((1, HEAD_PAD), F32)],
        compiler_params=pltpu.CompilerParams(
            dimension_semantics=("arbitrary", "arbitrary"), vmem_limit_bytes=VMEM_LIMIT),
        name="outproj",
    )(x, ot, oc, wo, g, wrh, wrl, br)


def _dispatch_kernel(dest_ref, pend_ref, padded_ref, nused_ref, xn_ref, xb_ref, zero_ref, sem,
                     zsem):
    tm = xn_ref.shape[0]
    i = pl.program_id(0)
    base = i * tm

    @pl.when(i == 0)
    def _():
        zero_ref[...] = jnp.zeros_like(zero_ref)

        def zero_copy(start):
            start = pl.multiple_of(start, MOE_BLK)
            return pltpu.make_async_copy(zero_ref, xb_ref.at[pl.ds(start, MOE_BLK)], zsem)

        def tail_start(j, c):
            zero_copy(j * MOE_BLK).start()
            return c

        def tail_wait(j, c):
            zero_copy(j * MOE_BLK).wait()
            return c

        n_blocks = xb_ref.shape[0] // MOE_BLK
        for e in range(N_EXPERTS):
            @pl.when(padded_ref[e] > 0)
            def _():
                zero_copy(pend_ref[e] - MOE_BLK).start()
        lax.fori_loop(nused_ref[0], n_blocks, tail_start, 0)
        for e in range(N_EXPERTS):
            @pl.when(padded_ref[e] > 0)
            def _():
                zero_copy(pend_ref[e] - MOE_BLK).wait()
        lax.fori_loop(nused_ref[0], n_blocks, tail_wait, 0)

    def row_copy(r, d):
        return pltpu.make_async_copy(xn_ref.at[pl.ds(r, 1)], xb_ref.at[pl.ds(d, 1)], sem)

    def issue(r, c):
        t = (base + r) * 2
        row_copy(r, dest_ref[t]).start()
        row_copy(r, dest_ref[t + 1]).start()
        return c

    lax.fori_loop(0, tm, issue, 0, unroll=8)
    for _ in range(2):
        pltpu.make_async_copy(xn_ref, xb_ref.at[pl.ds(0, tm)], sem).wait()


def _dispatch(dest, pad_ends, padded, n_used, xn, n_rows):
    N, D = xn.shape
    tm = TM
    return pl.pallas_call(
        _dispatch_kernel,
        grid_spec=pltpu.PrefetchScalarGridSpec(
            num_scalar_prefetch=4,
            grid=(N // tm,),
            in_specs=[pl.BlockSpec((tm, D), lambda i, *_: (i, 0))],
            out_specs=pl.BlockSpec(memory_space=pl.ANY),
            scratch_shapes=[pltpu.VMEM((MOE_BLK, D), F32), pltpu.SemaphoreType.DMA,
                            pltpu.SemaphoreType.DMA],
        ),
        out_shape=jax.ShapeDtypeStruct((n_rows, D), F32),
        compiler_params=pltpu.CompilerParams(
            dimension_semantics=("arbitrary",), vmem_limit_bytes=VMEM_LIMIT),
        name="dispatch",
    )(dest, pad_ends, padded, n_used, xn)


def _expert_kernel(blke_ref, nused_ref, xb_ref, wg_ref, wu_ref, wd_ref, yb_ref,
                   wgs_ref, wus_ref, wds_ref):
    i = pl.program_id(0)
    used = i < nused_ref[0]
    changed = (i == 0) | (blke_ref[i] != blke_ref[jnp.maximum(i - 1, 0)])

    @pl.when(used & changed)
    def _():
        wgs_ref[...] = wg_ref[0].astype(BF16)
        wus_ref[...] = wu_ref[0].astype(BF16)
        wds_ref[...] = wd_ref[0].astype(BF16)

    @pl.when(used)
    def _():
        xb = xb_ref[...].astype(BF16)
        half = EXPERT_FF // 2
        g0 = _dot(xb, wgs_ref[:, :half])
        u0 = _dot(xb, wus_ref[:, :half])
        g1 = _dot(xb, wgs_ref[:, half:])
        u1 = _dot(xb, wus_ref[:, half:])
        h0 = ((g0 * jax.nn.sigmoid(g0)) * u0).astype(BF16)
        y = _dot(h0, wds_ref[:half, :])
        h1 = ((g1 * jax.nn.sigmoid(g1)) * u1).astype(BF16)
        yb_ref[...] = y + _dot(h1, wds_ref[half:, :])

    @pl.when(jnp.logical_not(used))
    def _():
        yb_ref[...] = jnp.zeros_like(yb_ref)


def _experts(blk_e, n_used, xb, wg, wu, wd):
    P, D = xb.shape
    blk = MOE_BLK
    return pl.pallas_call(
        _expert_kernel,
        grid_spec=pltpu.PrefetchScalarGridSpec(
            num_scalar_prefetch=2,
            grid=(P // blk,),
            in_specs=[
                pl.BlockSpec((blk, D), lambda i, be, nu: (jnp.minimum(i, nu[0] - 1), 0)),
                pl.BlockSpec((1, D, EXPERT_FF), lambda i, be, nu: (be[i], 0, 0)),
                pl.BlockSpec((1, D, EXPERT_FF), lambda i, be, nu: (be[i], 0, 0)),
                pl.BlockSpec((1, EXPERT_FF, D), lambda i, be, nu: (be[i], 0, 0)),
            ],
            out_specs=pl.BlockSpec((blk, D), lambda i, be, nu: (i, 0)),
            scratch_shapes=[pltpu.VMEM((D, EXPERT_FF), BF16), pltpu.VMEM((D, EXPERT_FF), BF16),
                            pltpu.VMEM((EXPERT_FF, D), BF16)],
        ),
        out_shape=jax.ShapeDtypeStruct((P, D), F32),
        compiler_params=pltpu.CompilerParams(
            dimension_semantics=("arbitrary",), vmem_limit_bytes=VMEM_LIMIT),
        name="experts",
    )(blk_e, n_used, xb, wg, wu, wd)


def _final_kernel(dest_ref, h1_ref, p_ref, wt_ref, yb_ref, gp_ref, wg_ref, bg_ref, wp_ref,
                  gf_ref, out_ref, buf_ref, sem):
    tm = h1_ref.shape[0]
    base = pl.program_id(0) * tm

    def row_copy(k, r, d):
        return pltpu.make_async_copy(yb_ref.at[pl.ds(d, 1)], buf_ref.at[k, pl.ds(r, 1)], sem)

    def issue(r, c):
        t = (base + r) * 2
        row_copy(0, r, dest_ref[t]).start()
        row_copy(1, r, dest_ref[t + 1]).start()
        return c

    lax.fori_loop(0, tm, issue, 0, unroll=8)
    pe = _dot(p_ref[...].astype(BF16), wp_ref[...])
    for k in range(2):
        pltpu.make_async_copy(yb_ref.at[pl.ds(0, tm)], buf_ref.at[k], sem).wait()

    wt = wt_ref[...]
    h2 = h1_ref[...] + wt[:, 0:1] * buf_ref[0] + wt[:, 1:2] * buf_ref[1]
    n = _rms(h2, gp_ref[...]).astype(BF16)
    gate = jax.nn.sigmoid(_dot(n, wg_ref[...]) + bg_ref[...])
    h3 = h2 + gate * pe
    out_ref[...] = _rms(h3, gf_ref[...])


def _final(dest, h1, p, wt, yb, gp, wg, bg, wp, gf):
    N, D = h1.shape
    tm = TM
    full = lambda a: pl.BlockSpec(a.shape, lambda i, dest: (0,) * a.ndim)
    return pl.pallas_call(
        _final_kernel,
        grid_spec=pltpu.PrefetchScalarGridSpec(
            num_scalar_prefetch=1,
            grid=(N // tm,),
            in_specs=[
                pl.BlockSpec((tm, D), lambda i, dest: (i, 0)),
                pl.BlockSpec((tm, PLE_DIM), lambda i, dest: (i, 0)),
                pl.BlockSpec((tm, HEAD_PAD), lambda i, dest: (i, 0)),
                pl.BlockSpec(memory_space=pl.ANY),
                full(gp), full(wg), full(bg), full(wp), full(gf),
            ],
            out_specs=pl.BlockSpec((tm, D), lambda i, dest: (i, 0)),
            scratch_shapes=[pltpu.VMEM((2, tm, D), F32), pltpu.SemaphoreType.DMA],
        ),
        out_shape=jax.ShapeDtypeStruct((N, D), F32),
        compiler_params=pltpu.CompilerParams(
            dimension_semantics=("arbitrary",), vmem_limit_bytes=VMEM_LIMIT),
        name="final",
    )(dest, h1, p, wt, yb, gp, wg, bg, wp, gf)


def _prep_weights(w_in, w_uq, w_ukv):
    cq, ckv, kr, bg, cg, u = jnp.split(
        w_in, [Q_LORA, Q_LORA + KV_LORA, Q_LORA + KV_LORA + QK_ROPE_DIM,
               Q_LORA + KV_LORA + QK_ROPE_DIM + CONV_WIDTH,
               Q_LORA + KV_LORA + QK_ROPE_DIM + 2 * CONV_WIDTH], axis=1)
    kr_rot = jnp.concatenate([-kr[:, HALF_ROPE:], kr[:, :HALF_ROPE]], axis=1)
    pad = jnp.zeros((D_MODEL, HEAD_PAD - 2 * QK_ROPE_DIM), w_in.dtype)
    win = jnp.concatenate([cq, ckv, bg, cg, u, kr, kr_rot, pad], axis=1).astype(BF16)

    zq = jnp.zeros((Q_LORA, N_HEADS, HEAD_PAD - QK_DIM), w_uq.dtype)
    wqa = jnp.concatenate([w_uq, zq], axis=2)
    rope = w_uq[:, :, QK_NOPE_DIM:]
    wqb = jnp.concatenate([-rope[:, :, HALF_ROPE:], rope[:, :, :HALF_ROPE]], axis=2)
    wqa = wqa.reshape(Q_LORA, N_HEADS * HEAD_PAD).T.astype(BF16)
    wqb = wqb.reshape(Q_LORA, N_HEADS * QK_ROPE_DIM).T.astype(BF16)

    zk = jnp.zeros((KV_LORA, N_HEADS, HEAD_PAD - QK_NOPE_DIM), w_ukv.dtype)
    wuk = jnp.concatenate([w_ukv[:, :, :QK_NOPE_DIM], zk], axis=2)
    wuk = wuk.reshape(KV_LORA, N_HEADS * HEAD_PAD).astype(BF16)
    wuvt = w_ukv[:, :, QK_NOPE_DIM:].reshape(KV_LORA, MLA_WIDTH).T.astype(BF16)

    src = jnp.arange(QK_ROPE_DIM)[:, None]
    dst = jnp.arange(N_HEADS * HEAD_PAD)[None, :]
    pk = (dst % HEAD_PAD - QK_NOPE_DIM == src).astype(BF16)
    return win, wqa, wqb, wuk, wuvt, pk


def kernel(x, p, positions, attn_norm_g, w_in, q_norm_g, w_uq, kv_norm_g, w_ukv, conv_w, w_out,
           moe_norm_g, w_group_router, b_group_router, w_expert_router, b_expert_router,
           w_gate, w_up, w_down, ple_norm_g, w_ple_gate, b_ple_gate, w_ple_proj, final_norm_g):
    B, S, D = x.shape
    N = B * S
    assert w_in.shape[0] == 1, "single-layer trunk: the final norm is fused into the layer"
    pos = positions.astype(F32).reshape(B, 1, S)
    invf = (ROPE_BASE ** (-jnp.arange(0, QK_ROPE_DIM, 2, dtype=F32) / QK_ROPE_DIM)).reshape(-1, 1)
    row = lambda v: v.reshape(1, -1)
    h = x
    for i in range(1):
        win, wqa, wqb, wuk, wuvt, pk = _prep_weights(w_in[i], w_uq[i], w_ukv[i])
        qt, k, vt, oc = _inproj(h, pos, invf, row(attn_norm_g[i]), win, row(q_norm_g[i]), wqa,
                                wqb, row(kv_norm_g[i]), wuk, pk, wuvt,
                                conv_w[i].reshape(CONV_K, CONV_WIDTH))
        ot = _attention(qt, k, vt)

        wr = jnp.concatenate(
            [w_group_router[i], w_expert_router[i],
             jnp.zeros((D, HEAD_PAD - N_GROUPS - N_EXPERTS), F32)], axis=1)
        wrh = wr.astype(BF16)
        wrl = (wr - wrh.astype(F32)).astype(BF16)
        br = jnp.concatenate([b_group_router[i], b_expert_router[i],
                              jnp.zeros((HEAD_PAD - N_GROUPS - N_EXPERTS,), F32)]).reshape(1, -1)
        h1, xn, eid, wt, rank, cnt = _outproj(h, ot, oc, w_out[i].astype(BF16),
                                              row(moe_norm_g[i]), wrh, wrl, br)

        n_blocks = (2 * N) // MOE_BLK + N_EXPERTS
        counts = cnt[0, :N_EXPERTS].astype(jnp.int32)
        padded = (counts + MOE_BLK - 1) // MOE_BLK * MOE_BLK
        pad_ends = jnp.cumsum(padded)
        pad_starts = pad_ends - padded
        sel = eid[..., :2, None] == jnp.arange(N_EXPERTS, dtype=jnp.int32)
        dest = (jnp.sum(jnp.where(sel, pad_starts, 0), axis=-1) + rank[..., :2]).reshape(2 * N)
        blk_start = jnp.arange(n_blocks, dtype=jnp.int32) * MOE_BLK
        blk_e = jnp.minimum(jnp.sum((pad_ends[None, :] <= blk_start[:, None]).astype(jnp.int32),
                                    axis=1), N_EXPERTS - 1)
        n_used = (pad_ends[-1:] // MOE_BLK).astype(jnp.int32)

        xb = _dispatch(dest, pad_ends, padded, n_used, xn.reshape(N, D), n_blocks * MOE_BLK)
        yb = _experts(blk_e, n_used, xb, w_gate[i], w_up[i], w_down[i])
        out = _final(dest, h1.reshape(N, D), p[i].reshape(N, PLE_DIM), wt.reshape(N, HEAD_PAD), yb,
                     row(ple_norm_g[i]), w_ple_gate[i].astype(BF16), row(b_ple_gate[i]),
                     w_ple_proj[i].astype(BF16), row(final_norm_g))
        h = out.reshape(B, S, D)
    return h
```

```python
import functools
import math

import jax
import jax.numpy as jnp
from jax import lax
from jax.experimental import pallas as pl
from jax.experimental.pallas import tpu as pltpu

D_MODEL = 1024
PLE_DIM = 256
MLA_WIDTH = 512
CONV_WIDTH = 512
N_HEADS = 8
V_HEAD_DIM = 64
QK_NOPE_DIM = 64
QK_ROPE_DIM = 32
Q_LORA = 384
KV_LORA = 256
CONV_K = 3
N_GROUPS = 4
EXPERTS_PER_GROUP = 8
N_EXPERTS = N_GROUPS * EXPERTS_PER_GROUP
EXPERT_FF = 512
ROPE_BASE = 10000.0
EPS = 1e-6

HEAD_PAD = 128
V_ROWS = 80
QK_DIM = QK_NOPE_DIM + QK_ROPE_DIM
HALF_ROPE = QK_ROPE_DIM // 2

_C_Q = 0
_C_KV = _C_Q + Q_LORA
_C_BG = _C_KV + KV_LORA
_C_CG = _C_BG + CONV_WIDTH
_C_U = _C_CG + CONV_WIDTH
_C_KR = _C_U + CONV_WIDTH
IN_COLS_PAD = _C_KR + HEAD_PAD

TM = 256
TQ = 512
TK = 256
ATT_HG = 4
ATT_TQC = 256
MOE_BLK = 256
NEG = -1e30
Q_SCALE = (QK_DIM ** -0.5) * math.log2(math.e)
VMEM_LIMIT = 48 * 1024 * 1024

F32 = jnp.float32
BF16 = jnp.bfloat16


def _rms(x, g):
    return x * lax.rsqrt(jnp.mean(x * x, axis=-1, keepdims=True) + EPS) * g


def _dot(a, b):
    return jnp.dot(a, b, preferred_element_type=F32)


def _dot_nt(a, b):
    return lax.dot_general(a, b, (((1,), (1,)), ((), ())), preferred_element_type=F32)


def _dot_tn(a, b):
    return lax.dot_general(a, b, (((0,), (0,)), ((), ())), preferred_element_type=F32)


def _inproj_kernel(x_ref, pos_ref, invf_ref, g_ref, win_ref, qg_ref, wqa_ref, wqb_ref, kvg_ref,
                   wuk_ref, pk_ref, wuvt_ref, convw_ref,
                   qt_ref, k_ref, vt_ref, oc_ref, carry_ref, ext_ref):
    tm = x_ref.shape[1]
    x = x_ref[0]
    xn = _rms(x, g_ref[...])
    z = _dot(xn.astype(BF16), win_ref[...])

    ang = invf_ref[...] * pos_ref[0]
    cos = jnp.cos(ang)
    sin = jnp.sin(ang)
    cos2 = jnp.concatenate([cos, cos], axis=0)
    sin2 = jnp.concatenate([sin, sin], axis=0)

    cqn = _rms(z[:, _C_Q:_C_Q + Q_LORA], qg_ref[...]).astype(BF16)
    qa = _dot_nt(wqa_ref[...], cqn)
    qb = _dot_nt(wqb_ref[...], cqn)
    for h in range(N_HEADS):
        r0 = h * HEAD_PAD
        nope = qa[r0:r0 + QK_NOPE_DIM]
        rope = (qa[r0 + QK_NOPE_DIM:r0 + QK_DIM] * cos2
                + qb[h * QK_ROPE_DIM:(h + 1) * QK_ROPE_DIM] * sin2)
        qh = jnp.concatenate([nope, rope, qa[r0 + QK_DIM:r0 + HEAD_PAD]], axis=0) * Q_SCALE
        qt_ref[0, r0:r0 + HEAD_PAD, :] = qh.astype(BF16)

    kvn = _rms(z[:, _C_KV:_C_KV + KV_LORA], kvg_ref[...]).astype(BF16)
    krt = z[:, _C_KR:_C_KR + HEAD_PAD].T
    krot = krt[0:QK_ROPE_DIM] * cos2 + krt[QK_ROPE_DIM:2 * QK_ROPE_DIM] * sin2
    k_ref[0] = (_dot(kvn, wuk_ref[...]) + _dot_tn(krot.astype(BF16), pk_ref[...])).astype(BF16)
    vt = _dot_nt(wuvt_ref[...], kvn).astype(BF16)
    ones_row = jnp.where(lax.broadcasted_iota(jnp.int32, (V_ROWS - V_HEAD_DIM, tm), 0) == 0,
                         1.0, 0.0).astype(BF16)
    for h in range(N_HEADS):
        vt_ref[0, h * V_ROWS:h * V_ROWS + V_HEAD_DIM, :] = vt[h * V_HEAD_DIM:(h + 1) * V_HEAD_DIM]
        vt_ref[0, h * V_ROWS + V_HEAD_DIM:(h + 1) * V_ROWS, :] = ones_row

    @pl.when(pl.program_id(1) == 0)
    def _():
        carry_ref[...] = jnp.zeros_like(carry_ref)

    cu = z[:, _C_CG:_C_CG + CONV_WIDTH] * z[:, _C_U:_C_U + CONV_WIDTH]
    ext_ref[0:8, :] = carry_ref[...]
    ext_ref[8:8 + tm, :] = cu
    cu1 = ext_ref[7:7 + tm, :]
    cu2 = ext_ref[6:6 + tm, :]
    w = convw_ref[...]
    y = w[0:1] * cu2 + w[1:2] * cu1 + w[2:3] * cu
    oc_ref[0] = (z[:, _C_BG:_C_BG + CONV_WIDTH] * y).astype(BF16)
    carry_ref[...] = ext_ref[tm:tm + 8, :]


def _inproj(x, pos, invf, g, win, qg, wqa, wqb, kvg, wuk, pk, wuvt, convw):
    B, S, D = x.shape
    tm = TM
    full = lambda a: pl.BlockSpec(a.shape, lambda b, i: (0,) * a.ndim)
    return pl.pallas_call(
        _inproj_kernel,
        grid=(B, S // tm),
        in_specs=[
            pl.BlockSpec((1, tm, D), lambda b, i: (b, i, 0)),
            pl.BlockSpec((1, 1, tm), lambda b, i: (b, 0, i)),
            full(invf), full(g), full(win), full(qg), full(wqa), full(wqb), full(kvg), full(wuk),
            full(pk), full(wuvt), full(convw),
        ],
        out_specs=[
            pl.BlockSpec((1, N_HEADS * HEAD_PAD, tm), lambda b, i: (b, 0, i)),
            pl.BlockSpec((1, tm, N_HEADS * HEAD_PAD), lambda b, i: (b, i, 0)),
            pl.BlockSpec((1, N_HEADS * V_ROWS, tm), lambda b, i: (b, 0, i)),
            pl.BlockSpec((1, tm, CONV_WIDTH), lambda b, i: (b, i, 0)),
        ],
        out_shape=[
            jax.ShapeDtypeStruct((B, N_HEADS * HEAD_PAD, S), BF16),
            jax.ShapeDtypeStruct((B, S, N_HEADS * HEAD_PAD), BF16),
            jax.ShapeDtypeStruct((B, N_HEADS * V_ROWS, S), BF16),
            jax.ShapeDtypeStruct((B, S, CONV_WIDTH), BF16),
        ],
        scratch_shapes=[pltpu.VMEM((8, CONV_WIDTH), F32), pltpu.VMEM((tm + 8, CONV_WIDTH), F32)],
        compiler_params=pltpu.CompilerParams(
            dimension_semantics=("arbitrary", "arbitrary"), vmem_limit_bytes=VMEM_LIMIT),
        name="inproj",
    )(x, pos, invf, g, win, qg, wqa, wqb, kvg, wuk, pk, wuvt, convw)


def _attn_kernel(qt_ref, k_ref, vt_ref, o_ref, m_ref, acc_ref, sa_ref, sb_ref):
    tq = qt_ref.shape[2]
    tk = TK
    assert tq == 2 * tk
    i = pl.program_id(2)
    m_ref[...] = jnp.full_like(m_ref, NEG)
    acc_ref[...] = jnp.zeros_like(acc_ref)

    def scores(j, s_ref, g):
        k0 = pl.multiple_of(j * tk, tk)
        qt = qt_ref[0, g * HEAD_PAD:(g + 1) * HEAD_PAD, :]
        s_ref[g] = _dot(k_ref[0, pl.ds(k0, tk), g * HEAD_PAD:(g + 1) * HEAD_PAD], qt)

    def softmax_pv(j, s_ref, g, masked):
        k0 = pl.multiple_of(j * tk, tk)
        vt = vt_ref[0, g * V_ROWS:(g + 1) * V_ROWS, pl.ds(k0, tk)]
        for c in range(tq // ATT_TQC):
            cols = slice(c * ATT_TQC, (c + 1) * ATT_TQC)
            s = s_ref[g, :, cols]
            if masked:
                kidx = k0 + lax.broadcasted_iota(jnp.int32, (tk, ATT_TQC), 0)
                qidx = i * tq + c * ATT_TQC + lax.broadcasted_iota(jnp.int32, (tk, ATT_TQC), 1)
                s = jnp.where(kidx <= qidx, s, NEG)
            m_old = m_ref[g, :, cols]
            m_new = jnp.maximum(m_old, jnp.max(s, axis=0, keepdims=True))
            alpha = jnp.exp2(m_old - m_new)
            p = jnp.exp2((s - m_new).astype(BF16))
            acc_ref[g, :, cols] = alpha * acc_ref[g, :, cols] + _dot(vt, p)
            m_ref[g, :, cols] = m_new

    def stage(j_next, s_next, j, s_cur, masked):
        for g in range(ATT_HG):
            if j_next is not None:
                scores(j_next, s_next, g)
            softmax_pv(j, s_cur, g, masked)

    for g in range(ATT_HG):
        scores(0, sa_ref, g)

    def body(t, c):
        j = 2 * t
        stage(j + 1, sb_ref, j, sa_ref, False)
        stage(j + 2, sa_ref, j + 1, sb_ref, False)
        return c

    lax.fori_loop(0, i, body, 0)
    n_full = 2 * i
    stage(n_full + 1, sb_ref, n_full, sa_ref, True)
    stage(None, None, n_full + 1, sb_ref, True)
    for g in range(ATT_HG):
        o_ref[0, g * V_HEAD_DIM:(g + 1) * V_HEAD_DIM, :] = (
            acc_ref[g, 0:V_HEAD_DIM, :] / acc_ref[g, V_HEAD_DIM:V_HEAD_DIM + 1, :]).astype(o_ref.dtype)


def _attention(qt, k, vt):
    B, _, S = qt.shape
    hg = ATT_HG
    return pl.pallas_call(
        _attn_kernel,
        grid=(B, N_HEADS // hg, S // TQ),
        in_specs=[
            pl.BlockSpec((1, hg * HEAD_PAD, TQ), lambda b, h, i: (b, h, i)),
            pl.BlockSpec((1, S, hg * HEAD_PAD), lambda b, h, i: (b, 0, h)),
            pl.BlockSpec((1, hg * V_ROWS, S), lambda b, h, i: (b, h, 0)),
        ],
        out_specs=pl.BlockSpec((1, hg * V_HEAD_DIM, TQ), lambda b, h, i: (b, h, i)),
        out_shape=jax.ShapeDtypeStruct((B, MLA_WIDTH, S), BF16),
        scratch_shapes=[pltpu.VMEM((hg, 1, TQ), F32), pltpu.VMEM((hg, V_ROWS, TQ), F32),
                        pltpu.VMEM((hg, TK, TQ), F32), pltpu.VMEM((hg, TK, TQ), F32)],
        compiler_params=pltpu.CompilerParams(
            dimension_semantics=("arbitrary", "arbitrary", "arbitrary"),
            vmem_limit_bytes=VMEM_LIMIT),
        name="attn",
    )(qt, k, vt)


def _outproj_kernel(x_ref, ot_ref, oc_ref, wo_ref, g_ref, wrh_ref, wrl_ref, br_ref,
                    h1_ref, xn_ref, eid_ref, wt_ref, rank_ref, cnt_ref, carry_ref):
    tm = x_ref.shape[1]

    @pl.when((pl.program_id(0) == 0) & (pl.program_id(1) == 0))
    def _():
        carry_ref[...] = jnp.zeros_like(carry_ref)

    attn = (_dot_tn(ot_ref[0], wo_ref[0:MLA_WIDTH, :])
            + _dot(oc_ref[0], wo_ref[MLA_WIDTH:MLA_WIDTH + CONV_WIDTH, :]))
    h1 = x_ref[0] + attn
    h1_ref[0] = h1
    xn = _rms(h1, g_ref[...])
    xn_ref[0] = xn

    xh = xn.astype(BF16)
    xl = (xn - xh.astype(F32)).astype(BF16)
    logits = (_dot(xh, wrh_ref[...]) + _dot(xl, wrh_ref[...]) + _dot(xh, wrl_ref[...])
              + br_ref[...])
    lane = lax.broadcasted_iota(jnp.int32, logits.shape, 1)
    big = jnp.int32(1 << 20)

    glog = jnp.where(lane < N_GROUPS, logits, NEG)
    gmax = jnp.max(glog, axis=-1, keepdims=True)
    gsum = jnp.sum(jnp.exp(glog - gmax), axis=-1, keepdims=True)
    g_p = 1.0 / gsum
    g_idx = jnp.min(jnp.where(glog == gmax, lane, big), axis=-1, keepdims=True)

    e_lo = N_GROUPS + g_idx * EXPERTS_PER_GROUP
    elog = jnp.where((lane >= e_lo) & (lane < e_lo + EXPERTS_PER_GROUP), logits, NEG)
    emax = jnp.max(elog, axis=-1, keepdims=True)
    esum = jnp.sum(jnp.exp(elog - emax), axis=-1, keepdims=True)
    e1 = jnp.min(jnp.where(elog == emax, lane, big), axis=-1, keepdims=True)
    elog2 = jnp.where(lane == e1, NEG, elog)
    emax2 = jnp.max(elog2, axis=-1, keepdims=True)
    e2 = jnp.min(jnp.where(elog2 == emax2, lane, big), axis=-1, keepdims=True)
    p1 = 1.0 / esum
    p2 = jnp.exp(emax2 - emax) / esum
    psum = p1 + p2
    w1 = g_p * (p1 / psum)
    w2 = g_p * (p2 / psum)
    e1 = e1 - N_GROUPS
    e2 = e2 - N_GROUPS

    oh1 = lane == e1
    oh2 = lane == e2
    oh = jnp.where(oh1 | oh2, 1.0, 0.0)
    row = lax.broadcasted_iota(jnp.int32, (tm, tm), 0)
    col = lax.broadcasted_iota(jnp.int32, (tm, tm), 1)
    lower = jnp.where(row > col, 1.0, 0.0).astype(BF16)
    cum = _dot(lower, oh.astype(BF16)) + carry_ref[...]
    r1 = jnp.sum(jnp.where(oh1, cum, 0.0), axis=-1, keepdims=True)
    r2 = jnp.sum(jnp.where(oh2, cum, 0.0), axis=-1, keepdims=True)
    carry_ref[...] = carry_ref[...] + jnp.sum(oh, axis=0, keepdims=True)
    cnt_ref[...] = carry_ref[...]

    eid_ref[0] = jnp.where(lane == 0, e1, jnp.where(lane == 1, e2, 0))
    rank_ref[0] = jnp.where(lane == 0, r1, jnp.where(lane == 1, r2, 0.0)).astype(jnp.int32)
    wt_ref[0] = jnp.where(lane == 0, w1, jnp.where(lane == 1, w2, 0.0))


def _outproj(x, ot, oc, wo, g, wrh, wrl, br):
    B, S, D = x.shape
    tm = TM
    full = lambda a: pl.BlockSpec(a.shape, lambda b, i: (0,) * a.ndim)
    tile = lambda w: pl.BlockSpec((1, tm, w), lambda b, i: (b, i, 0))
    return pl.pallas_call(
        _outproj_kernel,
        grid=(B, S // tm),
        in_specs=[
            tile(D),
            pl.BlockSpec((1, MLA_WIDTH, tm), lambda b, i: (b, 0, i)),
            tile(CONV_WIDTH),
            full(wo), full(g), full(wrh), full(wrl), full(br),
        ],
        out_specs=[tile(D), tile(D), tile(HEAD_PAD), tile(HEAD_PAD), tile(HEAD_PAD),
                   pl.BlockSpec((1, HEAD_PAD), lambda b, i: (0, 0))],
        out_shape=[
            jax.ShapeDtypeStruct((B, S, D), F32),
            jax.ShapeDtypeStruct((B, S, D), F32),
            jax.ShapeDtypeStruct((B, S, HEAD_PAD), jnp.int32),
            jax.ShapeDtypeStruct((B, S, HEAD_PAD), F32),
            jax.ShapeDtypeStruct((B, S, HEAD_PAD), jnp.int32),
            jax.ShapeDtypeStruct((1, HEAD_PAD), F32),
        ],
        scratch_shapes=[pltpu.VMEM((1, HEAD_PAD), F32)],
        compiler_params=pltpu.CompilerParams(
            dimension_semantics=("arbitrary", "arbitrary"), vmem_limit_bytes=VMEM_LIMIT),
        name="outproj",
    )(x, ot, oc, wo, g, wrh, wrl, br)


def _dispatch_kernel(dest_ref, pend_ref, padded_ref, nused_ref, xn_ref, xb_ref, zero_ref, sem,
                     zsem):
    tm = xn_ref.shape[0]
    i = pl.program_id(0)
    base = i * tm

    @pl.when(i == 0)
    def _():
        zero_ref[...] = jnp.zeros_like(zero_ref)

        def zero_copy(start):
            start = pl.multiple_of(start, MOE_BLK)
            return pltpu.make_async_copy(zero_ref, xb_ref.at[pl.ds(start, MOE_BLK)], zsem)

        def tail_start(j, c):
            zero_copy(j * MOE_BLK).start()
            return c

        def tail_wait(j, c):
            zero_copy(j * MOE_BLK).wait()
            return c

        n_blocks = xb_ref.shape[0] // MOE_BLK
        for e in range(N_EXPERTS):
            @pl.when(padded_ref[e] > 0)
            def _():
                zero_copy(pend_ref[e] - MOE_BLK).start()
        lax.fori_loop(nused_ref[0], n_blocks, tail_start, 0)
        for e in range(N_EXPERTS):
            @pl.when(padded_ref[e] > 0)
            def _():
                zero_copy(pend_ref[e] - MOE_BLK).wait()
        lax.fori_loop(nused_ref[0], n_blocks, tail_wait, 0)

    def row_copy(r, d):
        return pltpu.make_async_copy(xn_ref.at[pl.ds(r, 1)], xb_ref.at[pl.ds(d, 1)], sem)

    for r in range(tm):
        t = (base + r) * 2
        row_copy(r, dest_ref[t]).start()
        row_copy(r, dest_ref[t + 1]).start()
    for _ in range(2):
        pltpu.make_async_copy(xn_ref, xb_ref.at[pl.ds(0, tm)], sem).wait()


def _dispatch(dest, pad_ends, padded, n_used, xn, n_rows):
    N, D = xn.shape
    tm = TM
    return pl.pallas_call(
        _dispatch_kernel,
        grid_spec=pltpu.PrefetchScalarGridSpec(
            num_scalar_prefetch=4,
            grid=(N // tm,),
            in_specs=[pl.BlockSpec((tm, D), lambda i, *_: (i, 0))],
            out_specs=pl.BlockSpec(memory_space=pl.ANY),
            scratch_shapes=[pltpu.VMEM((MOE_BLK, D), F32), pltpu.SemaphoreType.DMA,
                            pltpu.SemaphoreType.DMA],
        ),
        out_shape=jax.ShapeDtypeStruct((n_rows, D), F32),
        compiler_params=pltpu.CompilerParams(
            dimension_semantics=("arbitrary",), vmem_limit_bytes=VMEM_LIMIT),
        name="dispatch",
    )(dest, pad_ends, padded, n_used, xn)


def _expert_kernel(blke_ref, nused_ref, xb_ref, wg_ref, wu_ref, wd_ref, yb_ref,
                   wgs_ref, wus_ref, wds_ref):
    i = pl.program_id(0)
    used = i < nused_ref[0]
    changed = (i == 0) | (blke_ref[i] != blke_ref[jnp.maximum(i - 1, 0)])

    @pl.when(used & changed)
    def _():
        wgs_ref[...] = wg_ref[0].astype(BF16)
        wus_ref[...] = wu_ref[0].astype(BF16)
        wds_ref[...] = wd_ref[0].astype(BF16)

    @pl.when(used)
    def _():
        xb = xb_ref[...].astype(BF16)
        half = EXPERT_FF // 2
        g0 = _dot(xb, wgs_ref[:, :half])
        u0 = _dot(xb, wus_ref[:, :half])
        g1 = _dot(xb, wgs_ref[:, half:])
        u1 = _dot(xb, wus_ref[:, half:])
        h0 = ((g0 * jax.nn.sigmoid(g0)) * u0).astype(BF16)
        y = _dot(h0, wds_ref[:half, :])
        h1 = ((g1 * jax.nn.sigmoid(g1)) * u1).astype(BF16)
        yb_ref[...] = y + _dot(h1, wds_ref[half:, :])

    @pl.when(jnp.logical_not(used))
    def _():
        yb_ref[...] = jnp.zeros_like(yb_ref)


def _experts(blk_e, n_used, xb, wg, wu, wd):
    P, D = xb.shape
    blk = MOE_BLK
    return pl.pallas_call(
        _expert_kernel,
        grid_spec=pltpu.PrefetchScalarGridSpec(
            num_scalar_prefetch=2,
            grid=(P // blk,),
            in_specs=[
                pl.BlockSpec((blk, D), lambda i, be, nu: (jnp.minimum(i, nu[0] - 1), 0)),
                pl.BlockSpec((1, D, EXPERT_FF), lambda i, be, nu: (be[i], 0, 0)),
                pl.BlockSpec((1, D, EXPERT_FF), lambda i, be, nu: (be[i], 0, 0)),
                pl.BlockSpec((1, EXPERT_FF, D), lambda i, be, nu: (be[i], 0, 0)),
            ],
            out_specs=pl.BlockSpec((blk, D), lambda i, be, nu: (i, 0)),
            scratch_shapes=[pltpu.VMEM((D, EXPERT_FF), BF16), pltpu.VMEM((D, EXPERT_FF), BF16),
                            pltpu.VMEM((EXPERT_FF, D), BF16)],
        ),
        out_shape=jax.ShapeDtypeStruct((P, D), F32),
        compiler_params=pltpu.CompilerParams(
            dimension_semantics=("arbitrary",), vmem_limit_bytes=VMEM_LIMIT),
        name="experts",
    )(blk_e, n_used, xb, wg, wu, wd)


def _final_kernel(dest_ref, h1_ref, p_ref, wt_ref, yb_ref, gp_ref, wg_ref, bg_ref, wp_ref,
                  gf_ref, out_ref, buf_ref, sem):
    tm = h1_ref.shape[0] // 2
    i = pl.program_id(0)

    def row_copy(slot, k, r, d):
        return pltpu.make_async_copy(yb_ref.at[pl.ds(d, 1)], buf_ref.at[slot, k, pl.ds(r, 1)],
                                     sem.at[slot])

    def issue(tile, slot):
        for r in range(tm):
            t = (tile * tm + r) * 2
            row_copy(slot, 0, r, dest_ref[t]).start()
            row_copy(slot, 1, r, dest_ref[t + 1]).start()

    def wait(slot):
        for k in range(2):
            pltpu.make_async_copy(yb_ref.at[pl.ds(0, tm)], buf_ref.at[slot, k], sem.at[slot]).wait()

    def compute(slot):
        rows = slice(slot * tm, (slot + 1) * tm)
        pe = _dot(p_ref[rows, :].astype(BF16), wp_ref[...])
        wt = wt_ref[rows, :]
        h2 = h1_ref[rows, :] + wt[:, 0:1] * buf_ref[slot, 0] + wt[:, 1:2] * buf_ref[slot, 1]
        n = _rms(h2, gp_ref[...]).astype(BF16)
        gate = jax.nn.sigmoid(_dot(n, wg_ref[...]) + bg_ref[...])
        h3 = h2 + gate * pe
        out_ref[rows, :] = _rms(h3, gf_ref[...])

    @pl.when(i == 0)
    def _():
        issue(0, 0)

    issue(2 * i + 1, 1)
    wait(0)
    compute(0)

    last = pl.num_programs(0) - 1
    issue(jnp.minimum(2 * i + 2, 2 * last + 1), 0)
    wait(1)
    compute(1)

    @pl.when(i == last)
    def _():
        wait(0)


def _final(dest, h1, p, wt, yb, gp, wg, bg, wp, gf):
    N, D = h1.shape
    tm = TM
    full = lambda a: pl.BlockSpec(a.shape, lambda i, dest: (0,) * a.ndim)
    return pl.pallas_call(
        _final_kernel,
        grid_spec=pltpu.PrefetchScalarGridSpec(
            num_scalar_prefetch=1,
            grid=(N // (2 * tm),),
            in_specs=[
                pl.BlockSpec((2 * tm, D), lambda i, dest: (i, 0)),
                pl.BlockSpec((2 * tm, PLE_DIM), lambda i, dest: (i, 0)),
                pl.BlockSpec((2 * tm, HEAD_PAD), lambda i, dest: (i, 0)),
                pl.BlockSpec(memory_space=pl.ANY),
                full(gp), full(wg), full(bg), full(wp), full(gf),
            ],
            out_specs=pl.BlockSpec((2 * tm, D), lambda i, dest: (i, 0)),
            scratch_shapes=[pltpu.VMEM((2, 2, tm, D), F32), pltpu.SemaphoreType.DMA((2,))],
        ),
        out_shape=jax.ShapeDtypeStruct((N, D), F32),
        compiler_params=pltpu.CompilerParams(
            dimension_semantics=("arbitrary",), vmem_limit_bytes=VMEM_LIMIT),
        name="final",
    )(dest, h1, p, wt, yb, gp, wg, bg, wp, gf)


def _prep_weights(w_in, w_uq, w_ukv):
    cq, ckv, kr, bg, cg, u = jnp.split(
        w_in, [Q_LORA, Q_LORA + KV_LORA, Q_LORA + KV_LORA + QK_ROPE_DIM,
               Q_LORA + KV_LORA + QK_ROPE_DIM + CONV_WIDTH,
               Q_LORA + KV_LORA + QK_ROPE_DIM + 2 * CONV_WIDTH], axis=1)
    kr_rot = jnp.concatenate([-kr[:, HALF_ROPE:], kr[:, :HALF_ROPE]], axis=1)
    pad = jnp.zeros((D_MODEL, HEAD_PAD - 2 * QK_ROPE_DIM), w_in.dtype)
    win = jnp.concatenate([cq, ckv, bg, cg, u, kr, kr_rot, pad], axis=1).astype(BF16)

    zq = jnp.zeros((Q_LORA, N_HEADS, HEAD_PAD - QK_DIM), w_uq.dtype)
    wqa = jnp.concatenate([w_uq, zq], axis=2)
    rope = w_uq[:, :, QK_NOPE_DIM:]
    wqb = jnp.concatenate([-rope[:, :, HALF_ROPE:], rope[:, :, :HALF_ROPE]], axis=2)
    wqa = wqa.reshape(Q_LORA, N_HEADS * HEAD_PAD).T.astype(BF16)
    wqb = wqb.reshape(Q_LORA, N_HEADS * QK_ROPE_DIM).T.astype(BF16)

    zk = jnp.zeros((KV_LORA, N_HEADS, HEAD_PAD - QK_NOPE_DIM), w_ukv.dtype)
    wuk = jnp.concatenate([w_ukv[:, :, :QK_NOPE_DIM], zk], axis=2)
    wuk = wuk.reshape(KV_LORA, N_HEADS * HEAD_PAD).astype(BF16)
    wuvt = w_ukv[:, :, QK_NOPE_DIM:].reshape(KV_LORA, MLA_WIDTH).T.astype(BF16)

    src = jnp.arange(QK_ROPE_DIM)[:, None]
    dst = jnp.arange(N_HEADS * HEAD_PAD)[None, :]
    pk = (dst % HEAD_PAD - QK_NOPE_DIM == src).astype(BF16)
    return win, wqa, wqb, wuk, wuvt, pk


def kernel(x, p, positions, attn_norm_g, w_in, q_norm_g, w_uq, kv_norm_g, w_ukv, conv_w, w_out,
           moe_norm_g, w_group_router, b_group_router, w_expert_router, b_expert_router,
           w_gate, w_up, w_down, ple_norm_g, w_ple_gate, b_ple_gate, w_ple_proj, final_norm_g):
    B, S, D = x.shape
    N = B * S
    assert w_in.shape[0] == 1, "single-layer trunk: the final norm is fused into the layer"
    pos = positions.astype(F32).reshape(B, 1, S)
    invf = (ROPE_BASE ** (-jnp.arange(0, QK_ROPE_DIM, 2, dtype=F32) / QK_ROPE_DIM)).reshape(-1, 1)
    row = lambda v: v.reshape(1, -1)
    h = x
    for i in range(1):
        win, wqa, wqb, wuk, wuvt, pk = _prep_weights(w_in[i], w_uq[i], w_ukv[i])
        qt, k, vt, oc = _inproj(h, pos, invf, row(attn_norm_g[i]), win, row(q_norm_g[i]), wqa,
                                wqb, row(kv_norm_g[i]), wuk, pk, wuvt,
                                conv_w[i].reshape(CONV_K, CONV_WIDTH))
        ot = _attention(qt, k, vt)

        wr = jnp.concatenate(
            [w_group_router[i], w_expert_router[i],
             jnp.zeros((D, HEAD_PAD - N_GROUPS - N_EXPERTS), F32)], axis=1)
        wrh = wr.astype(BF16)
        wrl = (wr - wrh.astype(F32)).astype(BF16)
        br = jnp.concatenate([b_group_router[i], b_expert_router[i],
                              jnp.zeros((HEAD_PAD - N_GROUPS - N_EXPERTS,), F32)]).reshape(1, -1)
        h1, xn, eid, wt, rank, cnt = _outproj(h, ot, oc, w_out[i].astype(BF16),
                                              row(moe_norm_g[i]), wrh, wrl, br)

        n_blocks = (2 * N) // MOE_BLK + N_EXPERTS
        counts = cnt[0, :N_EXPERTS].astype(jnp.int32)
        padded = (counts + MOE_BLK - 1) // MOE_BLK * MOE_BLK
        pad_ends = jnp.cumsum(padded)
        pad_starts = pad_ends - padded
        sel = eid[..., :2, None] == jnp.arange(N_EXPERTS, dtype=jnp.int32)
        dest = (jnp.sum(jnp.where(sel, pad_starts, 0), axis=-1) + rank[..., :2]).reshape(2 * N)
        blk_start = jnp.arange(n_blocks, dtype=jnp.int32) * MOE_BLK
        blk_e = jnp.minimum(jnp.sum((pad_ends[None, :] <= blk_start[:, None]).astype(jnp.int32),
                                    axis=1), N_EXPERTS - 1)
        n_used = (pad_ends[-1:] // MOE_BLK).astype(jnp.int32)

        xb = _dispatch(dest, pad_ends, padded, n_used, xn.reshape(N, D), n_blocks * MOE_BLK)
        yb = _experts(blk_e, n_used, xb, w_gate[i], w_up[i], w_down[i])
        out = _final(dest, h1.reshape(N, D), p[i].reshape(N, PLE_DIM), wt.reshape(N, HEAD_PAD), yb,
                     row(ple_norm_g[i]), w_ple_gate[i].astype(BF16), row(b_ple_gate[i]),
                     w_ple_proj[i].astype(BF16), row(final_norm_g))
        h = out.reshape(B, S, D)
    return h
```

```python
import functools
import math

import jax
import jax.numpy as jnp
from jax import lax
from jax.experimental import pallas as pl
from jax.experimental.pallas import tpu as pltpu

D_MODEL = 1024
PLE_DIM = 256
MLA_WIDTH = 512
CONV_WIDTH = 512
N_HEADS = 8
V_HEAD_DIM = 64
QK_NOPE_DIM = 64
QK_ROPE_DIM = 32
Q_LORA = 384
KV_LORA = 256
CONV_K = 3
N_GROUPS = 4
EXPERTS_PER_GROUP = 8
N_EXPERTS = N_GROUPS * EXPERTS_PER_GROUP
EXPERT_FF = 512
ROPE_BASE = 10000.0
EPS = 1e-6

HEAD_PAD = 128
V_ROWS = 80
QK_DIM = QK_NOPE_DIM + QK_ROPE_DIM
HALF_ROPE = QK_ROPE_DIM // 2

_C_Q = 0
_C_KV = _C_Q + Q_LORA
_C_BG = _C_KV + KV_LORA
_C_CG = _C_BG + CONV_WIDTH
_C_U = _C_CG + CONV_WIDTH
_C_KR = _C_U + CONV_WIDTH
IN_COLS_PAD = _C_KR + HEAD_PAD

TM = 256
TQ = 512
TK = 256
ATT_HG = 4
ATT_TQC = 256
MOE_BLK = 256
ROUTER_E0 = 8
ROUTER_ROWS = 48
NEG = -1e30
Q_SCALE = (QK_DIM ** -0.5) * math.log2(math.e)
VMEM_LIMIT = 48 * 1024 * 1024

F32 = jnp.float32
BF16 = jnp.bfloat16


def _rms(x, g):
    return x * lax.rsqrt(jnp.mean(x * x, axis=-1, keepdims=True) + EPS) * g


def _dot(a, b):
    return jnp.dot(a, b, preferred_element_type=F32)


def _dot_nt(a, b):
    return lax.dot_general(a, b, (((1,), (1,)), ((), ())), preferred_element_type=F32)


def _dot_tn(a, b):
    return lax.dot_general(a, b, (((0,), (0,)), ((), ())), preferred_element_type=F32)


def _inproj_kernel(x_ref, pos_ref, invf_ref, g_ref, win_ref, qg_ref, wqa_ref, wqb_ref, kvg_ref,
                   wuk_ref, pk_ref, wuvt_ref, convw_ref,
                   qt_ref, k_ref, vt_ref, oc_ref, carry_ref, ext_ref):
    tm = x_ref.shape[1]
    x = x_ref[0]
    xn = _rms(x, g_ref[...])
    z = _dot(xn.astype(BF16), win_ref[...])

    ang = invf_ref[...] * pos_ref[0]
    cos = jnp.cos(ang)
    sin = jnp.sin(ang)
    cos2 = jnp.concatenate([cos, cos], axis=0)
    sin2 = jnp.concatenate([sin, sin], axis=0)

    cqn = _rms(z[:, _C_Q:_C_Q + Q_LORA], qg_ref[...]).astype(BF16)
    qa = _dot_nt(wqa_ref[...], cqn)
    qb = _dot_nt(wqb_ref[...], cqn)
    for h in range(N_HEADS):
        r0 = h * HEAD_PAD
        nope = qa[r0:r0 + QK_NOPE_DIM]
        rope = (qa[r0 + QK_NOPE_DIM:r0 + QK_DIM] * cos2
                + qb[h * QK_ROPE_DIM:(h + 1) * QK_ROPE_DIM] * sin2)
        qh = jnp.concatenate([nope, rope, qa[r0 + QK_DIM:r0 + HEAD_PAD]], axis=0) * Q_SCALE
        qt_ref[0, r0:r0 + HEAD_PAD, :] = qh.astype(BF16)

    kvn = _rms(z[:, _C_KV:_C_KV + KV_LORA], kvg_ref[...]).astype(BF16)
    krt = z[:, _C_KR:_C_KR + HEAD_PAD].T
    krot = krt[0:QK_ROPE_DIM] * cos2 + krt[QK_ROPE_DIM:2 * QK_ROPE_DIM] * sin2
    k_ref[0] = (_dot(kvn, wuk_ref[...]) + _dot_tn(krot.astype(BF16), pk_ref[...])).astype(BF16)
    vt = _dot_nt(wuvt_ref[...], kvn).astype(BF16)
    ones_row = jnp.where(lax.broadcasted_iota(jnp.int32, (V_ROWS - V_HEAD_DIM, tm), 0) == 0,
                         1.0, 0.0).astype(BF16)
    for h in range(N_HEADS):
        vt_ref[0, h * V_ROWS:h * V_ROWS + V_HEAD_DIM, :] = vt[h * V_HEAD_DIM:(h + 1) * V_HEAD_DIM]
        vt_ref[0, h * V_ROWS + V_HEAD_DIM:(h + 1) * V_ROWS, :] = ones_row

    @pl.when(pl.program_id(1) == 0)
    def _():
        carry_ref[...] = jnp.zeros_like(carry_ref)

    cu = z[:, _C_CG:_C_CG + CONV_WIDTH] * z[:, _C_U:_C_U + CONV_WIDTH]
    ext_ref[0:8, :] = carry_ref[...]
    ext_ref[8:8 + tm, :] = cu
    cu1 = ext_ref[7:7 + tm, :]
    cu2 = ext_ref[6:6 + tm, :]
    w = convw_ref[...]
    y = w[0:1] * cu2 + w[1:2] * cu1 + w[2:3] * cu
    oc_ref[0] = (z[:, _C_BG:_C_BG + CONV_WIDTH] * y).astype(BF16)
    carry_ref[...] = ext_ref[tm:tm + 8, :]


def _inproj(x, pos, invf, g, win, qg, wqa, wqb, kvg, wuk, pk, wuvt, convw):
    B, S, D = x.shape
    tm = TM
    full = lambda a: pl.BlockSpec(a.shape, lambda b, i: (0,) * a.ndim)
    return pl.pallas_call(
        _inproj_kernel,
        grid=(B, S // tm),
        in_specs=[
            pl.BlockSpec((1, tm, D), lambda b, i: (b, i, 0)),
            pl.BlockSpec((1, 1, tm), lambda b, i: (b, 0, i)),
            full(invf), full(g), full(win), full(qg), full(wqa), full(wqb), full(kvg), full(wuk),
            full(pk), full(wuvt), full(convw),
        ],
        out_specs=[
            pl.BlockSpec((1, N_HEADS * HEAD_PAD, tm), lambda b, i: (b, 0, i)),
            pl.BlockSpec((1, tm, N_HEADS * HEAD_PAD), lambda b, i: (b, i, 0)),
            pl.BlockSpec((1, N_HEADS * V_ROWS, tm), lambda b, i: (b, 0, i)),
            pl.BlockSpec((1, tm, CONV_WIDTH), lambda b, i: (b, i, 0)),
        ],
        out_shape=[
            jax.ShapeDtypeStruct((B, N_HEADS * HEAD_PAD, S), BF16),
            jax.ShapeDtypeStruct((B, S, N_HEADS * HEAD_PAD), BF16),
            jax.ShapeDtypeStruct((B, N_HEADS * V_ROWS, S), BF16),
            jax.ShapeDtypeStruct((B, S, CONV_WIDTH), BF16),
        ],
        scratch_shapes=[pltpu.VMEM((8, CONV_WIDTH), F32), pltpu.VMEM((tm + 8, CONV_WIDTH), F32)],
        compiler_params=pltpu.CompilerParams(
            dimension_semantics=("arbitrary", "arbitrary"), vmem_limit_bytes=VMEM_LIMIT),
        name="inproj",
    )(x, pos, invf, g, win, qg, wqa, wqb, kvg, wuk, pk, wuvt, convw)


def _attn_kernel(qt_ref, k_ref, vt_ref, o_ref, m_ref, acc_ref, sa_ref, sb_ref):
    tq = qt_ref.shape[2]
    tk = TK
    assert tq == 2 * tk
    i = pl.program_id(2)
    m_ref[...] = jnp.full_like(m_ref, NEG)
    acc_ref[...] = jnp.zeros_like(acc_ref)

    def scores(j, s_ref, g):
        k0 = pl.multiple_of(j * tk, tk)
        qt = qt_ref[0, g * HEAD_PAD:(g + 1) * HEAD_PAD, :]
        s_ref[g] = _dot(k_ref[0, pl.ds(k0, tk), g * HEAD_PAD:(g + 1) * HEAD_PAD], qt)

    def softmax_pv(j, s_ref, g, masked):
        k0 = pl.multiple_of(j * tk, tk)
        vt = vt_ref[0, g * V_ROWS:(g + 1) * V_ROWS, pl.ds(k0, tk)]
        for c in range(tq // ATT_TQC):
            cols = slice(c * ATT_TQC, (c + 1) * ATT_TQC)
            s = s_ref[g, :, cols]
            if masked:
                kidx = k0 + lax.broadcasted_iota(jnp.int32, (tk, ATT_TQC), 0)
                qidx = i * tq + c * ATT_TQC + lax.broadcasted_iota(jnp.int32, (tk, ATT_TQC), 1)
                s = jnp.where(kidx <= qidx, s, NEG)
            m_old = m_ref[g, :, cols]
            m_new = jnp.maximum(m_old, jnp.max(s, axis=0, keepdims=True))
            alpha = jnp.exp2(m_old - m_new)
            p = jnp.exp2((s - m_new).astype(BF16))
            acc_ref[g, :, cols] = alpha * acc_ref[g, :, cols] + _dot(vt, p)
            m_ref[g, :, cols] = m_new

    def stage(j_next, s_next, j, s_cur, masked):
        for g in range(ATT_HG):
            if j_next is not None:
                scores(j_next, s_next, g)
            softmax_pv(j, s_cur, g, masked)

    for g in range(ATT_HG):
        scores(0, sa_ref, g)

    def pair(j):
        stage(j + 1, sb_ref, j, sa_ref, False)
        stage(j + 2, sa_ref, j + 1, sb_ref, False)

    def body(t, c):
        pair(4 * t)
        pair(4 * t + 2)
        return c

    lax.fori_loop(0, i // 2, body, 0)

    @pl.when(i % 2 == 1)
    def _():
        pair(2 * i - 2)

    n_full = 2 * i
    stage(n_full + 1, sb_ref, n_full, sa_ref, True)
    stage(None, None, n_full + 1, sb_ref, True)
    for g in range(ATT_HG):
        o_ref[0, g * V_HEAD_DIM:(g + 1) * V_HEAD_DIM, :] = (
            acc_ref[g, 0:V_HEAD_DIM, :] / acc_ref[g, V_HEAD_DIM:V_HEAD_DIM + 1, :]).astype(o_ref.dtype)


def _attention(qt, k, vt):
    B, _, S = qt.shape
    hg = ATT_HG
    return pl.pallas_call(
        _attn_kernel,
        grid=(B, N_HEADS // hg, S // TQ),
        in_specs=[
            pl.BlockSpec((1, hg * HEAD_PAD, TQ), lambda b, h, i: (b, h, i)),
            pl.BlockSpec((1, S, hg * HEAD_PAD), lambda b, h, i: (b, 0, h)),
            pl.BlockSpec((1, hg * V_ROWS, S), lambda b, h, i: (b, h, 0)),
        ],
        out_specs=pl.BlockSpec((1, hg * V_HEAD_DIM, TQ), lambda b, h, i: (b, h, i)),
        out_shape=jax.ShapeDtypeStruct((B, MLA_WIDTH, S), BF16),
        scratch_shapes=[pltpu.VMEM((hg, 1, TQ), F32), pltpu.VMEM((hg, V_ROWS, TQ), F32),
                        pltpu.VMEM((hg, TK, TQ), F32), pltpu.VMEM((hg, TK, TQ), F32)],
        compiler_params=pltpu.CompilerParams(
            dimension_semantics=("arbitrary", "arbitrary", "arbitrary"),
            vmem_limit_bytes=VMEM_LIMIT),
        name="attn",
    )(qt, k, vt)


def _outproj_kernel(x_ref, ot_ref, oc_ref, wo_ref, g_ref, wrhl_ref, br_ref,
                    h1_ref, xn_ref, mi_ref, mw_ref, cnt_ref, carry_ref):
    tm = x_ref.shape[1]

    @pl.when((pl.program_id(0) == 0) & (pl.program_id(1) == 0))
    def _():
        carry_ref[...] = jnp.zeros_like(carry_ref)

    attn = (_dot_tn(ot_ref[0], wo_ref[0:MLA_WIDTH, :])
            + _dot(oc_ref[0], wo_ref[MLA_WIDTH:MLA_WIDTH + CONV_WIDTH, :]))
    h1 = x_ref[0] + attn
    h1_ref[0] = h1
    xn = _rms(h1, g_ref[...])
    xn_ref[0] = xn

    xh = xn.astype(BF16)
    xl = (xn - xh.astype(F32)).astype(BF16)
    hl = _dot_nt(wrhl_ref[...], xh)
    logits = (hl[0:ROUTER_ROWS] + hl[ROUTER_ROWS:2 * ROUTER_ROWS]
              + _dot_nt(wrhl_ref[0:ROUTER_ROWS, :], xl) + br_ref[...])
    big = jnp.int32(1 << 20)

    grow = lax.broadcasted_iota(jnp.int32, (ROUTER_E0, tm), 0)
    glog = jnp.where(grow < N_GROUPS, logits[0:ROUTER_E0], NEG)
    gmax = jnp.max(glog, axis=0, keepdims=True)
    gsum = jnp.sum(jnp.exp(glog - gmax), axis=0, keepdims=True)
    g_p = 1.0 / gsum
    g_idx = jnp.min(jnp.where(glog == gmax, grow, big), axis=0, keepdims=True)

    erow = lax.broadcasted_iota(jnp.int32, (N_EXPERTS, tm), 0)
    e_lo = g_idx * EXPERTS_PER_GROUP
    in_group = (erow >= e_lo) & (erow < e_lo + EXPERTS_PER_GROUP)
    elog = jnp.where(in_group, logits[ROUTER_E0:ROUTER_E0 + N_EXPERTS], NEG)
    emax = jnp.max(elog, axis=0, keepdims=True)
    esum = jnp.sum(jnp.exp(elog - emax), axis=0, keepdims=True)
    e1 = jnp.min(jnp.where(elog == emax, erow, big), axis=0, keepdims=True)
    elog2 = jnp.where(erow == e1, NEG, elog)
    emax2 = jnp.max(elog2, axis=0, keepdims=True)
    e2 = jnp.min(jnp.where(elog2 == emax2, erow, big), axis=0, keepdims=True)
    p1 = 1.0 / esum
    p2 = jnp.exp(emax2 - emax) / esum
    psum = p1 + p2
    w1 = g_p * (p1 / psum)
    w2 = g_p * (p2 / psum)

    oh1 = erow == e1
    oh2 = erow == e2
    oh = jnp.where(oh1 | oh2, 1.0, 0.0)
    srow = lax.broadcasted_iota(jnp.int32, (tm, tm), 0)
    scol = lax.broadcasted_iota(jnp.int32, (tm, tm), 1)
    earlier = jnp.where(srow < scol, 1.0, 0.0).astype(BF16)
    cum = _dot(oh.astype(BF16), earlier) + carry_ref[:, 0:1]
    r1 = jnp.sum(jnp.where(oh1, cum, 0.0), axis=0, keepdims=True)
    r2 = jnp.sum(jnp.where(oh2, cum, 0.0), axis=0, keepdims=True)
    carry_ref[...] = carry_ref[...] + jnp.sum(oh, axis=1, keepdims=True)
    cnt_ref[...] = carry_ref[...]

    mrow = lax.broadcasted_iota(jnp.int32, (8, tm), 0)
    r1i = r1.astype(jnp.int32)
    r2i = r2.astype(jnp.int32)
    mi_ref[0] = jnp.where(mrow == 0, e1, jnp.where(mrow == 1, e2, jnp.where(
        mrow == 2, r1i, jnp.where(mrow == 3, r2i, 0))))
    mw_ref[0] = jnp.where(mrow == 0, w1, jnp.where(mrow == 1, w2, 0.0))


def _outproj(x, ot, oc, wo, g, wrhl, br):
    B, S, D = x.shape
    tm = TM
    full = lambda a: pl.BlockSpec(a.shape, lambda b, i: (0,) * a.ndim)
    tile = lambda w: pl.BlockSpec((1, tm, w), lambda b, i: (b, i, 0))
    meta = pl.BlockSpec((1, 8, tm), lambda b, i: (b, 0, i))
    return pl.pallas_call(
        _outproj_kernel,
        grid=(B, S // tm),
        in_specs=[
            tile(D),
            pl.BlockSpec((1, MLA_WIDTH, tm), lambda b, i: (b, 0, i)),
            tile(CONV_WIDTH),
            full(wo), full(g), full(wrhl), full(br),
        ],
        out_specs=[tile(D), tile(D), meta, meta,
                   pl.BlockSpec((N_EXPERTS, HEAD_PAD), lambda b, i: (0, 0))],
        out_shape=[
            jax.ShapeDtypeStruct((B, S, D), F32),
            jax.ShapeDtypeStruct((B, S, D), F32),
            jax.ShapeDtypeStruct((B, 8, S), jnp.int32),
            jax.ShapeDtypeStruct((B, 8, S), F32),
            jax.ShapeDtypeStruct((N_EXPERTS, HEAD_PAD), F32),
        ],
        scratch_shapes=[pltpu.VMEM((N_EXPERTS, HEAD_PAD), F32)],
        compiler_params=pltpu.CompilerParams(
            dimension_semantics=("arbitrary", "arbitrary"), vmem_limit_bytes=VMEM_LIMIT),
        name="outproj",
    )(x, ot, oc, wo, g, wrhl, br)


def _dispatch_kernel(dest_ref, pend_ref, padded_ref, nused_ref, xn_ref, xb_ref, zero_ref, sem,
                     zsem):
    tm = xn_ref.shape[0]
    i = pl.program_id(0)
    base = i * tm

    @pl.when(i == 0)
    def _():
        zero_ref[...] = jnp.zeros_like(zero_ref)

        def zero_copy(start):
            start = pl.multiple_of(start, MOE_BLK)
            return pltpu.make_async_copy(zero_ref, xb_ref.at[pl.ds(start, MOE_BLK)], zsem)

        def tail_start(j, c):
            zero_copy(j * MOE_BLK).start()
            return c

        def tail_wait(j, c):
            zero_copy(j * MOE_BLK).wait()
            return c

        n_blocks = xb_ref.shape[0] // MOE_BLK
        for e in range(N_EXPERTS):
            @pl.when(padded_ref[e] > 0)
            def _():
                zero_copy(pend_ref[e] - MOE_BLK).start()
        lax.fori_loop(nused_ref[0], n_blocks, tail_start, 0)
        for e in range(N_EXPERTS):
            @pl.when(padded_ref[e] > 0)
            def _():
                zero_copy(pend_ref[e] - MOE_BLK).wait()
        lax.fori_loop(nused_ref[0], n_blocks, tail_wait, 0)

    def row_copy(r, d):
        return pltpu.make_async_copy(xn_ref.at[pl.ds(r, 1)], xb_ref.at[pl.ds(d, 1)], sem)

    for r in range(tm):
        t = (base + r) * 2
        row_copy(r, dest_ref[t]).start()
        row_copy(r, dest_ref[t + 1]).start()
    for _ in range(2):
        pltpu.make_async_copy(xn_ref, xb_ref.at[pl.ds(0, tm)], sem).wait()


def _dispatch(dest, pad_ends, padded, n_used, xn, n_rows):
    N, D = xn.shape
    tm = TM
    return pl.pallas_call(
        _dispatch_kernel,
        grid_spec=pltpu.PrefetchScalarGridSpec(
            num_scalar_prefetch=4,
            grid=(N // tm,),
            in_specs=[pl.BlockSpec((tm, D), lambda i, *_: (i, 0))],
            out_specs=pl.BlockSpec(memory_space=pl.ANY),
            scratch_shapes=[pltpu.VMEM((MOE_BLK, D), F32), pltpu.SemaphoreType.DMA,
                            pltpu.SemaphoreType.DMA],
        ),
        out_shape=jax.ShapeDtypeStruct((n_rows, D), F32),
        compiler_params=pltpu.CompilerParams(
            dimension_semantics=("arbitrary",), vmem_limit_bytes=VMEM_LIMIT),
        name="dispatch",
    )(dest, pad_ends, padded, n_used, xn)


def _expert_kernel(blke_ref, nused_ref, xb_ref, wg_ref, wu_ref, wd_ref, yb_ref,
                   wgs_ref, wus_ref, wds_ref):
    i = pl.program_id(0)
    used = i < nused_ref[0]
    changed = (i == 0) | (blke_ref[i] != blke_ref[jnp.maximum(i - 1, 0)])

    @pl.when(used & changed)
    def _():
        wgs_ref[...] = wg_ref[0].astype(BF16)
        wus_ref[...] = wu_ref[0].astype(BF16)
        wds_ref[...] = wd_ref[0].astype(BF16)

    @pl.when(used)
    def _():
        xb = xb_ref[...].astype(BF16)
        half = EXPERT_FF // 2
        g0 = _dot(xb, wgs_ref[:, :half])
        u0 = _dot(xb, wus_ref[:, :half])
        g1 = _dot(xb, wgs_ref[:, half:])
        u1 = _dot(xb, wus_ref[:, half:])
        h0 = ((g0 * jax.nn.sigmoid(g0)) * u0).astype(BF16)
        y = _dot(h0, wds_ref[:half, :])
        h1 = ((g1 * jax.nn.sigmoid(g1)) * u1).astype(BF16)
        yb_ref[...] = y + _dot(h1, wds_ref[half:, :])

    @pl.when(jnp.logical_not(used))
    def _():
        yb_ref[...] = jnp.zeros_like(yb_ref)


def _experts(blk_e, n_used, xb, wg, wu, wd):
    P, D = xb.shape
    blk = MOE_BLK
    return pl.pallas_call(
        _expert_kernel,
        grid_spec=pltpu.PrefetchScalarGridSpec(
            num_scalar_prefetch=2,
            grid=(P // blk,),
            in_specs=[
                pl.BlockSpec((blk, D), lambda i, be, nu: (jnp.minimum(i, nu[0] - 1), 0)),
                pl.BlockSpec((1, D, EXPERT_FF), lambda i, be, nu: (be[i], 0, 0)),
                pl.BlockSpec((1, D, EXPERT_FF), lambda i, be, nu: (be[i], 0, 0)),
                pl.BlockSpec((1, EXPERT_FF, D), lambda i, be, nu: (be[i], 0, 0)),
            ],
            out_specs=pl.BlockSpec((blk, D), lambda i, be, nu: (i, 0)),
            scratch_shapes=[pltpu.VMEM((D, EXPERT_FF), BF16), pltpu.VMEM((D, EXPERT_FF), BF16),
                            pltpu.VMEM((EXPERT_FF, D), BF16)],
        ),
        out_shape=jax.ShapeDtypeStruct((P, D), F32),
        compiler_params=pltpu.CompilerParams(
            dimension_semantics=("arbitrary",), vmem_limit_bytes=VMEM_LIMIT),
        name="experts",
    )(blk_e, n_used, xb, wg, wu, wd)


def _final_kernel(dest_ref, h1_ref, p_ref, wt_ref, yb_ref, gp_ref, wg_ref, bg_ref, wp_ref,
                  gf_ref, out_ref, buf_ref, sem):
    tm = h1_ref.shape[0] // 2
    i = pl.program_id(0)

    def row_copy(slot, k, r, d):
        return pltpu.make_async_copy(yb_ref.at[pl.ds(d, 1)], buf_ref.at[slot, k, pl.ds(r, 1)],
                                     sem.at[slot])

    def issue(tile, slot):
        for r in range(tm):
            t = (tile * tm + r) * 2
            row_copy(slot, 0, r, dest_ref[t]).start()
            row_copy(slot, 1, r, dest_ref[t + 1]).start()

    def wait(slot):
        for k in range(2):
            pltpu.make_async_copy(yb_ref.at[pl.ds(0, tm)], buf_ref.at[slot, k], sem.at[slot]).wait()

    def compute(slot):
        rows = slice(slot * tm, (slot + 1) * tm)
        pe = _dot(p_ref[rows, :].astype(BF16), wp_ref[...])
        wt = wt_ref[rows, :]
        h2 = h1_ref[rows, :] + wt[:, 0:1] * buf_ref[slot, 0] + wt[:, 1:2] * buf_ref[slot, 1]
        n = _rms(h2, gp_ref[...]).astype(BF16)
        gate = jax.nn.sigmoid(_dot(n, wg_ref[...]) + bg_ref[...])
        h3 = h2 + gate * pe
        out_ref[rows, :] = _rms(h3, gf_ref[...])

    @pl.when(i == 0)
    def _():
        issue(0, 0)

    issue(2 * i + 1, 1)
    wait(0)
    compute(0)

    last = pl.num_programs(0) - 1
    issue(jnp.minimum(2 * i + 2, 2 * last + 1), 0)
    wait(1)
    compute(1)

    @pl.when(i == last)
    def _():
        wait(0)


def _final(dest, h1, p, wt, yb, gp, wg, bg, wp, gf):
    N, D = h1.shape
    tm = TM
    full = lambda a: pl.BlockSpec(a.shape, lambda i, dest: (0,) * a.ndim)
    return pl.pallas_call(
        _final_kernel,
        grid_spec=pltpu.PrefetchScalarGridSpec(
            num_scalar_prefetch=1,
            grid=(N // (2 * tm),),
            in_specs=[
                pl.BlockSpec((2 * tm, D), lambda i, dest: (i, 0)),
                pl.BlockSpec((2 * tm, PLE_DIM), lambda i, dest: (i, 0)),
                pl.BlockSpec((2 * tm, 2), lambda i, dest: (i, 0)),
                pl.BlockSpec(memory_space=pl.ANY),
                full(gp), full(wg), full(bg), full(wp), full(gf),
            ],
            out_specs=pl.BlockSpec((2 * tm, D), lambda i, dest: (i, 0)),
            scratch_shapes=[pltpu.VMEM((2, 2, tm, D), F32), pltpu.SemaphoreType.DMA((2,))],
        ),
        out_shape=jax.ShapeDtypeStruct((N, D), F32),
        compiler_params=pltpu.CompilerParams(
            dimension_semantics=("arbitrary",), vmem_limit_bytes=VMEM_LIMIT),
        name="final",
    )(dest, h1, p, wt, yb, gp, wg, bg, wp, gf)


def _prep_weights(w_in, w_uq, w_ukv):
    cq, ckv, kr, bg, cg, u = jnp.split(
        w_in, [Q_LORA, Q_LORA + KV_LORA, Q_LORA + KV_LORA + QK_ROPE_DIM,
               Q_LORA + KV_LORA + QK_ROPE_DIM + CONV_WIDTH,
               Q_LORA + KV_LORA + QK_ROPE_DIM + 2 * CONV_WIDTH], axis=1)
    kr_rot = jnp.concatenate([-kr[:, HALF_ROPE:], kr[:, :HALF_ROPE]], axis=1)
    pad = jnp.zeros((D_MODEL, HEAD_PAD - 2 * QK_ROPE_DIM), w_in.dtype)
    win = jnp.concatenate([cq, ckv, bg, cg, u, kr, kr_rot, pad], axis=1).astype(BF16)

    zq = jnp.zeros((Q_LORA, N_HEADS, HEAD_PAD - QK_DIM), w_uq.dtype)
    wqa = jnp.concatenate([w_uq, zq], axis=2)
    rope = w_uq[:, :, QK_NOPE_DIM:]
    wqb = jnp.concatenate([-rope[:, :, HALF_ROPE:], rope[:, :, :HALF_ROPE]], axis=2)
    wqa = wqa.reshape(Q_LORA, N_HEADS * HEAD_PAD).T.astype(BF16)
    wqb = wqb.reshape(Q_LORA, N_HEADS * QK_ROPE_DIM).T.astype(BF16)

    zk = jnp.zeros((KV_LORA, N_HEADS, HEAD_PAD - QK_NOPE_DIM), w_ukv.dtype)
    wuk = jnp.concatenate([w_ukv[:, :, :QK_NOPE_DIM], zk], axis=2)
    wuk = wuk.reshape(KV_LORA, N_HEADS * HEAD_PAD).astype(BF16)
    wuvt = w_ukv[:, :, QK_NOPE_DIM:].reshape(KV_LORA, MLA_WIDTH).T.astype(BF16)

    src = jnp.arange(QK_ROPE_DIM)[:, None]
    dst = jnp.arange(N_HEADS * HEAD_PAD)[None, :]
    pk = (dst % HEAD_PAD - QK_NOPE_DIM == src).astype(BF16)
    return win, wqa, wqb, wuk, wuvt, pk


def kernel(x, p, positions, attn_norm_g, w_in, q_norm_g, w_uq, kv_norm_g, w_ukv, conv_w, w_out,
           moe_norm_g, w_group_router, b_group_router, w_expert_router, b_expert_router,
           w_gate, w_up, w_down, ple_norm_g, w_ple_gate, b_ple_gate, w_ple_proj, final_norm_g):
    B, S, D = x.shape
    N = B * S
    assert w_in.shape[0] == 1, "single-layer trunk: the final norm is fused into the layer"
    pos = positions.astype(F32).reshape(B, 1, S)
    invf = (ROPE_BASE ** (-jnp.arange(0, QK_ROPE_DIM, 2, dtype=F32) / QK_ROPE_DIM)).reshape(-1, 1)
    row = lambda v: v.reshape(1, -1)
    h = x
    for i in range(1):
        win, wqa, wqb, wuk, wuvt, pk = _prep_weights(w_in[i], w_uq[i], w_ukv[i])
        qt, k, vt, oc = _inproj(h, pos, invf, row(attn_norm_g[i]), win, row(q_norm_g[i]), wqa,
                                wqb, row(kv_norm_g[i]), wuk, pk, wuvt,
                                conv_w[i].reshape(CONV_K, CONV_WIDTH))
        ot = _attention(qt, k, vt)

        zrow = lambda n: jnp.zeros((n, D), F32)
        wr = jnp.concatenate(
            [w_group_router[i].T, zrow(ROUTER_E0 - N_GROUPS), w_expert_router[i].T,
             zrow(ROUTER_ROWS - ROUTER_E0 - N_EXPERTS)], axis=0)
        wrh = wr.astype(BF16)
        wrhl = jnp.concatenate([wrh, (wr - wrh.astype(F32)).astype(BF16)], axis=0)
        br = jnp.concatenate(
            [b_group_router[i], jnp.zeros((ROUTER_E0 - N_GROUPS,), F32), b_expert_router[i],
             jnp.zeros((ROUTER_ROWS - ROUTER_E0 - N_EXPERTS,), F32)]).reshape(-1, 1)
        h1, xn, meta_i, meta_w, cnt = _outproj(h, ot, oc, w_out[i].astype(BF16),
                                              row(moe_norm_g[i]), wrhl, br)

        n_blocks = (2 * N) // MOE_BLK + N_EXPERTS
        counts = cnt[:, 0].astype(jnp.int32)
        padded = (counts + MOE_BLK - 1) // MOE_BLK * MOE_BLK
        pad_ends = jnp.cumsum(padded)
        pad_starts = pad_ends - padded
        eid = jnp.swapaxes(meta_i[:, 0:2, :], 1, 2)
        rank = jnp.swapaxes(meta_i[:, 2:4, :], 1, 2)
        wt = jnp.swapaxes(meta_w[:, 0:2, :], 1, 2).reshape(N, 2)
        sel = eid[..., None] == jnp.arange(N_EXPERTS, dtype=jnp.int32)
        dest = (jnp.sum(jnp.where(sel, pad_starts, 0), axis=-1) + rank).reshape(2 * N)
        blk_start = jnp.arange(n_blocks, dtype=jnp.int32) * MOE_BLK
        blk_e = jnp.minimum(jnp.sum((pad_ends[None, :] <= blk_start[:, None]).astype(jnp.int32),
                                    axis=1), N_EXPERTS - 1)
        n_used = (pad_ends[-1:] // MOE_BLK).astype(jnp.int32)

        xb = _dispatch(dest, pad_ends, padded, n_used, xn.reshape(N, D), n_blocks * MOE_BLK)
        yb = _experts(blk_e, n_used, xb, w_gate[i], w_up[i], w_down[i])
        out = _final(dest, h1.reshape(N, D), p[i].reshape(N, PLE_DIM), wt, yb,
                     row(ple_norm_g[i]), w_ple_gate[i].astype(BF16), row(b_ple_gate[i]),
                     w_ple_proj[i].astype(BF16), row(final_norm_g))
        h = out.reshape(B, S, D)
    return h
```

```python
import functools
import math

import jax
import jax.numpy as jnp
from jax import lax
from jax.experimental import pallas as pl
from jax.experimental.pallas import tpu as pltpu

D_MODEL = 1024
PLE_DIM = 256
MLA_WIDTH = 512
CONV_WIDTH = 512
N_HEADS = 8
V_HEAD_DIM = 64
QK_NOPE_DIM = 64
QK_ROPE_DIM = 32
Q_LORA = 384
KV_LORA = 256
CONV_K = 3
N_GROUPS = 4
EXPERTS_PER_GROUP = 8
N_EXPERTS = N_GROUPS * EXPERTS_PER_GROUP
EXPERT_FF = 512
ROPE_BASE = 10000.0
EPS = 1e-6

HEAD_PAD = 128
V_ROWS = 80
QK_DIM = QK_NOPE_DIM + QK_ROPE_DIM
HALF_ROPE = QK_ROPE_DIM // 2

_C_Q = 0
_C_KV = _C_Q + Q_LORA
_C_BG = _C_KV + KV_LORA
_C_CG = _C_BG + CONV_WIDTH
_C_U = _C_CG + CONV_WIDTH
_C_KR = _C_U + CONV_WIDTH
IN_COLS_PAD = _C_KR + HEAD_PAD

TM = 256
TM_IN = 512
TQ = 512
TK = 256
ATT_HG = 4
ATT_TQC = 256
MOE_BLK = 512
ROUTER_E0 = 8
ROUTER_ROWS = 48
NEG = -1e30
Q_SCALE = (QK_DIM ** -0.5) * math.log2(math.e)
VMEM_LIMIT = 48 * 1024 * 1024

F32 = jnp.float32
BF16 = jnp.bfloat16


def _rms(x, g):
    return x * lax.rsqrt(jnp.mean(x * x, axis=-1, keepdims=True) + EPS) * g


def _dot(a, b):
    return jnp.dot(a, b, preferred_element_type=F32)


def _dot_nt(a, b):
    return lax.dot_general(a, b, (((1,), (1,)), ((), ())), preferred_element_type=F32)


def _dot_tn(a, b):
    return lax.dot_general(a, b, (((0,), (0,)), ((), ())), preferred_element_type=F32)


def _inproj_kernel(x_ref, pos_ref, invf_ref, g_ref, win_ref, qg_ref, wqa_ref, wqb_ref, kvg_ref,
                   wuk_ref, pk_ref, wuvt_ref, convw_ref,
                   qt_ref, k_ref, vt_ref, oc_ref, carry_ref, ext_ref):
    tm = x_ref.shape[1]
    x = x_ref[0]
    xn = _rms(x, g_ref[...])
    z = _dot(xn.astype(BF16), win_ref[...])

    ang = invf_ref[...] * pos_ref[0]
    cos = jnp.cos(ang)
    sin = jnp.sin(ang)
    cos2 = jnp.concatenate([cos, cos], axis=0)
    sin2 = jnp.concatenate([sin, sin], axis=0)

    cqn = _rms(z[:, _C_Q:_C_Q + Q_LORA], qg_ref[...]).astype(BF16)
    qa = _dot_nt(wqa_ref[...], cqn)
    qb = _dot_nt(wqb_ref[...], cqn)
    for h in range(N_HEADS):
        r0 = h * HEAD_PAD
        nope = qa[r0:r0 + QK_NOPE_DIM]
        rope = (qa[r0 + QK_NOPE_DIM:r0 + QK_DIM] * cos2
                + qb[h * QK_ROPE_DIM:(h + 1) * QK_ROPE_DIM] * sin2)
        qh = jnp.concatenate([nope, rope, qa[r0 + QK_DIM:r0 + HEAD_PAD]], axis=0) * Q_SCALE
        qt_ref[0, r0:r0 + HEAD_PAD, :] = qh.astype(BF16)

    kvn = _rms(z[:, _C_KV:_C_KV + KV_LORA], kvg_ref[...]).astype(BF16)
    krt = z[:, _C_KR:_C_KR + HEAD_PAD].T
    krot = krt[0:QK_ROPE_DIM] * cos2 + krt[QK_ROPE_DIM:2 * QK_ROPE_DIM] * sin2
    k_ref[0] = (_dot(kvn, wuk_ref[...]) + _dot_tn(krot.astype(BF16), pk_ref[...])).astype(BF16)
    vt = _dot_nt(wuvt_ref[...], kvn).astype(BF16)
    ones_row = jnp.where(lax.broadcasted_iota(jnp.int32, (V_ROWS - V_HEAD_DIM, tm), 0) == 0,
                         1.0, 0.0).astype(BF16)
    for h in range(N_HEADS):
        vt_ref[0, h * V_ROWS:h * V_ROWS + V_HEAD_DIM, :] = vt[h * V_HEAD_DIM:(h + 1) * V_HEAD_DIM]
        vt_ref[0, h * V_ROWS + V_HEAD_DIM:(h + 1) * V_ROWS, :] = ones_row

    @pl.when(pl.program_id(1) == 0)
    def _():
        carry_ref[...] = jnp.zeros_like(carry_ref)

    cu = z[:, _C_CG:_C_CG + CONV_WIDTH] * z[:, _C_U:_C_U + CONV_WIDTH]
    ext_ref[0:8, :] = carry_ref[...]
    ext_ref[8:8 + tm, :] = cu
    cu1 = ext_ref[7:7 + tm, :]
    cu2 = ext_ref[6:6 + tm, :]
    w = convw_ref[...]
    y = w[0:1] * cu2 + w[1:2] * cu1 + w[2:3] * cu
    oc_ref[0] = (z[:, _C_BG:_C_BG + CONV_WIDTH] * y).astype(BF16)
    carry_ref[...] = ext_ref[tm:tm + 8, :]


def _inproj(x, pos, invf, g, win, qg, wqa, wqb, kvg, wuk, pk, wuvt, convw):
    B, S, D = x.shape
    tm = TM_IN
    full = lambda a: pl.BlockSpec(a.shape, lambda b, i: (0,) * a.ndim)
    return pl.pallas_call(
        _inproj_kernel,
        grid=(B, S // tm),
        in_specs=[
            pl.BlockSpec((1, tm, D), lambda b, i: (b, i, 0)),
            pl.BlockSpec((1, 1, tm), lambda b, i: (b, 0, i)),
            full(invf), full(g), full(win), full(qg), full(wqa), full(wqb), full(kvg), full(wuk),
            full(pk), full(wuvt), full(convw),
        ],
        out_specs=[
            pl.BlockSpec((1, N_HEADS * HEAD_PAD, tm), lambda b, i: (b, 0, i)),
            pl.BlockSpec((1, tm, N_HEADS * HEAD_PAD), lambda b, i: (b, i, 0)),
            pl.BlockSpec((1, N_HEADS * V_ROWS, tm), lambda b, i: (b, 0, i)),
            pl.BlockSpec((1, tm, CONV_WIDTH), lambda b, i: (b, i, 0)),
        ],
        out_shape=[
            jax.ShapeDtypeStruct((B, N_HEADS * HEAD_PAD, S), BF16),
            jax.ShapeDtypeStruct((B, S, N_HEADS * HEAD_PAD), BF16),
            jax.ShapeDtypeStruct((B, N_HEADS * V_ROWS, S), BF16),
            jax.ShapeDtypeStruct((B, S, CONV_WIDTH), BF16),
        ],
        scratch_shapes=[pltpu.VMEM((8, CONV_WIDTH), F32), pltpu.VMEM((tm + 8, CONV_WIDTH), F32)],
        compiler_params=pltpu.CompilerParams(
            dimension_semantics=("arbitrary", "arbitrary"), vmem_limit_bytes=VMEM_LIMIT),
        name="inproj",
    )(x, pos, invf, g, win, qg, wqa, wqb, kvg, wuk, pk, wuvt, convw)


def _attn_kernel(qt_ref, k_ref, vt_ref, o_ref, m_ref, acc_ref, sa_ref, sb_ref):
    tq = qt_ref.shape[2]
    tk = TK
    assert tq == 2 * tk and ATT_TQC == tk
    i = pl.program_id(2)
    m_ref[...] = jnp.full_like(m_ref, NEG)
    acc_ref[...] = jnp.zeros_like(acc_ref)

    def scores(j, s_ref, g, col0=0):
        k0 = pl.multiple_of(j * tk, tk)
        qt = qt_ref[0, g * HEAD_PAD:(g + 1) * HEAD_PAD, col0:]
        s_ref[g, :, col0:] = _dot(k_ref[0, pl.ds(k0, tk), g * HEAD_PAD:(g + 1) * HEAD_PAD], qt)

    def softmax_pv(j, s_ref, g, diag):
        k0 = pl.multiple_of(j * tk, tk)
        vt = vt_ref[0, g * V_ROWS:(g + 1) * V_ROWS, pl.ds(k0, tk)]
        for c in range(tq // ATT_TQC):
            if diag is not None and c < diag:
                continue
            cols = slice(c * ATT_TQC, (c + 1) * ATT_TQC)
            s = s_ref[g, :, cols]
            if diag is not None and c == diag:
                krow = lax.broadcasted_iota(jnp.int32, (tk, ATT_TQC), 0)
                qcol = lax.broadcasted_iota(jnp.int32, (tk, ATT_TQC), 1)
                s = jnp.where(krow <= qcol, s, NEG)
            m_old = m_ref[g, :, cols]
            m_new = jnp.maximum(m_old, jnp.max(s, axis=0, keepdims=True))
            alpha = jnp.exp2(m_old - m_new)
            p = jnp.exp2((s - m_new).astype(BF16))
            acc_ref[g, :, cols] = alpha * acc_ref[g, :, cols] + _dot(vt, p)
            m_ref[g, :, cols] = m_new

    def stage(j_next, s_next, j, s_cur, diag=None, next_col0=0):
        for g in range(ATT_HG):
            if j_next is not None:
                scores(j_next, s_next, g, next_col0)
            softmax_pv(j, s_cur, g, diag)

    for g in range(ATT_HG):
        scores(0, sa_ref, g)

    def pair(j):
        stage(j + 1, sb_ref, j, sa_ref)
        stage(j + 2, sa_ref, j + 1, sb_ref)

    def body(t, c):
        pair(4 * t)
        pair(4 * t + 2)
        return c

    lax.fori_loop(0, i // 2, body, 0)

    @pl.when(i % 2 == 1)
    def _():
        pair(2 * i - 2)

    n_full = 2 * i
    stage(n_full + 1, sb_ref, n_full, sa_ref, diag=0, next_col0=ATT_TQC)
    stage(None, None, n_full + 1, sb_ref, diag=1)
    for g in range(ATT_HG):
        o_ref[0, g * V_HEAD_DIM:(g + 1) * V_HEAD_DIM, :] = (
            acc_ref[g, 0:V_HEAD_DIM, :] / acc_ref[g, V_HEAD_DIM:V_HEAD_DIM + 1, :]).astype(o_ref.dtype)


def _attention(qt, k, vt):
    B, _, S = qt.shape
    hg = ATT_HG
    return pl.pallas_call(
        _attn_kernel,
        grid=(B, N_HEADS // hg, S // TQ),
        in_specs=[
            pl.BlockSpec((1, hg * HEAD_PAD, TQ), lambda b, h, i: (b, h, i)),
            pl.BlockSpec((1, S, hg * HEAD_PAD), lambda b, h, i: (b, 0, h)),
            pl.BlockSpec((1, hg * V_ROWS, S), lambda b, h, i: (b, h, 0)),
        ],
        out_specs=pl.BlockSpec((1, hg * V_HEAD_DIM, TQ), lambda b, h, i: (b, h, i)),
        out_shape=jax.ShapeDtypeStruct((B, MLA_WIDTH, S), BF16),
        scratch_shapes=[pltpu.VMEM((hg, 1, TQ), F32), pltpu.VMEM((hg, V_ROWS, TQ), F32),
                        pltpu.VMEM((hg, TK, TQ), F32), pltpu.VMEM((hg, TK, TQ), F32)],
        compiler_params=pltpu.CompilerParams(
            dimension_semantics=("arbitrary", "arbitrary", "arbitrary"),
            vmem_limit_bytes=VMEM_LIMIT),
        name="attn",
    )(qt, k, vt)


def _outproj_kernel(x_ref, ot_ref, oc_ref, wo_ref, g_ref, wrhl_ref, br_ref,
                    h1_ref, xn_ref, mi_ref, mw_ref, cnt_ref, carry_ref):
    tm = x_ref.shape[1]

    @pl.when((pl.program_id(0) == 0) & (pl.program_id(1) == 0))
    def _():
        carry_ref[...] = jnp.zeros_like(carry_ref)

    attn = (_dot_tn(ot_ref[0], wo_ref[0:MLA_WIDTH, :])
            + _dot(oc_ref[0], wo_ref[MLA_WIDTH:MLA_WIDTH + CONV_WIDTH, :]))
    h1 = x_ref[0] + attn
    h1_ref[0] = h1
    xn = _rms(h1, g_ref[...])
    xn_ref[0] = xn

    xh = xn.astype(BF16)
    xl = (xn - xh.astype(F32)).astype(BF16)
    hl = _dot_nt(wrhl_ref[...], xh)
    logits = (hl[0:ROUTER_ROWS] + hl[ROUTER_ROWS:2 * ROUTER_ROWS]
              + _dot_nt(wrhl_ref[0:ROUTER_ROWS, :], xl) + br_ref[...])
    big = jnp.int32(1 << 20)

    grow = lax.broadcasted_iota(jnp.int32, (ROUTER_E0, tm), 0)
    glog = jnp.where(grow < N_GROUPS, logits[0:ROUTER_E0], NEG)
    gmax = jnp.max(glog, axis=0, keepdims=True)
    gsum = jnp.sum(jnp.exp(glog - gmax), axis=0, keepdims=True)
    g_p = 1.0 / gsum
    g_idx = jnp.min(jnp.where(glog == gmax, grow, big), axis=0, keepdims=True)

    erow = lax.broadcasted_iota(jnp.int32, (N_EXPERTS, tm), 0)
    e_lo = g_idx * EXPERTS_PER_GROUP
    in_group = (erow >= e_lo) & (erow < e_lo + EXPERTS_PER_GROUP)
    elog = jnp.where(in_group, logits[ROUTER_E0:ROUTER_E0 + N_EXPERTS], NEG)
    emax = jnp.max(elog, axis=0, keepdims=True)
    esum = jnp.sum(jnp.exp(elog - emax), axis=0, keepdims=True)
    e1 = jnp.min(jnp.where(elog == emax, erow, big), axis=0, keepdims=True)
    elog2 = jnp.where(erow == e1, NEG, elog)
    emax2 = jnp.max(elog2, axis=0, keepdims=True)
    e2 = jnp.min(jnp.where(elog2 == emax2, erow, big), axis=0, keepdims=True)
    p1 = 1.0 / esum
    p2 = jnp.exp(emax2 - emax) / esum
    psum = p1 + p2
    w1 = g_p * (p1 / psum)
    w2 = g_p * (p2 / psum)

    oh1 = erow == e1
    oh2 = erow == e2
    oh = jnp.where(oh1 | oh2, 1.0, 0.0)
    srow = lax.broadcasted_iota(jnp.int32, (tm, tm), 0)
    scol = lax.broadcasted_iota(jnp.int32, (tm, tm), 1)
    earlier = jnp.where(srow < scol, 1.0, 0.0).astype(BF16)
    cum = _dot(oh.astype(BF16), earlier) + carry_ref[:, 0:1]
    r1 = jnp.sum(jnp.where(oh1, cum, 0.0), axis=0, keepdims=True)
    r2 = jnp.sum(jnp.where(oh2, cum, 0.0), axis=0, keepdims=True)
    carry_ref[...] = carry_ref[...] + jnp.sum(oh, axis=1, keepdims=True)
    cnt_ref[...] = carry_ref[...]

    mrow = lax.broadcasted_iota(jnp.int32, (8, tm), 0)
    r1i = r1.astype(jnp.int32)
    r2i = r2.astype(jnp.int32)
    mi_ref[0] = jnp.where(mrow == 0, e1, jnp.where(mrow == 1, e2, jnp.where(
        mrow == 2, r1i, jnp.where(mrow == 3, r2i, 0))))
    mw_ref[0] = jnp.where(mrow == 0, w1, jnp.where(mrow == 1, w2, 0.0))


def _outproj(x, ot, oc, wo, g, wrhl, br):
    B, S, D = x.shape
    tm = TM
    full = lambda a: pl.BlockSpec(a.shape, lambda b, i: (0,) * a.ndim)
    tile = lambda w: pl.BlockSpec((1, tm, w), lambda b, i: (b, i, 0))
    meta = pl.BlockSpec((1, 8, tm), lambda b, i: (b, 0, i))
    return pl.pallas_call(
        _outproj_kernel,
        grid=(B, S // tm),
        in_specs=[
            tile(D),
            pl.BlockSpec((1, MLA_WIDTH, tm), lambda b, i: (b, 0, i)),
            tile(CONV_WIDTH),
            full(wo), full(g), full(wrhl), full(br),
        ],
        out_specs=[tile(D), tile(D), meta, meta,
                   pl.BlockSpec((N_EXPERTS, HEAD_PAD), lambda b, i: (0, 0))],
        out_shape=[
            jax.ShapeDtypeStruct((B, S, D), F32),
            jax.ShapeDtypeStruct((B, S, D), F32),
            jax.ShapeDtypeStruct((B, 8, S), jnp.int32),
            jax.ShapeDtypeStruct((B, 8, S), F32),
            jax.ShapeDtypeStruct((N_EXPERTS, HEAD_PAD), F32),
        ],
        scratch_shapes=[pltpu.VMEM((N_EXPERTS, HEAD_PAD), F32)],
        compiler_params=pltpu.CompilerParams(
            dimension_semantics=("arbitrary", "arbitrary"), vmem_limit_bytes=VMEM_LIMIT),
        name="outproj",
    )(x, ot, oc, wo, g, wrhl, br)


def _dispatch_kernel(dest_ref, pend_ref, padded_ref, nused_ref, xn_ref, xb_ref, zero_ref, sem,
                     zsem):
    tm = xn_ref.shape[0]
    i = pl.program_id(0)
    base = i * tm

    @pl.when(i == 0)
    def _():
        zero_ref[...] = jnp.zeros_like(zero_ref)

        def zero_copy(start):
            start = pl.multiple_of(start, MOE_BLK)
            return pltpu.make_async_copy(zero_ref, xb_ref.at[pl.ds(start, MOE_BLK)], zsem)

        def tail_start(j, c):
            zero_copy(j * MOE_BLK).start()
            return c

        def tail_wait(j, c):
            zero_copy(j * MOE_BLK).wait()
            return c

        n_blocks = xb_ref.shape[0] // MOE_BLK
        for e in range(N_EXPERTS):
            @pl.when(padded_ref[e] > 0)
            def _():
                zero_copy(pend_ref[e] - MOE_BLK).start()
        lax.fori_loop(nused_ref[0], n_blocks, tail_start, 0)
        for e in range(N_EXPERTS):
            @pl.when(padded_ref[e] > 0)
            def _():
                zero_copy(pend_ref[e] - MOE_BLK).wait()
        lax.fori_loop(nused_ref[0], n_blocks, tail_wait, 0)

    def row_copy(r, d):
        return pltpu.make_async_copy(xn_ref.at[pl.ds(r, 1)], xb_ref.at[pl.ds(d, 1)], sem)

    for r in range(tm):
        t = (base + r) * 2
        row_copy(r, dest_ref[t]).start()
        row_copy(r, dest_ref[t + 1]).start()
    for _ in range(2):
        pltpu.make_async_copy(xn_ref, xb_ref.at[pl.ds(0, tm)], sem).wait()


def _dispatch(dest, pad_ends, padded, n_used, xn, n_rows):
    N, D = xn.shape
    tm = TM
    return pl.pallas_call(
        _dispatch_kernel,
        grid_spec=pltpu.PrefetchScalarGridSpec(
            num_scalar_prefetch=4,
            grid=(N // tm,),
            in_specs=[pl.BlockSpec((tm, D), lambda i, *_: (i, 0))],
            out_specs=pl.BlockSpec(memory_space=pl.ANY),
            scratch_shapes=[pltpu.VMEM((MOE_BLK, D), F32), pltpu.SemaphoreType.DMA,
                            pltpu.SemaphoreType.DMA],
        ),
        out_shape=jax.ShapeDtypeStruct((n_rows, D), F32),
        compiler_params=pltpu.CompilerParams(
            dimension_semantics=("arbitrary",), vmem_limit_bytes=VMEM_LIMIT),
        name="dispatch",
    )(dest, pad_ends, padded, n_used, xn)


def _expert_kernel(blke_ref, nused_ref, xb_ref, wg_ref, wu_ref, wd_ref, yb_ref,
                   wgs_ref, wus_ref, wds_ref):
    i = pl.program_id(0)
    used = i < nused_ref[0]
    changed = (i == 0) | (blke_ref[i] != blke_ref[jnp.maximum(i - 1, 0)])

    @pl.when(used & changed)
    def _():
        wgs_ref[...] = wg_ref[0].astype(BF16)
        wus_ref[...] = wu_ref[0].astype(BF16)
        wds_ref[...] = wd_ref[0].astype(BF16)

    @pl.when(used)
    def _():
        xb = xb_ref[...].astype(BF16)
        half = EXPERT_FF // 2
        g0 = _dot(xb, wgs_ref[:, :half])
        u0 = _dot(xb, wus_ref[:, :half])
        g1 = _dot(xb, wgs_ref[:, half:])
        u1 = _dot(xb, wus_ref[:, half:])
        h0 = ((g0 * jax.nn.sigmoid(g0)) * u0).astype(BF16)
        y = _dot(h0, wds_ref[:half, :])
        h1 = ((g1 * jax.nn.sigmoid(g1)) * u1).astype(BF16)
        yb_ref[...] = y + _dot(h1, wds_ref[half:, :])

    @pl.when(jnp.logical_not(used))
    def _():
        yb_ref[...] = jnp.zeros_like(yb_ref)


def _experts(blk_e, n_used, xb, wg, wu, wd):
    P, D = xb.shape
    blk = MOE_BLK
    return pl.pallas_call(
        _expert_kernel,
        grid_spec=pltpu.PrefetchScalarGridSpec(
            num_scalar_prefetch=2,
            grid=(P // blk,),
            in_specs=[
                pl.BlockSpec((blk, D), lambda i, be, nu: (jnp.minimum(i, nu[0] - 1), 0)),
                pl.BlockSpec((1, D, EXPERT_FF), lambda i, be, nu: (be[i], 0, 0)),
                pl.BlockSpec((1, D, EXPERT_FF), lambda i, be, nu: (be[i], 0, 0)),
                pl.BlockSpec((1, EXPERT_FF, D), lambda i, be, nu: (be[i], 0, 0)),
            ],
            out_specs=pl.BlockSpec((blk, D), lambda i, be, nu: (i, 0)),
            scratch_shapes=[pltpu.VMEM((D, EXPERT_FF), BF16), pltpu.VMEM((D, EXPERT_FF), BF16),
                            pltpu.VMEM((EXPERT_FF, D), BF16)],
        ),
        out_shape=jax.ShapeDtypeStruct((P, D), F32),
        compiler_params=pltpu.CompilerParams(
            dimension_semantics=("arbitrary",), vmem_limit_bytes=VMEM_LIMIT),
        name="experts",
    )(blk_e, n_used, xb, wg, wu, wd)


def _final_kernel(dest_ref, h1_ref, p_ref, wt_ref, yb_ref, gp_ref, wg_ref, bg_ref, wp_ref,
                  gf_ref, out_ref, buf_ref, sem):
    tm = h1_ref.shape[0] // 2
    i = pl.program_id(0)

    def row_copy(slot, k, r, d):
        return pltpu.make_async_copy(yb_ref.at[pl.ds(d, 1)], buf_ref.at[slot, k, pl.ds(r, 1)],
                                     sem.at[slot])

    def issue(tile, slot):
        for r in range(tm):
            t = (tile * tm + r) * 2
            row_copy(slot, 0, r, dest_ref[t]).start()
            row_copy(slot, 1, r, dest_ref[t + 1]).start()

    def wait(slot):
        for k in range(2):
            pltpu.make_async_copy(yb_ref.at[pl.ds(0, tm)], buf_ref.at[slot, k], sem.at[slot]).wait()

    def compute(slot):
        rows = slice(slot * tm, (slot + 1) * tm)
        pe = _dot(p_ref[rows, :].astype(BF16), wp_ref[...])
        wt = wt_ref[rows, :]
        h2 = h1_ref[rows, :] + wt[:, 0:1] * buf_ref[slot, 0] + wt[:, 1:2] * buf_ref[slot, 1]
        n = _rms(h2, gp_ref[...]).astype(BF16)
        gate = jax.nn.sigmoid(_dot(n, wg_ref[...]) + bg_ref[...])
        h3 = h2 + gate * pe
        out_ref[rows, :] = _rms(h3, gf_ref[...])

    @pl.when(i == 0)
    def _():
        issue(0, 0)

    issue(2 * i + 1, 1)
    wait(0)
    compute(0)

    last = pl.num_programs(0) - 1
    issue(jnp.minimum(2 * i + 2, 2 * last + 1), 0)
    wait(1)
    compute(1)

    @pl.when(i == last)
    def _():
        wait(0)


def _final(dest, h1, p, wt, yb, gp, wg, bg, wp, gf):
    N, D = h1.shape
    tm = TM
    full = lambda a: pl.BlockSpec(a.shape, lambda i, dest: (0,) * a.ndim)
    return pl.pallas_call(
        _final_kernel,
        grid_spec=pltpu.PrefetchScalarGridSpec(
            num_scalar_prefetch=1,
            grid=(N // (2 * tm),),
            in_specs=[
                pl.BlockSpec((2 * tm, D), lambda i, dest: (i, 0)),
                pl.BlockSpec((2 * tm, PLE_DIM), lambda i, dest: (i, 0)),
                pl.BlockSpec((2 * tm, 2), lambda i, dest: (i, 0)),
                pl.BlockSpec(memory_space=pl.ANY),
                full(gp), full(wg), full(bg), full(wp), full(gf),
            ],
            out_specs=pl.BlockSpec((2 * tm, D), lambda i, dest: (i, 0)),
            scratch_shapes=[pltpu.VMEM((2, 2, tm, D), F32), pltpu.SemaphoreType.DMA((2,))],
        ),
        out_shape=jax.ShapeDtypeStruct((N, D), F32),
        compiler_params=pltpu.CompilerParams(
            dimension_semantics=("arbitrary",), vmem_limit_bytes=VMEM_LIMIT),
        name="final",
    )(dest, h1, p, wt, yb, gp, wg, bg, wp, gf)


def _prep_weights(w_in, w_uq, w_ukv):
    cq, ckv, kr, bg, cg, u = jnp.split(
        w_in, [Q_LORA, Q_LORA + KV_LORA, Q_LORA + KV_LORA + QK_ROPE_DIM,
               Q_LORA + KV_LORA + QK_ROPE_DIM + CONV_WIDTH,
               Q_LORA + KV_LORA + QK_ROPE_DIM + 2 * CONV_WIDTH], axis=1)
    kr_rot = jnp.concatenate([-kr[:, HALF_ROPE:], kr[:, :HALF_ROPE]], axis=1)
    pad = jnp.zeros((D_MODEL, HEAD_PAD - 2 * QK_ROPE_DIM), w_in.dtype)
    win = jnp.concatenate([cq, ckv, bg, cg, u, kr, kr_rot, pad], axis=1).astype(BF16)

    zq = jnp.zeros((Q_LORA, N_HEADS, HEAD_PAD - QK_DIM), w_uq.dtype)
    wqa = jnp.concatenate([w_uq, zq], axis=2)
    rope = w_uq[:, :, QK_NOPE_DIM:]
    wqb = jnp.concatenate([-rope[:, :, HALF_ROPE:], rope[:, :, :HALF_ROPE]], axis=2)
    wqa = wqa.reshape(Q_LORA, N_HEADS * HEAD_PAD).T.astype(BF16)
    wqb = wqb.reshape(Q_LORA, N_HEADS * QK_ROPE_DIM).T.astype(BF16)

    zk = jnp.zeros((KV_LORA, N_HEADS, HEAD_PAD - QK_NOPE_DIM), w_ukv.dtype)
    wuk = jnp.concatenate([w_ukv[:, :, :QK_NOPE_DIM], zk], axis=2)
    wuk = wuk.reshape(KV_LORA, N_HEADS * HEAD_PAD).astype(BF16)
    wuvt = w_ukv[:, :, QK_NOPE_DIM:].reshape(KV_LORA, MLA_WIDTH).T.astype(BF16)

    src = jnp.arange(QK_ROPE_DIM)[:, None]
    dst = jnp.arange(N_HEADS * HEAD_PAD)[None, :]
    pk = (dst % HEAD_PAD - QK_NOPE_DIM == src).astype(BF16)
    return win, wqa, wqb, wuk, wuvt, pk


def kernel(x, p, positions, attn_norm_g, w_in, q_norm_g, w_uq, kv_norm_g, w_ukv, conv_w, w_out,
           moe_norm_g, w_group_router, b_group_router, w_expert_router, b_expert_router,
           w_gate, w_up, w_down, ple_norm_g, w_ple_gate, b_ple_gate, w_ple_proj, final_norm_g):
    B, S, D = x.shape
    N = B * S
    assert w_in.shape[0] == 1, "single-layer trunk: the final norm is fused into the layer"
    pos = positions.astype(F32).reshape(B, 1, S)
    invf = (ROPE_BASE ** (-jnp.arange(0, QK_ROPE_DIM, 2, dtype=F32) / QK_ROPE_DIM)).reshape(-1, 1)
    row = lambda v: v.reshape(1, -1)
    h = x
    for i in range(1):
        win, wqa, wqb, wuk, wuvt, pk = _prep_weights(w_in[i], w_uq[i], w_ukv[i])
        qt, k, vt, oc = _inproj(h, pos, invf, row(attn_norm_g[i]), win, row(q_norm_g[i]), wqa,
                                wqb, row(kv_norm_g[i]), wuk, pk, wuvt,
                                conv_w[i].reshape(CONV_K, CONV_WIDTH))
        ot = _attention(qt, k, vt)

        zrow = lambda n: jnp.zeros((n, D), F32)
        wr = jnp.concatenate(
            [w_group_router[i].T, zrow(ROUTER_E0 - N_GROUPS), w_expert_router[i].T,
             zrow(ROUTER_ROWS - ROUTER_E0 - N_EXPERTS)], axis=0)
        wrh = wr.astype(BF16)
        wrhl = jnp.concatenate([wrh, (wr - wrh.astype(F32)).astype(BF16)], axis=0)
        br = jnp.concatenate(
            [b_group_router[i], jnp.zeros((ROUTER_E0 - N_GROUPS,), F32), b_expert_router[i],
             jnp.zeros((ROUTER_ROWS - ROUTER_E0 - N_EXPERTS,), F32)]).reshape(-1, 1)
        h1, xn, meta_i, meta_w, cnt = _outproj(h, ot, oc, w_out[i].astype(BF16),
                                              row(moe_norm_g[i]), wrhl, br)

        n_blocks = (2 * N) // MOE_BLK + N_EXPERTS
        counts = cnt[:, 0].astype(jnp.int32)
        padded = (counts + MOE_BLK - 1) // MOE_BLK * MOE_BLK
        pad_ends = jnp.cumsum(padded)
        pad_starts = pad_ends - padded
        eid = jnp.swapaxes(meta_i[:, 0:2, :], 1, 2)
        rank = jnp.swapaxes(meta_i[:, 2:4, :], 1, 2)
        wt = jnp.swapaxes(meta_w[:, 0:2, :], 1, 2).reshape(N, 2)
        sel = eid[..., None] == jnp.arange(N_EXPERTS, dtype=jnp.int32)
        dest = (jnp.sum(jnp.where(sel, pad_starts, 0), axis=-1) + rank).reshape(2 * N)
        blk_start = jnp.arange(n_blocks, dtype=jnp.int32) * MOE_BLK
        blk_e = jnp.minimum(jnp.sum((pad_ends[None, :] <= blk_start[:, None]).astype(jnp.int32),
                                    axis=1), N_EXPERTS - 1)
        n_used = (pad_ends[-1:] // MOE_BLK).astype(jnp.int32)

        xb = _dispatch(dest, pad_ends, padded, n_used, xn.reshape(N, D), n_blocks * MOE_BLK)
        yb = _experts(blk_e, n_used, xb, w_gate[i], w_up[i], w_down[i])
        out = _final(dest, h1.reshape(N, D), p[i].reshape(N, PLE_DIM), wt, yb,
                     row(ple_norm_g[i]), w_ple_gate[i].astype(BF16), row(b_ple_gate[i]),
                     w_ple_proj[i].astype(BF16), row(final_norm_g))
        h = out.reshape(B, S, D)
    return h
```

```python
import functools
import math

import jax
import jax.numpy as jnp
from jax import lax
from jax.experimental import pallas as pl
from jax.experimental.pallas import tpu as pltpu

D_MODEL = 1024
PLE_DIM = 256
MLA_WIDTH = 512
CONV_WIDTH = 512
N_HEADS = 8
V_HEAD_DIM = 64
QK_NOPE_DIM = 64
QK_ROPE_DIM = 32
Q_LORA = 384
KV_LORA = 256
CONV_K = 3
N_GROUPS = 4
EXPERTS_PER_GROUP = 8
N_EXPERTS = N_GROUPS * EXPERTS_PER_GROUP
EXPERT_FF = 512
ROPE_BASE = 10000.0
EPS = 1e-6

HEAD_PAD = 128
V_ROWS = 80
QK_DIM = QK_NOPE_DIM + QK_ROPE_DIM
HALF_ROPE = QK_ROPE_DIM // 2

_C_Q = 0
_C_KV = _C_Q + Q_LORA
_C_BG = _C_KV + KV_LORA
_C_CG = _C_BG + CONV_WIDTH
_C_U = _C_CG + CONV_WIDTH
_C_KR = _C_U + CONV_WIDTH
IN_COLS_PAD = _C_KR + HEAD_PAD

TM = 256
TM_IN = 512
TQ = 512
TK = 256
ATT_HG = 4
ATT_TQC = 256
MOE_BLK = 512
ROUTER_E0 = 8
ROUTER_ROWS = 48
NEG = -1e30
Q_SCALE = (QK_DIM ** -0.5) * math.log2(math.e)
VMEM_LIMIT = 48 * 1024 * 1024

F32 = jnp.float32
BF16 = jnp.bfloat16


def _rms(x, g):
    return x * lax.rsqrt(jnp.mean(x * x, axis=-1, keepdims=True) + EPS) * g


def _dot(a, b):
    return jnp.dot(a, b, preferred_element_type=F32)


def _dot_nt(a, b):
    return lax.dot_general(a, b, (((1,), (1,)), ((), ())), preferred_element_type=F32)


def _dot_tn(a, b):
    return lax.dot_general(a, b, (((0,), (0,)), ((), ())), preferred_element_type=F32)


def _inproj_kernel(x_ref, pos_ref, invf_ref, g_ref, win_ref, qg_ref, wqa_ref, wqb_ref, kvg_ref,
                   wuk_ref, pk_ref, wuvt_ref, convw_ref,
                   qt_ref, k_ref, vt_ref, oc_ref, carry_ref, ext_ref):
    tm = x_ref.shape[1]
    x = x_ref[0]
    xn = _rms(x, g_ref[...])
    z = _dot(xn.astype(BF16), win_ref[...])

    ang = invf_ref[...] * pos_ref[0]
    cos = jnp.cos(ang)
    sin = jnp.sin(ang)
    cos2 = jnp.concatenate([cos, cos], axis=0)
    sin2 = jnp.concatenate([sin, sin], axis=0)

    cqn = _rms(z[:, _C_Q:_C_Q + Q_LORA], qg_ref[...]).astype(BF16)
    qa = _dot_nt(wqa_ref[...], cqn)
    qb = _dot_nt(wqb_ref[...], cqn)
    for h in range(N_HEADS):
        r0 = h * HEAD_PAD
        nope = qa[r0:r0 + QK_NOPE_DIM]
        rope = (qa[r0 + QK_NOPE_DIM:r0 + QK_DIM] * cos2
                + qb[h * QK_ROPE_DIM:(h + 1) * QK_ROPE_DIM] * sin2)
        qh = jnp.concatenate([nope, rope, qa[r0 + QK_DIM:r0 + HEAD_PAD]], axis=0) * Q_SCALE
        qt_ref[0, r0:r0 + HEAD_PAD, :] = qh.astype(BF16)

    kvn = _rms(z[:, _C_KV:_C_KV + KV_LORA], kvg_ref[...]).astype(BF16)
    krt = z[:, _C_KR:_C_KR + HEAD_PAD].T
    krot = krt[0:QK_ROPE_DIM] * cos2 + krt[QK_ROPE_DIM:2 * QK_ROPE_DIM] * sin2
    k_ref[0] = (_dot(kvn, wuk_ref[...]) + _dot_tn(krot.astype(BF16), pk_ref[...])).astype(BF16)
    vt = _dot_nt(wuvt_ref[...], kvn).astype(BF16)
    ones_row = jnp.where(lax.broadcasted_iota(jnp.int32, (V_ROWS - V_HEAD_DIM, tm), 0) == 0,
                         1.0, 0.0).astype(BF16)
    for h in range(N_HEADS):
        vt_ref[0, h * V_ROWS:h * V_ROWS + V_HEAD_DIM, :] = vt[h * V_HEAD_DIM:(h + 1) * V_HEAD_DIM]
        vt_ref[0, h * V_ROWS + V_HEAD_DIM:(h + 1) * V_ROWS, :] = ones_row

    @pl.when(pl.program_id(1) == 0)
    def _():
        carry_ref[...] = jnp.zeros_like(carry_ref)

    cu = z[:, _C_CG:_C_CG + CONV_WIDTH] * z[:, _C_U:_C_U + CONV_WIDTH]
    ext_ref[0:8, :] = carry_ref[...]
    ext_ref[8:8 + tm, :] = cu
    cu1 = ext_ref[7:7 + tm, :]
    cu2 = ext_ref[6:6 + tm, :]
    w = convw_ref[...]
    y = w[0:1] * cu2 + w[1:2] * cu1 + w[2:3] * cu
    oc_ref[0] = (z[:, _C_BG:_C_BG + CONV_WIDTH] * y).astype(BF16)
    carry_ref[...] = ext_ref[tm:tm + 8, :]


def _inproj(x, pos, invf, g, win, qg, wqa, wqb, kvg, wuk, pk, wuvt, convw):
    B, S, D = x.shape
    tm = TM_IN
    full = lambda a: pl.BlockSpec(a.shape, lambda b, i: (0,) * a.ndim)
    return pl.pallas_call(
        _inproj_kernel,
        grid=(B, S // tm),
        in_specs=[
            pl.BlockSpec((1, tm, D), lambda b, i: (b, i, 0)),
            pl.BlockSpec((1, 1, tm), lambda b, i: (b, 0, i)),
            full(invf), full(g), full(win), full(qg), full(wqa), full(wqb), full(kvg), full(wuk),
            full(pk), full(wuvt), full(convw),
        ],
        out_specs=[
            pl.BlockSpec((1, N_HEADS * HEAD_PAD, tm), lambda b, i: (b, 0, i)),
            pl.BlockSpec((1, tm, N_HEADS * HEAD_PAD), lambda b, i: (b, i, 0)),
            pl.BlockSpec((1, N_HEADS * V_ROWS, tm), lambda b, i: (b, 0, i)),
            pl.BlockSpec((1, tm, CONV_WIDTH), lambda b, i: (b, i, 0)),
        ],
        out_shape=[
            jax.ShapeDtypeStruct((B, N_HEADS * HEAD_PAD, S), BF16),
            jax.ShapeDtypeStruct((B, S, N_HEADS * HEAD_PAD), BF16),
            jax.ShapeDtypeStruct((B, N_HEADS * V_ROWS, S), BF16),
            jax.ShapeDtypeStruct((B, S, CONV_WIDTH), BF16),
        ],
        scratch_shapes=[pltpu.VMEM((8, CONV_WIDTH), F32), pltpu.VMEM((tm + 8, CONV_WIDTH), F32)],
        compiler_params=pltpu.CompilerParams(
            dimension_semantics=("arbitrary", "arbitrary"), vmem_limit_bytes=VMEM_LIMIT),
        name="inproj",
    )(x, pos, invf, g, win, qg, wqa, wqb, kvg, wuk, pk, wuvt, convw)


def _attn_kernel(qt_ref, k_ref, vt_ref, o_ref, z_ref, m_ref, acc_ref, sa_ref, sb_ref, zero_ref,
                 zsem):
    tq = qt_ref.shape[2]
    tk = TK
    assert tq == 2 * tk and ATT_TQC == tk
    i = pl.program_id(2)
    m_ref[...] = jnp.full_like(m_ref, NEG)
    acc_ref[...] = jnp.zeros_like(acc_ref)

    step = ((pl.program_id(0) * pl.num_programs(1) + pl.program_id(1)) * pl.num_programs(2) + i)
    zrows = zero_ref.shape[0]

    @pl.when(step == 0)
    def _():
        zero_ref[...] = jnp.zeros_like(zero_ref)

    zero_copy = pltpu.make_async_copy(
        zero_ref, z_ref.at[pl.ds(pl.multiple_of(step * zrows, 8), zrows)], zsem)
    zero_copy.start()

    def scores(j, s_ref, g, col0=0):
        k0 = pl.multiple_of(j * tk, tk)
        qt = qt_ref[0, g * HEAD_PAD:(g + 1) * HEAD_PAD, col0:]
        s_ref[g, :, col0:] = _dot(k_ref[0, pl.ds(k0, tk), g * HEAD_PAD:(g + 1) * HEAD_PAD], qt)

    def softmax_pv(j, s_ref, g, diag):
        k0 = pl.multiple_of(j * tk, tk)
        vt = vt_ref[0, g * V_ROWS:(g + 1) * V_ROWS, pl.ds(k0, tk)]
        for c in range(tq // ATT_TQC):
            if diag is not None and c < diag:
                continue
            cols = slice(c * ATT_TQC, (c + 1) * ATT_TQC)
            s = s_ref[g, :, cols]
            if diag is not None and c == diag:
                krow = lax.broadcasted_iota(jnp.int32, (tk, ATT_TQC), 0)
                qcol = lax.broadcasted_iota(jnp.int32, (tk, ATT_TQC), 1)
                s = jnp.where(krow <= qcol, s, NEG)
            m_old = m_ref[g, :, cols]
            m_new = jnp.maximum(m_old, jnp.max(s, axis=0, keepdims=True))
            alpha = jnp.exp2(m_old - m_new)
            p = jnp.exp2((s - m_new).astype(BF16))
            acc_ref[g, :, cols] = alpha * acc_ref[g, :, cols] + _dot(vt, p)
            m_ref[g, :, cols] = m_new

    def stage(j_next, s_next, j, s_cur, diag=None, next_col0=0):
        for g in range(ATT_HG):
            if j_next is not None:
                scores(j_next, s_next, g, next_col0)
            softmax_pv(j, s_cur, g, diag)

    for g in range(ATT_HG):
        scores(0, sa_ref, g)

    def pair(j):
        stage(j + 1, sb_ref, j, sa_ref)
        stage(j + 2, sa_ref, j + 1, sb_ref)

    def body(t, c):
        pair(4 * t)
        pair(4 * t + 2)
        return c

    lax.fori_loop(0, i // 2, body, 0)

    @pl.when(i % 2 == 1)
    def _():
        pair(2 * i - 2)

    n_full = 2 * i
    stage(n_full + 1, sb_ref, n_full, sa_ref, diag=0, next_col0=ATT_TQC)
    stage(None, None, n_full + 1, sb_ref, diag=1)
    for g in range(ATT_HG):
        o_ref[0, g * V_HEAD_DIM:(g + 1) * V_HEAD_DIM, :] = (
            acc_ref[g, 0:V_HEAD_DIM, :] / acc_ref[g, V_HEAD_DIM:V_HEAD_DIM + 1, :]).astype(o_ref.dtype)
    zero_copy.wait()


def _attention(qt, k, vt, min_zero_rows):
    B, _, S = qt.shape
    hg = ATT_HG
    grid = (B, N_HEADS // hg, S // TQ)
    n_steps = grid[0] * grid[1] * grid[2]
    zrows = -(-min_zero_rows // (8 * n_steps)) * 8
    return pl.pallas_call(
        _attn_kernel,
        grid=grid,
        in_specs=[
            pl.BlockSpec((1, hg * HEAD_PAD, TQ), lambda b, h, i: (b, h, i)),
            pl.BlockSpec((1, S, hg * HEAD_PAD), lambda b, h, i: (b, 0, h)),
            pl.BlockSpec((1, hg * V_ROWS, S), lambda b, h, i: (b, h, 0)),
        ],
        out_specs=[pl.BlockSpec((1, hg * V_HEAD_DIM, TQ), lambda b, h, i: (b, h, i)),
                   pl.BlockSpec(memory_space=pl.ANY)],
        out_shape=[jax.ShapeDtypeStruct((B, MLA_WIDTH, S), BF16),
                   jax.ShapeDtypeStruct((n_steps * zrows, D_MODEL), F32)],
        scratch_shapes=[pltpu.VMEM((hg, 1, TQ), F32), pltpu.VMEM((hg, V_ROWS, TQ), F32),
                        pltpu.VMEM((hg, TK, TQ), F32), pltpu.VMEM((hg, TK, TQ), F32),
                        pltpu.VMEM((zrows, D_MODEL), F32), pltpu.SemaphoreType.DMA],
        compiler_params=pltpu.CompilerParams(
            dimension_semantics=("arbitrary", "arbitrary", "arbitrary"),
            vmem_limit_bytes=VMEM_LIMIT),
        name="attn",
    )(qt, k, vt)


def _route_tile(xn, wrhl_ref, br_ref, carry_ref):
    tm = xn.shape[0]
    xh = xn.astype(BF16)
    xl = (xn - xh.astype(F32)).astype(BF16)
    hl = _dot_nt(wrhl_ref[...], xh)
    logits = (hl[0:ROUTER_ROWS] + hl[ROUTER_ROWS:2 * ROUTER_ROWS]
              + _dot_nt(wrhl_ref[0:ROUTER_ROWS, :], xl) + br_ref[...])
    big = jnp.int32(1 << 20)

    grow = lax.broadcasted_iota(jnp.int32, (ROUTER_E0, tm), 0)
    glog = jnp.where(grow < N_GROUPS, logits[0:ROUTER_E0], NEG)
    gmax = jnp.max(glog, axis=0, keepdims=True)
    gsum = jnp.sum(jnp.exp(glog - gmax), axis=0, keepdims=True)
    g_p = 1.0 / gsum
    g_idx = jnp.min(jnp.where(glog == gmax, grow, big), axis=0, keepdims=True)

    erow = lax.broadcasted_iota(jnp.int32, (N_EXPERTS, tm), 0)
    e_lo = g_idx * EXPERTS_PER_GROUP
    in_group = (erow >= e_lo) & (erow < e_lo + EXPERTS_PER_GROUP)
    elog = jnp.where(in_group, logits[ROUTER_E0:ROUTER_E0 + N_EXPERTS], NEG)
    emax = jnp.max(elog, axis=0, keepdims=True)
    esum = jnp.sum(jnp.exp(elog - emax), axis=0, keepdims=True)
    e1 = jnp.min(jnp.where(elog == emax, erow, big), axis=0, keepdims=True)
    elog2 = jnp.where(erow == e1, NEG, elog)
    emax2 = jnp.max(elog2, axis=0, keepdims=True)
    e2 = jnp.min(jnp.where(elog2 == emax2, erow, big), axis=0, keepdims=True)
    p1 = 1.0 / esum
    p2 = jnp.exp(emax2 - emax) / esum
    psum = p1 + p2
    w1 = g_p * (p1 / psum)
    w2 = g_p * (p2 / psum)

    oh1 = erow == e1
    oh2 = erow == e2
    oh = jnp.where(oh1 | oh2, 1.0, 0.0)
    srow = lax.broadcasted_iota(jnp.int32, (tm, tm), 0)
    scol = lax.broadcasted_iota(jnp.int32, (tm, tm), 1)
    earlier = jnp.where(srow < scol, 1.0, 0.0).astype(BF16)
    cum = _dot(oh.astype(BF16), earlier) + carry_ref[:, 0:1]
    r1 = jnp.sum(jnp.where(oh1, cum, 0.0), axis=0, keepdims=True)
    r2 = jnp.sum(jnp.where(oh2, cum, 0.0), axis=0, keepdims=True)
    return oh, oh1, oh2, r1, r2, w1, w2


def _outproj_kernel(x_ref, ot_ref, oc_ref, wo_ref, g_ref, wrhl_ref, br_ref, xbz_ref,
                    h1_ref, pos_ref, mw_ref, nb_ref, xb_ref,
                    carry_ref, last_ref, free_ref, xs0_ref, xs1_ref, dv0_ref, dv1_ref,
                    ds0_ref, ds1_ref, rsem, ssem):
    del xbz_ref
    tm = TM
    step = pl.program_id(0) * pl.num_programs(1) + pl.program_id(1)
    last_step = pl.num_programs(0) * pl.num_programs(1) - 1
    xs = (xs0_ref, xs1_ref)
    dv = (dv0_ref, dv1_ref)
    ds = (ds0_ref, ds1_ref)
    spare_row0 = xb_ref.shape[0] - 2 * tm

    def meta_copy(slot):
        return pltpu.make_async_copy(dv[slot], ds[slot], ssem.at[slot])

    def issue_rows(slot):
        for r in range(tm):
            for k in range(2):
                pltpu.make_async_copy(xs[slot].at[pl.ds(r, 1)],
                                      xb_ref.at[pl.ds(ds[slot][k, r], 1)], rsem.at[slot]).start()

    def wait_rows(slot):
        for _ in range(2):
            pltpu.make_async_copy(xs[slot], xb_ref.at[pl.ds(0, tm)], rsem.at[slot]).wait()

    def compute(slot):
        rows = slice(slot * tm, (slot + 1) * tm)
        attn = (_dot_tn(ot_ref[0, :, rows], wo_ref[0:MLA_WIDTH, :])
                + _dot(oc_ref[0, rows, :], wo_ref[MLA_WIDTH:MLA_WIDTH + CONV_WIDTH, :]))
        h1 = x_ref[0, rows, :] + attn
        h1_ref[0, rows, :] = h1
        xn = _rms(h1, g_ref[...])
        xs[slot][...] = xn
        oh, oh1, oh2, r1, r2, w1, w2 = _route_tile(xn, wrhl_ref, br_ref, carry_ref)

        inv_blk = 1.0 / MOE_BLK
        cnt = carry_ref[:, 0:1]
        tile_cnt = jnp.sum(oh, axis=1, keepdims=True)
        nb_before = jnp.floor((cnt + (MOE_BLK - 1)) * inv_blk)
        nb_after = jnp.floor((cnt + tile_cnt + (MOE_BLK - 1)) * inv_blk)
        new = nb_after - nb_before
        erow = lax.broadcasted_iota(jnp.int32, (N_EXPERTS, N_EXPERTS), 0)
        ecol = lax.broadcasted_iota(jnp.int32, (N_EXPERTS, N_EXPERTS), 1)
        lower = jnp.where(ecol < erow, 1.0, 0.0).astype(BF16)
        new_rep = jnp.broadcast_to(new, (N_EXPERTS, HEAD_PAD))
        new_id = free_ref[0:1, :] + _dot(lower, new_rep.astype(BF16))
        free_ref[...] = free_ref[...] + jnp.sum(new, axis=0, keepdims=True)
        last_before = last_ref[...]
        is_new = new_rep > 0.0
        last_ref[...] = jnp.where(is_new, new_id, last_before)
        nb_ref[slot] = jnp.where(is_new, new_id, -1.0).astype(jnp.int32)
        carry_ref[...] = carry_ref[...] + tile_cnt

        def place(ohk, rk):
            bi = jnp.floor(rk * inv_blk)
            blk = jnp.where(bi == nb_before - 1.0, last_before[:, 0:1], new_id[:, 0:1])
            phys = jnp.sum(jnp.where(ohk, blk, 0.0), axis=0, keepdims=True)
            return (phys * MOE_BLK + (rk - bi * MOE_BLK)).astype(jnp.int32)

        mrow = lax.broadcasted_iota(jnp.int32, (8, tm), 0)
        dest = jnp.where(mrow == 0, place(oh1, r1), jnp.where(mrow == 1, place(oh2, r2), 0))
        dv[slot][...] = dest
        pos_ref[0, :, rows] = dest
        mw_ref[0, :, rows] = jnp.where(mrow == 0, w1, jnp.where(mrow == 1, w2, 0.0))
        meta_copy(slot).start()

    @pl.when(step == 0)
    def _():
        carry_ref[...] = jnp.zeros_like(carry_ref)
        free_ref[...] = jnp.zeros_like(free_ref)
        last_ref[...] = jnp.full_like(last_ref, -1.0)
        xs1_ref[...] = jnp.zeros_like(xs1_ref)
        mrow = lax.broadcasted_iota(jnp.int32, (8, tm), 0)
        lane = lax.broadcasted_iota(jnp.int32, (8, tm), 1)
        dv1_ref[...] = jnp.where(mrow < 2, spare_row0 + 2 * lane + mrow, 0)
        meta_copy(1).start()

    meta_copy(1).wait()
    issue_rows(1)
    compute(0)
    wait_rows(1)
    meta_copy(0).wait()
    issue_rows(0)
    compute(1)
    wait_rows(0)

    @pl.when(step == last_step)
    def _():
        meta_copy(1).wait()
        issue_rows(1)
        wait_rows(1)


def _outproj(x, ot, oc, wo, g, wrhl, br, xbz):
    B, S, D = x.shape
    tm2 = 2 * TM
    n_s = S // tm2
    full = lambda a: pl.BlockSpec(a.shape, lambda b, i: (0,) * a.ndim)
    tile = lambda w: pl.BlockSpec((1, tm2, w), lambda b, i: (b, i, 0))
    meta = pl.BlockSpec((1, 8, tm2), lambda b, i: (b, 0, i))
    any_space = pl.BlockSpec(memory_space=pl.ANY)
    return pl.pallas_call(
        _outproj_kernel,
        grid=(B, n_s),
        in_specs=[
            tile(D),
            pl.BlockSpec((1, MLA_WIDTH, tm2), lambda b, i: (b, 0, i)),
            tile(CONV_WIDTH),
            full(wo), full(g), full(wrhl), full(br), any_space,
        ],
        out_specs=[tile(D), meta, meta,
                   pl.BlockSpec((2, N_EXPERTS, HEAD_PAD), lambda b, i: (b * n_s + i, 0, 0)),
                   any_space],
        out_shape=[
            jax.ShapeDtypeStruct((B, S, D), F32),
            jax.ShapeDtypeStruct((B, 8, S), jnp.int32),
            jax.ShapeDtypeStruct((B, 8, S), F32),
            jax.ShapeDtypeStruct((B * S // TM, N_EXPERTS, HEAD_PAD), jnp.int32),
            jax.ShapeDtypeStruct(xbz.shape, xbz.dtype),
        ],
        input_output_aliases={7: 4},
        scratch_shapes=[
            pltpu.VMEM((N_EXPERTS, HEAD_PAD), F32), pltpu.VMEM((N_EXPERTS, HEAD_PAD), F32),
            pltpu.VMEM((8, HEAD_PAD), F32),
            pltpu.VMEM((TM, D), F32), pltpu.VMEM((TM, D), F32),
            pltpu.VMEM((8, TM), jnp.int32), pltpu.VMEM((8, TM), jnp.int32),
            pltpu.SMEM((8, TM), jnp.int32), pltpu.SMEM((8, TM), jnp.int32),
            pltpu.SemaphoreType.DMA((2,)), pltpu.SemaphoreType.DMA((2,)),
        ],
        compiler_params=pltpu.CompilerParams(
            dimension_semantics=("arbitrary", "arbitrary"), vmem_limit_bytes=VMEM_LIMIT),
        name="outproj",
    )(x, ot, oc, wo, g, wrhl, br, xbz)


def _expert_kernel(order_ref, blke_ref, nused_ref, xb_ref, wg_ref, wu_ref, wd_ref, yb_ref,
                   wgs_ref, wus_ref, wds_ref):
    del order_ref
    i = pl.program_id(0)
    used = i < nused_ref[0]
    changed = (i == 0) | (blke_ref[i] != blke_ref[jnp.maximum(i - 1, 0)])

    @pl.when(used & changed)
    def _():
        wgs_ref[...] = wg_ref[0].astype(BF16)
        wus_ref[...] = wu_ref[0].astype(BF16)
        wds_ref[...] = wd_ref[0].astype(BF16)

    @pl.when(used)
    def _():
        xb = xb_ref[...].astype(BF16)
        half = EXPERT_FF // 2
        g0 = _dot(xb, wgs_ref[:, :half])
        u0 = _dot(xb, wus_ref[:, :half])
        g1 = _dot(xb, wgs_ref[:, half:])
        u1 = _dot(xb, wus_ref[:, half:])
        h0 = ((g0 * jax.nn.sigmoid(g0)) * u0).astype(BF16)
        y = _dot(h0, wds_ref[:half, :])
        h1 = ((g1 * jax.nn.sigmoid(g1)) * u1).astype(BF16)
        yb_ref[...] = y + _dot(h1, wds_ref[half:, :])

    @pl.when(jnp.logical_not(used))
    def _():
        yb_ref[...] = jnp.zeros_like(yb_ref)


def _experts(order, blk_e, n_used, xb, wg, wu, wd):
    D = xb.shape[1]
    blk = MOE_BLK
    n_blocks = order.shape[0]
    return pl.pallas_call(
        _expert_kernel,
        grid_spec=pltpu.PrefetchScalarGridSpec(
            num_scalar_prefetch=3,
            grid=(n_blocks,),
            in_specs=[
                pl.BlockSpec((blk, D), lambda i, o, be, nu: (o[jnp.minimum(i, nu[0] - 1)], 0)),
                pl.BlockSpec((1, D, EXPERT_FF), lambda i, o, be, nu: (be[i], 0, 0)),
                pl.BlockSpec((1, D, EXPERT_FF), lambda i, o, be, nu: (be[i], 0, 0)),
                pl.BlockSpec((1, EXPERT_FF, D), lambda i, o, be, nu: (be[i], 0, 0)),
            ],
            out_specs=pl.BlockSpec((blk, D), lambda i, o, be, nu: (o[i], 0)),
            scratch_shapes=[pltpu.VMEM((D, EXPERT_FF), BF16), pltpu.VMEM((D, EXPERT_FF), BF16),
                            pltpu.VMEM((EXPERT_FF, D), BF16)],
        ),
        out_shape=jax.ShapeDtypeStruct((n_blocks * blk, D), F32),
        compiler_params=pltpu.CompilerParams(
            dimension_semantics=("arbitrary",), vmem_limit_bytes=VMEM_LIMIT),
        name="experts",
    )(order, blk_e, n_used, xb, wg, wu, wd)


def _final_kernel(dest_ref, h1_ref, p_ref, wt_ref, yb_ref, gp_ref, wg_ref, bg_ref, wp_ref,
                  gf_ref, out_ref, buf0_ref, buf1_ref, sem):
    tm = h1_ref.shape[0] // 2
    i = pl.program_id(0)
    bufs = (buf0_ref, buf1_ref)

    def row_copy(slot, k, r, d):
        return pltpu.make_async_copy(yb_ref.at[pl.ds(d, 1)], bufs[slot].at[k, pl.ds(r, 1)],
                                     sem.at[slot])

    def issue(tile, slot):
        for r in range(tm):
            t = (tile * tm + r) * 2
            row_copy(slot, 0, r, dest_ref[t]).start()
            row_copy(slot, 1, r, dest_ref[t + 1]).start()

    def wait(slot):
        for k in range(2):
            pltpu.make_async_copy(yb_ref.at[pl.ds(0, tm)], bufs[slot].at[k], sem.at[slot]).wait()

    def compute(slot):
        rows = slice(slot * tm, (slot + 1) * tm)
        pe = _dot(p_ref[rows, :].astype(BF16), wp_ref[...])
        wt = wt_ref[rows, :]
        h2 = h1_ref[rows, :] + wt[:, 0:1] * bufs[slot][0] + wt[:, 1:2] * bufs[slot][1]
        n = _rms(h2, gp_ref[...]).astype(BF16)
        gate = jax.nn.sigmoid(_dot(n, wg_ref[...]) + bg_ref[...])
        h3 = h2 + gate * pe
        out_ref[rows, :] = _rms(h3, gf_ref[...])

    @pl.when(i == 0)
    def _():
        issue(0, 0)

    wait(0)
    issue(2 * i + 1, 1)
    compute(0)

    last = pl.num_programs(0) - 1
    wait(1)
    issue(jnp.minimum(2 * i + 2, 2 * last + 1), 0)
    compute(1)

    @pl.when(i == last)
    def _():
        wait(0)


def _final(dest, h1, p, wt, yb, gp, wg, bg, wp, gf):
    N, D = h1.shape
    tm = TM
    full = lambda a: pl.BlockSpec(a.shape, lambda i, dest: (0,) * a.ndim)
    return pl.pallas_call(
        _final_kernel,
        grid_spec=pltpu.PrefetchScalarGridSpec(
            num_scalar_prefetch=1,
            grid=(N // (2 * tm),),
            in_specs=[
                pl.BlockSpec((2 * tm, D), lambda i, dest: (i, 0)),
                pl.BlockSpec((2 * tm, PLE_DIM), lambda i, dest: (i, 0)),
                pl.BlockSpec((2 * tm, 2), lambda i, dest: (i, 0)),
                pl.BlockSpec(memory_space=pl.ANY),
                full(gp), full(wg), full(bg), full(wp), full(gf),
            ],
            out_specs=pl.BlockSpec((2 * tm, D), lambda i, dest: (i, 0)),
            scratch_shapes=[pltpu.VMEM((2, tm, D), F32), pltpu.VMEM((2, tm, D), F32),
                            pltpu.SemaphoreType.DMA((2,))],
        ),
        out_shape=jax.ShapeDtypeStruct((N, D), F32),
        compiler_params=pltpu.CompilerParams(
            dimension_semantics=("arbitrary",), vmem_limit_bytes=VMEM_LIMIT),
        name="final",
    )(dest, h1, p, wt, yb, gp, wg, bg, wp, gf)


def _prep_weights(w_in, w_uq, w_ukv):
    cq, ckv, kr, bg, cg, u = jnp.split(
        w_in, [Q_LORA, Q_LORA + KV_LORA, Q_LORA + KV_LORA + QK_ROPE_DIM,
               Q_LORA + KV_LORA + QK_ROPE_DIM + CONV_WIDTH,
               Q_LORA + KV_LORA + QK_ROPE_DIM + 2 * CONV_WIDTH], axis=1)
    kr_rot = jnp.concatenate([-kr[:, HALF_ROPE:], kr[:, :HALF_ROPE]], axis=1)
    pad = jnp.zeros((D_MODEL, HEAD_PAD - 2 * QK_ROPE_DIM), w_in.dtype)
    win = jnp.concatenate([cq, ckv, bg, cg, u, kr, kr_rot, pad], axis=1).astype(BF16)

    zq = jnp.zeros((Q_LORA, N_HEADS, HEAD_PAD - QK_DIM), w_uq.dtype)
    wqa = jnp.concatenate([w_uq, zq], axis=2)
    rope = w_uq[:, :, QK_NOPE_DIM:]
    wqb = jnp.concatenate([-rope[:, :, HALF_ROPE:], rope[:, :, :HALF_ROPE]], axis=2)
    wqa = wqa.reshape(Q_LORA, N_HEADS * HEAD_PAD).T.astype(BF16)
    wqb = wqb.reshape(Q_LORA, N_HEADS * QK_ROPE_DIM).T.astype(BF16)

    zk = jnp.zeros((KV_LORA, N_HEADS, HEAD_PAD - QK_NOPE_DIM), w_ukv.dtype)
    wuk = jnp.concatenate([w_ukv[:, :, :QK_NOPE_DIM], zk], axis=2)
    wuk = wuk.reshape(KV_LORA, N_HEADS * HEAD_PAD).astype(BF16)
    wuvt = w_ukv[:, :, QK_NOPE_DIM:].reshape(KV_LORA, MLA_WIDTH).T.astype(BF16)

    src = jnp.arange(QK_ROPE_DIM)[:, None]
    dst = jnp.arange(N_HEADS * HEAD_PAD)[None, :]
    pk = (dst % HEAD_PAD - QK_NOPE_DIM == src).astype(BF16)
    return win, wqa, wqb, wuk, wuvt, pk


def kernel(x, p, positions, attn_norm_g, w_in, q_norm_g, w_uq, kv_norm_g, w_ukv, conv_w, w_out,
           moe_norm_g, w_group_router, b_group_router, w_expert_router, b_expert_router,
           w_gate, w_up, w_down, ple_norm_g, w_ple_gate, b_ple_gate, w_ple_proj, final_norm_g):
    B, S, D = x.shape
    N = B * S
    assert w_in.shape[0] == 1, "single-layer trunk: the final norm is fused into the layer"
    pos = positions.astype(F32).reshape(B, 1, S)
    invf = (ROPE_BASE ** (-jnp.arange(0, QK_ROPE_DIM, 2, dtype=F32) / QK_ROPE_DIM)).reshape(-1, 1)
    row = lambda v: v.reshape(1, -1)
    h = x
    for i in range(1):
        win, wqa, wqb, wuk, wuvt, pk = _prep_weights(w_in[i], w_uq[i], w_ukv[i])
        qt, k, vt, oc = _inproj(h, pos, invf, row(attn_norm_g[i]), win, row(q_norm_g[i]), wqa,
                                wqb, row(kv_norm_g[i]), wuk, pk, wuvt,
                                conv_w[i].reshape(CONV_K, CONV_WIDTH))
        n_blocks = (2 * N) // MOE_BLK + N_EXPERTS
        ot, xbz = _attention(qt, k, vt, n_blocks * MOE_BLK + 2 * TM)

        zrow = lambda n: jnp.zeros((n, D), F32)
        wr = jnp.concatenate(
            [w_group_router[i].T, zrow(ROUTER_E0 - N_GROUPS), w_expert_router[i].T,
             zrow(ROUTER_ROWS - ROUTER_E0 - N_EXPERTS)], axis=0)
        wrh = wr.astype(BF16)
        wrhl = jnp.concatenate([wrh, (wr - wrh.astype(F32)).astype(BF16)], axis=0)
        br = jnp.concatenate(
            [b_group_router[i], jnp.zeros((ROUTER_E0 - N_GROUPS,), F32), b_expert_router[i],
             jnp.zeros((ROUTER_ROWS - ROUTER_E0 - N_EXPERTS,), F32)]).reshape(-1, 1)
        h1, pos, meta_w, opened, xb = _outproj(h, ot, oc, w_out[i].astype(BF16),
                                               row(moe_norm_g[i]), wrhl, br, xbz)

        opened = opened[:, :, 0]
        blk_ids = jnp.arange(n_blocks, dtype=jnp.int32)
        hit = opened[None, :, :] == blk_ids[:, None, None]
        used = jnp.any(hit, axis=(1, 2))
        e_of_blk = jnp.sum(jnp.where(hit, jnp.arange(N_EXPERTS, dtype=jnp.int32), 0), axis=(1, 2))
        n_used = jnp.sum(used.astype(jnp.int32)).reshape(1)
        key = jnp.where(used, e_of_blk, N_EXPERTS) * n_blocks + blk_ids
        slot_of_blk = jnp.sum((key[None, :] < key[:, None]).astype(jnp.int32), axis=1)
        at_slot = slot_of_blk[None, :] == blk_ids[:, None]
        order = jnp.sum(jnp.where(at_slot, blk_ids[None, :], 0), axis=1)
        blk_e = jnp.sum(jnp.where(at_slot, e_of_blk[None, :], 0), axis=1)
        blk_e = jnp.where(blk_ids < n_used, blk_e, jnp.max(jnp.where(used, e_of_blk, 0)))
        dest = jnp.swapaxes(pos[:, 0:2, :], 1, 2).reshape(2 * N)
        wt = jnp.swapaxes(meta_w[:, 0:2, :], 1, 2).reshape(N, 2)

        yb = _experts(order, blk_e, n_used, xb, w_gate[i], w_up[i], w_down[i])
        out = _final(dest, h1.reshape(N, D), p[i].reshape(N, PLE_DIM), wt, yb,
                     row(ple_norm_g[i]), w_ple_gate[i].astype(BF16), row(b_ple_gate[i]),
                     w_ple_proj[i].astype(BF16), row(final_norm_g))
        h = out.reshape(B, S, D)
    return h
```

```python
import functools
import math

import jax
import jax.numpy as jnp
from jax import lax
from jax.experimental import pallas as pl
from jax.experimental.pallas import tpu as pltpu

D_MODEL = 1024
PLE_DIM = 256
MLA_WIDTH = 512
CONV_WIDTH = 512
N_HEADS = 8
V_HEAD_DIM = 64
QK_NOPE_DIM = 64
QK_ROPE_DIM = 32
Q_LORA = 384
KV_LORA = 256
CONV_K = 3
N_GROUPS = 4
EXPERTS_PER_GROUP = 8
N_EXPERTS = N_GROUPS * EXPERTS_PER_GROUP
EXPERT_FF = 512
ROPE_BASE = 10000.0
EPS = 1e-6

HEAD_PAD = 128
V_ROWS = 80
QK_DIM = QK_NOPE_DIM + QK_ROPE_DIM
HALF_ROPE = QK_ROPE_DIM // 2

_C_Q = 0
_C_KV = _C_Q + Q_LORA
_C_BG = _C_KV + KV_LORA
_C_CG = _C_BG + CONV_WIDTH
_C_U = _C_CG + CONV_WIDTH
_C_KR = _C_U + CONV_WIDTH
IN_COLS_PAD = _C_KR + HEAD_PAD

TM = 256
TM_IN = 512
TQ = 512
TK = 256
ATT_HG = 4
ATT_TQC = 256
MOE_BLK = 512
ROUTER_E0 = 8
ROUTER_ROWS = 48
NEG = -1e30
Q_SCALE = (QK_DIM ** -0.5) * math.log2(math.e)
VMEM_LIMIT = 48 * 1024 * 1024

F32 = jnp.float32
BF16 = jnp.bfloat16


def _rms(x, g):
    return x * lax.rsqrt(jnp.mean(x * x, axis=-1, keepdims=True) + EPS) * g


def _dot(a, b):
    return jnp.dot(a, b, preferred_element_type=F32)


def _dot_nt(a, b):
    return lax.dot_general(a, b, (((1,), (1,)), ((), ())), preferred_element_type=F32)


def _dot_tn(a, b):
    return lax.dot_general(a, b, (((0,), (0,)), ((), ())), preferred_element_type=F32)


def _inproj_kernel(x_ref, pos_ref, invf_ref, g_ref, win_ref, qg_ref, wqa_ref, wqb_ref, kvg_ref,
                   wuk_ref, pk_ref, wuvt_ref, convw_ref,
                   qt_ref, k_ref, vt_ref, oc_ref, carry_ref, ext_ref):
    tm = x_ref.shape[1]
    x = x_ref[0]
    xn = _rms(x, g_ref[...])
    z = _dot(xn.astype(BF16), win_ref[...])

    ang = invf_ref[...] * pos_ref[0]
    cos = jnp.cos(ang)
    sin = jnp.sin(ang)
    cos2 = jnp.concatenate([cos, cos], axis=0)
    sin2 = jnp.concatenate([sin, sin], axis=0)

    cqn = _rms(z[:, _C_Q:_C_Q + Q_LORA], qg_ref[...]).astype(BF16)
    qa = _dot_nt(wqa_ref[...], cqn)
    qb = _dot_nt(wqb_ref[...], cqn)
    for h in range(N_HEADS):
        r0 = h * HEAD_PAD
        nope = qa[r0:r0 + QK_NOPE_DIM]
        rope = (qa[r0 + QK_NOPE_DIM:r0 + QK_DIM] * cos2
                + qb[h * QK_ROPE_DIM:(h + 1) * QK_ROPE_DIM] * sin2)
        qh = jnp.concatenate([nope, rope, qa[r0 + QK_DIM:r0 + HEAD_PAD]], axis=0) * Q_SCALE
        qt_ref[0, r0:r0 + HEAD_PAD, :] = qh.astype(BF16)

    kvn = _rms(z[:, _C_KV:_C_KV + KV_LORA], kvg_ref[...]).astype(BF16)
    krt = z[:, _C_KR:_C_KR + HEAD_PAD].T
    krot = krt[0:QK_ROPE_DIM] * cos2 + krt[QK_ROPE_DIM:2 * QK_ROPE_DIM] * sin2
    k_ref[0] = (_dot(kvn, wuk_ref[...]) + _dot_tn(krot.astype(BF16), pk_ref[...])).astype(BF16)
    vt = _dot_nt(wuvt_ref[...], kvn).astype(BF16)
    ones_row = jnp.where(lax.broadcasted_iota(jnp.int32, (V_ROWS - V_HEAD_DIM, tm), 0) == 0,
                         1.0, 0.0).astype(BF16)
    for h in range(N_HEADS):
        vt_ref[0, h * V_ROWS:h * V_ROWS + V_HEAD_DIM, :] = vt[h * V_HEAD_DIM:(h + 1) * V_HEAD_DIM]
        vt_ref[0, h * V_ROWS + V_HEAD_DIM:(h + 1) * V_ROWS, :] = ones_row

    @pl.when(pl.program_id(1) == 0)
    def _():
        carry_ref[...] = jnp.zeros_like(carry_ref)

    cu = z[:, _C_CG:_C_CG + CONV_WIDTH] * z[:, _C_U:_C_U + CONV_WIDTH]
    ext_ref[0:8, :] = carry_ref[...]
    ext_ref[8:8 + tm, :] = cu
    cu1 = ext_ref[7:7 + tm, :]
    cu2 = ext_ref[6:6 + tm, :]
    w = convw_ref[...]
    y = w[0:1] * cu2 + w[1:2] * cu1 + w[2:3] * cu
    oc_ref[0] = (z[:, _C_BG:_C_BG + CONV_WIDTH] * y).astype(BF16)
    carry_ref[...] = ext_ref[tm:tm + 8, :]


def _inproj(x, pos, invf, g, win, qg, wqa, wqb, kvg, wuk, pk, wuvt, convw):
    B, S, D = x.shape
    tm = TM_IN
    full = lambda a: pl.BlockSpec(a.shape, lambda b, i: (0,) * a.ndim)
    return pl.pallas_call(
        _inproj_kernel,
        grid=(B, S // tm),
        in_specs=[
            pl.BlockSpec((1, tm, D), lambda b, i: (b, i, 0)),
            pl.BlockSpec((1, 1, tm), lambda b, i: (b, 0, i)),
            full(invf), full(g), full(win), full(qg), full(wqa), full(wqb), full(kvg), full(wuk),
            full(pk), full(wuvt), full(convw),
        ],
        out_specs=[
            pl.BlockSpec((1, N_HEADS * HEAD_PAD, tm), lambda b, i: (b, 0, i)),
            pl.BlockSpec((1, tm, N_HEADS * HEAD_PAD), lambda b, i: (b, i, 0)),
            pl.BlockSpec((1, N_HEADS * V_ROWS, tm), lambda b, i: (b, 0, i)),
            pl.BlockSpec((1, tm, CONV_WIDTH), lambda b, i: (b, i, 0)),
        ],
        out_shape=[
            jax.ShapeDtypeStruct((B, N_HEADS * HEAD_PAD, S), BF16),
            jax.ShapeDtypeStruct((B, S, N_HEADS * HEAD_PAD), BF16),
            jax.ShapeDtypeStruct((B, N_HEADS * V_ROWS, S), BF16),
            jax.ShapeDtypeStruct((B, S, CONV_WIDTH), BF16),
        ],
        scratch_shapes=[pltpu.VMEM((8, CONV_WIDTH), F32), pltpu.VMEM((tm + 8, CONV_WIDTH), F32)],
        compiler_params=pltpu.CompilerParams(
            dimension_semantics=("arbitrary", "arbitrary"), vmem_limit_bytes=VMEM_LIMIT),
        name="inproj",
    )(x, pos, invf, g, win, qg, wqa, wqb, kvg, wuk, pk, wuvt, convw)


def _attn_kernel(qt_ref, k_ref, vt_ref, o_ref, z_ref, m_ref, acc_ref, sa_ref, sb_ref, zero_ref,
                 zsem):
    tq = qt_ref.shape[2]
    tk = TK
    assert tq == 2 * tk and ATT_TQC == tk
    i = pl.program_id(2)
    m_ref[...] = jnp.full_like(m_ref, NEG)
    acc_ref[...] = jnp.zeros_like(acc_ref)

    step = ((pl.program_id(0) * pl.num_programs(1) + pl.program_id(1)) * pl.num_programs(2) + i)
    zrows = zero_ref.shape[0]

    @pl.when(step == 0)
    def _():
        zero_ref[...] = jnp.zeros_like(zero_ref)

    zero_copy = pltpu.make_async_copy(
        zero_ref, z_ref.at[pl.ds(pl.multiple_of(step * zrows, 8), zrows)], zsem)
    zero_copy.start()

    def scores(j, s_ref, g, col0=0):
        k0 = pl.multiple_of(j * tk, tk)
        qt = qt_ref[0, g * HEAD_PAD:(g + 1) * HEAD_PAD, col0:]
        s_ref[g, :, col0:] = _dot(k_ref[0, pl.ds(k0, tk), g * HEAD_PAD:(g + 1) * HEAD_PAD], qt)

    def softmax_pv(j, s_ref, g, diag):
        k0 = pl.multiple_of(j * tk, tk)
        vt = vt_ref[0, g * V_ROWS:(g + 1) * V_ROWS, pl.ds(k0, tk)]
        for c in range(tq // ATT_TQC):
            if diag is not None and c < diag:
                continue
            cols = slice(c * ATT_TQC, (c + 1) * ATT_TQC)
            s = s_ref[g, :, cols]
            if diag is not None and c == diag:
                krow = lax.broadcasted_iota(jnp.int32, (tk, ATT_TQC), 0)
                qcol = lax.broadcasted_iota(jnp.int32, (tk, ATT_TQC), 1)
                s = jnp.where(krow <= qcol, s, NEG)
            m_old = m_ref[g, :, cols]
            m_new = jnp.maximum(m_old, jnp.max(s, axis=0, keepdims=True))
            alpha = jnp.exp2(m_old - m_new)
            p = jnp.exp2((s - m_new).astype(BF16))
            acc_ref[g, :, cols] = alpha * acc_ref[g, :, cols] + _dot(vt, p)
            m_ref[g, :, cols] = m_new

    def stage(j_next, s_next, j, s_cur, diag=None, next_col0=0):
        for g in range(ATT_HG):
            if j_next is not None:
                scores(j_next, s_next, g, next_col0)
            softmax_pv(j, s_cur, g, diag)

    for g in range(ATT_HG):
        scores(0, sa_ref, g)

    def pair(j):
        stage(j + 1, sb_ref, j, sa_ref)
        stage(j + 2, sa_ref, j + 1, sb_ref)

    def body(t, c):
        pair(4 * t)
        pair(4 * t + 2)
        return c

    lax.fori_loop(0, i // 2, body, 0)

    @pl.when(i % 2 == 1)
    def _():
        pair(2 * i - 2)

    n_full = 2 * i
    stage(n_full + 1, sb_ref, n_full, sa_ref, diag=0, next_col0=ATT_TQC)
    stage(None, None, n_full + 1, sb_ref, diag=1)
    for g in range(ATT_HG):
        o_ref[0, g * V_HEAD_DIM:(g + 1) * V_HEAD_DIM, :] = (
            acc_ref[g, 0:V_HEAD_DIM, :] / acc_ref[g, V_HEAD_DIM:V_HEAD_DIM + 1, :]).astype(o_ref.dtype)
    zero_copy.wait()


def _attention(qt, k, vt, min_zero_rows):
    B, _, S = qt.shape
    hg = ATT_HG
    grid = (B, N_HEADS // hg, S // TQ)
    n_steps = grid[0] * grid[1] * grid[2]
    zrows = -(-min_zero_rows // (8 * n_steps)) * 8
    return pl.pallas_call(
        _attn_kernel,
        grid=grid,
        in_specs=[
            pl.BlockSpec((1, hg * HEAD_PAD, TQ), lambda b, h, i: (b, h, i)),
            pl.BlockSpec((1, S, hg * HEAD_PAD), lambda b, h, i: (b, 0, h)),
            pl.BlockSpec((1, hg * V_ROWS, S), lambda b, h, i: (b, h, 0)),
        ],
        out_specs=[pl.BlockSpec((1, hg * V_HEAD_DIM, TQ), lambda b, h, i: (b, h, i)),
                   pl.BlockSpec(memory_space=pl.ANY)],
        out_shape=[jax.ShapeDtypeStruct((B, MLA_WIDTH, S), BF16),
                   jax.ShapeDtypeStruct((n_steps * zrows, D_MODEL), F32)],
        scratch_shapes=[pltpu.VMEM((hg, 1, TQ), F32), pltpu.VMEM((hg, V_ROWS, TQ), F32),
                        pltpu.VMEM((hg, TK, TQ), F32), pltpu.VMEM((hg, TK, TQ), F32),
                        pltpu.VMEM((zrows, D_MODEL), F32), pltpu.SemaphoreType.DMA],
        compiler_params=pltpu.CompilerParams(
            dimension_semantics=("arbitrary", "arbitrary", "arbitrary"),
            vmem_limit_bytes=VMEM_LIMIT),
        name="attn",
    )(qt, k, vt)


def _route_tile(xn, wrhl_ref, br_ref, carry_ref):
    tm = xn.shape[0]
    xh = xn.astype(BF16)
    xl = (xn - xh.astype(F32)).astype(BF16)
    hl = _dot_nt(wrhl_ref[...], xh)
    logits = (hl[0:ROUTER_ROWS] + hl[ROUTER_ROWS:2 * ROUTER_ROWS]
              + _dot_nt(wrhl_ref[0:ROUTER_ROWS, :], xl) + br_ref[...])
    big = jnp.int32(1 << 20)

    grow = lax.broadcasted_iota(jnp.int32, (ROUTER_E0, tm), 0)
    glog = jnp.where(grow < N_GROUPS, logits[0:ROUTER_E0], NEG)
    gmax = jnp.max(glog, axis=0, keepdims=True)
    gsum = jnp.sum(jnp.exp(glog - gmax), axis=0, keepdims=True)
    g_p = 1.0 / gsum
    g_idx = jnp.min(jnp.where(glog == gmax, grow, big), axis=0, keepdims=True)

    erow = lax.broadcasted_iota(jnp.int32, (N_EXPERTS, tm), 0)
    e_lo = g_idx * EXPERTS_PER_GROUP
    in_group = (erow >= e_lo) & (erow < e_lo + EXPERTS_PER_GROUP)
    elog = jnp.where(in_group, logits[ROUTER_E0:ROUTER_E0 + N_EXPERTS], NEG)
    emax = jnp.max(elog, axis=0, keepdims=True)
    esum = jnp.sum(jnp.exp(elog - emax), axis=0, keepdims=True)
    e1 = jnp.min(jnp.where(elog == emax, erow, big), axis=0, keepdims=True)
    elog2 = jnp.where(erow == e1, NEG, elog)
    emax2 = jnp.max(elog2, axis=0, keepdims=True)
    e2 = jnp.min(jnp.where(elog2 == emax2, erow, big), axis=0, keepdims=True)
    p1 = 1.0 / esum
    p2 = jnp.exp(emax2 - emax) / esum
    psum = p1 + p2
    w1 = g_p * (p1 / psum)
    w2 = g_p * (p2 / psum)

    oh1 = erow == e1
    oh2 = erow == e2
    oh = jnp.where(oh1 | oh2, 1.0, 0.0)
    srow = lax.broadcasted_iota(jnp.int32, (tm, tm), 0)
    scol = lax.broadcasted_iota(jnp.int32, (tm, tm), 1)
    earlier = jnp.where(srow < scol, 1.0, 0.0).astype(BF16)
    cum = _dot(oh.astype(BF16), earlier) + carry_ref[:, 0:1]
    r1 = jnp.sum(jnp.where(oh1, cum, 0.0), axis=0, keepdims=True)
    r2 = jnp.sum(jnp.where(oh2, cum, 0.0), axis=0, keepdims=True)
    return oh, oh1, oh2, r1, r2, w1, w2


def _outproj_kernel(x_ref, ot_ref, oc_ref, wo_ref, g_ref, wrhl_ref, br_ref, xbz_ref,
                    h1_ref, pos_ref, mw_ref, nb_ref, xb_ref,
                    carry_ref, last_ref, free_ref, xs0_ref, xs1_ref, dv0_ref, dv1_ref,
                    ds0_ref, ds1_ref, rsem, ssem):
    del xbz_ref
    tm = TM
    step = pl.program_id(0) * pl.num_programs(1) + pl.program_id(1)
    last_step = pl.num_programs(0) * pl.num_programs(1) - 1
    xs = (xs0_ref, xs1_ref)
    dv = (dv0_ref, dv1_ref)
    ds = (ds0_ref, ds1_ref)
    spare_row0 = xb_ref.shape[0] - 2 * tm

    def meta_copy(slot):
        return pltpu.make_async_copy(dv[slot], ds[slot], ssem.at[slot])

    def issue_rows(slot):
        for r in range(tm):
            for k in range(2):
                pltpu.make_async_copy(xs[slot].at[pl.ds(r, 1)],
                                      xb_ref.at[pl.ds(ds[slot][k, r], 1)], rsem.at[slot]
                                      ).start(priority=k)

    def wait_rows(slot):
        for _ in range(2):
            pltpu.make_async_copy(xs[slot], xb_ref.at[pl.ds(0, tm)], rsem.at[slot]).wait()

    def compute(slot):
        rows = slice(slot * tm, (slot + 1) * tm)
        attn = (_dot_tn(ot_ref[0, :, rows], wo_ref[0:MLA_WIDTH, :])
                + _dot(oc_ref[0, rows, :], wo_ref[MLA_WIDTH:MLA_WIDTH + CONV_WIDTH, :]))
        h1 = x_ref[0, rows, :] + attn
        h1_ref[0, rows, :] = h1
        xn = _rms(h1, g_ref[...])
        xs[slot][...] = xn
        oh, oh1, oh2, r1, r2, w1, w2 = _route_tile(xn, wrhl_ref, br_ref, carry_ref)

        inv_blk = 1.0 / MOE_BLK
        cnt = carry_ref[:, 0:1]
        tile_cnt = jnp.sum(oh, axis=1, keepdims=True)
        nb_before = jnp.floor((cnt + (MOE_BLK - 1)) * inv_blk)
        nb_after = jnp.floor((cnt + tile_cnt + (MOE_BLK - 1)) * inv_blk)
        new = nb_after - nb_before
        erow = lax.broadcasted_iota(jnp.int32, (N_EXPERTS, N_EXPERTS), 0)
        ecol = lax.broadcasted_iota(jnp.int32, (N_EXPERTS, N_EXPERTS), 1)
        lower = jnp.where(ecol < erow, 1.0, 0.0).astype(BF16)
        new_rep = jnp.broadcast_to(new, (N_EXPERTS, HEAD_PAD))
        new_id = free_ref[0:1, :] + _dot(lower, new_rep.astype(BF16))
        free_ref[...] = free_ref[...] + jnp.sum(new, axis=0, keepdims=True)
        last_before = last_ref[...]
        is_new = new_rep > 0.0
        last_ref[...] = jnp.where(is_new, new_id, last_before)
        nb_ref[slot] = jnp.where(is_new, new_id, -1.0).astype(jnp.int32)
        carry_ref[...] = carry_ref[...] + tile_cnt

        def place(ohk, rk):
            bi = jnp.floor(rk * inv_blk)
            blk = jnp.where(bi == nb_before - 1.0, last_before[:, 0:1], new_id[:, 0:1])
            phys = jnp.sum(jnp.where(ohk, blk, 0.0), axis=0, keepdims=True)
            return (phys * MOE_BLK + (rk - bi * MOE_BLK)).astype(jnp.int32)

        mrow = lax.broadcasted_iota(jnp.int32, (8, tm), 0)
        dest = jnp.where(mrow == 0, place(oh1, r1), jnp.where(mrow == 1, place(oh2, r2), 0))
        dv[slot][...] = dest
        pos_ref[0, :, rows] = dest
        mw_ref[0, :, rows] = jnp.where(mrow == 0, w1, jnp.where(mrow == 1, w2, 0.0))
        meta_copy(slot).start()

    @pl.when(step == 0)
    def _():
        carry_ref[...] = jnp.zeros_like(carry_ref)
        free_ref[...] = jnp.zeros_like(free_ref)
        last_ref[...] = jnp.full_like(last_ref, -1.0)
        xs1_ref[...] = jnp.zeros_like(xs1_ref)
        mrow = lax.broadcasted_iota(jnp.int32, (8, tm), 0)
        lane = lax.broadcasted_iota(jnp.int32, (8, tm), 1)
        dv1_ref[...] = jnp.where(mrow < 2, spare_row0 + 2 * lane + mrow, 0)
        meta_copy(1).start()

    meta_copy(1).wait()
    issue_rows(1)
    compute(0)
    wait_rows(1)
    meta_copy(0).wait()
    issue_rows(0)
    compute(1)
    wait_rows(0)

    @pl.when(step == last_step)
    def _():
        meta_copy(1).wait()
        issue_rows(1)
        wait_rows(1)


def _outproj(x, ot, oc, wo, g, wrhl, br, xbz):
    B, S, D = x.shape
    tm2 = 2 * TM
    n_s = S // tm2
    full = lambda a: pl.BlockSpec(a.shape, lambda b, i: (0,) * a.ndim)
    tile = lambda w: pl.BlockSpec((1, tm2, w), lambda b, i: (b, i, 0))
    meta = pl.BlockSpec((1, 8, tm2), lambda b, i: (b, 0, i))
    any_space = pl.BlockSpec(memory_space=pl.ANY)
    return pl.pallas_call(
        _outproj_kernel,
        grid=(B, n_s),
        in_specs=[
            tile(D),
            pl.BlockSpec((1, MLA_WIDTH, tm2), lambda b, i: (b, 0, i)),
            tile(CONV_WIDTH),
            full(wo), full(g), full(wrhl), full(br), any_space,
        ],
        out_specs=[tile(D), meta, meta,
                   pl.BlockSpec((2, N_EXPERTS, HEAD_PAD), lambda b, i: (b * n_s + i, 0, 0)),
                   any_space],
        out_shape=[
            jax.ShapeDtypeStruct((B, S, D), F32),
            jax.ShapeDtypeStruct((B, 8, S), jnp.int32),
            jax.ShapeDtypeStruct((B, 8, S), F32),
            jax.ShapeDtypeStruct((B * S // TM, N_EXPERTS, HEAD_PAD), jnp.int32),
            jax.ShapeDtypeStruct(xbz.shape, xbz.dtype),
        ],
        input_output_aliases={7: 4},
        scratch_shapes=[
            pltpu.VMEM((N_EXPERTS, HEAD_PAD), F32), pltpu.VMEM((N_EXPERTS, HEAD_PAD), F32),
            pltpu.VMEM((8, HEAD_PAD), F32),
            pltpu.VMEM((TM, D), F32), pltpu.VMEM((TM, D), F32),
            pltpu.VMEM((8, TM), jnp.int32), pltpu.VMEM((8, TM), jnp.int32),
            pltpu.SMEM((8, TM), jnp.int32), pltpu.SMEM((8, TM), jnp.int32),
            pltpu.SemaphoreType.DMA((2,)), pltpu.SemaphoreType.DMA((2,)),
        ],
        compiler_params=pltpu.CompilerParams(
            dimension_semantics=("arbitrary", "arbitrary"), vmem_limit_bytes=VMEM_LIMIT),
        name="outproj",
    )(x, ot, oc, wo, g, wrhl, br, xbz)


def _expert_kernel(order_ref, blke_ref, nused_ref, xb_ref, wg_ref, wu_ref, wd_ref, yb_ref,
                   wgs_ref, wus_ref, wds_ref):
    del order_ref
    i = pl.program_id(0)
    used = i < nused_ref[0]
    changed = (i == 0) | (blke_ref[i] != blke_ref[jnp.maximum(i - 1, 0)])

    @pl.when(used & changed)
    def _():
        wgs_ref[...] = wg_ref[0].astype(BF16)
        wus_ref[...] = wu_ref[0].astype(BF16)
        wds_ref[...] = wd_ref[0].astype(BF16)

    @pl.when(used)
    def _():
        xb = xb_ref[...].astype(BF16)
        half = EXPERT_FF // 2
        g0 = _dot(xb, wgs_ref[:, :half])
        u0 = _dot(xb, wus_ref[:, :half])
        g1 = _dot(xb, wgs_ref[:, half:])
        u1 = _dot(xb, wus_ref[:, half:])
        h0 = ((g0 * jax.nn.sigmoid(g0)) * u0).astype(BF16)
        y = _dot(h0, wds_ref[:half, :])
        h1 = ((g1 * jax.nn.sigmoid(g1)) * u1).astype(BF16)
        yb_ref[...] = y + _dot(h1, wds_ref[half:, :])

    @pl.when(jnp.logical_not(used))
    def _():
        yb_ref[...] = jnp.zeros_like(yb_ref)


def _experts(order, blk_e, n_used, xb, wg, wu, wd):
    D = xb.shape[1]
    blk = MOE_BLK
    n_blocks = order.shape[0]
    return pl.pallas_call(
        _expert_kernel,
        grid_spec=pltpu.PrefetchScalarGridSpec(
            num_scalar_prefetch=3,
            grid=(n_blocks,),
            in_specs=[
                pl.BlockSpec((blk, D), lambda i, o, be, nu: (o[jnp.minimum(i, nu[0] - 1)], 0)),
                pl.BlockSpec((1, D, EXPERT_FF), lambda i, o, be, nu: (be[i], 0, 0)),
                pl.BlockSpec((1, D, EXPERT_FF), lambda i, o, be, nu: (be[i], 0, 0)),
                pl.BlockSpec((1, EXPERT_FF, D), lambda i, o, be, nu: (be[i], 0, 0)),
            ],
            out_specs=pl.BlockSpec((blk, D), lambda i, o, be, nu: (o[i], 0)),
            scratch_shapes=[pltpu.VMEM((D, EXPERT_FF), BF16), pltpu.VMEM((D, EXPERT_FF), BF16),
                            pltpu.VMEM((EXPERT_FF, D), BF16)],
        ),
        out_shape=jax.ShapeDtypeStruct((n_blocks * blk, D), F32),
        compiler_params=pltpu.CompilerParams(
            dimension_semantics=("arbitrary",), vmem_limit_bytes=VMEM_LIMIT),
        name="experts",
    )(order, blk_e, n_used, xb, wg, wu, wd)


def _final_kernel(dest_ref, h1_ref, p_ref, wt_ref, yb_ref, gp_ref, wg_ref, bg_ref, wp_ref,
                  gf_ref, out_ref, buf0_ref, buf1_ref, sem):
    tm = h1_ref.shape[0] // 2
    i = pl.program_id(0)
    bufs = (buf0_ref, buf1_ref)

    def row_copy(slot, k, r, d):
        return pltpu.make_async_copy(yb_ref.at[pl.ds(d, 1)], bufs[slot].at[k, pl.ds(r, 1)],
                                     sem.at[slot])

    def issue(tile, slot):
        for r in range(tm):
            t = (tile * tm + r) * 2
            row_copy(slot, 0, r, dest_ref[t]).start(priority=0)
            row_copy(slot, 1, r, dest_ref[t + 1]).start(priority=1)

    def wait(slot):
        for k in range(2):
            pltpu.make_async_copy(yb_ref.at[pl.ds(0, tm)], bufs[slot].at[k], sem.at[slot]).wait()

    def compute(slot):
        rows = slice(slot * tm, (slot + 1) * tm)
        pe = _dot(p_ref[rows, :].astype(BF16), wp_ref[...])
        wt = wt_ref[rows, :]
        h2 = h1_ref[rows, :] + wt[:, 0:1] * bufs[slot][0] + wt[:, 1:2] * bufs[slot][1]
        n = _rms(h2, gp_ref[...]).astype(BF16)
        gate = jax.nn.sigmoid(_dot(n, wg_ref[...]) + bg_ref[...])
        h3 = h2 + gate * pe
        out_ref[rows, :] = _rms(h3, gf_ref[...])

    @pl.when(i == 0)
    def _():
        issue(0, 0)

    wait(0)
    issue(2 * i + 1, 1)
    compute(0)

    last = pl.num_programs(0) - 1
    wait(1)
    issue(jnp.minimum(2 * i + 2, 2 * last + 1), 0)
    compute(1)

    @pl.when(i == last)
    def _():
        wait(0)


def _final(dest, h1, p, wt, yb, gp, wg, bg, wp, gf):
    N, D = h1.shape
    tm = TM
    full = lambda a: pl.BlockSpec(a.shape, lambda i, dest: (0,) * a.ndim)
    return pl.pallas_call(
        _final_kernel,
        grid_spec=pltpu.PrefetchScalarGridSpec(
            num_scalar_prefetch=1,
            grid=(N // (2 * tm),),
            in_specs=[
                pl.BlockSpec((2 * tm, D), lambda i, dest: (i, 0)),
                pl.BlockSpec((2 * tm, PLE_DIM), lambda i, dest: (i, 0)),
                pl.BlockSpec((2 * tm, 2), lambda i, dest: (i, 0)),
                pl.BlockSpec(memory_space=pl.ANY),
                full(gp), full(wg), full(bg), full(wp), full(gf),
            ],
            out_specs=pl.BlockSpec((2 * tm, D), lambda i, dest: (i, 0)),
            scratch_shapes=[pltpu.VMEM((2, tm, D), F32), pltpu.VMEM((2, tm, D), F32),
                            pltpu.SemaphoreType.DMA((2,))],
        ),
        out_shape=jax.ShapeDtypeStruct((N, D), F32),
        compiler_params=pltpu.CompilerParams(
            dimension_semantics=("arbitrary",), vmem_limit_bytes=VMEM_LIMIT),
        name="final",
    )(dest, h1, p, wt, yb, gp, wg, bg, wp, gf)


def _prep_weights(w_in, w_uq, w_ukv):
    cq, ckv, kr, bg, cg, u = jnp.split(
        w_in, [Q_LORA, Q_LORA + KV_LORA, Q_LORA + KV_LORA + QK_ROPE_DIM,
               Q_LORA + KV_LORA + QK_ROPE_DIM + CONV_WIDTH,
               Q_LORA + KV_LORA + QK_ROPE_DIM + 2 * CONV_WIDTH], axis=1)
    kr_rot = jnp.concatenate([-kr[:, HALF_ROPE:], kr[:, :HALF_ROPE]], axis=1)
    pad = jnp.zeros((D_MODEL, HEAD_PAD - 2 * QK_ROPE_DIM), w_in.dtype)
    win = jnp.concatenate([cq, ckv, bg, cg, u, kr, kr_rot, pad], axis=1).astype(BF16)

    zq = jnp.zeros((Q_LORA, N_HEADS, HEAD_PAD - QK_DIM), w_uq.dtype)
    wqa = jnp.concatenate([w_uq, zq], axis=2)
    rope = w_uq[:, :, QK_NOPE_DIM:]
    wqb = jnp.concatenate([-rope[:, :, HALF_ROPE:], rope[:, :, :HALF_ROPE]], axis=2)
    wqa = wqa.reshape(Q_LORA, N_HEADS * HEAD_PAD).T.astype(BF16)
    wqb = wqb.reshape(Q_LORA, N_HEADS * QK_ROPE_DIM).T.astype(BF16)

    zk = jnp.zeros((KV_LORA, N_HEADS, HEAD_PAD - QK_NOPE_DIM), w_ukv.dtype)
    wuk = jnp.concatenate([w_ukv[:, :, :QK_NOPE_DIM], zk], axis=2)
    wuk = wuk.reshape(KV_LORA, N_HEADS * HEAD_PAD).astype(BF16)
    wuvt = w_ukv[:, :, QK_NOPE_DIM:].reshape(KV_LORA, MLA_WIDTH).T.astype(BF16)

    src = jnp.arange(QK_ROPE_DIM)[:, None]
    dst = jnp.arange(N_HEADS * HEAD_PAD)[None, :]
    pk = (dst % HEAD_PAD - QK_NOPE_DIM == src).astype(BF16)
    return win, wqa, wqb, wuk, wuvt, pk


def kernel(x, p, positions, attn_norm_g, w_in, q_norm_g, w_uq, kv_norm_g, w_ukv, conv_w, w_out,
           moe_norm_g, w_group_router, b_group_router, w_expert_router, b_expert_router,
           w_gate, w_up, w_down, ple_norm_g, w_ple_gate, b_ple_gate, w_ple_proj, final_norm_g):
    B, S, D = x.shape
    N = B * S
    assert w_in.shape[0] == 1, "single-layer trunk: the final norm is fused into the layer"
    pos = positions.astype(F32).reshape(B, 1, S)
    invf = (ROPE_BASE ** (-jnp.arange(0, QK_ROPE_DIM, 2, dtype=F32) / QK_ROPE_DIM)).reshape(-1, 1)
    row = lambda v: v.reshape(1, -1)
    h = x
    for i in range(1):
        win, wqa, wqb, wuk, wuvt, pk = _prep_weights(w_in[i], w_uq[i], w_ukv[i])
        qt, k, vt, oc = _inproj(h, pos, invf, row(attn_norm_g[i]), win, row(q_norm_g[i]), wqa,
                                wqb, row(kv_norm_g[i]), wuk, pk, wuvt,
                                conv_w[i].reshape(CONV_K, CONV_WIDTH))
        n_blocks = (2 * N) // MOE_BLK + N_EXPERTS
        ot, xbz = _attention(qt, k, vt, n_blocks * MOE_BLK + 2 * TM)

        zrow = lambda n: jnp.zeros((n, D), F32)
        wr = jnp.concatenate(
            [w_group_router[i].T, zrow(ROUTER_E0 - N_GROUPS), w_expert_router[i].T,
             zrow(ROUTER_ROWS - ROUTER_E0 - N_EXPERTS)], axis=0)
        wrh = wr.astype(BF16)
        wrhl = jnp.concatenate([wrh, (wr - wrh.astype(F32)).astype(BF16)], axis=0)
        br = jnp.concatenate(
            [b_group_router[i], jnp.zeros((ROUTER_E0 - N_GROUPS,), F32), b_expert_router[i],
             jnp.zeros((ROUTER_ROWS - ROUTER_E0 - N_EXPERTS,), F32)]).reshape(-1, 1)
        h1, pos, meta_w, opened, xb = _outproj(h, ot, oc, w_out[i].astype(BF16),
                                               row(moe_norm_g[i]), wrhl, br, xbz)

        opened = opened[:, :, 0]
        blk_ids = jnp.arange(n_blocks, dtype=jnp.int32)
        hit = opened[None, :, :] == blk_ids[:, None, None]
        used = jnp.any(hit, axis=(1, 2))
        e_of_blk = jnp.sum(jnp.where(hit, jnp.arange(N_EXPERTS, dtype=jnp.int32), 0), axis=(1, 2))
        n_used = jnp.sum(used.astype(jnp.int32)).reshape(1)
        key = jnp.where(used, e_of_blk, N_EXPERTS) * n_blocks + blk_ids
        slot_of_blk = jnp.sum((key[None, :] < key[:, None]).astype(jnp.int32), axis=1)
        at_slot = slot_of_blk[None, :] == blk_ids[:, None]
        order = jnp.sum(jnp.where(at_slot, blk_ids[None, :], 0), axis=1)
        blk_e = jnp.sum(jnp.where(at_slot, e_of_blk[None, :], 0), axis=1)
        blk_e = jnp.where(blk_ids < n_used, blk_e, jnp.max(jnp.where(used, e_of_blk, 0)))
        dest = jnp.swapaxes(pos[:, 0:2, :], 1, 2).reshape(2 * N)
        wt = jnp.swapaxes(meta_w[:, 0:2, :], 1, 2).reshape(N, 2)

        yb = _experts(order, blk_e, n_used, xb, w_gate[i], w_up[i], w_down[i])
        out = _final(dest, h1.reshape(N, D), p[i].reshape(N, PLE_DIM), wt, yb,
                     row(ple_norm_g[i]), w_ple_gate[i].astype(BF16), row(b_ple_gate[i]),
                     w_ple_proj[i].astype(BF16), row(final_norm_g))
        h = out.reshape(B, S, D)
    return h
```

```python
import functools
import math

import jax
import jax.numpy as jnp
from jax import lax
from jax.experimental import pallas as pl
from jax.experimental.pallas import tpu as pltpu

D_MODEL = 1024
PLE_DIM = 256
MLA_WIDTH = 512
CONV_WIDTH = 512
N_HEADS = 8
V_HEAD_DIM = 64
QK_NOPE_DIM = 64
QK_ROPE_DIM = 32
Q_LORA = 384
KV_LORA = 256
CONV_K = 3
N_GROUPS = 4
EXPERTS_PER_GROUP = 8
N_EXPERTS = N_GROUPS * EXPERTS_PER_GROUP
EXPERT_FF = 512
ROPE_BASE = 10000.0
EPS = 1e-6

HEAD_PAD = 128
V_ROWS = 80
QK_DIM = QK_NOPE_DIM + QK_ROPE_DIM
HALF_ROPE = QK_ROPE_DIM // 2

_C_Q = 0
_C_KV = _C_Q + Q_LORA
_C_BG = _C_KV + KV_LORA
_C_CG = _C_BG + CONV_WIDTH
_C_U = _C_CG + CONV_WIDTH
_C_KR = _C_U + CONV_WIDTH
IN_COLS_PAD = _C_KR + HEAD_PAD

TM = 256
TM_IN = 512
FIN_TILES = 4
FIN_AHEAD = 2
TQ = 512
TK = 256
ATT_HG = 4
ATT_TQC = 256
MOE_BLK = 512
ROUTER_E0 = 8
ROUTER_ROWS = 48
NEG = -1e30
Q_SCALE = (QK_DIM ** -0.5) * math.log2(math.e)
VMEM_LIMIT = 48 * 1024 * 1024

F32 = jnp.float32
BF16 = jnp.bfloat16


def _rms(x, g):
    return x * lax.rsqrt(jnp.mean(x * x, axis=-1, keepdims=True) + EPS) * g


def _dot(a, b):
    return jnp.dot(a, b, preferred_element_type=F32)


def _dot_nt(a, b):
    return lax.dot_general(a, b, (((1,), (1,)), ((), ())), preferred_element_type=F32)


def _dot_tn(a, b):
    return lax.dot_general(a, b, (((0,), (0,)), ((), ())), preferred_element_type=F32)


def _inproj_kernel(x_ref, pos_ref, invf_ref, g_ref, win_ref, qg_ref, wqa_ref, wqb_ref, kvg_ref,
                   wuk_ref, pk_ref, wuvt_ref, convw_ref,
                   qt_ref, k_ref, vt_ref, oc_ref, carry_ref, ext_ref):
    tm = x_ref.shape[1]
    x = x_ref[0]
    xn = _rms(x, g_ref[...])
    z = _dot(xn.astype(BF16), win_ref[...])

    ang = invf_ref[...] * pos_ref[0]
    cos = jnp.cos(ang)
    sin = jnp.sin(ang)
    cos2 = jnp.concatenate([cos, cos], axis=0)
    sin2 = jnp.concatenate([sin, sin], axis=0)

    cqn = _rms(z[:, _C_Q:_C_Q + Q_LORA], qg_ref[...]).astype(BF16)
    qa = _dot_nt(wqa_ref[...], cqn)
    qb = _dot_nt(wqb_ref[...], cqn)
    for h in range(N_HEADS):
        r0 = h * HEAD_PAD
        nope = qa[r0:r0 + QK_NOPE_DIM]
        rope = (qa[r0 + QK_NOPE_DIM:r0 + QK_DIM] * cos2
                + qb[h * QK_ROPE_DIM:(h + 1) * QK_ROPE_DIM] * sin2)
        qh = jnp.concatenate([nope, rope, qa[r0 + QK_DIM:r0 + HEAD_PAD]], axis=0) * Q_SCALE
        qt_ref[0, r0:r0 + HEAD_PAD, :] = qh.astype(BF16)

    kvn = _rms(z[:, _C_KV:_C_KV + KV_LORA], kvg_ref[...]).astype(BF16)
    krt = z[:, _C_KR:_C_KR + HEAD_PAD].T
    krot = krt[0:QK_ROPE_DIM] * cos2 + krt[QK_ROPE_DIM:2 * QK_ROPE_DIM] * sin2
    k_ref[0] = (_dot(kvn, wuk_ref[...]) + _dot_tn(krot.astype(BF16), pk_ref[...])).astype(BF16)
    vt = _dot_nt(wuvt_ref[...], kvn).astype(BF16)
    ones_row = jnp.where(lax.broadcasted_iota(jnp.int32, (V_ROWS - V_HEAD_DIM, tm), 0) == 0,
                         1.0, 0.0).astype(BF16)
    for h in range(N_HEADS):
        vt_ref[0, h * V_ROWS:h * V_ROWS + V_HEAD_DIM, :] = vt[h * V_HEAD_DIM:(h + 1) * V_HEAD_DIM]
        vt_ref[0, h * V_ROWS + V_HEAD_DIM:(h + 1) * V_ROWS, :] = ones_row

    @pl.when(pl.program_id(1) == 0)
    def _():
        carry_ref[...] = jnp.zeros_like(carry_ref)

    cu = z[:, _C_CG:_C_CG + CONV_WIDTH] * z[:, _C_U:_C_U + CONV_WIDTH]
    ext_ref[0:8, :] = carry_ref[...]
    ext_ref[8:8 + tm, :] = cu
    cu1 = ext_ref[7:7 + tm, :]
    cu2 = ext_ref[6:6 + tm, :]
    w = convw_ref[...]
    y = w[0:1] * cu2 + w[1:2] * cu1 + w[2:3] * cu
    oc_ref[0] = (z[:, _C_BG:_C_BG + CONV_WIDTH] * y).astype(BF16)
    carry_ref[...] = ext_ref[tm:tm + 8, :]


def _inproj(x, pos, invf, g, win, qg, wqa, wqb, kvg, wuk, pk, wuvt, convw):
    B, S, D = x.shape
    tm = TM_IN
    full = lambda a: pl.BlockSpec(a.shape, lambda b, i: (0,) * a.ndim)
    return pl.pallas_call(
        _inproj_kernel,
        grid=(B, S // tm),
        in_specs=[
            pl.BlockSpec((1, tm, D), lambda b, i: (b, i, 0)),
            pl.BlockSpec((1, 1, tm), lambda b, i: (b, 0, i)),
            full(invf), full(g), full(win), full(qg), full(wqa), full(wqb), full(kvg), full(wuk),
            full(pk), full(wuvt), full(convw),
        ],
        out_specs=[
            pl.BlockSpec((1, N_HEADS * HEAD_PAD, tm), lambda b, i: (b, 0, i)),
            pl.BlockSpec((1, tm, N_HEADS * HEAD_PAD), lambda b, i: (b, i, 0)),
            pl.BlockSpec((1, N_HEADS * V_ROWS, tm), lambda b, i: (b, 0, i)),
            pl.BlockSpec((1, tm, CONV_WIDTH), lambda b, i: (b, i, 0)),
        ],
        out_shape=[
            jax.ShapeDtypeStruct((B, N_HEADS * HEAD_PAD, S), BF16),
            jax.ShapeDtypeStruct((B, S, N_HEADS * HEAD_PAD), BF16),
            jax.ShapeDtypeStruct((B, N_HEADS * V_ROWS, S), BF16),
            jax.ShapeDtypeStruct((B, S, CONV_WIDTH), BF16),
        ],
        scratch_shapes=[pltpu.VMEM((8, CONV_WIDTH), F32), pltpu.VMEM((tm + 8, CONV_WIDTH), F32)],
        compiler_params=pltpu.CompilerParams(
            dimension_semantics=("arbitrary", "arbitrary"), vmem_limit_bytes=VMEM_LIMIT),
        name="inproj",
    )(x, pos, invf, g, win, qg, wqa, wqb, kvg, wuk, pk, wuvt, convw)


def _attn_kernel(qt_ref, k_ref, vt_ref, o_ref, z_ref, m_ref, acc_ref, sa_ref, sb_ref, zero_ref,
                 zsem):
    tq = qt_ref.shape[2]
    tk = TK
    assert tq == 2 * tk and ATT_TQC == tk
    i = pl.program_id(2)
    m_ref[...] = jnp.full_like(m_ref, NEG)
    acc_ref[...] = jnp.zeros_like(acc_ref)

    step = ((pl.program_id(0) * pl.num_programs(1) + pl.program_id(1)) * pl.num_programs(2) + i)
    zrows = zero_ref.shape[0]

    @pl.when(step == 0)
    def _():
        zero_ref[...] = jnp.zeros_like(zero_ref)

    zero_copy = pltpu.make_async_copy(
        zero_ref, z_ref.at[pl.ds(pl.multiple_of(step * zrows, 8), zrows)], zsem)
    zero_copy.start()

    def scores(j, s_ref, g, col0=0):
        k0 = pl.multiple_of(j * tk, tk)
        qt = qt_ref[0, g * HEAD_PAD:(g + 1) * HEAD_PAD, col0:]
        s_ref[g, :, col0:] = _dot(k_ref[0, pl.ds(k0, tk), g * HEAD_PAD:(g + 1) * HEAD_PAD], qt)

    def softmax_pv(j, s_ref, g, diag):
        k0 = pl.multiple_of(j * tk, tk)
        vt = vt_ref[0, g * V_ROWS:(g + 1) * V_ROWS, pl.ds(k0, tk)]
        for c in range(tq // ATT_TQC):
            if diag is not None and c < diag:
                continue
            cols = slice(c * ATT_TQC, (c + 1) * ATT_TQC)
            s = s_ref[g, :, cols]
            if diag is not None and c == diag:
                krow = lax.broadcasted_iota(jnp.int32, (tk, ATT_TQC), 0)
                qcol = lax.broadcasted_iota(jnp.int32, (tk, ATT_TQC), 1)
                s = jnp.where(krow <= qcol, s, NEG)
            m_old = m_ref[g, :, cols]
            m_new = jnp.maximum(m_old, jnp.max(s, axis=0, keepdims=True))
            alpha = jnp.exp2(m_old - m_new)
            p = jnp.exp2((s - m_new).astype(BF16))
            acc_ref[g, :, cols] = alpha * acc_ref[g, :, cols] + _dot(vt, p)
            m_ref[g, :, cols] = m_new

    def stage(j_next, s_next, j, s_cur, diag=None, next_col0=0):
        for g in range(ATT_HG):
            if j_next is not None:
                scores(j_next, s_next, g, next_col0)
            softmax_pv(j, s_cur, g, diag)

    for g in range(ATT_HG):
        scores(0, sa_ref, g)

    def pair(j):
        stage(j + 1, sb_ref, j, sa_ref)
        stage(j + 2, sa_ref, j + 1, sb_ref)

    def body(t, c):
        pair(4 * t)
        pair(4 * t + 2)
        return c

    lax.fori_loop(0, i // 2, body, 0)

    @pl.when(i % 2 == 1)
    def _():
        pair(2 * i - 2)

    n_full = 2 * i
    stage(n_full + 1, sb_ref, n_full, sa_ref, diag=0, next_col0=ATT_TQC)
    stage(None, None, n_full + 1, sb_ref, diag=1)
    for g in range(ATT_HG):
        o_ref[0, g * V_HEAD_DIM:(g + 1) * V_HEAD_DIM, :] = (
            acc_ref[g, 0:V_HEAD_DIM, :] / acc_ref[g, V_HEAD_DIM:V_HEAD_DIM + 1, :]).astype(o_ref.dtype)
    zero_copy.wait()


def _attention(qt, k, vt, min_zero_rows):
    B, _, S = qt.shape
    hg = ATT_HG
    grid = (B, N_HEADS // hg, S // TQ)
    n_steps = grid[0] * grid[1] * grid[2]
    zrows = -(-min_zero_rows // (8 * n_steps)) * 8
    return pl.pallas_call(
        _attn_kernel,
        grid=grid,
        in_specs=[
            pl.BlockSpec((1, hg * HEAD_PAD, TQ), lambda b, h, i: (b, h, i)),
            pl.BlockSpec((1, S, hg * HEAD_PAD), lambda b, h, i: (b, 0, h)),
            pl.BlockSpec((1, hg * V_ROWS, S), lambda b, h, i: (b, h, 0)),
        ],
        out_specs=[pl.BlockSpec((1, hg * V_HEAD_DIM, TQ), lambda b, h, i: (b, h, i)),
                   pl.BlockSpec(memory_space=pl.ANY)],
        out_shape=[jax.ShapeDtypeStruct((B, MLA_WIDTH, S), BF16),
                   jax.ShapeDtypeStruct((n_steps * zrows, D_MODEL), F32)],
        scratch_shapes=[pltpu.VMEM((hg, 1, TQ), F32), pltpu.VMEM((hg, V_ROWS, TQ), F32),
                        pltpu.VMEM((hg, TK, TQ), F32), pltpu.VMEM((hg, TK, TQ), F32),
                        pltpu.VMEM((zrows, D_MODEL), F32), pltpu.SemaphoreType.DMA],
        compiler_params=pltpu.CompilerParams(
            dimension_semantics=("arbitrary", "arbitrary", "arbitrary"),
            vmem_limit_bytes=VMEM_LIMIT),
        name="attn",
    )(qt, k, vt)


def _route_tile(xn, wrhl_ref, br_ref, carry_ref, after_logits=None):
    tm = xn.shape[0]
    xh = xn.astype(BF16)
    xl = (xn - xh.astype(F32)).astype(BF16)
    hl = _dot_nt(wrhl_ref[...], xh)
    logits = (hl[0:ROUTER_ROWS] + hl[ROUTER_ROWS:2 * ROUTER_ROWS]
              + _dot_nt(wrhl_ref[0:ROUTER_ROWS, :], xl) + br_ref[...])
    if after_logits is not None:
        after_logits()
    big = jnp.int32(1 << 20)

    grow = lax.broadcasted_iota(jnp.int32, (ROUTER_E0, tm), 0)
    glog = jnp.where(grow < N_GROUPS, logits[0:ROUTER_E0], NEG)
    gmax = jnp.max(glog, axis=0, keepdims=True)
    gsum = jnp.sum(jnp.exp(glog - gmax), axis=0, keepdims=True)
    g_p = 1.0 / gsum
    g_idx = jnp.min(jnp.where(glog == gmax, grow, big), axis=0, keepdims=True)

    erow = lax.broadcasted_iota(jnp.int32, (N_EXPERTS, tm), 0)
    e_lo = g_idx * EXPERTS_PER_GROUP
    in_group = (erow >= e_lo) & (erow < e_lo + EXPERTS_PER_GROUP)
    elog = jnp.where(in_group, logits[ROUTER_E0:ROUTER_E0 + N_EXPERTS], NEG)
    emax = jnp.max(elog, axis=0, keepdims=True)
    esum = jnp.sum(jnp.exp(elog - emax), axis=0, keepdims=True)
    e1 = jnp.min(jnp.where(elog == emax, erow, big), axis=0, keepdims=True)
    elog2 = jnp.where(erow == e1, NEG, elog)
    emax2 = jnp.max(elog2, axis=0, keepdims=True)
    e2 = jnp.min(jnp.where(elog2 == emax2, erow, big), axis=0, keepdims=True)
    p1 = 1.0 / esum
    p2 = jnp.exp(emax2 - emax) / esum
    psum = p1 + p2
    w1 = g_p * (p1 / psum)
    w2 = g_p * (p2 / psum)

    oh1 = erow == e1
    oh2 = erow == e2
    oh = jnp.where(oh1 | oh2, 1.0, 0.0)
    srow = lax.broadcasted_iota(jnp.int32, (tm, tm), 0)
    scol = lax.broadcasted_iota(jnp.int32, (tm, tm), 1)
    earlier = jnp.where(srow < scol, 1.0, 0.0).astype(BF16)
    cum = _dot(oh.astype(BF16), earlier) + carry_ref[:, 0:1]
    r1 = jnp.sum(jnp.where(oh1, cum, 0.0), axis=0, keepdims=True)
    r2 = jnp.sum(jnp.where(oh2, cum, 0.0), axis=0, keepdims=True)
    return oh, oh1, oh2, r1, r2, w1, w2


def _outproj_kernel(x_ref, ot_ref, oc_ref, wo_ref, g_ref, wrhl_ref, br_ref, xbz_ref,
                    h1_ref, pos_ref, mw_ref, nb_ref, xb_ref,
                    carry_ref, last_ref, free_ref, xs0_ref, xs1_ref, dv0_ref, dv1_ref,
                    ds0_ref, ds1_ref, rsem, ssem):
    del xbz_ref
    tm = TM
    step = pl.program_id(0) * pl.num_programs(1) + pl.program_id(1)
    last_step = pl.num_programs(0) * pl.num_programs(1) - 1
    xs = (xs0_ref, xs1_ref)
    dv = (dv0_ref, dv1_ref)
    ds = (ds0_ref, ds1_ref)
    spare_row0 = xb_ref.shape[0] - 2 * tm

    def meta_copy(slot):
        return pltpu.make_async_copy(dv[slot], ds[slot], ssem.at[slot])

    def issue_rows(slot, r0=0, r1=TM):
        for r in range(r0, r1):
            for k in range(2):
                pltpu.make_async_copy(xs[slot].at[pl.ds(r, 1)],
                                      xb_ref.at[pl.ds(ds[slot][k, r], 1)], rsem.at[slot]
                                      ).start(priority=k)

    def issue_batch(slot, b):
        bounds = (0, 3 * tm // 8, 6 * tm // 8, tm)

        @pl.when(step >= 0)
        def _():
            issue_rows(slot, bounds[b], bounds[b + 1])

    def wait_rows(slot):
        for _ in range(2):
            pltpu.make_async_copy(xs[slot], xb_ref.at[pl.ds(0, tm)], rsem.at[slot]).wait()

    def compute(slot):
        other = 1 - slot
        rows = slice(slot * tm, (slot + 1) * tm)
        issue_batch(other, 0)
        attn = (_dot_tn(ot_ref[0, :, rows], wo_ref[0:MLA_WIDTH, :])
                + _dot(oc_ref[0, rows, :], wo_ref[MLA_WIDTH:MLA_WIDTH + CONV_WIDTH, :]))
        h1 = x_ref[0, rows, :] + attn
        h1_ref[0, rows, :] = h1
        xn = _rms(h1, g_ref[...])
        xs[slot][...] = xn
        issue_batch(other, 1)
        oh, oh1, oh2, r1, r2, w1, w2 = _route_tile(
            xn, wrhl_ref, br_ref, carry_ref, after_logits=lambda: issue_batch(other, 2))

        inv_blk = 1.0 / MOE_BLK
        cnt = carry_ref[:, 0:1]
        tile_cnt = jnp.sum(oh, axis=1, keepdims=True)
        nb_before = jnp.floor((cnt + (MOE_BLK - 1)) * inv_blk)
        nb_after = jnp.floor((cnt + tile_cnt + (MOE_BLK - 1)) * inv_blk)
        new = nb_after - nb_before
        erow = lax.broadcasted_iota(jnp.int32, (N_EXPERTS, N_EXPERTS), 0)
        ecol = lax.broadcasted_iota(jnp.int32, (N_EXPERTS, N_EXPERTS), 1)
        lower = jnp.where(ecol < erow, 1.0, 0.0).astype(BF16)
        new_rep = jnp.broadcast_to(new, (N_EXPERTS, HEAD_PAD))
        new_id = free_ref[0:1, :] + _dot(lower, new_rep.astype(BF16))
        free_ref[...] = free_ref[...] + jnp.sum(new, axis=0, keepdims=True)
        last_before = last_ref[...]
        is_new = new_rep > 0.0
        last_ref[...] = jnp.where(is_new, new_id, last_before)
        nb_ref[slot] = jnp.where(is_new, new_id, -1.0).astype(jnp.int32)
        carry_ref[...] = carry_ref[...] + tile_cnt

        def place(ohk, rk):
            bi = jnp.floor(rk * inv_blk)
            blk = jnp.where(bi == nb_before - 1.0, last_before[:, 0:1], new_id[:, 0:1])
            phys = jnp.sum(jnp.where(ohk, blk, 0.0), axis=0, keepdims=True)
            return (phys * MOE_BLK + (rk - bi * MOE_BLK)).astype(jnp.int32)

        mrow = lax.broadcasted_iota(jnp.int32, (8, tm), 0)
        dest = jnp.where(mrow == 0, place(oh1, r1), jnp.where(mrow == 1, place(oh2, r2), 0))
        dv[slot][...] = dest
        pos_ref[0, :, rows] = dest
        mw_ref[0, :, rows] = jnp.where(mrow == 0, w1, jnp.where(mrow == 1, w2, 0.0))
        meta_copy(slot).start()

    @pl.when(step == 0)
    def _():
        carry_ref[...] = jnp.zeros_like(carry_ref)
        free_ref[...] = jnp.zeros_like(free_ref)
        last_ref[...] = jnp.full_like(last_ref, -1.0)
        xs1_ref[...] = jnp.zeros_like(xs1_ref)
        mrow = lax.broadcasted_iota(jnp.int32, (8, tm), 0)
        lane = lax.broadcasted_iota(jnp.int32, (8, tm), 1)
        dv1_ref[...] = jnp.where(mrow < 2, spare_row0 + 2 * lane + mrow, 0)
        meta_copy(1).start()

    meta_copy(1).wait()
    compute(0)
    wait_rows(1)
    meta_copy(0).wait()
    compute(1)
    wait_rows(0)

    @pl.when(step == last_step)
    def _():
        meta_copy(1).wait()
        issue_rows(1)
        wait_rows(1)


def _outproj(x, ot, oc, wo, g, wrhl, br, xbz):
    B, S, D = x.shape
    tm2 = 2 * TM
    n_s = S // tm2
    full = lambda a: pl.BlockSpec(a.shape, lambda b, i: (0,) * a.ndim)
    tile = lambda w: pl.BlockSpec((1, tm2, w), lambda b, i: (b, i, 0))
    meta = pl.BlockSpec((1, 8, tm2), lambda b, i: (b, 0, i))
    any_space = pl.BlockSpec(memory_space=pl.ANY)
    return pl.pallas_call(
        _outproj_kernel,
        grid=(B, n_s),
        in_specs=[
            tile(D),
            pl.BlockSpec((1, MLA_WIDTH, tm2), lambda b, i: (b, 0, i)),
            tile(CONV_WIDTH),
            full(wo), full(g), full(wrhl), full(br), any_space,
        ],
        out_specs=[tile(D), meta, meta,
                   pl.BlockSpec((2, N_EXPERTS, HEAD_PAD), lambda b, i: (b * n_s + i, 0, 0)),
                   any_space],
        out_shape=[
            jax.ShapeDtypeStruct((B, S, D), F32),
            jax.ShapeDtypeStruct((B, 8, S), jnp.int32),
            jax.ShapeDtypeStruct((B, 8, S), F32),
            jax.ShapeDtypeStruct((B * S // TM, N_EXPERTS, HEAD_PAD), jnp.int32),
            jax.ShapeDtypeStruct(xbz.shape, xbz.dtype),
        ],
        input_output_aliases={7: 4},
        scratch_shapes=[
            pltpu.VMEM((N_EXPERTS, HEAD_PAD), F32), pltpu.VMEM((N_EXPERTS, HEAD_PAD), F32),
            pltpu.VMEM((8, HEAD_PAD), F32),
            pltpu.VMEM((TM, D), F32), pltpu.VMEM((TM, D), F32),
            pltpu.VMEM((8, TM), jnp.int32), pltpu.VMEM((8, TM), jnp.int32),
            pltpu.SMEM((8, TM), jnp.int32), pltpu.SMEM((8, TM), jnp.int32),
            pltpu.SemaphoreType.DMA((2,)), pltpu.SemaphoreType.DMA((2,)),
        ],
        compiler_params=pltpu.CompilerParams(
            dimension_semantics=("arbitrary", "arbitrary"), vmem_limit_bytes=VMEM_LIMIT),
        name="outproj",
    )(x, ot, oc, wo, g, wrhl, br, xbz)


def _expert_kernel(order_ref, blke_ref, nused_ref, xb_ref, wg_ref, wu_ref, wd_ref, yb_ref,
                   wgs_ref, wus_ref, wds_ref):
    del order_ref
    i = pl.program_id(0)
    used = i < nused_ref[0]
    changed = (i == 0) | (blke_ref[i] != blke_ref[jnp.maximum(i - 1, 0)])

    @pl.when(used & changed)
    def _():
        wgs_ref[...] = wg_ref[0].astype(BF16)
        wus_ref[...] = wu_ref[0].astype(BF16)
        wds_ref[...] = wd_ref[0].astype(BF16)

    @pl.when(used)
    def _():
        xb = xb_ref[...].astype(BF16)
        half = EXPERT_FF // 2
        g0 = _dot(xb, wgs_ref[:, :half])
        u0 = _dot(xb, wus_ref[:, :half])
        g1 = _dot(xb, wgs_ref[:, half:])
        u1 = _dot(xb, wus_ref[:, half:])
        h0 = ((g0 * jax.nn.sigmoid(g0)) * u0).astype(BF16)
        y = _dot(h0, wds_ref[:half, :])
        h1 = ((g1 * jax.nn.sigmoid(g1)) * u1).astype(BF16)
        yb_ref[...] = y + _dot(h1, wds_ref[half:, :])

    @pl.when(jnp.logical_not(used))
    def _():
        yb_ref[...] = jnp.zeros_like(yb_ref)


def _experts(order, blk_e, n_used, xb, wg, wu, wd):
    D = xb.shape[1]
    blk = MOE_BLK
    n_blocks = order.shape[0]
    return pl.pallas_call(
        _expert_kernel,
        grid_spec=pltpu.PrefetchScalarGridSpec(
            num_scalar_prefetch=3,
            grid=(n_blocks,),
            in_specs=[
                pl.BlockSpec((blk, D), lambda i, o, be, nu: (o[jnp.minimum(i, nu[0] - 1)], 0)),
                pl.BlockSpec((1, D, EXPERT_FF), lambda i, o, be, nu: (be[i], 0, 0)),
                pl.BlockSpec((1, D, EXPERT_FF), lambda i, o, be, nu: (be[i], 0, 0)),
                pl.BlockSpec((1, EXPERT_FF, D), lambda i, o, be, nu: (be[i], 0, 0)),
            ],
            out_specs=pl.BlockSpec((blk, D), lambda i, o, be, nu: (o[i], 0)),
            scratch_shapes=[pltpu.VMEM((D, EXPERT_FF), BF16), pltpu.VMEM((D, EXPERT_FF), BF16),
                            pltpu.VMEM((EXPERT_FF, D), BF16)],
        ),
        out_shape=jax.ShapeDtypeStruct((n_blocks * blk, D), F32),
        compiler_params=pltpu.CompilerParams(
            dimension_semantics=("arbitrary",), vmem_limit_bytes=VMEM_LIMIT),
        name="experts",
    )(order, blk_e, n_used, xb, wg, wu, wd)


def _final_kernel(dest_ref, h1_ref, p_ref, wt_ref, yb_ref, gp_ref, wg_ref, bg_ref, wp_ref,
                  gf_ref, out_ref, buf0_ref, buf1_ref, buf2_ref, buf3_ref, sem):
    tm = h1_ref.shape[0] // FIN_TILES
    i = pl.program_id(0)
    bufs = (buf0_ref, buf1_ref, buf2_ref, buf3_ref)

    def row_copy(slot, k, r, d):
        return pltpu.make_async_copy(yb_ref.at[pl.ds(d, 1)], bufs[slot].at[k, pl.ds(r, 1)],
                                     sem.at[slot])

    def issue(tile, slot):
        for r in range(tm):
            t = (tile * tm + r) * 2
            row_copy(slot, 0, r, dest_ref[t]).start(priority=0)
            row_copy(slot, 1, r, dest_ref[t + 1]).start(priority=1)

    def wait(slot):
        for k in range(2):
            pltpu.make_async_copy(yb_ref.at[pl.ds(0, tm)], bufs[slot].at[k], sem.at[slot]).wait()

    def compute(slot):
        rows = slice(slot * tm, (slot + 1) * tm)
        pe = _dot(p_ref[rows, :].astype(BF16), wp_ref[...])
        wt = wt_ref[rows, :]
        h2 = h1_ref[rows, :] + wt[:, 0:1] * bufs[slot][0] + wt[:, 1:2] * bufs[slot][1]
        n = _rms(h2, gp_ref[...]).astype(BF16)
        gate = jax.nn.sigmoid(_dot(n, wg_ref[...]) + bg_ref[...])
        h3 = h2 + gate * pe
        out_ref[rows, :] = _rms(h3, gf_ref[...])

    @pl.when(i == 0)
    def _():
        for j in range(FIN_AHEAD):
            issue(j, j)

    last_tile = FIN_TILES * pl.num_programs(0) - 1
    for j in range(FIN_TILES):
        issue(jnp.minimum(FIN_TILES * i + j + FIN_AHEAD, last_tile), (j + FIN_AHEAD) % FIN_TILES)
        wait(j)
        compute(j)

    @pl.when(i == pl.num_programs(0) - 1)
    def _():
        for j in range(FIN_AHEAD):
            wait(j)


def _final(dest, h1, p, wt, yb, gp, wg, bg, wp, gf):
    N, D = h1.shape
    tm = TM
    full = lambda a: pl.BlockSpec(a.shape, lambda i, dest: (0,) * a.ndim)
    return pl.pallas_call(
        _final_kernel,
        grid_spec=pltpu.PrefetchScalarGridSpec(
            num_scalar_prefetch=1,
            grid=(N // (FIN_TILES * tm),),
            in_specs=[
                pl.BlockSpec((FIN_TILES * tm, D), lambda i, dest: (i, 0)),
                pl.BlockSpec((FIN_TILES * tm, PLE_DIM), lambda i, dest: (i, 0)),
                pl.BlockSpec((FIN_TILES * tm, 2), lambda i, dest: (i, 0)),
                pl.BlockSpec(memory_space=pl.ANY),
                full(gp), full(wg), full(bg), full(wp), full(gf),
            ],
            out_specs=pl.BlockSpec((FIN_TILES * tm, D), lambda i, dest: (i, 0)),
            scratch_shapes=[pltpu.VMEM((2, tm, D), F32) for _ in range(FIN_TILES)]
            + [pltpu.SemaphoreType.DMA((FIN_TILES,))],
        ),
        out_shape=jax.ShapeDtypeStruct((N, D), F32),
        compiler_params=pltpu.CompilerParams(
            dimension_semantics=("arbitrary",), vmem_limit_bytes=VMEM_LIMIT),
        name="final",
    )(dest, h1, p, wt, yb, gp, wg, bg, wp, gf)


def _prep_weights(w_in, w_uq, w_ukv):
    cq, ckv, kr, bg, cg, u = jnp.split(
        w_in, [Q_LORA, Q_LORA + KV_LORA, Q_LORA + KV_LORA + QK_ROPE_DIM,
               Q_LORA + KV_LORA + QK_ROPE_DIM + CONV_WIDTH,
               Q_LORA + KV_LORA + QK_ROPE_DIM + 2 * CONV_WIDTH], axis=1)
    kr_rot = jnp.concatenate([-kr[:, HALF_ROPE:], kr[:, :HALF_ROPE]], axis=1)
    pad = jnp.zeros((D_MODEL, HEAD_PAD - 2 * QK_ROPE_DIM), w_in.dtype)
    win = jnp.concatenate([cq, ckv, bg, cg, u, kr, kr_rot, pad], axis=1).astype(BF16)

    zq = jnp.zeros((Q_LORA, N_HEADS, HEAD_PAD - QK_DIM), w_uq.dtype)
    wqa = jnp.concatenate([w_uq, zq], axis=2)
    rope = w_uq[:, :, QK_NOPE_DIM:]
    wqb = jnp.concatenate([-rope[:, :, HALF_ROPE:], rope[:, :, :HALF_ROPE]], axis=2)
    wqa = wqa.reshape(Q_LORA, N_HEADS * HEAD_PAD).T.astype(BF16)
    wqb = wqb.reshape(Q_LORA, N_HEADS * QK_ROPE_DIM).T.astype(BF16)

    zk = jnp.zeros((KV_LORA, N_HEADS, HEAD_PAD - QK_NOPE_DIM), w_ukv.dtype)
    wuk = jnp.concatenate([w_ukv[:, :, :QK_NOPE_DIM], zk], axis=2)
    wuk = wuk.reshape(KV_LORA, N_HEADS * HEAD_PAD).astype(BF16)
    wuvt = w_ukv[:, :, QK_NOPE_DIM:].reshape(KV_LORA, MLA_WIDTH).T.astype(BF16)

    src = jnp.arange(QK_ROPE_DIM)[:, None]
    dst = jnp.arange(N_HEADS * HEAD_PAD)[None, :]
    pk = (dst % HEAD_PAD - QK_NOPE_DIM == src).astype(BF16)
    return win, wqa, wqb, wuk, wuvt, pk


def kernel(x, p, positions, attn_norm_g, w_in, q_norm_g, w_uq, kv_norm_g, w_ukv, conv_w, w_out,
           moe_norm_g, w_group_router, b_group_router, w_expert_router, b_expert_router,
           w_gate, w_up, w_down, ple_norm_g, w_ple_gate, b_ple_gate, w_ple_proj, final_norm_g):
    B, S, D = x.shape
    N = B * S
    assert w_in.shape[0] == 1, "single-layer trunk: the final norm is fused into the layer"
    pos = positions.astype(F32).reshape(B, 1, S)
    invf = (ROPE_BASE ** (-jnp.arange(0, QK_ROPE_DIM, 2, dtype=F32) / QK_ROPE_DIM)).reshape(-1, 1)
    row = lambda v: v.reshape(1, -1)
    h = x
    for i in range(1):
        win, wqa, wqb, wuk, wuvt, pk = _prep_weights(w_in[i], w_uq[i], w_ukv[i])
        qt, k, vt, oc = _inproj(h, pos, invf, row(attn_norm_g[i]), win, row(q_norm_g[i]), wqa,
                                wqb, row(kv_norm_g[i]), wuk, pk, wuvt,
                                conv_w[i].reshape(CONV_K, CONV_WIDTH))
        n_blocks = (2 * N) // MOE_BLK + N_EXPERTS
        ot, xbz = _attention(qt, k, vt, n_blocks * MOE_BLK + 2 * TM)

        zrow = lambda n: jnp.zeros((n, D), F32)
        wr = jnp.concatenate(
            [w_group_router[i].T, zrow(ROUTER_E0 - N_GROUPS), w_expert_router[i].T,
             zrow(ROUTER_ROWS - ROUTER_E0 - N_EXPERTS)], axis=0)
        wrh = wr.astype(BF16)
        wrhl = jnp.concatenate([wrh, (wr - wrh.astype(F32)).astype(BF16)], axis=0)
        br = jnp.concatenate(
            [b_group_router[i], jnp.zeros((ROUTER_E0 - N_GROUPS,), F32), b_expert_router[i],
             jnp.zeros((ROUTER_ROWS - ROUTER_E0 - N_EXPERTS,), F32)]).reshape(-1, 1)
        h1, pos, meta_w, opened, xb = _outproj(h, ot, oc, w_out[i].astype(BF16),
                                               row(moe_norm_g[i]), wrhl, br, xbz)

        opened = opened[:, :, 0]
        blk_ids = jnp.arange(n_blocks, dtype=jnp.int32)
        hit = opened[None, :, :] == blk_ids[:, None, None]
        used = jnp.any(hit, axis=(1, 2))
        e_of_blk = jnp.sum(jnp.where(hit, jnp.arange(N_EXPERTS, dtype=jnp.int32), 0), axis=(1, 2))
        n_used = jnp.sum(used.astype(jnp.int32)).reshape(1)
        key = jnp.where(used, e_of_blk, N_EXPERTS) * n_blocks + blk_ids
        slot_of_blk = jnp.sum((key[None, :] < key[:, None]).astype(jnp.int32), axis=1)
        at_slot = slot_of_blk[None, :] == blk_ids[:, None]
        order = jnp.sum(jnp.where(at_slot, blk_ids[None, :], 0), axis=1)
        blk_e = jnp.sum(jnp.where(at_slot, e_of_blk[None, :], 0), axis=1)
        blk_e = jnp.where(blk_ids < n_used, blk_e, jnp.max(jnp.where(used, e_of_blk, 0)))
        dest = jnp.swapaxes(pos[:, 0:2, :], 1, 2).reshape(2 * N)
        wt = jnp.swapaxes(meta_w[:, 0:2, :], 1, 2).reshape(N, 2)

        yb = _experts(order, blk_e, n_used, xb, w_gate[i], w_up[i], w_down[i])
        out = _final(dest, h1.reshape(N, D), p[i].reshape(N, PLE_DIM), wt, yb,
                     row(ple_norm_g[i]), w_ple_gate[i].astype(BF16), row(b_ple_gate[i]),
                     w_ple_proj[i].astype(BF16), row(final_norm_g))
        h = out.reshape(B, S, D)
    return h
```

```python
import functools
import math

import jax
import jax.numpy as jnp
from jax import lax
from jax.experimental import pallas as pl
from jax.experimental.pallas import tpu as pltpu

D_MODEL = 1024
PLE_DIM = 256
MLA_WIDTH = 512
CONV_WIDTH = 512
N_HEADS = 8
V_HEAD_DIM = 64
QK_NOPE_DIM = 64
QK_ROPE_DIM = 32
Q_LORA = 384
KV_LORA = 256
CONV_K = 3
N_GROUPS = 4
EXPERTS_PER_GROUP = 8
N_EXPERTS = N_GROUPS * EXPERTS_PER_GROUP
EXPERT_FF = 512
ROPE_BASE = 10000.0
EPS = 1e-6

HEAD_PAD = 128
V_ROWS = 80
QK_DIM = QK_NOPE_DIM + QK_ROPE_DIM
HALF_ROPE = QK_ROPE_DIM // 2

_C_Q = 0
_C_KV = _C_Q + Q_LORA
_C_BG = _C_KV + KV_LORA
_C_CG = _C_BG + CONV_WIDTH
_C_U = _C_CG + CONV_WIDTH
_C_KR = _C_U + CONV_WIDTH
IN_COLS_PAD = _C_KR + HEAD_PAD

TM = 256
TM_IN = 512
FIN_TILES = 4
FIN_AHEAD = 2
TQ = 1024
TK = 256
ATT_HG = 4
ATT_TQC = 256
MOE_BLK = 512
ROUTER_E0 = 8
ROUTER_ROWS = 48
NEG = -1e30
Q_SCALE = (QK_DIM ** -0.5) * math.log2(math.e)
VMEM_LIMIT = 48 * 1024 * 1024

F32 = jnp.float32
BF16 = jnp.bfloat16


def _rms(x, g):
    return x * lax.rsqrt(jnp.mean(x * x, axis=-1, keepdims=True) + EPS) * g


def _dot(a, b):
    return jnp.dot(a, b, preferred_element_type=F32)


def _dot_nt(a, b):
    return lax.dot_general(a, b, (((1,), (1,)), ((), ())), preferred_element_type=F32)


def _dot_tn(a, b):
    return lax.dot_general(a, b, (((0,), (0,)), ((), ())), preferred_element_type=F32)


def _inproj_kernel(x_ref, pos_ref, invf_ref, g_ref, win_ref, qg_ref, wqa_ref, wqb_ref, kvg_ref,
                   wuk_ref, pk_ref, wuvt_ref, convw_ref,
                   qt_ref, k_ref, vt_ref, oc_ref, carry_ref, ext_ref):
    tm = x_ref.shape[1]
    x = x_ref[0]
    xn = _rms(x, g_ref[...])
    z = _dot(xn.astype(BF16), win_ref[...])

    ang = invf_ref[...] * pos_ref[0]
    cos = jnp.cos(ang)
    sin = jnp.sin(ang)
    cos2 = jnp.concatenate([cos, cos], axis=0)
    sin2 = jnp.concatenate([sin, sin], axis=0)

    cqn = _rms(z[:, _C_Q:_C_Q + Q_LORA], qg_ref[...]).astype(BF16)
    qa = _dot_nt(wqa_ref[...], cqn)
    qb = _dot_nt(wqb_ref[...], cqn)
    for h in range(N_HEADS):
        r0 = h * HEAD_PAD
        nope = qa[r0:r0 + QK_NOPE_DIM]
        rope = (qa[r0 + QK_NOPE_DIM:r0 + QK_DIM] * cos2
                + qb[h * QK_ROPE_DIM:(h + 1) * QK_ROPE_DIM] * sin2)
        qh = jnp.concatenate([nope, rope, qa[r0 + QK_DIM:r0 + HEAD_PAD]], axis=0) * Q_SCALE
        qt_ref[0, r0:r0 + HEAD_PAD, :] = qh.astype(BF16)

    kvn = _rms(z[:, _C_KV:_C_KV + KV_LORA], kvg_ref[...]).astype(BF16)
    krt = z[:, _C_KR:_C_KR + HEAD_PAD].T
    krot = krt[0:QK_ROPE_DIM] * cos2 + krt[QK_ROPE_DIM:2 * QK_ROPE_DIM] * sin2
    k_ref[0] = (_dot(kvn, wuk_ref[...]) + _dot_tn(krot.astype(BF16), pk_ref[...])).astype(BF16)
    vt = _dot_nt(wuvt_ref[...], kvn).astype(BF16)
    ones_row = jnp.where(lax.broadcasted_iota(jnp.int32, (V_ROWS - V_HEAD_DIM, tm), 0) == 0,
                         1.0, 0.0).astype(BF16)
    for h in range(N_HEADS):
        vt_ref[0, h * V_ROWS:h * V_ROWS + V_HEAD_DIM, :] = vt[h * V_HEAD_DIM:(h + 1) * V_HEAD_DIM]
        vt_ref[0, h * V_ROWS + V_HEAD_DIM:(h + 1) * V_ROWS, :] = ones_row

    @pl.when(pl.program_id(1) == 0)
    def _():
        carry_ref[...] = jnp.zeros_like(carry_ref)

    cu = z[:, _C_CG:_C_CG + CONV_WIDTH] * z[:, _C_U:_C_U + CONV_WIDTH]
    ext_ref[0:8, :] = carry_ref[...]
    ext_ref[8:8 + tm, :] = cu
    cu1 = ext_ref[7:7 + tm, :]
    cu2 = ext_ref[6:6 + tm, :]
    w = convw_ref[...]
    y = w[0:1] * cu2 + w[1:2] * cu1 + w[2:3] * cu
    oc_ref[0] = (z[:, _C_BG:_C_BG + CONV_WIDTH] * y).astype(BF16)
    carry_ref[...] = ext_ref[tm:tm + 8, :]


def _inproj(x, pos, invf, g, win, qg, wqa, wqb, kvg, wuk, pk, wuvt, convw):
    B, S, D = x.shape
    tm = TM_IN
    full = lambda a: pl.BlockSpec(a.shape, lambda b, i: (0,) * a.ndim)
    return pl.pallas_call(
        _inproj_kernel,
        grid=(B, S // tm),
        in_specs=[
            pl.BlockSpec((1, tm, D), lambda b, i: (b, i, 0)),
            pl.BlockSpec((1, 1, tm), lambda b, i: (b, 0, i)),
            full(invf), full(g), full(win), full(qg), full(wqa), full(wqb), full(kvg), full(wuk),
            full(pk), full(wuvt), full(convw),
        ],
        out_specs=[
            pl.BlockSpec((1, N_HEADS * HEAD_PAD, tm), lambda b, i: (b, 0, i)),
            pl.BlockSpec((1, tm, N_HEADS * HEAD_PAD), lambda b, i: (b, i, 0)),
            pl.BlockSpec((1, N_HEADS * V_ROWS, tm), lambda b, i: (b, 0, i)),
            pl.BlockSpec((1, tm, CONV_WIDTH), lambda b, i: (b, i, 0)),
        ],
        out_shape=[
            jax.ShapeDtypeStruct((B, N_HEADS * HEAD_PAD, S), BF16),
            jax.ShapeDtypeStruct((B, S, N_HEADS * HEAD_PAD), BF16),
            jax.ShapeDtypeStruct((B, N_HEADS * V_ROWS, S), BF16),
            jax.ShapeDtypeStruct((B, S, CONV_WIDTH), BF16),
        ],
        scratch_shapes=[pltpu.VMEM((8, CONV_WIDTH), F32), pltpu.VMEM((tm + 8, CONV_WIDTH), F32)],
        compiler_params=pltpu.CompilerParams(
            dimension_semantics=("arbitrary", "arbitrary"), vmem_limit_bytes=VMEM_LIMIT),
        name="inproj",
    )(x, pos, invf, g, win, qg, wqa, wqb, kvg, wuk, pk, wuvt, convw)


def _attn_kernel(qt_ref, k_ref, vt_ref, o_ref, z_ref, m_ref, acc_ref, sa_ref, sb_ref, zero_ref,
                 zsem):
    tq = qt_ref.shape[2]
    tk = TK
    n_diag = tq // tk
    assert n_diag % 2 == 0 and ATT_TQC == tk
    i = pl.program_id(2)
    m_ref[...] = jnp.full_like(m_ref, NEG)
    acc_ref[...] = jnp.zeros_like(acc_ref)

    step = ((pl.program_id(0) * pl.num_programs(1) + pl.program_id(1)) * pl.num_programs(2) + i)
    zrows = zero_ref.shape[0]

    @pl.when(step == 0)
    def _():
        zero_ref[...] = jnp.zeros_like(zero_ref)

    zero_copy = pltpu.make_async_copy(
        zero_ref, z_ref.at[pl.ds(pl.multiple_of(step * zrows, 8), zrows)], zsem)
    zero_copy.start()

    def scores(j, s_ref, g, col0=0):
        k0 = pl.multiple_of(j * tk, tk)
        qt = qt_ref[0, g * HEAD_PAD:(g + 1) * HEAD_PAD, col0:]
        s_ref[g, :, col0:] = _dot(k_ref[0, pl.ds(k0, tk), g * HEAD_PAD:(g + 1) * HEAD_PAD], qt)

    def softmax_pv(j, s_ref, g, diag):
        k0 = pl.multiple_of(j * tk, tk)
        vt = vt_ref[0, g * V_ROWS:(g + 1) * V_ROWS, pl.ds(k0, tk)]
        for c in range(tq // ATT_TQC):
            if diag is not None and c < diag:
                continue
            cols = slice(c * ATT_TQC, (c + 1) * ATT_TQC)
            s = s_ref[g, :, cols]
            if diag is not None and c == diag:
                krow = lax.broadcasted_iota(jnp.int32, (tk, ATT_TQC), 0)
                qcol = lax.broadcasted_iota(jnp.int32, (tk, ATT_TQC), 1)
                s = jnp.where(krow <= qcol, s, NEG)
            m_old = m_ref[g, :, cols]
            m_new = jnp.maximum(m_old, jnp.max(s, axis=0, keepdims=True))
            alpha = jnp.exp2(m_old - m_new)
            p = jnp.exp2((s - m_new).astype(BF16))
            acc_ref[g, :, cols] = alpha * acc_ref[g, :, cols] + _dot(vt, p)
            m_ref[g, :, cols] = m_new

    def stage(j_next, s_next, j, s_cur, diag=None, next_col0=0):
        for g in range(ATT_HG):
            if j_next is not None:
                scores(j_next, s_next, g, next_col0)
            softmax_pv(j, s_cur, g, diag)

    for g in range(ATT_HG):
        scores(0, sa_ref, g)

    def pair(j):
        stage(j + 1, sb_ref, j, sa_ref)
        stage(j + 2, sa_ref, j + 1, sb_ref)

    def body(t, c):
        for u in range(n_diag // 2):
            pair(n_diag * t + 2 * u)
        return c

    lax.fori_loop(0, i, body, 0)

    n_full = n_diag * i
    s_bufs = (sa_ref, sb_ref)
    for d in range(n_diag):
        if d + 1 < n_diag:
            stage(n_full + d + 1, s_bufs[(d + 1) % 2], n_full + d, s_bufs[d % 2], diag=d,
                  next_col0=(d + 1) * ATT_TQC)
        else:
            stage(None, None, n_full + d, s_bufs[d % 2], diag=d)
    for g in range(ATT_HG):
        o_ref[0, g * V_HEAD_DIM:(g + 1) * V_HEAD_DIM, :] = (
            acc_ref[g, 0:V_HEAD_DIM, :] / acc_ref[g, V_HEAD_DIM:V_HEAD_DIM + 1, :]).astype(o_ref.dtype)
    zero_copy.wait()


def _attention(qt, k, vt, min_zero_rows):
    B, _, S = qt.shape
    hg = ATT_HG
    grid = (B, N_HEADS // hg, S // TQ)
    n_steps = grid[0] * grid[1] * grid[2]
    zrows = -(-min_zero_rows // (8 * n_steps)) * 8
    return pl.pallas_call(
        _attn_kernel,
        grid=grid,
        in_specs=[
            pl.BlockSpec((1, hg * HEAD_PAD, TQ), lambda b, h, i: (b, h, i)),
            pl.BlockSpec((1, S, hg * HEAD_PAD), lambda b, h, i: (b, 0, h)),
            pl.BlockSpec((1, hg * V_ROWS, S), lambda b, h, i: (b, h, 0)),
        ],
        out_specs=[pl.BlockSpec((1, hg * V_HEAD_DIM, TQ), lambda b, h, i: (b, h, i)),
                   pl.BlockSpec(memory_space=pl.ANY)],
        out_shape=[jax.ShapeDtypeStruct((B, MLA_WIDTH, S), BF16),
                   jax.ShapeDtypeStruct((n_steps * zrows, D_MODEL), F32)],
        scratch_shapes=[pltpu.VMEM((hg, 1, TQ), F32), pltpu.VMEM((hg, V_ROWS, TQ), F32),
                        pltpu.VMEM((hg, TK, TQ), F32), pltpu.VMEM((hg, TK, TQ), F32),
                        pltpu.VMEM((zrows, D_MODEL), F32), pltpu.SemaphoreType.DMA],
        compiler_params=pltpu.CompilerParams(
            dimension_semantics=("arbitrary", "arbitrary", "arbitrary"),
            vmem_limit_bytes=VMEM_LIMIT),
        name="attn",
    )(qt, k, vt)


def _route_tile(xn, wrhl_ref, br_ref, carry_ref, after_logits=None):
    tm = xn.shape[0]
    xh = xn.astype(BF16)
    xl = (xn - xh.astype(F32)).astype(BF16)
    hl = _dot_nt(wrhl_ref[...], xh)
    logits = (hl[0:ROUTER_ROWS] + hl[ROUTER_ROWS:2 * ROUTER_ROWS]
              + _dot_nt(wrhl_ref[0:ROUTER_ROWS, :], xl) + br_ref[...])
    if after_logits is not None:
        after_logits()
    big = jnp.int32(1 << 20)

    grow = lax.broadcasted_iota(jnp.int32, (ROUTER_E0, tm), 0)
    glog = jnp.where(grow < N_GROUPS, logits[0:ROUTER_E0], NEG)
    gmax = jnp.max(glog, axis=0, keepdims=True)
    gsum = jnp.sum(jnp.exp(glog - gmax), axis=0, keepdims=True)
    g_p = 1.0 / gsum
    g_idx = jnp.min(jnp.where(glog == gmax, grow, big), axis=0, keepdims=True)

    erow = lax.broadcasted_iota(jnp.int32, (N_EXPERTS, tm), 0)
    e_lo = g_idx * EXPERTS_PER_GROUP
    in_group = (erow >= e_lo) & (erow < e_lo + EXPERTS_PER_GROUP)
    elog = jnp.where(in_group, logits[ROUTER_E0:ROUTER_E0 + N_EXPERTS], NEG)
    emax = jnp.max(elog, axis=0, keepdims=True)
    esum = jnp.sum(jnp.exp(elog - emax), axis=0, keepdims=True)
    e1 = jnp.min(jnp.where(elog == emax, erow, big), axis=0, keepdims=True)
    elog2 = jnp.where(erow == e1, NEG, elog)
    emax2 = jnp.max(elog2, axis=0, keepdims=True)
    e2 = jnp.min(jnp.where(elog2 == emax2, erow, big), axis=0, keepdims=True)
    p1 = 1.0 / esum
    p2 = jnp.exp(emax2 - emax) / esum
    psum = p1 + p2
    w1 = g_p * (p1 / psum)
    w2 = g_p * (p2 / psum)

    oh1 = erow == e1
    oh2 = erow == e2
    oh = jnp.where(oh1 | oh2, 1.0, 0.0)
    srow = lax.broadcasted_iota(jnp.int32, (tm, tm), 0)
    scol = lax.broadcasted_iota(jnp.int32, (tm, tm), 1)
    earlier = jnp.where(srow < scol, 1.0, 0.0).astype(BF16)
    cum = _dot(oh.astype(BF16), earlier) + carry_ref[:, 0:1]
    r1 = jnp.sum(jnp.where(oh1, cum, 0.0), axis=0, keepdims=True)
    r2 = jnp.sum(jnp.where(oh2, cum, 0.0), axis=0, keepdims=True)
    return oh, oh1, oh2, r1, r2, w1, w2


def _outproj_kernel(x_ref, ot_ref, oc_ref, wo_ref, g_ref, wrhl_ref, br_ref, xbz_ref,
                    h1_ref, pos_ref, mw_ref, nb_ref, xb_ref,
                    carry_ref, last_ref, free_ref, xs0_ref, xs1_ref, dv0_ref, dv1_ref,
                    ds0_ref, ds1_ref, rsem, ssem):
    del xbz_ref
    tm = TM
    step = pl.program_id(0) * pl.num_programs(1) + pl.program_id(1)
    last_step = pl.num_programs(0) * pl.num_programs(1) - 1
    xs = (xs0_ref, xs1_ref)
    dv = (dv0_ref, dv1_ref)
    ds = (ds0_ref, ds1_ref)
    spare_row0 = xb_ref.shape[0] - 2 * tm

    def meta_copy(slot):
        return pltpu.make_async_copy(dv[slot], ds[slot], ssem.at[slot])

    def issue_rows(slot, r0=0, r1=TM):
        for r in range(r0, r1):
            for k in range(2):
                pltpu.make_async_copy(xs[slot].at[pl.ds(r, 1)],
                                      xb_ref.at[pl.ds(ds[slot][k, r], 1)], rsem.at[slot]
                                      ).start(priority=k)

    def issue_batch(slot, b):
        bounds = (0, 3 * tm // 8, 6 * tm // 8, tm)

        @pl.when(step >= 0)
        def _():
            issue_rows(slot, bounds[b], bounds[b + 1])

    def wait_rows(slot):
        for _ in range(2):
            pltpu.make_async_copy(xs[slot], xb_ref.at[pl.ds(0, tm)], rsem.at[slot]).wait()

    def compute(slot):
        other = 1 - slot
        rows = slice(slot * tm, (slot + 1) * tm)
        issue_batch(other, 0)
        attn = (_dot_tn(ot_ref[0, :, rows], wo_ref[0:MLA_WIDTH, :])
                + _dot(oc_ref[0, rows, :], wo_ref[MLA_WIDTH:MLA_WIDTH + CONV_WIDTH, :]))
        h1 = x_ref[0, rows, :] + attn
        h1_ref[0, rows, :] = h1
        xn = _rms(h1, g_ref[...])
        xs[slot][...] = xn
        issue_batch(other, 1)
        oh, oh1, oh2, r1, r2, w1, w2 = _route_tile(
            xn, wrhl_ref, br_ref, carry_ref, after_logits=lambda: issue_batch(other, 2))

        inv_blk = 1.0 / MOE_BLK
        cnt = carry_ref[:, 0:1]
        tile_cnt = jnp.sum(oh, axis=1, keepdims=True)
        nb_before = jnp.floor((cnt + (MOE_BLK - 1)) * inv_blk)
        nb_after = jnp.floor((cnt + tile_cnt + (MOE_BLK - 1)) * inv_blk)
        new = nb_after - nb_before
        erow = lax.broadcasted_iota(jnp.int32, (N_EXPERTS, N_EXPERTS), 0)
        ecol = lax.broadcasted_iota(jnp.int32, (N_EXPERTS, N_EXPERTS), 1)
        lower = jnp.where(ecol < erow, 1.0, 0.0).astype(BF16)
        new_rep = jnp.broadcast_to(new, (N_EXPERTS, HEAD_PAD))
        new_id = free_ref[0:1, :] + _dot(lower, new_rep.astype(BF16))
        free_ref[...] = free_ref[...] + jnp.sum(new, axis=0, keepdims=True)
        last_before = last_ref[...]
        is_new = new_rep > 0.0
        last_ref[...] = jnp.where(is_new, new_id, last_before)
        nb_ref[slot] = jnp.where(is_new, new_id, -1.0).astype(jnp.int32)
        carry_ref[...] = carry_ref[...] + tile_cnt

        def place(ohk, rk):
            bi = jnp.floor(rk * inv_blk)
            blk = jnp.where(bi == nb_before - 1.0, last_before[:, 0:1], new_id[:, 0:1])
            phys = jnp.sum(jnp.where(ohk, blk, 0.0), axis=0, keepdims=True)
            return (phys * MOE_BLK + (rk - bi * MOE_BLK)).astype(jnp.int32)

        mrow = lax.broadcasted_iota(jnp.int32, (8, tm), 0)
        dest = jnp.where(mrow == 0, place(oh1, r1), jnp.where(mrow == 1, place(oh2, r2), 0))
        dv[slot][...] = dest
        pos_ref[0, :, rows] = dest
        mw_ref[0, :, rows] = jnp.where(mrow == 0, w1, jnp.where(mrow == 1, w2, 0.0))
        meta_copy(slot).start()

    @pl.when(step == 0)
    def _():
        carry_ref[...] = jnp.zeros_like(carry_ref)
        free_ref[...] = jnp.zeros_like(free_ref)
        last_ref[...] = jnp.full_like(last_ref, -1.0)
        xs1_ref[...] = jnp.zeros_like(xs1_ref)
        mrow = lax.broadcasted_iota(jnp.int32, (8, tm), 0)
        lane = lax.broadcasted_iota(jnp.int32, (8, tm), 1)
        dv1_ref[...] = jnp.where(mrow < 2, spare_row0 + 2 * lane + mrow, 0)
        meta_copy(1).start()

    meta_copy(1).wait()
    compute(0)
    wait_rows(1)
    meta_copy(0).wait()
    compute(1)
    wait_rows(0)

    @pl.when(step == last_step)
    def _():
        meta_copy(1).wait()
        issue_rows(1)
        wait_rows(1)


def _outproj(x, ot, oc, wo, g, wrhl, br, xbz):
    B, S, D = x.shape
    tm2 = 2 * TM
    n_s = S // tm2
    full = lambda a: pl.BlockSpec(a.shape, lambda b, i: (0,) * a.ndim)
    tile = lambda w: pl.BlockSpec((1, tm2, w), lambda b, i: (b, i, 0))
    meta = pl.BlockSpec((1, 8, tm2), lambda b, i: (b, 0, i))
    any_space = pl.BlockSpec(memory_space=pl.ANY)
    return pl.pallas_call(
        _outproj_kernel,
        grid=(B, n_s),
        in_specs=[
            tile(D),
            pl.BlockSpec((1, MLA_WIDTH, tm2), lambda b, i: (b, 0, i)),
            tile(CONV_WIDTH),
            full(wo), full(g), full(wrhl), full(br), any_space,
        ],
        out_specs=[tile(D), meta, meta,
                   pl.BlockSpec((2, N_EXPERTS, HEAD_PAD), lambda b, i: (b * n_s + i, 0, 0)),
                   any_space],
        out_shape=[
            jax.ShapeDtypeStruct((B, S, D), F32),
            jax.ShapeDtypeStruct((B, 8, S), jnp.int32),
            jax.ShapeDtypeStruct((B, 8, S), F32),
            jax.ShapeDtypeStruct((B * S // TM, N_EXPERTS, HEAD_PAD), jnp.int32),
            jax.ShapeDtypeStruct(xbz.shape, xbz.dtype),
        ],
        input_output_aliases={7: 4},
        scratch_shapes=[
            pltpu.VMEM((N_EXPERTS, HEAD_PAD), F32), pltpu.VMEM((N_EXPERTS, HEAD_PAD), F32),
            pltpu.VMEM((8, HEAD_PAD), F32),
            pltpu.VMEM((TM, D), F32), pltpu.VMEM((TM, D), F32),
            pltpu.VMEM((8, TM), jnp.int32), pltpu.VMEM((8, TM), jnp.int32),
            pltpu.SMEM((8, TM), jnp.int32), pltpu.SMEM((8, TM), jnp.int32),
            pltpu.SemaphoreType.DMA((2,)), pltpu.SemaphoreType.DMA((2,)),
        ],
        compiler_params=pltpu.CompilerParams(
            dimension_semantics=("arbitrary", "arbitrary"), vmem_limit_bytes=VMEM_LIMIT),
        name="outproj",
    )(x, ot, oc, wo, g, wrhl, br, xbz)


def _expert_kernel(order_ref, blke_ref, nused_ref, xb_ref, wg_ref, wu_ref, wd_ref, yb_ref,
                   wgs_ref, wus_ref, wds_ref):
    del order_ref
    i = pl.program_id(0)
    used = i < nused_ref[0]
    changed = (i == 0) | (blke_ref[i] != blke_ref[jnp.maximum(i - 1, 0)])

    @pl.when(used & changed)
    def _():
        wgs_ref[...] = wg_ref[0].astype(BF16)
        wus_ref[...] = wu_ref[0].astype(BF16)
        wds_ref[...] = wd_ref[0].astype(BF16)

    @pl.when(used)
    def _():
        xb = xb_ref[...].astype(BF16)
        half = EXPERT_FF // 2
        g0 = _dot(xb, wgs_ref[:, :half])
        u0 = _dot(xb, wus_ref[:, :half])
        g1 = _dot(xb, wgs_ref[:, half:])
        u1 = _dot(xb, wus_ref[:, half:])
        h0 = ((g0 * jax.nn.sigmoid(g0)) * u0).astype(BF16)
        y = _dot(h0, wds_ref[:half, :])
        h1 = ((g1 * jax.nn.sigmoid(g1)) * u1).astype(BF16)
        yb_ref[...] = y + _dot(h1, wds_ref[half:, :])

    @pl.when(jnp.logical_not(used))
    def _():
        yb_ref[...] = jnp.zeros_like(yb_ref)


def _experts(order, blk_e, n_used, xb, wg, wu, wd):
    D = xb.shape[1]
    blk = MOE_BLK
    n_blocks = order.shape[0]
    return pl.pallas_call(
        _expert_kernel,
        grid_spec=pltpu.PrefetchScalarGridSpec(
            num_scalar_prefetch=3,
            grid=(n_blocks,),
            in_specs=[
                pl.BlockSpec((blk, D), lambda i, o, be, nu: (o[jnp.minimum(i, nu[0] - 1)], 0)),
                pl.BlockSpec((1, D, EXPERT_FF), lambda i, o, be, nu: (be[i], 0, 0)),
                pl.BlockSpec((1, D, EXPERT_FF), lambda i, o, be, nu: (be[i], 0, 0)),
                pl.BlockSpec((1, EXPERT_FF, D), lambda i, o, be, nu: (be[i], 0, 0)),
            ],
            out_specs=pl.BlockSpec((blk, D), lambda i, o, be, nu: (o[i], 0)),
            scratch_shapes=[pltpu.VMEM((D, EXPERT_FF), BF16), pltpu.VMEM((D, EXPERT_FF), BF16),
                            pltpu.VMEM((EXPERT_FF, D), BF16)],
        ),
        out_shape=jax.ShapeDtypeStruct((n_blocks * blk, D), F32),
        compiler_params=pltpu.CompilerParams(
            dimension_semantics=("arbitrary",), vmem_limit_bytes=VMEM_LIMIT),
        name="experts",
    )(order, blk_e, n_used, xb, wg, wu, wd)


def _final_kernel(dest_ref, h1_ref, p_ref, wt_ref, yb_ref, gp_ref, wg_ref, bg_ref, wp_ref,
                  gf_ref, out_ref, buf0_ref, buf1_ref, buf2_ref, buf3_ref, sem):
    tm = h1_ref.shape[0] // FIN_TILES
    i = pl.program_id(0)
    bufs = (buf0_ref, buf1_ref, buf2_ref, buf3_ref)

    def row_copy(slot, k, r, d):
        return pltpu.make_async_copy(yb_ref.at[pl.ds(d, 1)], bufs[slot].at[k, pl.ds(r, 1)],
                                     sem.at[slot])

    def issue(tile, slot):
        for r in range(tm):
            t = (tile * tm + r) * 2
            row_copy(slot, 0, r, dest_ref[t]).start(priority=0)
            row_copy(slot, 1, r, dest_ref[t + 1]).start(priority=1)

    def wait(slot):
        for k in range(2):
            pltpu.make_async_copy(yb_ref.at[pl.ds(0, tm)], bufs[slot].at[k], sem.at[slot]).wait()

    def compute(slot):
        rows = slice(slot * tm, (slot + 1) * tm)
        pe = _dot(p_ref[rows, :].astype(BF16), wp_ref[...])
        wt = wt_ref[rows, :]
        h2 = h1_ref[rows, :] + wt[:, 0:1] * bufs[slot][0] + wt[:, 1:2] * bufs[slot][1]
        n = _rms(h2, gp_ref[...]).astype(BF16)
        gate = jax.nn.sigmoid(_dot(n, wg_ref[...]) + bg_ref[...])
        h3 = h2 + gate * pe
        out_ref[rows, :] = _rms(h3, gf_ref[...])

    @pl.when(i == 0)
    def _():
        for j in range(FIN_AHEAD):
            issue(j, j)

    last_tile = FIN_TILES * pl.num_programs(0) - 1
    for j in range(FIN_TILES):
        issue(jnp.minimum(FIN_TILES * i + j + FIN_AHEAD, last_tile), (j + FIN_AHEAD) % FIN_TILES)
        wait(j)
        compute(j)

    @pl.when(i == pl.num_programs(0) - 1)
    def _():
        for j in range(FIN_AHEAD):
            wait(j)


def _final(dest, h1, p, wt, yb, gp, wg, bg, wp, gf):
    N, D = h1.shape
    tm = TM
    full = lambda a: pl.BlockSpec(a.shape, lambda i, dest: (0,) * a.ndim)
    return pl.pallas_call(
        _final_kernel,
        grid_spec=pltpu.PrefetchScalarGridSpec(
            num_scalar_prefetch=1,
            grid=(N // (FIN_TILES * tm),),
            in_specs=[
                pl.BlockSpec((FIN_TILES * tm, D), lambda i, dest: (i, 0)),
                pl.BlockSpec((FIN_TILES * tm, PLE_DIM), lambda i, dest: (i, 0)),
                pl.BlockSpec((FIN_TILES * tm, 2), lambda i, dest: (i, 0)),
                pl.BlockSpec(memory_space=pl.ANY),
                full(gp), full(wg), full(bg), full(wp), full(gf),
            ],
            out_specs=pl.BlockSpec((FIN_TILES * tm, D), lambda i, dest: (i, 0)),
            scratch_shapes=[pltpu.VMEM((2, tm, D), F32) for _ in range(FIN_TILES)]
            + [pltpu.SemaphoreType.DMA((FIN_TILES,))],
        ),
        out_shape=jax.ShapeDtypeStruct((N, D), F32),
        compiler_params=pltpu.CompilerParams(
            dimension_semantics=("arbitrary",), vmem_limit_bytes=VMEM_LIMIT),
        name="final",
    )(dest, h1, p, wt, yb, gp, wg, bg, wp, gf)


def _prep_weights(w_in, w_uq, w_ukv):
    cq, ckv, kr, bg, cg, u = jnp.split(
        w_in, [Q_LORA, Q_LORA + KV_LORA, Q_LORA + KV_LORA + QK_ROPE_DIM,
               Q_LORA + KV_LORA + QK_ROPE_DIM + CONV_WIDTH,
               Q_LORA + KV_LORA + QK_ROPE_DIM + 2 * CONV_WIDTH], axis=1)
    kr_rot = jnp.concatenate([-kr[:, HALF_ROPE:], kr[:, :HALF_ROPE]], axis=1)
    pad = jnp.zeros((D_MODEL, HEAD_PAD - 2 * QK_ROPE_DIM), w_in.dtype)
    win = jnp.concatenate([cq, ckv, bg, cg, u, kr, kr_rot, pad], axis=1).astype(BF16)

    zq = jnp.zeros((Q_LORA, N_HEADS, HEAD_PAD - QK_DIM), w_uq.dtype)
    wqa = jnp.concatenate([w_uq, zq], axis=2)
    rope = w_uq[:, :, QK_NOPE_DIM:]
    wqb = jnp.concatenate([-rope[:, :, HALF_ROPE:], rope[:, :, :HALF_ROPE]], axis=2)
    wqa = wqa.reshape(Q_LORA, N_HEADS * HEAD_PAD).T.astype(BF16)
    wqb = wqb.reshape(Q_LORA, N_HEADS * QK_ROPE_DIM).T.astype(BF16)

    zk = jnp.zeros((KV_LORA, N_HEADS, HEAD_PAD - QK_NOPE_DIM), w_ukv.dtype)
    wuk = jnp.concatenate([w_ukv[:, :, :QK_NOPE_DIM], zk], axis=2)
    wuk = wuk.reshape(KV_LORA, N_HEADS * HEAD_PAD).astype(BF16)
    wuvt = w_ukv[:, :, QK_NOPE_DIM:].reshape(KV_LORA, MLA_WIDTH).T.astype(BF16)

    src = jnp.arange(QK_ROPE_DIM)[:, None]
    dst = jnp.arange(N_HEADS * HEAD_PAD)[None, :]
    pk = (dst % HEAD_PAD - QK_NOPE_DIM == src).astype(BF16)
    return win, wqa, wqb, wuk, wuvt, pk


def kernel(x, p, positions, attn_norm_g, w_in, q_norm_g, w_uq, kv_norm_g, w_ukv, conv_w, w_out,
           moe_norm_g, w_group_router, b_group_router, w_expert_router, b_expert_router,
           w_gate, w_up, w_down, ple_norm_g, w_ple_gate, b_ple_gate, w_ple_proj, final_norm_g):
    B, S, D = x.shape
    N = B * S
    assert w_in.shape[0] == 1, "single-layer trunk: the final norm is fused into the layer"
    pos = positions.astype(F32).reshape(B, 1, S)
    invf = (ROPE_BASE ** (-jnp.arange(0, QK_ROPE_DIM, 2, dtype=F32) / QK_ROPE_DIM)).reshape(-1, 1)
    row = lambda v: v.reshape(1, -1)
    h = x
    for i in range(1):
        win, wqa, wqb, wuk, wuvt, pk = _prep_weights(w_in[i], w_uq[i], w_ukv[i])
        qt, k, vt, oc = _inproj(h, pos, invf, row(attn_norm_g[i]), win, row(q_norm_g[i]), wqa,
                                wqb, row(kv_norm_g[i]), wuk, pk, wuvt,
                                conv_w[i].reshape(CONV_K, CONV_WIDTH))
        n_blocks = (2 * N) // MOE_BLK + N_EXPERTS
        ot, xbz = _attention(qt, k, vt, n_blocks * MOE_BLK + 2 * TM)

        zrow = lambda n: jnp.zeros((n, D), F32)
        wr = jnp.concatenate(
            [w_group_router[i].T, zrow(ROUTER_E0 - N_GROUPS), w_expert_router[i].T,
             zrow(ROUTER_ROWS - ROUTER_E0 - N_EXPERTS)], axis=0)
        wrh = wr.astype(BF16)
        wrhl = jnp.concatenate([wrh, (wr - wrh.astype(F32)).astype(BF16)], axis=0)
        br = jnp.concatenate(
            [b_group_router[i], jnp.zeros((ROUTER_E0 - N_GROUPS,), F32), b_expert_router[i],
             jnp.zeros((ROUTER_ROWS - ROUTER_E0 - N_EXPERTS,), F32)]).reshape(-1, 1)
        h1, pos, meta_w, opened, xb = _outproj(h, ot, oc, w_out[i].astype(BF16),
                                               row(moe_norm_g[i]), wrhl, br, xbz)

        opened = opened[:, :, 0]
        blk_ids = jnp.arange(n_blocks, dtype=jnp.int32)
        hit = opened[None, :, :] == blk_ids[:, None, None]
        used = jnp.any(hit, axis=(1, 2))
        e_of_blk = jnp.sum(jnp.where(hit, jnp.arange(N_EXPERTS, dtype=jnp.int32), 0), axis=(1, 2))
        n_used = jnp.sum(used.astype(jnp.int32)).reshape(1)
        key = jnp.where(used, e_of_blk, N_EXPERTS) * n_blocks + blk_ids
        slot_of_blk = jnp.sum((key[None, :] < key[:, None]).astype(jnp.int32), axis=1)
        at_slot = slot_of_blk[None, :] == blk_ids[:, None]
        order = jnp.sum(jnp.where(at_slot, blk_ids[None, :], 0), axis=1)
        blk_e = jnp.sum(jnp.where(at_slot, e_of_blk[None, :], 0), axis=1)
        blk_e = jnp.where(blk_ids < n_used, blk_e, jnp.max(jnp.where(used, e_of_blk, 0)))
        dest = jnp.swapaxes(pos[:, 0:2, :], 1, 2).reshape(2 * N)
        wt = jnp.swapaxes(meta_w[:, 0:2, :], 1, 2).reshape(N, 2)

        yb = _experts(order, blk_e, n_used, xb, w_gate[i], w_up[i], w_down[i])
        out = _final(dest, h1.reshape(N, D), p[i].reshape(N, PLE_DIM), wt, yb,
                     row(ple_norm_g[i]), w_ple_gate[i].astype(BF16), row(b_ple_gate[i]),
                     w_ple_proj[i].astype(BF16), row(final_norm_g))
        h = out.reshape(B, S, D)
    return h
```

```python
import functools
import math

import jax
import jax.numpy as jnp
from jax import lax
from jax.experimental import pallas as pl
from jax.experimental.pallas import tpu as pltpu

D_MODEL = 1024
PLE_DIM = 256
MLA_WIDTH = 512
CONV_WIDTH = 512
N_HEADS = 8
V_HEAD_DIM = 64
QK_NOPE_DIM = 64
QK_ROPE_DIM = 32
Q_LORA = 384
KV_LORA = 256
CONV_K = 3
N_GROUPS = 4
EXPERTS_PER_GROUP = 8
N_EXPERTS = N_GROUPS * EXPERTS_PER_GROUP
EXPERT_FF = 512
ROPE_BASE = 10000.0
EPS = 1e-6

HEAD_PAD = 128
V_ROWS = 80
QK_DIM = QK_NOPE_DIM + QK_ROPE_DIM
HALF_ROPE = QK_ROPE_DIM // 2

_C_Q = 0
_C_KV = _C_Q + Q_LORA
_C_BG = _C_KV + KV_LORA
_C_CG = _C_BG + CONV_WIDTH
_C_U = _C_CG + CONV_WIDTH
_C_KR = _C_U + CONV_WIDTH
IN_COLS_PAD = _C_KR + HEAD_PAD

TM = 256
TM_IN = 512
FIN_TILES = 4
FIN_AHEAD = 2
TQ = 512
TK = 256
ATT_HG = 4
ATT_TQC = 256
MOE_BLK = 512
ROUTER_E0 = 8
ROUTER_ROWS = 48
NEG = -1e30
Q_SCALE = (QK_DIM ** -0.5) * math.log2(math.e)
VMEM_LIMIT = 48 * 1024 * 1024

F32 = jnp.float32
BF16 = jnp.bfloat16


def _rms(x, g):
    return x * lax.rsqrt(jnp.mean(x * x, axis=-1, keepdims=True) + EPS) * g


def _dot(a, b):
    return jnp.dot(a, b, preferred_element_type=F32)


def _dot_nt(a, b):
    return lax.dot_general(a, b, (((1,), (1,)), ((), ())), preferred_element_type=F32)


def _pack_bf16_pairs(x):
    w = x.shape[1] // 2
    lo = pltpu.bitcast(x[:, :w].astype(BF16).astype(F32), jnp.uint32)
    hi = pltpu.bitcast(x[:, w:].astype(BF16).astype(F32), jnp.uint32)
    return (lo >> 16) | hi


def _unpack_bf16_pairs(p):
    lo = pltpu.bitcast(p << 16, F32).astype(BF16)
    hi = pltpu.bitcast(p & jnp.uint32(0xFFFF0000), F32).astype(BF16)
    return lo, hi


def _dot_tn(a, b):
    return lax.dot_general(a, b, (((0,), (0,)), ((), ())), preferred_element_type=F32)


def _inproj_kernel(x_ref, pos_ref, invf_ref, g_ref, win_ref, qg_ref, wqa_ref, wqb_ref, kvg_ref,
                   wuk_ref, pk_ref, wuvt_ref, convw_ref,
                   qt_ref, k_ref, vt_ref, oc_ref, carry_ref, ext_ref):
    tm = x_ref.shape[1]
    x = x_ref[0]
    xn = _rms(x, g_ref[...])
    z = _dot(xn.astype(BF16), win_ref[...])

    ang = invf_ref[...] * pos_ref[0]
    cos = jnp.cos(ang)
    sin = jnp.sin(ang)
    cos2 = jnp.concatenate([cos, cos], axis=0)
    sin2 = jnp.concatenate([sin, sin], axis=0)

    cqn = _rms(z[:, _C_Q:_C_Q + Q_LORA], qg_ref[...]).astype(BF16)
    qa = _dot_nt(wqa_ref[...], cqn)
    qb = _dot_nt(wqb_ref[...], cqn)
    for h in range(N_HEADS):
        r0 = h * HEAD_PAD
        nope = qa[r0:r0 + QK_NOPE_DIM]
        rope = (qa[r0 + QK_NOPE_DIM:r0 + QK_DIM] * cos2
                + qb[h * QK_ROPE_DIM:(h + 1) * QK_ROPE_DIM] * sin2)
        qh = jnp.concatenate([nope, rope, qa[r0 + QK_DIM:r0 + HEAD_PAD]], axis=0) * Q_SCALE
        qt_ref[0, r0:r0 + HEAD_PAD, :] = qh.astype(BF16)

    kvn = _rms(z[:, _C_KV:_C_KV + KV_LORA], kvg_ref[...]).astype(BF16)
    krt = z[:, _C_KR:_C_KR + HEAD_PAD].T
    krot = krt[0:QK_ROPE_DIM] * cos2 + krt[QK_ROPE_DIM:2 * QK_ROPE_DIM] * sin2
    k_ref[0] = (_dot(kvn, wuk_ref[...]) + _dot_tn(krot.astype(BF16), pk_ref[...])).astype(BF16)
    vt = _dot_nt(wuvt_ref[...], kvn).astype(BF16)
    ones_row = jnp.where(lax.broadcasted_iota(jnp.int32, (V_ROWS - V_HEAD_DIM, tm), 0) == 0,
                         1.0, 0.0).astype(BF16)
    for h in range(N_HEADS):
        vt_ref[0, h * V_ROWS:h * V_ROWS + V_HEAD_DIM, :] = vt[h * V_HEAD_DIM:(h + 1) * V_HEAD_DIM]
        vt_ref[0, h * V_ROWS + V_HEAD_DIM:(h + 1) * V_ROWS, :] = ones_row

    @pl.when(pl.program_id(1) == 0)
    def _():
        carry_ref[...] = jnp.zeros_like(carry_ref)

    cu = z[:, _C_CG:_C_CG + CONV_WIDTH] * z[:, _C_U:_C_U + CONV_WIDTH]
    ext_ref[0:8, :] = carry_ref[...]
    ext_ref[8:8 + tm, :] = cu
    cu1 = ext_ref[7:7 + tm, :]
    cu2 = ext_ref[6:6 + tm, :]
    w = convw_ref[...]
    y = w[0:1] * cu2 + w[1:2] * cu1 + w[2:3] * cu
    oc_ref[0] = (z[:, _C_BG:_C_BG + CONV_WIDTH] * y).astype(BF16)
    carry_ref[...] = ext_ref[tm:tm + 8, :]


def _inproj(x, pos, invf, g, win, qg, wqa, wqb, kvg, wuk, pk, wuvt, convw):
    B, S, D = x.shape
    tm = TM_IN
    full = lambda a: pl.BlockSpec(a.shape, lambda b, i: (0,) * a.ndim)
    return pl.pallas_call(
        _inproj_kernel,
        grid=(B, S // tm),
        in_specs=[
            pl.BlockSpec((1, tm, D), lambda b, i: (b, i, 0)),
            pl.BlockSpec((1, 1, tm), lambda b, i: (b, 0, i)),
            full(invf), full(g), full(win), full(qg), full(wqa), full(wqb), full(kvg), full(wuk),
            full(pk), full(wuvt), full(convw),
        ],
        out_specs=[
            pl.BlockSpec((1, N_HEADS * HEAD_PAD, tm), lambda b, i: (b, 0, i)),
            pl.BlockSpec((1, tm, N_HEADS * HEAD_PAD), lambda b, i: (b, i, 0)),
            pl.BlockSpec((1, N_HEADS * V_ROWS, tm), lambda b, i: (b, 0, i)),
            pl.BlockSpec((1, tm, CONV_WIDTH), lambda b, i: (b, i, 0)),
        ],
        out_shape=[
            jax.ShapeDtypeStruct((B, N_HEADS * HEAD_PAD, S), BF16),
            jax.ShapeDtypeStruct((B, S, N_HEADS * HEAD_PAD), BF16),
            jax.ShapeDtypeStruct((B, N_HEADS * V_ROWS, S), BF16),
            jax.ShapeDtypeStruct((B, S, CONV_WIDTH), BF16),
        ],
        scratch_shapes=[pltpu.VMEM((8, CONV_WIDTH), F32), pltpu.VMEM((tm + 8, CONV_WIDTH), F32)],
        compiler_params=pltpu.CompilerParams(
            dimension_semantics=("arbitrary", "arbitrary"), vmem_limit_bytes=VMEM_LIMIT),
        name="inproj",
    )(x, pos, invf, g, win, qg, wqa, wqb, kvg, wuk, pk, wuvt, convw)


def _attn_kernel(qt_ref, k_ref, vt_ref, o_ref, z_ref, m_ref, acc_ref, sa_ref, sb_ref, zero_ref,
                 zsem):
    tq = qt_ref.shape[2]
    tk = TK
    assert tq == 2 * tk and ATT_TQC == tk
    i = pl.program_id(2)
    m_ref[...] = jnp.full_like(m_ref, NEG)
    acc_ref[...] = jnp.zeros_like(acc_ref)

    step = ((pl.program_id(0) * pl.num_programs(1) + pl.program_id(1)) * pl.num_programs(2) + i)
    zrows = zero_ref.shape[0]

    @pl.when(step == 0)
    def _():
        zero_ref[...] = jnp.zeros_like(zero_ref)

    zero_copy = pltpu.make_async_copy(
        zero_ref, z_ref.at[pl.ds(pl.multiple_of(step * zrows, 8), zrows)], zsem)
    zero_copy.start()

    def scores(j, s_ref, g, col0=0):
        k0 = pl.multiple_of(j * tk, tk)
        qt = qt_ref[0, g * HEAD_PAD:(g + 1) * HEAD_PAD, col0:]
        s_ref[g, :, col0:] = _dot(k_ref[0, pl.ds(k0, tk), g * HEAD_PAD:(g + 1) * HEAD_PAD], qt)

    def softmax_pv(j, s_ref, g, diag):
        k0 = pl.multiple_of(j * tk, tk)
        vt = vt_ref[0, g * V_ROWS:(g + 1) * V_ROWS, pl.ds(k0, tk)]
        for c in range(tq // ATT_TQC):
            if diag is not None and c < diag:
                continue
            cols = slice(c * ATT_TQC, (c + 1) * ATT_TQC)
            s = s_ref[g, :, cols]
            if diag is not None and c == diag:
                krow = lax.broadcasted_iota(jnp.int32, (tk, ATT_TQC), 0)
                qcol = lax.broadcasted_iota(jnp.int32, (tk, ATT_TQC), 1)
                s = jnp.where(krow <= qcol, s, NEG)
            m_old = m_ref[g, :, cols]
            m_new = jnp.maximum(m_old, jnp.max(s, axis=0, keepdims=True))
            alpha = jnp.exp2(m_old - m_new)
            p = jnp.exp2((s - m_new).astype(BF16))
            acc_ref[g, :, cols] = alpha * acc_ref[g, :, cols] + _dot(vt, p)
            m_ref[g, :, cols] = m_new

    def stage(j_next, s_next, j, s_cur, diag=None, next_col0=0):
        for g in range(ATT_HG):
            if j_next is not None:
                scores(j_next, s_next, g, next_col0)
            softmax_pv(j, s_cur, g, diag)

    for g in range(ATT_HG):
        scores(0, sa_ref, g)

    def pair(j):
        stage(j + 1, sb_ref, j, sa_ref)
        stage(j + 2, sa_ref, j + 1, sb_ref)

    def body(t, c):
        pair(4 * t)
        pair(4 * t + 2)
        return c

    lax.fori_loop(0, i // 2, body, 0)

    @pl.when(i % 2 == 1)
    def _():
        pair(2 * i - 2)

    n_full = 2 * i
    stage(n_full + 1, sb_ref, n_full, sa_ref, diag=0, next_col0=ATT_TQC)
    stage(None, None, n_full + 1, sb_ref, diag=1)
    for g in range(ATT_HG):
        o_ref[0, g * V_HEAD_DIM:(g + 1) * V_HEAD_DIM, :] = (
            acc_ref[g, 0:V_HEAD_DIM, :] / acc_ref[g, V_HEAD_DIM:V_HEAD_DIM + 1, :]).astype(o_ref.dtype)
    zero_copy.wait()


def _attention(qt, k, vt, min_zero_rows):
    B, _, S = qt.shape
    hg = ATT_HG
    grid = (B, N_HEADS // hg, S // TQ)
    n_steps = grid[0] * grid[1] * grid[2]
    zrows = -(-min_zero_rows // (8 * n_steps)) * 8
    return pl.pallas_call(
        _attn_kernel,
        grid=grid,
        in_specs=[
            pl.BlockSpec((1, hg * HEAD_PAD, TQ), lambda b, h, i: (b, h, i)),
            pl.BlockSpec((1, S, hg * HEAD_PAD), lambda b, h, i: (b, 0, h)),
            pl.BlockSpec((1, hg * V_ROWS, S), lambda b, h, i: (b, h, 0)),
        ],
        out_specs=[pl.BlockSpec((1, hg * V_HEAD_DIM, TQ), lambda b, h, i: (b, h, i)),
                   pl.BlockSpec(memory_space=pl.ANY)],
        out_shape=[jax.ShapeDtypeStruct((B, MLA_WIDTH, S), BF16),
                   jax.ShapeDtypeStruct((n_steps * zrows, D_MODEL // 2), jnp.uint32)],
        scratch_shapes=[pltpu.VMEM((hg, 1, TQ), F32), pltpu.VMEM((hg, V_ROWS, TQ), F32),
                        pltpu.VMEM((hg, TK, TQ), F32), pltpu.VMEM((hg, TK, TQ), F32),
                        pltpu.VMEM((zrows, D_MODEL // 2), jnp.uint32), pltpu.SemaphoreType.DMA],
        compiler_params=pltpu.CompilerParams(
            dimension_semantics=("arbitrary", "arbitrary", "arbitrary"),
            vmem_limit_bytes=VMEM_LIMIT),
        name="attn",
    )(qt, k, vt)


def _route_tile(xn, wrhl_ref, br_ref, carry_ref, after_logits=None):
    tm = xn.shape[0]
    xh = xn.astype(BF16)
    xl = (xn - xh.astype(F32)).astype(BF16)
    hl = _dot_nt(wrhl_ref[...], xh)
    logits = (hl[0:ROUTER_ROWS] + hl[ROUTER_ROWS:2 * ROUTER_ROWS]
              + _dot_nt(wrhl_ref[0:ROUTER_ROWS, :], xl) + br_ref[...])
    if after_logits is not None:
        after_logits()
    big = jnp.int32(1 << 20)

    grow = lax.broadcasted_iota(jnp.int32, (ROUTER_E0, tm), 0)
    glog = jnp.where(grow < N_GROUPS, logits[0:ROUTER_E0], NEG)
    gmax = jnp.max(glog, axis=0, keepdims=True)
    gsum = jnp.sum(jnp.exp(glog - gmax), axis=0, keepdims=True)
    g_p = 1.0 / gsum
    g_idx = jnp.min(jnp.where(glog == gmax, grow, big), axis=0, keepdims=True)

    erow = lax.broadcasted_iota(jnp.int32, (N_EXPERTS, tm), 0)
    e_lo = g_idx * EXPERTS_PER_GROUP
    in_group = (erow >= e_lo) & (erow < e_lo + EXPERTS_PER_GROUP)
    elog = jnp.where(in_group, logits[ROUTER_E0:ROUTER_E0 + N_EXPERTS], NEG)
    emax = jnp.max(elog, axis=0, keepdims=True)
    esum = jnp.sum(jnp.exp(elog - emax), axis=0, keepdims=True)
    e1 = jnp.min(jnp.where(elog == emax, erow, big), axis=0, keepdims=True)
    elog2 = jnp.where(erow == e1, NEG, elog)
    emax2 = jnp.max(elog2, axis=0, keepdims=True)
    e2 = jnp.min(jnp.where(elog2 == emax2, erow, big), axis=0, keepdims=True)
    p1 = 1.0 / esum
    p2 = jnp.exp(emax2 - emax) / esum
    psum = p1 + p2
    w1 = g_p * (p1 / psum)
    w2 = g_p * (p2 / psum)

    oh1 = erow == e1
    oh2 = erow == e2
    oh = jnp.where(oh1 | oh2, 1.0, 0.0)
    srow = lax.broadcasted_iota(jnp.int32, (tm, tm), 0)
    scol = lax.broadcasted_iota(jnp.int32, (tm, tm), 1)
    earlier = jnp.where(srow < scol, 1.0, 0.0).astype(BF16)
    cum = _dot(oh.astype(BF16), earlier) + carry_ref[:, 0:1]
    r1 = jnp.sum(jnp.where(oh1, cum, 0.0), axis=0, keepdims=True)
    r2 = jnp.sum(jnp.where(oh2, cum, 0.0), axis=0, keepdims=True)
    return oh, oh1, oh2, r1, r2, w1, w2


def _outproj_kernel(x_ref, ot_ref, oc_ref, wo_ref, g_ref, wrhl_ref, br_ref, xbz_ref,
                    h1_ref, pos_ref, mw_ref, nb_ref, xb_ref,
                    carry_ref, last_ref, free_ref, xs0_ref, xs1_ref, dv0_ref, dv1_ref,
                    ds0_ref, ds1_ref, rsem, ssem):
    del xbz_ref
    tm = TM
    step = pl.program_id(0) * pl.num_programs(1) + pl.program_id(1)
    last_step = pl.num_programs(0) * pl.num_programs(1) - 1
    xs = (xs0_ref, xs1_ref)
    dv = (dv0_ref, dv1_ref)
    ds = (ds0_ref, ds1_ref)
    spare_row0 = xb_ref.shape[0] - 2 * tm

    def meta_copy(slot):
        return pltpu.make_async_copy(dv[slot], ds[slot], ssem.at[slot])

    def issue_rows(slot, r0=0, r1=TM):
        for r in range(r0, r1):
            for k in range(2):
                pltpu.make_async_copy(xs[slot].at[pl.ds(r, 1)],
                                      xb_ref.at[pl.ds(ds[slot][k, r], 1)], rsem.at[slot]
                                      ).start(priority=k)

    def issue_batch(slot, b):
        bounds = (0, 3 * tm // 8, 6 * tm // 8, tm)

        @pl.when(step >= 0)
        def _():
            issue_rows(slot, bounds[b], bounds[b + 1])

    def wait_rows(slot):
        for _ in range(2):
            pltpu.make_async_copy(xs[slot], xb_ref.at[pl.ds(0, tm)], rsem.at[slot]).wait()

    def compute(slot):
        other = 1 - slot
        rows = slice(slot * tm, (slot + 1) * tm)
        issue_batch(other, 0)
        attn = (_dot_tn(ot_ref[0, :, rows], wo_ref[0:MLA_WIDTH, :])
                + _dot(oc_ref[0, rows, :], wo_ref[MLA_WIDTH:MLA_WIDTH + CONV_WIDTH, :]))
        h1 = x_ref[0, rows, :] + attn
        h1_ref[0, rows, :] = h1
        xn = _rms(h1, g_ref[...])
        xs[slot][...] = _pack_bf16_pairs(xn)
        issue_batch(other, 1)
        oh, oh1, oh2, r1, r2, w1, w2 = _route_tile(
            xn, wrhl_ref, br_ref, carry_ref, after_logits=lambda: issue_batch(other, 2))

        inv_blk = 1.0 / MOE_BLK
        cnt = carry_ref[:, 0:1]
        tile_cnt = jnp.sum(oh, axis=1, keepdims=True)
        nb_before = jnp.floor((cnt + (MOE_BLK - 1)) * inv_blk)
        nb_after = jnp.floor((cnt + tile_cnt + (MOE_BLK - 1)) * inv_blk)
        new = nb_after - nb_before
        erow = lax.broadcasted_iota(jnp.int32, (N_EXPERTS, N_EXPERTS), 0)
        ecol = lax.broadcasted_iota(jnp.int32, (N_EXPERTS, N_EXPERTS), 1)
        lower = jnp.where(ecol < erow, 1.0, 0.0).astype(BF16)
        new_rep = jnp.broadcast_to(new, (N_EXPERTS, HEAD_PAD))
        new_id = free_ref[0:1, :] + _dot(lower, new_rep.astype(BF16))
        free_ref[...] = free_ref[...] + jnp.sum(new, axis=0, keepdims=True)
        last_before = last_ref[...]
        is_new = new_rep > 0.0
        last_ref[...] = jnp.where(is_new, new_id, last_before)
        nb_ref[slot] = jnp.where(is_new, new_id, -1.0).astype(jnp.int32)
        carry_ref[...] = carry_ref[...] + tile_cnt

        def place(ohk, rk):
            bi = jnp.floor(rk * inv_blk)
            blk = jnp.where(bi == nb_before - 1.0, last_before[:, 0:1], new_id[:, 0:1])
            phys = jnp.sum(jnp.where(ohk, blk, 0.0), axis=0, keepdims=True)
            return (phys * MOE_BLK + (rk - bi * MOE_BLK)).astype(jnp.int32)

        mrow = lax.broadcasted_iota(jnp.int32, (8, tm), 0)
        dest = jnp.where(mrow == 0, place(oh1, r1), jnp.where(mrow == 1, place(oh2, r2), 0))
        dv[slot][...] = dest
        pos_ref[0, :, rows] = dest
        mw_ref[0, :, rows] = jnp.where(mrow == 0, w1, jnp.where(mrow == 1, w2, 0.0))
        meta_copy(slot).start()

    @pl.when(step == 0)
    def _():
        carry_ref[...] = jnp.zeros_like(carry_ref)
        free_ref[...] = jnp.zeros_like(free_ref)
        last_ref[...] = jnp.full_like(last_ref, -1.0)
        xs1_ref[...] = jnp.zeros_like(xs1_ref)
        mrow = lax.broadcasted_iota(jnp.int32, (8, tm), 0)
        lane = lax.broadcasted_iota(jnp.int32, (8, tm), 1)
        dv1_ref[...] = jnp.where(mrow < 2, spare_row0 + 2 * lane + mrow, 0)
        meta_copy(1).start()

    meta_copy(1).wait()
    compute(0)
    wait_rows(1)
    meta_copy(0).wait()
    compute(1)
    wait_rows(0)

    @pl.when(step == last_step)
    def _():
        meta_copy(1).wait()
        issue_rows(1)
        wait_rows(1)


def _outproj(x, ot, oc, wo, g, wrhl, br, xbz):
    B, S, D = x.shape
    tm2 = 2 * TM
    n_s = S // tm2
    full = lambda a: pl.BlockSpec(a.shape, lambda b, i: (0,) * a.ndim)
    tile = lambda w: pl.BlockSpec((1, tm2, w), lambda b, i: (b, i, 0))
    meta = pl.BlockSpec((1, 8, tm2), lambda b, i: (b, 0, i))
    any_space = pl.BlockSpec(memory_space=pl.ANY)
    return pl.pallas_call(
        _outproj_kernel,
        grid=(B, n_s),
        in_specs=[
            tile(D),
            pl.BlockSpec((1, MLA_WIDTH, tm2), lambda b, i: (b, 0, i)),
            tile(CONV_WIDTH),
            full(wo), full(g), full(wrhl), full(br), any_space,
        ],
        out_specs=[tile(D), meta, meta,
                   pl.BlockSpec((2, N_EXPERTS, HEAD_PAD), lambda b, i: (b * n_s + i, 0, 0)),
                   any_space],
        out_shape=[
            jax.ShapeDtypeStruct((B, S, D), F32),
            jax.ShapeDtypeStruct((B, 8, S), jnp.int32),
            jax.ShapeDtypeStruct((B, 8, S), F32),
            jax.ShapeDtypeStruct((B * S // TM, N_EXPERTS, HEAD_PAD), jnp.int32),
            jax.ShapeDtypeStruct(xbz.shape, xbz.dtype),
        ],
        input_output_aliases={7: 4},
        scratch_shapes=[
            pltpu.VMEM((N_EXPERTS, HEAD_PAD), F32), pltpu.VMEM((N_EXPERTS, HEAD_PAD), F32),
            pltpu.VMEM((8, HEAD_PAD), F32),
            pltpu.VMEM((TM, D // 2), jnp.uint32), pltpu.VMEM((TM, D // 2), jnp.uint32),
            pltpu.VMEM((8, TM), jnp.int32), pltpu.VMEM((8, TM), jnp.int32),
            pltpu.SMEM((8, TM), jnp.int32), pltpu.SMEM((8, TM), jnp.int32),
            pltpu.SemaphoreType.DMA((2,)), pltpu.SemaphoreType.DMA((2,)),
        ],
        compiler_params=pltpu.CompilerParams(
            dimension_semantics=("arbitrary", "arbitrary"), vmem_limit_bytes=VMEM_LIMIT),
        name="outproj",
    )(x, ot, oc, wo, g, wrhl, br, xbz)


def _expert_kernel(order_ref, blke_ref, nused_ref, xb_ref, wg_ref, wu_ref, wd_ref, yb_ref,
                   wgs_ref, wus_ref, wds_ref):
    del order_ref
    i = pl.program_id(0)
    used = i < nused_ref[0]
    changed = (i == 0) | (blke_ref[i] != blke_ref[jnp.maximum(i - 1, 0)])

    @pl.when(used & changed)
    def _():
        wgs_ref[...] = wg_ref[0].astype(BF16)
        wus_ref[...] = wu_ref[0].astype(BF16)
        wds_ref[...] = wd_ref[0].astype(BF16)

    @pl.when(used)
    def _():
        x_lo, x_hi = _unpack_bf16_pairs(xb_ref[...])
        dh = x_lo.shape[1]
        half = EXPERT_FF // 2

        def xdot(w_ref, cols):
            return _dot(x_lo, w_ref[0:dh, cols]) + _dot(x_hi, w_ref[dh:2 * dh, cols])

        g0 = xdot(wgs_ref, slice(0, half))
        u0 = xdot(wus_ref, slice(0, half))
        g1 = xdot(wgs_ref, slice(half, EXPERT_FF))
        u1 = xdot(wus_ref, slice(half, EXPERT_FF))
        h0 = ((g0 * jax.nn.sigmoid(g0)) * u0).astype(BF16)
        y = _dot(h0, wds_ref[:half, :])
        h1 = ((g1 * jax.nn.sigmoid(g1)) * u1).astype(BF16)
        yb_ref[...] = y + _dot(h1, wds_ref[half:, :])

    @pl.when(jnp.logical_not(used))
    def _():
        yb_ref[...] = jnp.zeros_like(yb_ref)


def _experts(order, blk_e, n_used, xb, wg, wu, wd):
    D = 2 * xb.shape[1]
    blk = MOE_BLK
    n_blocks = order.shape[0]
    return pl.pallas_call(
        _expert_kernel,
        grid_spec=pltpu.PrefetchScalarGridSpec(
            num_scalar_prefetch=3,
            grid=(n_blocks,),
            in_specs=[
                pl.BlockSpec((blk, D // 2), lambda i, o, be, nu: (o[jnp.minimum(i, nu[0] - 1)], 0)),
                pl.BlockSpec((1, D, EXPERT_FF), lambda i, o, be, nu: (be[i], 0, 0)),
                pl.BlockSpec((1, D, EXPERT_FF), lambda i, o, be, nu: (be[i], 0, 0)),
                pl.BlockSpec((1, EXPERT_FF, D), lambda i, o, be, nu: (be[i], 0, 0)),
            ],
            out_specs=pl.BlockSpec((blk, D), lambda i, o, be, nu: (o[i], 0)),
            scratch_shapes=[pltpu.VMEM((D, EXPERT_FF), BF16), pltpu.VMEM((D, EXPERT_FF), BF16),
                            pltpu.VMEM((EXPERT_FF, D), BF16)],
        ),
        out_shape=jax.ShapeDtypeStruct((n_blocks * blk, D), F32),
        compiler_params=pltpu.CompilerParams(
            dimension_semantics=("arbitrary",), vmem_limit_bytes=VMEM_LIMIT),
        name="experts",
    )(order, blk_e, n_used, xb, wg, wu, wd)


def _final_kernel(dest_ref, h1_ref, p_ref, wt_ref, yb_ref, gp_ref, wg_ref, bg_ref, wp_ref,
                  gf_ref, out_ref, buf0_ref, buf1_ref, buf2_ref, buf3_ref, sem):
    tm = h1_ref.shape[0] // FIN_TILES
    i = pl.program_id(0)
    bufs = (buf0_ref, buf1_ref, buf2_ref, buf3_ref)

    def row_copy(slot, k, r, d):
        return pltpu.make_async_copy(yb_ref.at[pl.ds(d, 1)], bufs[slot].at[k, pl.ds(r, 1)],
                                     sem.at[slot])

    def issue(tile, slot):
        for r in range(tm):
            t = (tile * tm + r) * 2
            row_copy(slot, 0, r, dest_ref[t]).start(priority=0)
            row_copy(slot, 1, r, dest_ref[t + 1]).start(priority=1)

    def wait(slot):
        for k in range(2):
            pltpu.make_async_copy(yb_ref.at[pl.ds(0, tm)], bufs[slot].at[k], sem.at[slot]).wait()

    def compute(slot):
        rows = slice(slot * tm, (slot + 1) * tm)
        pe = _dot(p_ref[rows, :].astype(BF16), wp_ref[...])
        wt = wt_ref[rows, :]
        h2 = h1_ref[rows, :] + wt[:, 0:1] * bufs[slot][0] + wt[:, 1:2] * bufs[slot][1]
        n = _rms(h2, gp_ref[...]).astype(BF16)
        gate = jax.nn.sigmoid(_dot(n, wg_ref[...]) + bg_ref[...])
        h3 = h2 + gate * pe
        out_ref[rows, :] = _rms(h3, gf_ref[...])

    @pl.when(i == 0)
    def _():
        for j in range(FIN_AHEAD):
            issue(j, j)

    last_tile = FIN_TILES * pl.num_programs(0) - 1
    for j in range(FIN_TILES):
        issue(jnp.minimum(FIN_TILES * i + j + FIN_AHEAD, last_tile), (j + FIN_AHEAD) % FIN_TILES)
        wait(j)
        compute(j)

    @pl.when(i == pl.num_programs(0) - 1)
    def _():
        for j in range(FIN_AHEAD):
            wait(j)


def _final(dest, h1, p, wt, yb, gp, wg, bg, wp, gf):
    N, D = h1.shape
    tm = TM
    full = lambda a: pl.BlockSpec(a.shape, lambda i, dest: (0,) * a.ndim)
    return pl.pallas_call(
        _final_kernel,
        grid_spec=pltpu.PrefetchScalarGridSpec(
            num_scalar_prefetch=1,
            grid=(N // (FIN_TILES * tm),),
            in_specs=[
                pl.BlockSpec((FIN_TILES * tm, D), lambda i, dest: (i, 0)),
                pl.BlockSpec((FIN_TILES * tm, PLE_DIM), lambda i, dest: (i, 0)),
                pl.BlockSpec((FIN_TILES * tm, 2), lambda i, dest: (i, 0)),
                pl.BlockSpec(memory_space=pl.ANY),
                full(gp), full(wg), full(bg), full(wp), full(gf),
            ],
            out_specs=pl.BlockSpec((FIN_TILES * tm, D), lambda i, dest: (i, 0)),
            scratch_shapes=[pltpu.VMEM((2, tm, D), F32) for _ in range(FIN_TILES)]
            + [pltpu.SemaphoreType.DMA((FIN_TILES,))],
        ),
        out_shape=jax.ShapeDtypeStruct((N, D), F32),
        compiler_params=pltpu.CompilerParams(
            dimension_semantics=("arbitrary",), vmem_limit_bytes=VMEM_LIMIT),
        name="final",
    )(dest, h1, p, wt, yb, gp, wg, bg, wp, gf)


def _prep_weights(w_in, w_uq, w_ukv):
    cq, ckv, kr, bg, cg, u = jnp.split(
        w_in, [Q_LORA, Q_LORA + KV_LORA, Q_LORA + KV_LORA + QK_ROPE_DIM,
               Q_LORA + KV_LORA + QK_ROPE_DIM + CONV_WIDTH,
               Q_LORA + KV_LORA + QK_ROPE_DIM + 2 * CONV_WIDTH], axis=1)
    kr_rot = jnp.concatenate([-kr[:, HALF_ROPE:], kr[:, :HALF_ROPE]], axis=1)
    pad = jnp.zeros((D_MODEL, HEAD_PAD - 2 * QK_ROPE_DIM), w_in.dtype)
    win = jnp.concatenate([cq, ckv, bg, cg, u, kr, kr_rot, pad], axis=1).astype(BF16)

    zq = jnp.zeros((Q_LORA, N_HEADS, HEAD_PAD - QK_DIM), w_uq.dtype)
    wqa = jnp.concatenate([w_uq, zq], axis=2)
    rope = w_uq[:, :, QK_NOPE_DIM:]
    wqb = jnp.concatenate([-rope[:, :, HALF_ROPE:], rope[:, :, :HALF_ROPE]], axis=2)
    wqa = wqa.reshape(Q_LORA, N_HEADS * HEAD_PAD).T.astype(BF16)
    wqb = wqb.reshape(Q_LORA, N_HEADS * QK_ROPE_DIM).T.astype(BF16)

    zk = jnp.zeros((KV_LORA, N_HEADS, HEAD_PAD - QK_NOPE_DIM), w_ukv.dtype)
    wuk = jnp.concatenate([w_ukv[:, :, :QK_NOPE_DIM], zk], axis=2)
    wuk = wuk.reshape(KV_LORA, N_HEADS * HEAD_PAD).astype(BF16)
    wuvt = w_ukv[:, :, QK_NOPE_DIM:].reshape(KV_LORA, MLA_WIDTH).T.astype(BF16)

    src = jnp.arange(QK_ROPE_DIM)[:, None]
    dst = jnp.arange(N_HEADS * HEAD_PAD)[None, :]
    pk = (dst % HEAD_PAD - QK_NOPE_DIM == src).astype(BF16)
    return win, wqa, wqb, wuk, wuvt, pk


def kernel(x, p, positions, attn_norm_g, w_in, q_norm_g, w_uq, kv_norm_g, w_ukv, conv_w, w_out,
           moe_norm_g, w_group_router, b_group_router, w_expert_router, b_expert_router,
           w_gate, w_up, w_down, ple_norm_g, w_ple_gate, b_ple_gate, w_ple_proj, final_norm_g):
    B, S, D = x.shape
    N = B * S
    assert w_in.shape[0] == 1, "single-layer trunk: the final norm is fused into the layer"
    pos = positions.astype(F32).reshape(B, 1, S)
    invf = (ROPE_BASE ** (-jnp.arange(0, QK_ROPE_DIM, 2, dtype=F32) / QK_ROPE_DIM)).reshape(-1, 1)
    row = lambda v: v.reshape(1, -1)
    h = x
    for i in range(1):
        win, wqa, wqb, wuk, wuvt, pk = _prep_weights(w_in[i], w_uq[i], w_ukv[i])
        qt, k, vt, oc = _inproj(h, pos, invf, row(attn_norm_g[i]), win, row(q_norm_g[i]), wqa,
                                wqb, row(kv_norm_g[i]), wuk, pk, wuvt,
                                conv_w[i].reshape(CONV_K, CONV_WIDTH))
        n_blocks = (2 * N) // MOE_BLK + N_EXPERTS
        ot, xbz = _attention(qt, k, vt, n_blocks * MOE_BLK + 2 * TM)

        zrow = lambda n: jnp.zeros((n, D), F32)
        wr = jnp.concatenate(
            [w_group_router[i].T, zrow(ROUTER_E0 - N_GROUPS), w_expert_router[i].T,
             zrow(ROUTER_ROWS - ROUTER_E0 - N_EXPERTS)], axis=0)
        wrh = wr.astype(BF16)
        wrhl = jnp.concatenate([wrh, (wr - wrh.astype(F32)).astype(BF16)], axis=0)
        br = jnp.concatenate(
            [b_group_router[i], jnp.zeros((ROUTER_E0 - N_GROUPS,), F32), b_expert_router[i],
             jnp.zeros((ROUTER_ROWS - ROUTER_E0 - N_EXPERTS,), F32)]).reshape(-1, 1)
        h1, pos, meta_w, opened, xb = _outproj(h, ot, oc, w_out[i].astype(BF16),
                                               row(moe_norm_g[i]), wrhl, br, xbz)

        opened = opened[:, :, 0]
        blk_ids = jnp.arange(n_blocks, dtype=jnp.int32)
        hit = opened[None, :, :] == blk_ids[:, None, None]
        used = jnp.any(hit, axis=(1, 2))
        e_of_blk = jnp.sum(jnp.where(hit, jnp.arange(N_EXPERTS, dtype=jnp.int32), 0), axis=(1, 2))
        n_used = jnp.sum(used.astype(jnp.int32)).reshape(1)
        key = jnp.where(used, e_of_blk, N_EXPERTS) * n_blocks + blk_ids
        slot_of_blk = jnp.sum((key[None, :] < key[:, None]).astype(jnp.int32), axis=1)
        at_slot = slot_of_blk[None, :] == blk_ids[:, None]
        order = jnp.sum(jnp.where(at_slot, blk_ids[None, :], 0), axis=1)
        blk_e = jnp.sum(jnp.where(at_slot, e_of_blk[None, :], 0), axis=1)
        blk_e = jnp.where(blk_ids < n_used, blk_e, jnp.max(jnp.where(used, e_of_blk, 0)))
        dest = jnp.swapaxes(pos[:, 0:2, :], 1, 2).reshape(2 * N)
        wt = jnp.swapaxes(meta_w[:, 0:2, :], 1, 2).reshape(N, 2)

        yb = _experts(order, blk_e, n_used, xb, w_gate[i], w_up[i], w_down[i])
        out = _final(dest, h1.reshape(N, D), p[i].reshape(N, PLE_DIM), wt, yb,
                     row(ple_norm_g[i]), w_ple_gate[i].astype(BF16), row(b_ple_gate[i]),
                     w_ple_proj[i].astype(BF16), row(final_norm_g))
        h = out.reshape(B, S, D)
    return h
```

```python
import functools
import math

import jax
import jax.numpy as jnp
from jax import lax
from jax.experimental import pallas as pl
from jax.experimental.pallas import tpu as pltpu

D_MODEL = 1024
PLE_DIM = 256
MLA_WIDTH = 512
CONV_WIDTH = 512
N_HEADS = 8
V_HEAD_DIM = 64
QK_NOPE_DIM = 64
QK_ROPE_DIM = 32
Q_LORA = 384
KV_LORA = 256
CONV_K = 3
N_GROUPS = 4
EXPERTS_PER_GROUP = 8
N_EXPERTS = N_GROUPS * EXPERTS_PER_GROUP
EXPERT_FF = 512
ROPE_BASE = 10000.0
EPS = 1e-6

HEAD_PAD = 128
V_ROWS = 80
QK_DIM = QK_NOPE_DIM + QK_ROPE_DIM
HALF_ROPE = QK_ROPE_DIM // 2

_C_Q = 0
_C_KV = _C_Q + Q_LORA
_C_BG = _C_KV + KV_LORA
_C_CG = _C_BG + CONV_WIDTH
_C_U = _C_CG + CONV_WIDTH
_C_KR = _C_U + CONV_WIDTH
IN_COLS_PAD = _C_KR + HEAD_PAD

TM = 256
TM_IN = 512
OUT_CHUNKS = 4
OUT_BATCHES = 2 * OUT_CHUNKS + 3
FIN_TILES = 4
FIN_AHEAD = 2
TQ = 512
TK = 256
ATT_HG = 4
ATT_TQC = 256
MOE_BLK = 512
ROUTER_E0 = 8
ROUTER_ROWS = 48
NEG = -1e30
Q_SCALE = (QK_DIM ** -0.5) * math.log2(math.e)
VMEM_LIMIT = 48 * 1024 * 1024

F32 = jnp.float32
BF16 = jnp.bfloat16


def _rms(x, g):
    return x * lax.rsqrt(jnp.mean(x * x, axis=-1, keepdims=True) + EPS) * g


def _dot(a, b):
    return jnp.dot(a, b, preferred_element_type=F32)


def _dot_nt(a, b):
    return lax.dot_general(a, b, (((1,), (1,)), ((), ())), preferred_element_type=F32)


def _pack_bf16_pairs(x):
    w = x.shape[1] // 2
    lo = pltpu.bitcast(x[:, :w].astype(BF16).astype(F32), jnp.uint32)
    hi = pltpu.bitcast(x[:, w:].astype(BF16).astype(F32), jnp.uint32)
    return (lo >> 16) | hi


def _unpack_bf16_pairs(p):
    lo = pltpu.bitcast(p << 16, F32).astype(BF16)
    hi = pltpu.bitcast(p & jnp.uint32(0xFFFF0000), F32).astype(BF16)
    return lo, hi


def _dot_tn(a, b):
    return lax.dot_general(a, b, (((0,), (0,)), ((), ())), preferred_element_type=F32)


def _inproj_kernel(x_ref, pos_ref, invf_ref, g_ref, win_ref, qg_ref, wqa_ref, wqb_ref, kvg_ref,
                   wuk_ref, pk_ref, wuvt_ref, convw_ref,
                   qt_ref, k_ref, vt_ref, oc_ref, carry_ref, ext_ref):
    tm = x_ref.shape[1]
    x = x_ref[0]
    xn = _rms(x, g_ref[...])
    z = _dot(xn.astype(BF16), win_ref[...])

    ang = invf_ref[...] * pos_ref[0]
    cos = jnp.cos(ang)
    sin = jnp.sin(ang)
    cos2 = jnp.concatenate([cos, cos], axis=0)
    sin2 = jnp.concatenate([sin, sin], axis=0)

    cqn = _rms(z[:, _C_Q:_C_Q + Q_LORA], qg_ref[...]).astype(BF16)
    qa = _dot_nt(wqa_ref[...], cqn)
    qb = _dot_nt(wqb_ref[...], cqn)
    for h in range(N_HEADS):
        r0 = h * HEAD_PAD
        nope = qa[r0:r0 + QK_NOPE_DIM]
        rope = (qa[r0 + QK_NOPE_DIM:r0 + QK_DIM] * cos2
                + qb[h * QK_ROPE_DIM:(h + 1) * QK_ROPE_DIM] * sin2)
        qh = jnp.concatenate([nope, rope, qa[r0 + QK_DIM:r0 + HEAD_PAD]], axis=0) * Q_SCALE
        qt_ref[0, r0:r0 + HEAD_PAD, :] = qh.astype(BF16)

    kvn = _rms(z[:, _C_KV:_C_KV + KV_LORA], kvg_ref[...]).astype(BF16)
    krt = z[:, _C_KR:_C_KR + HEAD_PAD].T
    krot = krt[0:QK_ROPE_DIM] * cos2 + krt[QK_ROPE_DIM:2 * QK_ROPE_DIM] * sin2
    k_ref[0] = (_dot(kvn, wuk_ref[...]) + _dot_tn(krot.astype(BF16), pk_ref[...])).astype(BF16)
    vt = _dot_nt(wuvt_ref[...], kvn).astype(BF16)
    ones_row = jnp.where(lax.broadcasted_iota(jnp.int32, (V_ROWS - V_HEAD_DIM, tm), 0) == 0,
                         1.0, 0.0).astype(BF16)
    for h in range(N_HEADS):
        vt_ref[0, h * V_ROWS:h * V_ROWS + V_HEAD_DIM, :] = vt[h * V_HEAD_DIM:(h + 1) * V_HEAD_DIM]
        vt_ref[0, h * V_ROWS + V_HEAD_DIM:(h + 1) * V_ROWS, :] = ones_row

    @pl.when(pl.program_id(1) == 0)
    def _():
        carry_ref[...] = jnp.zeros_like(carry_ref)

    cu = z[:, _C_CG:_C_CG + CONV_WIDTH] * z[:, _C_U:_C_U + CONV_WIDTH]
    ext_ref[0:8, :] = carry_ref[...]
    ext_ref[8:8 + tm, :] = cu
    cu1 = ext_ref[7:7 + tm, :]
    cu2 = ext_ref[6:6 + tm, :]
    w = convw_ref[...]
    y = w[0:1] * cu2 + w[1:2] * cu1 + w[2:3] * cu
    oc_ref[0] = (z[:, _C_BG:_C_BG + CONV_WIDTH] * y).astype(BF16)
    carry_ref[...] = ext_ref[tm:tm + 8, :]


def _inproj(x, pos, invf, g, win, qg, wqa, wqb, kvg, wuk, pk, wuvt, convw):
    B, S, D = x.shape
    tm = TM_IN
    full = lambda a: pl.BlockSpec(a.shape, lambda b, i: (0,) * a.ndim)
    return pl.pallas_call(
        _inproj_kernel,
        grid=(B, S // tm),
        in_specs=[
            pl.BlockSpec((1, tm, D), lambda b, i: (b, i, 0)),
            pl.BlockSpec((1, 1, tm), lambda b, i: (b, 0, i)),
            full(invf), full(g), full(win), full(qg), full(wqa), full(wqb), full(kvg), full(wuk),
            full(pk), full(wuvt), full(convw),
        ],
        out_specs=[
            pl.BlockSpec((1, N_HEADS * HEAD_PAD, tm), lambda b, i: (b, 0, i)),
            pl.BlockSpec((1, tm, N_HEADS * HEAD_PAD), lambda b, i: (b, i, 0)),
            pl.BlockSpec((1, N_HEADS * V_ROWS, tm), lambda b, i: (b, 0, i)),
            pl.BlockSpec((1, tm, CONV_WIDTH), lambda b, i: (b, i, 0)),
        ],
        out_shape=[
            jax.ShapeDtypeStruct((B, N_HEADS * HEAD_PAD, S), BF16),
            jax.ShapeDtypeStruct((B, S, N_HEADS * HEAD_PAD), BF16),
            jax.ShapeDtypeStruct((B, N_HEADS * V_ROWS, S), BF16),
            jax.ShapeDtypeStruct((B, S, CONV_WIDTH), BF16),
        ],
        scratch_shapes=[pltpu.VMEM((8, CONV_WIDTH), F32), pltpu.VMEM((tm + 8, CONV_WIDTH), F32)],
        compiler_params=pltpu.CompilerParams(
            dimension_semantics=("arbitrary", "arbitrary"), vmem_limit_bytes=VMEM_LIMIT),
        name="inproj",
    )(x, pos, invf, g, win, qg, wqa, wqb, kvg, wuk, pk, wuvt, convw)


def _attn_kernel(qt_ref, k_ref, vt_ref, o_ref, z_ref, m_ref, acc_ref, sa_ref, sb_ref, zero_ref,
                 zsem):
    tq = qt_ref.shape[2]
    tk = TK
    assert tq == 2 * tk and ATT_TQC == tk
    i = pl.program_id(2)
    m_ref[...] = jnp.full_like(m_ref, NEG)
    acc_ref[...] = jnp.zeros_like(acc_ref)

    step = ((pl.program_id(0) * pl.num_programs(1) + pl.program_id(1)) * pl.num_programs(2) + i)
    zrows = zero_ref.shape[0]

    @pl.when(step == 0)
    def _():
        zero_ref[...] = jnp.zeros_like(zero_ref)

    zero_copy = pltpu.make_async_copy(
        zero_ref, z_ref.at[pl.ds(pl.multiple_of(step * zrows, 8), zrows)], zsem)
    zero_copy.start()

    def scores(j, s_ref, g, col0=0):
        k0 = pl.multiple_of(j * tk, tk)
        qt = qt_ref[0, g * HEAD_PAD:(g + 1) * HEAD_PAD, col0:]
        s_ref[g, :, col0:] = _dot(k_ref[0, pl.ds(k0, tk), g * HEAD_PAD:(g + 1) * HEAD_PAD], qt)

    def softmax_pv(j, s_ref, g, diag):
        k0 = pl.multiple_of(j * tk, tk)
        vt = vt_ref[0, g * V_ROWS:(g + 1) * V_ROWS, pl.ds(k0, tk)]
        for c in range(tq // ATT_TQC):
            if diag is not None and c < diag:
                continue
            cols = slice(c * ATT_TQC, (c + 1) * ATT_TQC)
            s = s_ref[g, :, cols]
            if diag is not None and c == diag:
                krow = lax.broadcasted_iota(jnp.int32, (tk, ATT_TQC), 0)
                qcol = lax.broadcasted_iota(jnp.int32, (tk, ATT_TQC), 1)
                s = jnp.where(krow <= qcol, s, NEG)
            m_old = m_ref[g, :, cols]
            m_new = jnp.maximum(m_old, jnp.max(s, axis=0, keepdims=True))
            alpha = jnp.exp2(m_old - m_new)
            p = jnp.exp2((s - m_new).astype(BF16))
            acc_ref[g, :, cols] = alpha * acc_ref[g, :, cols] + _dot(vt, p)
            m_ref[g, :, cols] = m_new

    def stage(j_next, s_next, j, s_cur, diag=None, next_col0=0):
        for g in range(ATT_HG):
            if j_next is not None:
                scores(j_next, s_next, g, next_col0)
            softmax_pv(j, s_cur, g, diag)

    for g in range(ATT_HG):
        scores(0, sa_ref, g)

    def pair(j):
        stage(j + 1, sb_ref, j, sa_ref)
        stage(j + 2, sa_ref, j + 1, sb_ref)

    def body(t, c):
        pair(4 * t)
        pair(4 * t + 2)
        return c

    lax.fori_loop(0, i // 2, body, 0)

    @pl.when(i % 2 == 1)
    def _():
        pair(2 * i - 2)

    n_full = 2 * i
    stage(n_full + 1, sb_ref, n_full, sa_ref, diag=0, next_col0=ATT_TQC)
    stage(None, None, n_full + 1, sb_ref, diag=1)
    for g in range(ATT_HG):
        o_ref[0, g * V_HEAD_DIM:(g + 1) * V_HEAD_DIM, :] = (
            acc_ref[g, 0:V_HEAD_DIM, :] / acc_ref[g, V_HEAD_DIM:V_HEAD_DIM + 1, :]).astype(o_ref.dtype)
    zero_copy.wait()


def _attention(qt, k, vt, min_zero_rows):
    B, _, S = qt.shape
    hg = ATT_HG
    grid = (B, N_HEADS // hg, S // TQ)
    n_steps = grid[0] * grid[1] * grid[2]
    zrows = -(-min_zero_rows // (8 * n_steps)) * 8
    return pl.pallas_call(
        _attn_kernel,
        grid=grid,
        in_specs=[
            pl.BlockSpec((1, hg * HEAD_PAD, TQ), lambda b, h, i: (b, h, i)),
            pl.BlockSpec((1, S, hg * HEAD_PAD), lambda b, h, i: (b, 0, h)),
            pl.BlockSpec((1, hg * V_ROWS, S), lambda b, h, i: (b, h, 0)),
        ],
        out_specs=[pl.BlockSpec((1, hg * V_HEAD_DIM, TQ), lambda b, h, i: (b, h, i)),
                   pl.BlockSpec(memory_space=pl.ANY)],
        out_shape=[jax.ShapeDtypeStruct((B, MLA_WIDTH, S), BF16),
                   jax.ShapeDtypeStruct((n_steps * zrows, D_MODEL // 2), jnp.uint32)],
        scratch_shapes=[pltpu.VMEM((hg, 1, TQ), F32), pltpu.VMEM((hg, V_ROWS, TQ), F32),
                        pltpu.VMEM((hg, TK, TQ), F32), pltpu.VMEM((hg, TK, TQ), F32),
                        pltpu.VMEM((zrows, D_MODEL // 2), jnp.uint32), pltpu.SemaphoreType.DMA],
        compiler_params=pltpu.CompilerParams(
            dimension_semantics=("arbitrary", "arbitrary", "arbitrary"),
            vmem_limit_bytes=VMEM_LIMIT),
        name="attn",
    )(qt, k, vt)


def _route_tile(logits, carry_ref, before_ranks):
    tm = logits.shape[1]
    big = jnp.int32(1 << 20)

    grow = lax.broadcasted_iota(jnp.int32, (ROUTER_E0, tm), 0)
    glog = jnp.where(grow < N_GROUPS, logits[0:ROUTER_E0], NEG)
    gmax = jnp.max(glog, axis=0, keepdims=True)
    gsum = jnp.sum(jnp.exp(glog - gmax), axis=0, keepdims=True)
    g_p = 1.0 / gsum
    g_idx = jnp.min(jnp.where(glog == gmax, grow, big), axis=0, keepdims=True)

    erow = lax.broadcasted_iota(jnp.int32, (N_EXPERTS, tm), 0)
    e_lo = g_idx * EXPERTS_PER_GROUP
    in_group = (erow >= e_lo) & (erow < e_lo + EXPERTS_PER_GROUP)
    elog = jnp.where(in_group, logits[ROUTER_E0:ROUTER_E0 + N_EXPERTS], NEG)
    emax = jnp.max(elog, axis=0, keepdims=True)
    esum = jnp.sum(jnp.exp(elog - emax), axis=0, keepdims=True)
    e1 = jnp.min(jnp.where(elog == emax, erow, big), axis=0, keepdims=True)
    elog2 = jnp.where(erow == e1, NEG, elog)
    emax2 = jnp.max(elog2, axis=0, keepdims=True)
    e2 = jnp.min(jnp.where(elog2 == emax2, erow, big), axis=0, keepdims=True)
    p1 = 1.0 / esum
    p2 = jnp.exp(emax2 - emax) / esum
    psum = p1 + p2
    w1 = g_p * (p1 / psum)
    w2 = g_p * (p2 / psum)

    oh1 = erow == e1
    oh2 = erow == e2
    oh = jnp.where(oh1 | oh2, 1.0, 0.0)
    before_ranks()
    srow = lax.broadcasted_iota(jnp.int32, (tm, tm), 0)
    scol = lax.broadcasted_iota(jnp.int32, (tm, tm), 1)
    earlier = jnp.where(srow < scol, 1.0, 0.0).astype(BF16)
    cum = _dot(oh.astype(BF16), earlier) + carry_ref[:, 0:1]
    r1 = jnp.sum(jnp.where(oh1, cum, 0.0), axis=0, keepdims=True)
    r2 = jnp.sum(jnp.where(oh2, cum, 0.0), axis=0, keepdims=True)
    return oh, oh1, oh2, r1, r2, w1, w2


def _outproj_kernel(x_ref, ot_ref, oc_ref, wo_ref, g_ref, wrhl_ref, br_ref, xbz_ref,
                    h1_ref, pos_ref, mw_ref, nb_ref, xb_ref,
                    carry_ref, last_ref, free_ref, xs0_ref, xs1_ref, dv0_ref, dv1_ref,
                    ds0_ref, ds1_ref, rsem, ssem):
    del xbz_ref
    tm = TM
    step = pl.program_id(0) * pl.num_programs(1) + pl.program_id(1)
    last_step = pl.num_programs(0) * pl.num_programs(1) - 1
    xs = (xs0_ref, xs1_ref)
    dv = (dv0_ref, dv1_ref)
    ds = (ds0_ref, ds1_ref)
    spare_row0 = xb_ref.shape[0] - 2 * tm

    def meta_copy(slot):
        return pltpu.make_async_copy(dv[slot], ds[slot], ssem.at[slot])

    def issue_rows(slot, r0=0, r1=TM):
        for r in range(r0, r1):
            for k in range(2):
                pltpu.make_async_copy(xs[slot].at[pl.ds(r, 1)],
                                      xb_ref.at[pl.ds(ds[slot][k, r], 1)], rsem.at[slot]
                                      ).start(priority=k)

    def issue_batch(slot, b):
        @pl.when(step >= 0)
        def _():
            issue_rows(slot, tm * b // OUT_BATCHES, tm * (b + 1) // OUT_BATCHES)

    def wait_rows(slot):
        for _ in range(2):
            pltpu.make_async_copy(xs[slot], xb_ref.at[pl.ds(0, tm)], rsem.at[slot]).wait()

    def compute(slot):
        other = 1 - slot
        rows = slice(slot * tm, (slot + 1) * tm)
        batches = iter(range(OUT_BATCHES))
        next_batch = lambda: issue_batch(other, next(batches))
        d_model = x_ref.shape[2]
        chunk = d_model // OUT_CHUNKS
        ot = ot_ref[0, :, rows]
        oc = oc_ref[0, rows, :]

        h1_chunks = []
        sumsq = jnp.zeros((tm, 1), F32)
        for c in range(OUT_CHUNKS):
            cols = slice(c * chunk, (c + 1) * chunk)
            next_batch()
            h1c = (x_ref[0, rows, cols] + _dot_tn(ot, wo_ref[0:MLA_WIDTH, cols])
                   + _dot(oc, wo_ref[MLA_WIDTH:MLA_WIDTH + CONV_WIDTH, cols]))
            h1_ref[0, rows, cols] = h1c
            sumsq = sumsq + jnp.sum(h1c * h1c, axis=-1, keepdims=True)
            h1_chunks.append(h1c)

        next_batch()
        inv_rms = lax.rsqrt(sumsq * (1.0 / d_model) + EPS)
        xn_chunks = [h1_chunks[c] * inv_rms * g_ref[:, c * chunk:(c + 1) * chunk]
                     for c in range(OUT_CHUNKS)]
        xs[slot][...] = _pack_bf16_pairs(jnp.concatenate(xn_chunks, axis=1))

        logits = jnp.broadcast_to(br_ref[...], (ROUTER_ROWS, tm))
        for c in range(OUT_CHUNKS):
            cols = slice(c * chunk, (c + 1) * chunk)
            next_batch()
            xh = xn_chunks[c].astype(BF16)
            xl = (xn_chunks[c] - xh.astype(F32)).astype(BF16)
            hl = _dot_nt(wrhl_ref[:, cols], xh)
            logits = (logits + hl[0:ROUTER_ROWS] + hl[ROUTER_ROWS:2 * ROUTER_ROWS]
                      + _dot_nt(wrhl_ref[0:ROUTER_ROWS, cols], xl))

        next_batch()
        oh, oh1, oh2, r1, r2, w1, w2 = _route_tile(logits, carry_ref, before_ranks=next_batch)
        assert next(batches, None) is None

        inv_blk = 1.0 / MOE_BLK
        cnt = carry_ref[:, 0:1]
        tile_cnt = jnp.sum(oh, axis=1, keepdims=True)
        nb_before = jnp.floor((cnt + (MOE_BLK - 1)) * inv_blk)
        nb_after = jnp.floor((cnt + tile_cnt + (MOE_BLK - 1)) * inv_blk)
        new = nb_after - nb_before
        erow = lax.broadcasted_iota(jnp.int32, (N_EXPERTS, N_EXPERTS), 0)
        ecol = lax.broadcasted_iota(jnp.int32, (N_EXPERTS, N_EXPERTS), 1)
        lower = jnp.where(ecol < erow, 1.0, 0.0).astype(BF16)
        new_rep = jnp.broadcast_to(new, (N_EXPERTS, HEAD_PAD))
        new_id = free_ref[0:1, :] + _dot(lower, new_rep.astype(BF16))
        free_ref[...] = free_ref[...] + jnp.sum(new, axis=0, keepdims=True)
        last_before = last_ref[...]
        is_new = new_rep > 0.0
        last_ref[...] = jnp.where(is_new, new_id, last_before)
        nb_ref[slot] = jnp.where(is_new, new_id, -1.0).astype(jnp.int32)
        carry_ref[...] = carry_ref[...] + tile_cnt

        def place(ohk, rk):
            bi = jnp.floor(rk * inv_blk)
            blk = jnp.where(bi == nb_before - 1.0, last_before[:, 0:1], new_id[:, 0:1])
            phys = jnp.sum(jnp.where(ohk, blk, 0.0), axis=0, keepdims=True)
            return (phys * MOE_BLK + (rk - bi * MOE_BLK)).astype(jnp.int32)

        mrow = lax.broadcasted_iota(jnp.int32, (8, tm), 0)
        dest = jnp.where(mrow == 0, place(oh1, r1), jnp.where(mrow == 1, place(oh2, r2), 0))
        dv[slot][...] = dest
        pos_ref[0, :, rows] = dest
        mw_ref[0, :, rows] = jnp.where(mrow == 0, w1, jnp.where(mrow == 1, w2, 0.0))
        meta_copy(slot).start()

    @pl.when(step == 0)
    def _():
        carry_ref[...] = jnp.zeros_like(carry_ref)
        free_ref[...] = jnp.zeros_like(free_ref)
        last_ref[...] = jnp.full_like(last_ref, -1.0)
        xs1_ref[...] = jnp.zeros_like(xs1_ref)
        mrow = lax.broadcasted_iota(jnp.int32, (8, tm), 0)
        lane = lax.broadcasted_iota(jnp.int32, (8, tm), 1)
        dv1_ref[...] = jnp.where(mrow < 2, spare_row0 + 2 * lane + mrow, 0)
        meta_copy(1).start()

    meta_copy(1).wait()
    compute(0)
    wait_rows(1)
    meta_copy(0).wait()
    compute(1)
    wait_rows(0)

    @pl.when(step == last_step)
    def _():
        meta_copy(1).wait()
        issue_rows(1)
        wait_rows(1)


def _outproj(x, ot, oc, wo, g, wrhl, br, xbz):
    B, S, D = x.shape
    tm2 = 2 * TM
    n_s = S // tm2
    full = lambda a: pl.BlockSpec(a.shape, lambda b, i: (0,) * a.ndim)
    tile = lambda w: pl.BlockSpec((1, tm2, w), lambda b, i: (b, i, 0))
    meta = pl.BlockSpec((1, 8, tm2), lambda b, i: (b, 0, i))
    any_space = pl.BlockSpec(memory_space=pl.ANY)
    return pl.pallas_call(
        _outproj_kernel,
        grid=(B, n_s),
        in_specs=[
            tile(D),
            pl.BlockSpec((1, MLA_WIDTH, tm2), lambda b, i: (b, 0, i)),
            tile(CONV_WIDTH),
            full(wo), full(g), full(wrhl), full(br), any_space,
        ],
        out_specs=[tile(D), meta, meta,
                   pl.BlockSpec((2, N_EXPERTS, HEAD_PAD), lambda b, i: (b * n_s + i, 0, 0)),
                   any_space],
        out_shape=[
            jax.ShapeDtypeStruct((B, S, D), F32),
            jax.ShapeDtypeStruct((B, 8, S), jnp.int32),
            jax.ShapeDtypeStruct((B, 8, S), F32),
            jax.ShapeDtypeStruct((B * S // TM, N_EXPERTS, HEAD_PAD), jnp.int32),
            jax.ShapeDtypeStruct(xbz.shape, xbz.dtype),
        ],
        input_output_aliases={7: 4},
        scratch_shapes=[
            pltpu.VMEM((N_EXPERTS, HEAD_PAD), F32), pltpu.VMEM((N_EXPERTS, HEAD_PAD), F32),
            pltpu.VMEM((8, HEAD_PAD), F32),
            pltpu.VMEM((TM, D // 2), jnp.uint32), pltpu.VMEM((TM, D // 2), jnp.uint32),
            pltpu.VMEM((8, TM), jnp.int32), pltpu.VMEM((8, TM), jnp.int32),
            pltpu.SMEM((8, TM), jnp.int32), pltpu.SMEM((8, TM), jnp.int32),
            pltpu.SemaphoreType.DMA((2,)), pltpu.SemaphoreType.DMA((2,)),
        ],
        compiler_params=pltpu.CompilerParams(
            dimension_semantics=("arbitrary", "arbitrary"), vmem_limit_bytes=VMEM_LIMIT),
        name="outproj",
    )(x, ot, oc, wo, g, wrhl, br, xbz)


def _expert_kernel(order_ref, blke_ref, nused_ref, xb_ref, wg_ref, wu_ref, wd_ref, yb_ref,
                   wgs_ref, wus_ref, wds_ref):
    del order_ref
    i = pl.program_id(0)
    used = i < nused_ref[0]
    changed = (i == 0) | (blke_ref[i] != blke_ref[jnp.maximum(i - 1, 0)])

    @pl.when(used & changed)
    def _():
        wgs_ref[...] = wg_ref[0].astype(BF16)
        wus_ref[...] = wu_ref[0].astype(BF16)
        wds_ref[...] = wd_ref[0].astype(BF16)

    @pl.when(used)
    def _():
        x_lo, x_hi = _unpack_bf16_pairs(xb_ref[...])
        dh = x_lo.shape[1]
        half = EXPERT_FF // 2

        def xdot(w_ref, cols):
            return _dot(x_lo, w_ref[0:dh, cols]) + _dot(x_hi, w_ref[dh:2 * dh, cols])

        g0 = xdot(wgs_ref, slice(0, half))
        u0 = xdot(wus_ref, slice(0, half))
        g1 = xdot(wgs_ref, slice(half, EXPERT_FF))
        u1 = xdot(wus_ref, slice(half, EXPERT_FF))
        h0 = ((g0 * jax.nn.sigmoid(g0)) * u0).astype(BF16)
        y = _dot(h0, wds_ref[:half, :])
        h1 = ((g1 * jax.nn.sigmoid(g1)) * u1).astype(BF16)
        yb_ref[...] = y + _dot(h1, wds_ref[half:, :])

    @pl.when(jnp.logical_not(used))
    def _():
        yb_ref[...] = jnp.zeros_like(yb_ref)


def _experts(order, blk_e, n_used, xb, wg, wu, wd):
    D = 2 * xb.shape[1]
    blk = MOE_BLK
    n_blocks = order.shape[0]
    return pl.pallas_call(
        _expert_kernel,
        grid_spec=pltpu.PrefetchScalarGridSpec(
            num_scalar_prefetch=3,
            grid=(n_blocks,),
            in_specs=[
                pl.BlockSpec((blk, D // 2), lambda i, o, be, nu: (o[jnp.minimum(i, nu[0] - 1)], 0)),
                pl.BlockSpec((1, D, EXPERT_FF), lambda i, o, be, nu: (be[i], 0, 0)),
                pl.BlockSpec((1, D, EXPERT_FF), lambda i, o, be, nu: (be[i], 0, 0)),
                pl.BlockSpec((1, EXPERT_FF, D), lambda i, o, be, nu: (be[i], 0, 0)),
            ],
            out_specs=pl.BlockSpec((blk, D), lambda i, o, be, nu: (o[i], 0)),
            scratch_shapes=[pltpu.VMEM((D, EXPERT_FF), BF16), pltpu.VMEM((D, EXPERT_FF), BF16),
                            pltpu.VMEM((EXPERT_FF, D), BF16)],
        ),
        out_shape=jax.ShapeDtypeStruct((n_blocks * blk, D), F32),
        compiler_params=pltpu.CompilerParams(
            dimension_semantics=("arbitrary",), vmem_limit_bytes=VMEM_LIMIT),
        name="experts",
    )(order, blk_e, n_used, xb, wg, wu, wd)


def _final_kernel(dest_ref, h1_ref, p_ref, wt_ref, yb_ref, gp_ref, wg_ref, bg_ref, wp_ref,
                  gf_ref, out_ref, buf0_ref, buf1_ref, buf2_ref, buf3_ref, sem):
    tm = h1_ref.shape[0] // FIN_TILES
    i = pl.program_id(0)
    bufs = (buf0_ref, buf1_ref, buf2_ref, buf3_ref)

    def row_copy(slot, k, r, d):
        return pltpu.make_async_copy(yb_ref.at[pl.ds(d, 1)], bufs[slot].at[k, pl.ds(r, 1)],
                                     sem.at[slot])

    def issue(tile, slot):
        for r in range(tm):
            t = (tile * tm + r) * 2
            row_copy(slot, 0, r, dest_ref[t]).start(priority=0)
            row_copy(slot, 1, r, dest_ref[t + 1]).start(priority=1)

    def wait(slot):
        for k in range(2):
            pltpu.make_async_copy(yb_ref.at[pl.ds(0, tm)], bufs[slot].at[k], sem.at[slot]).wait()

    def compute(slot):
        rows = slice(slot * tm, (slot + 1) * tm)
        pe = _dot(p_ref[rows, :].astype(BF16), wp_ref[...])
        wt = wt_ref[rows, :]
        h2 = h1_ref[rows, :] + wt[:, 0:1] * bufs[slot][0] + wt[:, 1:2] * bufs[slot][1]
        n = _rms(h2, gp_ref[...]).astype(BF16)
        gate = jax.nn.sigmoid(_dot(n, wg_ref[...]) + bg_ref[...])
        h3 = h2 + gate * pe
        out_ref[rows, :] = _rms(h3, gf_ref[...])

    @pl.when(i == 0)
    def _():
        for j in range(FIN_AHEAD):
            issue(j, j)

    last_tile = FIN_TILES * pl.num_programs(0) - 1
    for j in range(FIN_TILES):
        issue(jnp.minimum(FIN_TILES * i + j + FIN_AHEAD, last_tile), (j + FIN_AHEAD) % FIN_TILES)
        wait(j)
        compute(j)

    @pl.when(i == pl.num_programs(0) - 1)
    def _():
        for j in range(FIN_AHEAD):
            wait(j)


def _final(dest, h1, p, wt, yb, gp, wg, bg, wp, gf):
    N, D = h1.shape
    tm = TM
    full = lambda a: pl.BlockSpec(a.shape, lambda i, dest: (0,) * a.ndim)
    return pl.pallas_call(
        _final_kernel,
        grid_spec=pltpu.PrefetchScalarGridSpec(
            num_scalar_prefetch=1,
            grid=(N // (FIN_TILES * tm),),
            in_specs=[
                pl.BlockSpec((FIN_TILES * tm, D), lambda i, dest: (i, 0)),
                pl.BlockSpec((FIN_TILES * tm, PLE_DIM), lambda i, dest: (i, 0)),
                pl.BlockSpec((FIN_TILES * tm, 2), lambda i, dest: (i, 0)),
                pl.BlockSpec(memory_space=pl.ANY),
                full(gp), full(wg), full(bg), full(wp), full(gf),
            ],
            out_specs=pl.BlockSpec((FIN_TILES * tm, D), lambda i, dest: (i, 0)),
            scratch_shapes=[pltpu.VMEM((2, tm, D), F32) for _ in range(FIN_TILES)]
            + [pltpu.SemaphoreType.DMA((FIN_TILES,))],
        ),
        out_shape=jax.ShapeDtypeStruct((N, D), F32),
        compiler_params=pltpu.CompilerParams(
            dimension_semantics=("arbitrary",), vmem_limit_bytes=VMEM_LIMIT),
        name="final",
    )(dest, h1, p, wt, yb, gp, wg, bg, wp, gf)


def _prep_weights(w_in, w_uq, w_ukv):
    cq, ckv, kr, bg, cg, u = jnp.split(
        w_in, [Q_LORA, Q_LORA + KV_LORA, Q_LORA + KV_LORA + QK_ROPE_DIM,
               Q_LORA + KV_LORA + QK_ROPE_DIM + CONV_WIDTH,
               Q_LORA + KV_LORA + QK_ROPE_DIM + 2 * CONV_WIDTH], axis=1)
    kr_rot = jnp.concatenate([-kr[:, HALF_ROPE:], kr[:, :HALF_ROPE]], axis=1)
    pad = jnp.zeros((D_MODEL, HEAD_PAD - 2 * QK_ROPE_DIM), w_in.dtype)
    win = jnp.concatenate([cq, ckv, bg, cg, u, kr, kr_rot, pad], axis=1).astype(BF16)

    zq = jnp.zeros((Q_LORA, N_HEADS, HEAD_PAD - QK_DIM), w_uq.dtype)
    wqa = jnp.concatenate([w_uq, zq], axis=2)
    rope = w_uq[:, :, QK_NOPE_DIM:]
    wqb = jnp.concatenate([-rope[:, :, HALF_ROPE:], rope[:, :, :HALF_ROPE]], axis=2)
    wqa = wqa.reshape(Q_LORA, N_HEADS * HEAD_PAD).T.astype(BF16)
    wqb = wqb.reshape(Q_LORA, N_HEADS * QK_ROPE_DIM).T.astype(BF16)

    zk = jnp.zeros((KV_LORA, N_HEADS, HEAD_PAD - QK_NOPE_DIM), w_ukv.dtype)
    wuk = jnp.concatenate([w_ukv[:, :, :QK_NOPE_DIM], zk], axis=2)
    wuk = wuk.reshape(KV_LORA, N_HEADS * HEAD_PAD).astype(BF16)
    wuvt = w_ukv[:, :, QK_NOPE_DIM:].reshape(KV_LORA, MLA_WIDTH).T.astype(BF16)

    src = jnp.arange(QK_ROPE_DIM)[:, None]
    dst = jnp.arange(N_HEADS * HEAD_PAD)[None, :]
    pk = (dst % HEAD_PAD - QK_NOPE_DIM == src).astype(BF16)
    return win, wqa, wqb, wuk, wuvt, pk


def kernel(x, p, positions, attn_norm_g, w_in, q_norm_g, w_uq, kv_norm_g, w_ukv, conv_w, w_out,
           moe_norm_g, w_group_router, b_group_router, w_expert_router, b_expert_router,
           w_gate, w_up, w_down, ple_norm_g, w_ple_gate, b_ple_gate, w_ple_proj, final_norm_g):
    B, S, D = x.shape
    N = B * S
    assert w_in.shape[0] == 1, "single-layer trunk: the final norm is fused into the layer"
    pos = positions.astype(F32).reshape(B, 1, S)
    invf = (ROPE_BASE ** (-jnp.arange(0, QK_ROPE_DIM, 2, dtype=F32) / QK_ROPE_DIM)).reshape(-1, 1)
    row = lambda v: v.reshape(1, -1)
    h = x
    for i in range(1):
        win, wqa, wqb, wuk, wuvt, pk = _prep_weights(w_in[i], w_uq[i], w_ukv[i])
        qt, k, vt, oc = _inproj(h, pos, invf, row(attn_norm_g[i]), win, row(q_norm_g[i]), wqa,
                                wqb, row(kv_norm_g[i]), wuk, pk, wuvt,
                                conv_w[i].reshape(CONV_K, CONV_WIDTH))
        n_blocks = (2 * N) // MOE_BLK + N_EXPERTS
        ot, xbz = _attention(qt, k, vt, n_blocks * MOE_BLK + 2 * TM)

        zrow = lambda n: jnp.zeros((n, D), F32)
        wr = jnp.concatenate(
            [w_group_router[i].T, zrow(ROUTER_E0 - N_GROUPS), w_expert_router[i].T,
             zrow(ROUTER_ROWS - ROUTER_E0 - N_EXPERTS)], axis=0)
        wrh = wr.astype(BF16)
        wrhl = jnp.concatenate([wrh, (wr - wrh.astype(F32)).astype(BF16)], axis=0)
        br = jnp.concatenate(
            [b_group_router[i], jnp.zeros((ROUTER_E0 - N_GROUPS,), F32), b_expert_router[i],
             jnp.zeros((ROUTER_ROWS - ROUTER_E0 - N_EXPERTS,), F32)]).reshape(-1, 1)
        h1, pos, meta_w, opened, xb = _outproj(h, ot, oc, w_out[i].astype(BF16),
                                               row(moe_norm_g[i]), wrhl, br, xbz)

        opened = opened[:, :, 0]
        blk_ids = jnp.arange(n_blocks, dtype=jnp.int32)
        hit = opened[None, :, :] == blk_ids[:, None, None]
        used = jnp.any(hit, axis=(1, 2))
        e_of_blk = jnp.sum(jnp.where(hit, jnp.arange(N_EXPERTS, dtype=jnp.int32), 0), axis=(1, 2))
        n_used = jnp.sum(used.astype(jnp.int32)).reshape(1)
        key = jnp.where(used, e_of_blk, N_EXPERTS) * n_blocks + blk_ids
        slot_of_blk = jnp.sum((key[None, :] < key[:, None]).astype(jnp.int32), axis=1)
        at_slot = slot_of_blk[None, :] == blk_ids[:, None]
        order = jnp.sum(jnp.where(at_slot, blk_ids[None, :], 0), axis=1)
        blk_e = jnp.sum(jnp.where(at_slot, e_of_blk[None, :], 0), axis=1)
        blk_e = jnp.where(blk_ids < n_used, blk_e, jnp.max(jnp.where(used, e_of_blk, 0)))
        dest = jnp.swapaxes(pos[:, 0:2, :], 1, 2).reshape(2 * N)
        wt = jnp.swapaxes(meta_w[:, 0:2, :], 1, 2).reshape(N, 2)

        yb = _experts(order, blk_e, n_used, xb, w_gate[i], w_up[i], w_down[i])
        out = _final(dest, h1.reshape(N, D), p[i].reshape(N, PLE_DIM), wt, yb,
                     row(ple_norm_g[i]), w_ple_gate[i].astype(BF16), row(b_ple_gate[i]),
                     w_ple_proj[i].astype(BF16), row(final_norm_g))
        h = out.reshape(B, S, D)
    return h
```

```python
import functools
import math

import jax
import jax.numpy as jnp
from jax import lax
from jax.experimental import pallas as pl
from jax.experimental.pallas import tpu as pltpu

D_MODEL = 1024
PLE_DIM = 256
MLA_WIDTH = 512
CONV_WIDTH = 512
N_HEADS = 8
V_HEAD_DIM = 64
QK_NOPE_DIM = 64
QK_ROPE_DIM = 32
Q_LORA = 384
KV_LORA = 256
CONV_K = 3
N_GROUPS = 4
EXPERTS_PER_GROUP = 8
N_EXPERTS = N_GROUPS * EXPERTS_PER_GROUP
EXPERT_FF = 512
ROPE_BASE = 10000.0
EPS = 1e-6

HEAD_PAD = 128
V_ROWS = 80
QK_DIM = QK_NOPE_DIM + QK_ROPE_DIM
HALF_ROPE = QK_ROPE_DIM // 2

_C_Q = 0
_C_KV = _C_Q + Q_LORA
_C_BG = _C_KV + KV_LORA
_C_CG = _C_BG + CONV_WIDTH
_C_U = _C_CG + CONV_WIDTH
_C_KR = _C_U + CONV_WIDTH
IN_COLS_PAD = _C_KR + HEAD_PAD

TM = 256
TM_IN = 512
FIN_TILES = 4
FIN_AHEAD = 2
TQ = 512
TK = 256
ATT_HG = 4
ATT_TQC = 256
MOE_BLK = 512
ROUTER_E0 = 8
ROUTER_ROWS = 48
NEG = -1e30
Q_SCALE = (QK_DIM ** -0.5) * math.log2(math.e)
VMEM_LIMIT = 48 * 1024 * 1024

F32 = jnp.float32
BF16 = jnp.bfloat16


def _rms(x, g):
    return x * lax.rsqrt(jnp.mean(x * x, axis=-1, keepdims=True) + EPS) * g


def _dot(a, b):
    return jnp.dot(a, b, preferred_element_type=F32)


def _dot_nt(a, b):
    return lax.dot_general(a, b, (((1,), (1,)), ((), ())), preferred_element_type=F32)


def _pack_bf16_pairs(x):
    w = x.shape[1] // 2
    lo = pltpu.bitcast(x[:, :w].astype(BF16).astype(F32), jnp.uint32)
    hi = pltpu.bitcast(x[:, w:].astype(BF16).astype(F32), jnp.uint32)
    return (lo >> 16) | hi


def _unpack_bf16_pairs(p):
    lo = pltpu.bitcast(p << 16, F32).astype(BF16)
    hi = pltpu.bitcast(p & jnp.uint32(0xFFFF0000), F32).astype(BF16)
    return lo, hi


def _unpack_f32_pairs(p):
    lo = pltpu.bitcast(p << 16, F32)
    hi = pltpu.bitcast(p & jnp.uint32(0xFFFF0000), F32)
    return jnp.concatenate([lo, hi], axis=1)


def _dot_tn(a, b):
    return lax.dot_general(a, b, (((0,), (0,)), ((), ())), preferred_element_type=F32)


def _inproj_kernel(x_ref, pos_ref, invf_ref, g_ref, win_ref, qg_ref, wqa_ref, wqb_ref, kvg_ref,
                   wuk_ref, pk_ref, wuvt_ref, convw_ref,
                   qt_ref, k_ref, vt_ref, oc_ref, carry_ref, ext_ref):
    tm = x_ref.shape[1]
    x = x_ref[0]
    xn = _rms(x, g_ref[...])
    z = _dot(xn.astype(BF16), win_ref[...])

    ang = invf_ref[...] * pos_ref[0]
    cos = jnp.cos(ang)
    sin = jnp.sin(ang)
    cos2 = jnp.concatenate([cos, cos], axis=0)
    sin2 = jnp.concatenate([sin, sin], axis=0)

    cqn = _rms(z[:, _C_Q:_C_Q + Q_LORA], qg_ref[...]).astype(BF16)
    qa = _dot_nt(wqa_ref[...], cqn)
    qb = _dot_nt(wqb_ref[...], cqn)
    for h in range(N_HEADS):
        r0 = h * HEAD_PAD
        nope = qa[r0:r0 + QK_NOPE_DIM]
        rope = (qa[r0 + QK_NOPE_DIM:r0 + QK_DIM] * cos2
                + qb[h * QK_ROPE_DIM:(h + 1) * QK_ROPE_DIM] * sin2)
        qh = jnp.concatenate([nope, rope, qa[r0 + QK_DIM:r0 + HEAD_PAD]], axis=0) * Q_SCALE
        qt_ref[0, r0:r0 + HEAD_PAD, :] = qh.astype(BF16)

    kvn = _rms(z[:, _C_KV:_C_KV + KV_LORA], kvg_ref[...]).astype(BF16)
    krt = z[:, _C_KR:_C_KR + HEAD_PAD].T
    krot = krt[0:QK_ROPE_DIM] * cos2 + krt[QK_ROPE_DIM:2 * QK_ROPE_DIM] * sin2
    k_ref[0] = (_dot(kvn, wuk_ref[...]) + _dot_tn(krot.astype(BF16), pk_ref[...])).astype(BF16)
    vt = _dot_nt(wuvt_ref[...], kvn).astype(BF16)
    ones_row = jnp.where(lax.broadcasted_iota(jnp.int32, (V_ROWS - V_HEAD_DIM, tm), 0) == 0,
                         1.0, 0.0).astype(BF16)
    for h in range(N_HEADS):
        vt_ref[0, h * V_ROWS:h * V_ROWS + V_HEAD_DIM, :] = vt[h * V_HEAD_DIM:(h + 1) * V_HEAD_DIM]
        vt_ref[0, h * V_ROWS + V_HEAD_DIM:(h + 1) * V_ROWS, :] = ones_row

    @pl.when(pl.program_id(1) == 0)
    def _():
        carry_ref[...] = jnp.zeros_like(carry_ref)

    cu = z[:, _C_CG:_C_CG + CONV_WIDTH] * z[:, _C_U:_C_U + CONV_WIDTH]
    ext_ref[0:8, :] = carry_ref[...]
    ext_ref[8:8 + tm, :] = cu
    cu1 = ext_ref[7:7 + tm, :]
    cu2 = ext_ref[6:6 + tm, :]
    w = convw_ref[...]
    y = w[0:1] * cu2 + w[1:2] * cu1 + w[2:3] * cu
    oc_ref[0] = (z[:, _C_BG:_C_BG + CONV_WIDTH] * y).astype(BF16)
    carry_ref[...] = ext_ref[tm:tm + 8, :]


def _inproj(x, pos, invf, g, win, qg, wqa, wqb, kvg, wuk, pk, wuvt, convw):
    B, S, D = x.shape
    tm = TM_IN
    full = lambda a: pl.BlockSpec(a.shape, lambda b, i: (0,) * a.ndim)
    return pl.pallas_call(
        _inproj_kernel,
        grid=(B, S // tm),
        in_specs=[
            pl.BlockSpec((1, tm, D), lambda b, i: (b, i, 0)),
            pl.BlockSpec((1, 1, tm), lambda b, i: (b, 0, i)),
            full(invf), full(g), full(win), full(qg), full(wqa), full(wqb), full(kvg), full(wuk),
            full(pk), full(wuvt), full(convw),
        ],
        out_specs=[
            pl.BlockSpec((1, N_HEADS * HEAD_PAD, tm), lambda b, i: (b, 0, i)),
            pl.BlockSpec((1, tm, N_HEADS * HEAD_PAD), lambda b, i: (b, i, 0)),
            pl.BlockSpec((1, N_HEADS * V_ROWS, tm), lambda b, i: (b, 0, i)),
            pl.BlockSpec((1, tm, CONV_WIDTH), lambda b, i: (b, i, 0)),
        ],
        out_shape=[
            jax.ShapeDtypeStruct((B, N_HEADS * HEAD_PAD, S), BF16),
            jax.ShapeDtypeStruct((B, S, N_HEADS * HEAD_PAD), BF16),
            jax.ShapeDtypeStruct((B, N_HEADS * V_ROWS, S), BF16),
            jax.ShapeDtypeStruct((B, S, CONV_WIDTH), BF16),
        ],
        scratch_shapes=[pltpu.VMEM((8, CONV_WIDTH), F32), pltpu.VMEM((tm + 8, CONV_WIDTH), F32)],
        compiler_params=pltpu.CompilerParams(
            dimension_semantics=("arbitrary", "arbitrary"), vmem_limit_bytes=VMEM_LIMIT),
        name="inproj",
    )(x, pos, invf, g, win, qg, wqa, wqb, kvg, wuk, pk, wuvt, convw)


def _attn_kernel(qt_ref, k_ref, vt_ref, o_ref, z_ref, m_ref, acc_ref, sa_ref, sb_ref, zero_ref,
                 zsem):
    tq = qt_ref.shape[2]
    tk = TK
    assert tq == 2 * tk and ATT_TQC == tk
    i = pl.program_id(2)
    m_ref[...] = jnp.full_like(m_ref, NEG)
    acc_ref[...] = jnp.zeros_like(acc_ref)

    step = ((pl.program_id(0) * pl.num_programs(1) + pl.program_id(1)) * pl.num_programs(2) + i)
    zrows = zero_ref.shape[0]

    @pl.when(step == 0)
    def _():
        zero_ref[...] = jnp.zeros_like(zero_ref)

    zero_copy = pltpu.make_async_copy(
        zero_ref, z_ref.at[pl.ds(pl.multiple_of(step * zrows, 8), zrows)], zsem)
    zero_copy.start()

    def scores(j, s_ref, g, col0=0):
        k0 = pl.multiple_of(j * tk, tk)
        qt = qt_ref[0, g * HEAD_PAD:(g + 1) * HEAD_PAD, col0:]
        s_ref[g, :, col0:] = _dot(k_ref[0, pl.ds(k0, tk), g * HEAD_PAD:(g + 1) * HEAD_PAD], qt)

    def softmax_pv(j, s_ref, g, diag):
        k0 = pl.multiple_of(j * tk, tk)
        vt = vt_ref[0, g * V_ROWS:(g + 1) * V_ROWS, pl.ds(k0, tk)]
        for c in range(tq // ATT_TQC):
            if diag is not None and c < diag:
                continue
            cols = slice(c * ATT_TQC, (c + 1) * ATT_TQC)
            s = s_ref[g, :, cols]
            if diag is not None and c == diag:
                krow = lax.broadcasted_iota(jnp.int32, (tk, ATT_TQC), 0)
                qcol = lax.broadcasted_iota(jnp.int32, (tk, ATT_TQC), 1)
                s = jnp.where(krow <= qcol, s, NEG)
            m_old = m_ref[g, :, cols]
            m_new = jnp.maximum(m_old, jnp.max(s, axis=0, keepdims=True))
            alpha = jnp.exp2(m_old - m_new)
            p = jnp.exp2((s - m_new).astype(BF16))
            acc_ref[g, :, cols] = alpha * acc_ref[g, :, cols] + _dot(vt, p)
            m_ref[g, :, cols] = m_new

    def stage(j_next, s_next, j, s_cur, diag=None, next_col0=0):
        for g in range(ATT_HG):
            if j_next is not None:
                scores(j_next, s_next, g, next_col0)
            softmax_pv(j, s_cur, g, diag)

    for g in range(ATT_HG):
        scores(0, sa_ref, g)

    def pair(j):
        stage(j + 1, sb_ref, j, sa_ref)
        stage(j + 2, sa_ref, j + 1, sb_ref)

    def body(t, c):
        pair(4 * t)
        pair(4 * t + 2)
        return c

    lax.fori_loop(0, i // 2, body, 0)

    @pl.when(i % 2 == 1)
    def _():
        pair(2 * i - 2)

    n_full = 2 * i
    stage(n_full + 1, sb_ref, n_full, sa_ref, diag=0, next_col0=ATT_TQC)
    stage(None, None, n_full + 1, sb_ref, diag=1)
    for g in range(ATT_HG):
        o_ref[0, g * V_HEAD_DIM:(g + 1) * V_HEAD_DIM, :] = (
            acc_ref[g, 0:V_HEAD_DIM, :] / acc_ref[g, V_HEAD_DIM:V_HEAD_DIM + 1, :]).astype(o_ref.dtype)
    zero_copy.wait()


def _attention(qt, k, vt, min_zero_rows):
    B, _, S = qt.shape
    hg = ATT_HG
    grid = (B, N_HEADS // hg, S // TQ)
    n_steps = grid[0] * grid[1] * grid[2]
    zrows = -(-min_zero_rows // (8 * n_steps)) * 8
    return pl.pallas_call(
        _attn_kernel,
        grid=grid,
        in_specs=[
            pl.BlockSpec((1, hg * HEAD_PAD, TQ), lambda b, h, i: (b, h, i)),
            pl.BlockSpec((1, S, hg * HEAD_PAD), lambda b, h, i: (b, 0, h)),
            pl.BlockSpec((1, hg * V_ROWS, S), lambda b, h, i: (b, h, 0)),
        ],
        out_specs=[pl.BlockSpec((1, hg * V_HEAD_DIM, TQ), lambda b, h, i: (b, h, i)),
                   pl.BlockSpec(memory_space=pl.ANY)],
        out_shape=[jax.ShapeDtypeStruct((B, MLA_WIDTH, S), BF16),
                   jax.ShapeDtypeStruct((n_steps * zrows, D_MODEL // 2), jnp.uint32)],
        scratch_shapes=[pltpu.VMEM((hg, 1, TQ), F32), pltpu.VMEM((hg, V_ROWS, TQ), F32),
                        pltpu.VMEM((hg, TK, TQ), F32), pltpu.VMEM((hg, TK, TQ), F32),
                        pltpu.VMEM((zrows, D_MODEL // 2), jnp.uint32), pltpu.SemaphoreType.DMA],
        compiler_params=pltpu.CompilerParams(
            dimension_semantics=("arbitrary", "arbitrary", "arbitrary"),
            vmem_limit_bytes=VMEM_LIMIT),
        name="attn",
    )(qt, k, vt)


def _route_tile(xn, wrhl_ref, br_ref, carry_ref):
    tm = xn.shape[0]
    xh = xn.astype(BF16)
    xl = (xn - xh.astype(F32)).astype(BF16)
    hl = _dot_nt(wrhl_ref[...], xh)
    logits = (hl[0:ROUTER_ROWS] + hl[ROUTER_ROWS:2 * ROUTER_ROWS]
              + _dot_nt(wrhl_ref[0:ROUTER_ROWS, :], xl) + br_ref[...])
    big = jnp.int32(1 << 20)

    grow = lax.broadcasted_iota(jnp.int32, (ROUTER_E0, tm), 0)
    glog = jnp.where(grow < N_GROUPS, logits[0:ROUTER_E0], NEG)
    gmax = jnp.max(glog, axis=0, keepdims=True)
    gsum = jnp.sum(jnp.exp(glog - gmax), axis=0, keepdims=True)
    g_p = 1.0 / gsum
    g_idx = jnp.min(jnp.where(glog == gmax, grow, big), axis=0, keepdims=True)

    erow = lax.broadcasted_iota(jnp.int32, (N_EXPERTS, tm), 0)
    e_lo = g_idx * EXPERTS_PER_GROUP
    in_group = (erow >= e_lo) & (erow < e_lo + EXPERTS_PER_GROUP)
    elog = jnp.where(in_group, logits[ROUTER_E0:ROUTER_E0 + N_EXPERTS], NEG)
    emax = jnp.max(elog, axis=0, keepdims=True)
    esum = jnp.sum(jnp.exp(elog - emax), axis=0, keepdims=True)
    e1 = jnp.min(jnp.where(elog == emax, erow, big), axis=0, keepdims=True)
    elog2 = jnp.where(erow == e1, NEG, elog)
    emax2 = jnp.max(elog2, axis=0, keepdims=True)
    e2 = jnp.min(jnp.where(elog2 == emax2, erow, big), axis=0, keepdims=True)
    p1 = 1.0 / esum
    p2 = jnp.exp(emax2 - emax) / esum
    psum = p1 + p2
    w1 = g_p * (p1 / psum)
    w2 = g_p * (p2 / psum)

    oh1 = erow == e1
    oh2 = erow == e2
    oh = jnp.where(oh1 | oh2, 1.0, 0.0)
    srow = lax.broadcasted_iota(jnp.int32, (tm, tm), 0)
    scol = lax.broadcasted_iota(jnp.int32, (tm, tm), 1)
    earlier = jnp.where(srow < scol, 1.0, 0.0).astype(BF16)
    cum = _dot(oh.astype(BF16), earlier) + carry_ref[:, 0:1]
    r1 = jnp.sum(jnp.where(oh1, cum, 0.0), axis=0, keepdims=True)
    r2 = jnp.sum(jnp.where(oh2, cum, 0.0), axis=0, keepdims=True)
    return oh, oh1, oh2, r1, r2, w1, w2


def _outproj_kernel(x_ref, ot_ref, oc_ref, wo_ref, g_ref, wrhl_ref, br_ref, xbz_ref,
                    h1_ref, pos_ref, mw_ref, nb_ref, xb_ref,
                    carry_ref, last_ref, free_ref, xs0_ref, xs1_ref, dv0_ref, dv1_ref,
                    ds0_ref, ds1_ref, rsem, ssem):
    del xbz_ref
    tm = TM
    step = pl.program_id(0) * pl.num_programs(1) + pl.program_id(1)
    last_step = pl.num_programs(0) * pl.num_programs(1) - 1
    xs = (xs0_ref, xs1_ref)
    dv = (dv0_ref, dv1_ref)
    ds = (ds0_ref, ds1_ref)
    spare_row0 = xb_ref.shape[0] - 2 * tm

    def meta_copy(slot):
        return pltpu.make_async_copy(dv[slot], ds[slot], ssem.at[slot])

    def issue_rows(slot):
        for r in range(tm):
            for k in range(2):
                pltpu.make_async_copy(xs[slot].at[pl.ds(r, 1)],
                                      xb_ref.at[pl.ds(ds[slot][k, r], 1)], rsem.at[slot]
                                      ).start(priority=k)

    def wait_rows(slot):
        for _ in range(2):
            pltpu.make_async_copy(xs[slot], xb_ref.at[pl.ds(0, tm)], rsem.at[slot]).wait()

    def compute(slot):
        rows = slice(slot * tm, (slot + 1) * tm)
        attn = (_dot_tn(ot_ref[0, :, rows], wo_ref[0:MLA_WIDTH, :])
                + _dot(oc_ref[0, rows, :], wo_ref[MLA_WIDTH:MLA_WIDTH + CONV_WIDTH, :]))
        h1 = x_ref[0, rows, :] + attn
        h1_ref[0, rows, :] = h1
        xn = _rms(h1, g_ref[...])
        xs[slot][...] = _pack_bf16_pairs(xn)
        oh, oh1, oh2, r1, r2, w1, w2 = _route_tile(xn, wrhl_ref, br_ref, carry_ref)

        inv_blk = 1.0 / MOE_BLK
        cnt = carry_ref[:, 0:1]
        tile_cnt = jnp.sum(oh, axis=1, keepdims=True)
        nb_before = jnp.floor((cnt + (MOE_BLK - 1)) * inv_blk)
        nb_after = jnp.floor((cnt + tile_cnt + (MOE_BLK - 1)) * inv_blk)
        new = nb_after - nb_before
        erow = lax.broadcasted_iota(jnp.int32, (N_EXPERTS, N_EXPERTS), 0)
        ecol = lax.broadcasted_iota(jnp.int32, (N_EXPERTS, N_EXPERTS), 1)
        lower = jnp.where(ecol < erow, 1.0, 0.0).astype(BF16)
        new_rep = jnp.broadcast_to(new, (N_EXPERTS, HEAD_PAD))
        new_id = free_ref[0:1, :] + _dot(lower, new_rep.astype(BF16))
        free_ref[...] = free_ref[...] + jnp.sum(new, axis=0, keepdims=True)
        last_before = last_ref[...]
        is_new = new_rep > 0.0
        last_ref[...] = jnp.where(is_new, new_id, last_before)
        nb_ref[slot] = jnp.where(is_new, new_id, -1.0).astype(jnp.int32)
        carry_ref[...] = carry_ref[...] + tile_cnt

        def place(ohk, rk):
            bi = jnp.floor(rk * inv_blk)
            blk = jnp.where(bi == nb_before - 1.0, last_before[:, 0:1], new_id[:, 0:1])
            phys = jnp.sum(jnp.where(ohk, blk, 0.0), axis=0, keepdims=True)
            return (phys * MOE_BLK + (rk - bi * MOE_BLK)).astype(jnp.int32)

        mrow = lax.broadcasted_iota(jnp.int32, (8, tm), 0)
        dest = jnp.where(mrow == 0, place(oh1, r1), jnp.where(mrow == 1, place(oh2, r2), 0))
        dv[slot][...] = dest
        pos_ref[0, :, rows] = dest
        mw_ref[0, :, rows] = jnp.where(mrow == 0, w1, jnp.where(mrow == 1, w2, 0.0))
        meta_copy(slot).start()

    @pl.when(step == 0)
    def _():
        carry_ref[...] = jnp.zeros_like(carry_ref)
        free_ref[...] = jnp.zeros_like(free_ref)
        last_ref[...] = jnp.full_like(last_ref, -1.0)
        xs1_ref[...] = jnp.zeros_like(xs1_ref)
        mrow = lax.broadcasted_iota(jnp.int32, (8, tm), 0)
        lane = lax.broadcasted_iota(jnp.int32, (8, tm), 1)
        dv1_ref[...] = jnp.where(mrow < 2, spare_row0 + 2 * lane + mrow, 0)
        meta_copy(1).start()

    meta_copy(1).wait()
    issue_rows(1)
    compute(0)
    wait_rows(1)
    meta_copy(0).wait()
    issue_rows(0)
    compute(1)
    wait_rows(0)

    @pl.when(step == last_step)
    def _():
        meta_copy(1).wait()
        issue_rows(1)
        wait_rows(1)


def _outproj(x, ot, oc, wo, g, wrhl, br, xbz):
    B, S, D = x.shape
    tm2 = 2 * TM
    n_s = S // tm2
    full = lambda a: pl.BlockSpec(a.shape, lambda b, i: (0,) * a.ndim)
    tile = lambda w: pl.BlockSpec((1, tm2, w), lambda b, i: (b, i, 0))
    meta = pl.BlockSpec((1, 8, tm2), lambda b, i: (b, 0, i))
    any_space = pl.BlockSpec(memory_space=pl.ANY)
    return pl.pallas_call(
        _outproj_kernel,
        grid=(B, n_s),
        in_specs=[
            tile(D),
            pl.BlockSpec((1, MLA_WIDTH, tm2), lambda b, i: (b, 0, i)),
            tile(CONV_WIDTH),
            full(wo), full(g), full(wrhl), full(br), any_space,
        ],
        out_specs=[tile(D), meta, meta,
                   pl.BlockSpec((2, N_EXPERTS, HEAD_PAD), lambda b, i: (b * n_s + i, 0, 0)),
                   any_space],
        out_shape=[
            jax.ShapeDtypeStruct((B, S, D), F32),
            jax.ShapeDtypeStruct((B, 8, S), jnp.int32),
            jax.ShapeDtypeStruct((B, 8, S), F32),
            jax.ShapeDtypeStruct((B * S // TM, N_EXPERTS, HEAD_PAD), jnp.int32),
            jax.ShapeDtypeStruct(xbz.shape, xbz.dtype),
        ],
        input_output_aliases={7: 4},
        scratch_shapes=[
            pltpu.VMEM((N_EXPERTS, HEAD_PAD), F32), pltpu.VMEM((N_EXPERTS, HEAD_PAD), F32),
            pltpu.VMEM((8, HEAD_PAD), F32),
            pltpu.VMEM((TM, D // 2), jnp.uint32), pltpu.VMEM((TM, D // 2), jnp.uint32),
            pltpu.VMEM((8, TM), jnp.int32), pltpu.VMEM((8, TM), jnp.int32),
            pltpu.SMEM((8, TM), jnp.int32), pltpu.SMEM((8, TM), jnp.int32),
            pltpu.SemaphoreType.DMA((2,)), pltpu.SemaphoreType.DMA((2,)),
        ],
        compiler_params=pltpu.CompilerParams(
            dimension_semantics=("arbitrary", "arbitrary"), vmem_limit_bytes=VMEM_LIMIT),
        name="outproj",
    )(x, ot, oc, wo, g, wrhl, br, xbz)


def _expert_kernel(order_ref, blke_ref, nused_ref, xb_ref, wg_ref, wu_ref, wd_ref, yb_ref,
                   wgs_ref, wus_ref, wds_ref):
    del order_ref
    i = pl.program_id(0)
    used = i < nused_ref[0]
    changed = (i == 0) | (blke_ref[i] != blke_ref[jnp.maximum(i - 1, 0)])

    @pl.when(used & changed)
    def _():
        wgs_ref[...] = wg_ref[0].astype(BF16)
        wus_ref[...] = wu_ref[0].astype(BF16)
        wds_ref[...] = wd_ref[0].astype(BF16)

    @pl.when(used)
    def _():
        x_lo, x_hi = _unpack_bf16_pairs(xb_ref[...])
        dh = x_lo.shape[1]
        half = EXPERT_FF // 2

        def xdot(w_ref, cols):
            return _dot(x_lo, w_ref[0:dh, cols]) + _dot(x_hi, w_ref[dh:2 * dh, cols])

        g0 = xdot(wgs_ref, slice(0, half))
        u0 = xdot(wus_ref, slice(0, half))
        g1 = xdot(wgs_ref, slice(half, EXPERT_FF))
        u1 = xdot(wus_ref, slice(half, EXPERT_FF))
        h0 = ((g0 * jax.nn.sigmoid(g0)) * u0).astype(BF16)
        y = _dot(h0, wds_ref[:half, :])
        h1 = ((g1 * jax.nn.sigmoid(g1)) * u1).astype(BF16)
        yb_ref[...] = _pack_bf16_pairs(y + _dot(h1, wds_ref[half:, :]))

    @pl.when(jnp.logical_not(used))
    def _():
        yb_ref[...] = jnp.zeros_like(yb_ref)


def _experts(order, blk_e, n_used, xb, wg, wu, wd):
    D = 2 * xb.shape[1]
    blk = MOE_BLK
    n_blocks = order.shape[0]
    return pl.pallas_call(
        _expert_kernel,
        grid_spec=pltpu.PrefetchScalarGridSpec(
            num_scalar_prefetch=3,
            grid=(n_blocks,),
            in_specs=[
                pl.BlockSpec((blk, D // 2), lambda i, o, be, nu: (o[jnp.minimum(i, nu[0] - 1)], 0)),
                pl.BlockSpec((1, D, EXPERT_FF), lambda i, o, be, nu: (be[i], 0, 0)),
                pl.BlockSpec((1, D, EXPERT_FF), lambda i, o, be, nu: (be[i], 0, 0)),
                pl.BlockSpec((1, EXPERT_FF, D), lambda i, o, be, nu: (be[i], 0, 0)),
            ],
            out_specs=pl.BlockSpec((blk, D // 2), lambda i, o, be, nu: (o[i], 0)),
            scratch_shapes=[pltpu.VMEM((D, EXPERT_FF), BF16), pltpu.VMEM((D, EXPERT_FF), BF16),
                            pltpu.VMEM((EXPERT_FF, D), BF16)],
        ),
        out_shape=jax.ShapeDtypeStruct((n_blocks * blk, D // 2), jnp.uint32),
        compiler_params=pltpu.CompilerParams(
            dimension_semantics=("arbitrary",), vmem_limit_bytes=VMEM_LIMIT),
        name="experts",
    )(order, blk_e, n_used, xb, wg, wu, wd)


def _final_kernel(dest_ref, h1_ref, p_ref, wt_ref, yb_ref, gp_ref, wg_ref, bg_ref, wp_ref,
                  gf_ref, out_ref, buf0_ref, buf1_ref, buf2_ref, buf3_ref, sem):
    tm = h1_ref.shape[0] // FIN_TILES
    i = pl.program_id(0)
    bufs = (buf0_ref, buf1_ref, buf2_ref, buf3_ref)

    def row_copy(slot, k, r, d):
        return pltpu.make_async_copy(yb_ref.at[pl.ds(d, 1)], bufs[slot].at[k, pl.ds(r, 1)],
                                     sem.at[slot])

    def issue(tile, slot):
        for r in range(tm):
            t = (tile * tm + r) * 2
            row_copy(slot, 0, r, dest_ref[t]).start(priority=0)
            row_copy(slot, 1, r, dest_ref[t + 1]).start(priority=1)

    def wait(slot):
        for k in range(2):
            pltpu.make_async_copy(yb_ref.at[pl.ds(0, tm)], bufs[slot].at[k], sem.at[slot]).wait()

    def compute(slot):
        rows = slice(slot * tm, (slot + 1) * tm)
        pe = _dot(p_ref[rows, :].astype(BF16), wp_ref[...])
        wt = wt_ref[rows, :]
        h2 = (h1_ref[rows, :] + wt[:, 0:1] * _unpack_f32_pairs(bufs[slot][0])
              + wt[:, 1:2] * _unpack_f32_pairs(bufs[slot][1]))
        n = _rms(h2, gp_ref[...]).astype(BF16)
        gate = jax.nn.sigmoid(_dot(n, wg_ref[...]) + bg_ref[...])
        h3 = h2 + gate * pe
        out_ref[rows, :] = _rms(h3, gf_ref[...])

    @pl.when(i == 0)
    def _():
        for j in range(FIN_AHEAD):
            issue(j, j)

    last_tile = FIN_TILES * pl.num_programs(0) - 1
    for j in range(FIN_TILES):
        issue(jnp.minimum(FIN_TILES * i + j + FIN_AHEAD, last_tile), (j + FIN_AHEAD) % FIN_TILES)
        wait(j)
        compute(j)

    @pl.when(i == pl.num_programs(0) - 1)
    def _():
        for j in range(FIN_AHEAD):
            wait(j)


def _final(dest, h1, p, wt, yb, gp, wg, bg, wp, gf):
    N, D = h1.shape
    tm = TM
    full = lambda a: pl.BlockSpec(a.shape, lambda i, dest: (0,) * a.ndim)
    return pl.pallas_call(
        _final_kernel,
        grid_spec=pltpu.PrefetchScalarGridSpec(
            num_scalar_prefetch=1,
            grid=(N // (FIN_TILES * tm),),
            in_specs=[
                pl.BlockSpec((FIN_TILES * tm, D), lambda i, dest: (i, 0)),
                pl.BlockSpec((FIN_TILES * tm, PLE_DIM), lambda i, dest: (i, 0)),
                pl.BlockSpec((FIN_TILES * tm, 2), lambda i, dest: (i, 0)),
                pl.BlockSpec(memory_space=pl.ANY),
                full(gp), full(wg), full(bg), full(wp), full(gf),
            ],
            out_specs=pl.BlockSpec((FIN_TILES * tm, D), lambda i, dest: (i, 0)),
            scratch_shapes=[pltpu.VMEM((2, tm, D // 2), jnp.uint32) for _ in range(FIN_TILES)]
            + [pltpu.SemaphoreType.DMA((FIN_TILES,))],
        ),
        out_shape=jax.ShapeDtypeStruct((N, D), F32),
        compiler_params=pltpu.CompilerParams(
            dimension_semantics=("arbitrary",), vmem_limit_bytes=VMEM_LIMIT),
        name="final",
    )(dest, h1, p, wt, yb, gp, wg, bg, wp, gf)


def _prep_weights(w_in, w_uq, w_ukv):
    cq, ckv, kr, bg, cg, u = jnp.split(
        w_in, [Q_LORA, Q_LORA + KV_LORA, Q_LORA + KV_LORA + QK_ROPE_DIM,
               Q_LORA + KV_LORA + QK_ROPE_DIM + CONV_WIDTH,
               Q_LORA + KV_LORA + QK_ROPE_DIM + 2 * CONV_WIDTH], axis=1)
    kr_rot = jnp.concatenate([-kr[:, HALF_ROPE:], kr[:, :HALF_ROPE]], axis=1)
    pad = jnp.zeros((D_MODEL, HEAD_PAD - 2 * QK_ROPE_DIM), w_in.dtype)
    win = jnp.concatenate([cq, ckv, bg, cg, u, kr, kr_rot, pad], axis=1).astype(BF16)

    zq = jnp.zeros((Q_LORA, N_HEADS, HEAD_PAD - QK_DIM), w_uq.dtype)
    wqa = jnp.concatenate([w_uq, zq], axis=2)
    rope = w_uq[:, :, QK_NOPE_DIM:]
    wqb = jnp.concatenate([-rope[:, :, HALF_ROPE:], rope[:, :, :HALF_ROPE]], axis=2)
    wqa = wqa.reshape(Q_LORA, N_HEADS * HEAD_PAD).T.astype(BF16)
    wqb = wqb.reshape(Q_LORA, N_HEADS * QK_ROPE_DIM).T.astype(BF16)

    zk = jnp.zeros((KV_LORA, N_HEADS, HEAD_PAD - QK_NOPE_DIM), w_ukv.dtype)
    wuk = jnp.concatenate([w_ukv[:, :, :QK_NOPE_DIM], zk], axis=2)
    wuk = wuk.reshape(KV_LORA, N_HEADS * HEAD_PAD).astype(BF16)
    wuvt = w_ukv[:, :, QK_NOPE_DIM:].reshape(KV_LORA, MLA_WIDTH).T.astype(BF16)

    src = jnp.arange(QK_ROPE_DIM)[:, None]
    dst = jnp.arange(N_HEADS * HEAD_PAD)[None, :]
    pk = (dst % HEAD_PAD - QK_NOPE_DIM == src).astype(BF16)
    return win, wqa, wqb, wuk, wuvt, pk


def kernel(x, p, positions, attn_norm_g, w_in, q_norm_g, w_uq, kv_norm_g, w_ukv, conv_w, w_out,
           moe_norm_g, w_group_router, b_group_router, w_expert_router, b_expert_router,
           w_gate, w_up, w_down, ple_norm_g, w_ple_gate, b_ple_gate, w_ple_proj, final_norm_g):
    B, S, D = x.shape
    N = B * S
    assert w_in.shape[0] == 1, "single-layer trunk: the final norm is fused into the layer"
    pos = positions.astype(F32).reshape(B, 1, S)
    invf = (ROPE_BASE ** (-jnp.arange(0, QK_ROPE_DIM, 2, dtype=F32) / QK_ROPE_DIM)).reshape(-1, 1)
    row = lambda v: v.reshape(1, -1)
    h = x
    for i in range(1):
        win, wqa, wqb, wuk, wuvt, pk = _prep_weights(w_in[i], w_uq[i], w_ukv[i])
        qt, k, vt, oc = _inproj(h, pos, invf, row(attn_norm_g[i]), win, row(q_norm_g[i]), wqa,
                                wqb, row(kv_norm_g[i]), wuk, pk, wuvt,
                                conv_w[i].reshape(CONV_K, CONV_WIDTH))
        n_blocks = (2 * N) // MOE_BLK + N_EXPERTS
        ot, xbz = _attention(qt, k, vt, n_blocks * MOE_BLK + 2 * TM)

        zrow = lambda n: jnp.zeros((n, D), F32)
        wr = jnp.concatenate(
            [w_group_router[i].T, zrow(ROUTER_E0 - N_GROUPS), w_expert_router[i].T,
             zrow(ROUTER_ROWS - ROUTER_E0 - N_EXPERTS)], axis=0)
        wrh = wr.astype(BF16)
        wrhl = jnp.concatenate([wrh, (wr - wrh.astype(F32)).astype(BF16)], axis=0)
        br = jnp.concatenate(
            [b_group_router[i], jnp.zeros((ROUTER_E0 - N_GROUPS,), F32), b_expert_router[i],
             jnp.zeros((ROUTER_ROWS - ROUTER_E0 - N_EXPERTS,), F32)]).reshape(-1, 1)
        h1, pos, meta_w, opened, xb = _outproj(h, ot, oc, w_out[i].astype(BF16),
                                               row(moe_norm_g[i]), wrhl, br, xbz)

        opened = opened[:, :, 0]
        blk_ids = jnp.arange(n_blocks, dtype=jnp.int32)
        hit = opened[None, :, :] == blk_ids[:, None, None]
        used = jnp.any(hit, axis=(1, 2))
        e_of_blk = jnp.sum(jnp.where(hit, jnp.arange(N_EXPERTS, dtype=jnp.int32), 0), axis=(1, 2))
        n_used = jnp.sum(used.astype(jnp.int32)).reshape(1)
        key = jnp.where(used, e_of_blk, N_EXPERTS) * n_blocks + blk_ids
        slot_of_blk = jnp.sum((key[None, :] < key[:, None]).astype(jnp.int32), axis=1)
        at_slot = slot_of_blk[None, :] == blk_ids[:, None]
        order = jnp.sum(jnp.where(at_slot, blk_ids[None, :], 0), axis=1)
        blk_e = jnp.sum(jnp.where(at_slot, e_of_blk[None, :], 0), axis=1)
        blk_e = jnp.where(blk_ids < n_used, blk_e, jnp.max(jnp.where(used, e_of_blk, 0)))
        dest = jnp.swapaxes(pos[:, 0:2, :], 1, 2).reshape(2 * N)
        wt = jnp.swapaxes(meta_w[:, 0:2, :], 1, 2).reshape(N, 2)

        yb = _experts(order, blk_e, n_used, xb, w_gate[i], w_up[i], w_down[i])
        out = _final(dest, h1.reshape(N, D), p[i].reshape(N, PLE_DIM), wt, yb,
                     row(ple_norm_g[i]), w_ple_gate[i].astype(BF16), row(b_ple_gate[i]),
                     w_ple_proj[i].astype(BF16), row(final_norm_g))
        h = out.reshape(B, S, D)
    return h
```

```python
import functools
import math

import jax
import jax.numpy as jnp
from jax import lax
from jax.experimental import pallas as pl
from jax.experimental.pallas import tpu as pltpu

D_MODEL = 1024
PLE_DIM = 256
MLA_WIDTH = 512
CONV_WIDTH = 512
N_HEADS = 8
V_HEAD_DIM = 64
QK_NOPE_DIM = 64
QK_ROPE_DIM = 32
Q_LORA = 384
KV_LORA = 256
CONV_K = 3
N_GROUPS = 4
EXPERTS_PER_GROUP = 8
N_EXPERTS = N_GROUPS * EXPERTS_PER_GROUP
EXPERT_FF = 512
ROPE_BASE = 10000.0
EPS = 1e-6

HEAD_PAD = 128
V_ROWS = 80
QK_DIM = QK_NOPE_DIM + QK_ROPE_DIM
HALF_ROPE = QK_ROPE_DIM // 2

_C_Q = 0
_C_KV = _C_Q + Q_LORA
_C_BG = _C_KV + KV_LORA
_C_CG = _C_BG + CONV_WIDTH
_C_U = _C_CG + CONV_WIDTH
_C_KR = _C_U + CONV_WIDTH
IN_COLS_PAD = _C_KR + HEAD_PAD

TM = 256
TM_IN = 512
FIN_TILES = 4
FIN_AHEAD = 2
TQ = 512
TK = 256
ATT_HG = 4
ATT_TQC = 256
MOE_BLK = 512
ROUTER_E0 = 8
ROUTER_ROWS = 48
NEG = -1e30
Q_SCALE = (QK_DIM ** -0.5) * math.log2(math.e)
VMEM_LIMIT = 48 * 1024 * 1024

F32 = jnp.float32
BF16 = jnp.bfloat16


def _rms(x, g):
    return x * lax.rsqrt(jnp.mean(x * x, axis=-1, keepdims=True) + EPS) * g


def _dot(a, b):
    return jnp.dot(a, b, preferred_element_type=F32)


def _dot_nt(a, b):
    return lax.dot_general(a, b, (((1,), (1,)), ((), ())), preferred_element_type=F32)


def _pack_bf16_pairs(x):
    w = x.shape[1] // 2
    lo = pltpu.bitcast(x[:, :w].astype(BF16).astype(F32), jnp.uint32)
    hi = pltpu.bitcast(x[:, w:].astype(BF16).astype(F32), jnp.uint32)
    return (lo >> 16) | hi


def _unpack_bf16_pairs(p):
    lo = pltpu.bitcast(p << 16, F32).astype(BF16)
    hi = pltpu.bitcast(p & jnp.uint32(0xFFFF0000), F32).astype(BF16)
    return lo, hi


def _unpack_f32_pairs(p):
    lo = pltpu.bitcast(p << 16, F32)
    hi = pltpu.bitcast(p & jnp.uint32(0xFFFF0000), F32)
    return jnp.concatenate([lo, hi], axis=1)


def _dot_tn(a, b):
    return lax.dot_general(a, b, (((0,), (0,)), ((), ())), preferred_element_type=F32)


def _inproj_kernel(x_ref, pos_ref, invf_ref, g_ref, win_ref, qg_ref, wqa_ref, wqb_ref, kvg_ref,
                   wuk_ref, pk_ref, wuvt_ref, convw_ref,
                   qt_ref, k_ref, vt_ref, oc_ref, carry_ref, ext_ref):
    tm = x_ref.shape[1]
    x = x_ref[0]
    xn = _rms(x, g_ref[...])
    z = _dot(xn.astype(BF16), win_ref[...])

    ang = invf_ref[...] * pos_ref[0]
    cos = jnp.cos(ang)
    sin = jnp.sin(ang)
    cos2 = jnp.concatenate([cos, cos], axis=0)
    sin2 = jnp.concatenate([sin, sin], axis=0)

    cqn = _rms(z[:, _C_Q:_C_Q + Q_LORA], qg_ref[...]).astype(BF16)
    qa = _dot_nt(wqa_ref[...], cqn)
    qb = _dot_nt(wqb_ref[...], cqn)
    for h in range(N_HEADS):
        r0 = h * HEAD_PAD
        nope = qa[r0:r0 + QK_NOPE_DIM]
        rope = (qa[r0 + QK_NOPE_DIM:r0 + QK_DIM] * cos2
                + qb[h * QK_ROPE_DIM:(h + 1) * QK_ROPE_DIM] * sin2)
        qh = jnp.concatenate([nope, rope, qa[r0 + QK_DIM:r0 + HEAD_PAD]], axis=0) * Q_SCALE
        qt_ref[0, r0:r0 + HEAD_PAD, :] = qh.astype(BF16)

    kvn = _rms(z[:, _C_KV:_C_KV + KV_LORA], kvg_ref[...]).astype(BF16)
    krt = z[:, _C_KR:_C_KR + HEAD_PAD].T
    krot = krt[0:QK_ROPE_DIM] * cos2 + krt[QK_ROPE_DIM:2 * QK_ROPE_DIM] * sin2
    k_ref[0] = (_dot(kvn, wuk_ref[...]) + _dot_tn(krot.astype(BF16), pk_ref[...])).astype(BF16)
    vt = _dot_nt(wuvt_ref[...], kvn).astype(BF16)
    ones_row = jnp.where(lax.broadcasted_iota(jnp.int32, (V_ROWS - V_HEAD_DIM, tm), 0) == 0,
                         1.0, 0.0).astype(BF16)
    for h in range(N_HEADS):
        vt_ref[0, h * V_ROWS:h * V_ROWS + V_HEAD_DIM, :] = vt[h * V_HEAD_DIM:(h + 1) * V_HEAD_DIM]
        vt_ref[0, h * V_ROWS + V_HEAD_DIM:(h + 1) * V_ROWS, :] = ones_row

    @pl.when(pl.program_id(1) == 0)
    def _():
        carry_ref[...] = jnp.zeros_like(carry_ref)

    cu = z[:, _C_CG:_C_CG + CONV_WIDTH] * z[:, _C_U:_C_U + CONV_WIDTH]
    ext_ref[0:8, :] = carry_ref[...]
    ext_ref[8:8 + tm, :] = cu
    cu1 = ext_ref[7:7 + tm, :]
    cu2 = ext_ref[6:6 + tm, :]
    w = convw_ref[...]
    y = w[0:1] * cu2 + w[1:2] * cu1 + w[2:3] * cu
    oc_ref[0] = (z[:, _C_BG:_C_BG + CONV_WIDTH] * y).astype(BF16)
    carry_ref[...] = ext_ref[tm:tm + 8, :]


def _inproj(x, pos, invf, g, win, qg, wqa, wqb, kvg, wuk, pk, wuvt, convw):
    B, S, D = x.shape
    tm = TM_IN
    full = lambda a: pl.BlockSpec(a.shape, lambda b, i: (0,) * a.ndim)
    return pl.pallas_call(
        _inproj_kernel,
        grid=(B, S // tm),
        in_specs=[
            pl.BlockSpec((1, tm, D), lambda b, i: (b, i, 0)),
            pl.BlockSpec((1, 1, tm), lambda b, i: (b, 0, i)),
            full(invf), full(g), full(win), full(qg), full(wqa), full(wqb), full(kvg), full(wuk),
            full(pk), full(wuvt), full(convw),
        ],
        out_specs=[
            pl.BlockSpec((1, N_HEADS * HEAD_PAD, tm), lambda b, i: (b, 0, i)),
            pl.BlockSpec((1, tm, N_HEADS * HEAD_PAD), lambda b, i: (b, i, 0)),
            pl.BlockSpec((1, N_HEADS * V_ROWS, tm), lambda b, i: (b, 0, i)),
            pl.BlockSpec((1, tm, CONV_WIDTH), lambda b, i: (b, i, 0)),
        ],
        out_shape=[
            jax.ShapeDtypeStruct((B, N_HEADS * HEAD_PAD, S), BF16),
            jax.ShapeDtypeStruct((B, S, N_HEADS * HEAD_PAD), BF16),
            jax.ShapeDtypeStruct((B, N_HEADS * V_ROWS, S), BF16),
            jax.ShapeDtypeStruct((B, S, CONV_WIDTH), BF16),
        ],
        scratch_shapes=[pltpu.VMEM((8, CONV_WIDTH), F32), pltpu.VMEM((tm + 8, CONV_WIDTH), F32)],
        compiler_params=pltpu.CompilerParams(
            dimension_semantics=("arbitrary", "arbitrary"), vmem_limit_bytes=VMEM_LIMIT),
        name="inproj",
    )(x, pos, invf, g, win, qg, wqa, wqb, kvg, wuk, pk, wuvt, convw)


def _attn_kernel(qt_ref, k_ref, vt_ref, o_ref, z_ref, m_ref, acc_ref, sa_ref, sb_ref, zero_ref,
                 zsem):
    tq = qt_ref.shape[2]
    tk = TK
    assert tq == 2 * tk and ATT_TQC == tk
    i = pl.program_id(2)
    m_ref[...] = jnp.full_like(m_ref, NEG)
    acc_ref[...] = jnp.zeros_like(acc_ref)

    step = ((pl.program_id(0) * pl.num_programs(1) + pl.program_id(1)) * pl.num_programs(2) + i)
    zrows = zero_ref.shape[0]

    @pl.when(step == 0)
    def _():
        zero_ref[...] = jnp.zeros_like(zero_ref)

    zero_copy = pltpu.make_async_copy(
        zero_ref, z_ref.at[pl.ds(pl.multiple_of(step * zrows, 8), zrows)], zsem)
    zero_copy.start()

    def scores(j, s_ref, g, col0=0):
        k0 = pl.multiple_of(j * tk, tk)
        qt = qt_ref[0, g * HEAD_PAD:(g + 1) * HEAD_PAD, col0:]
        s_ref[g, :, col0:] = _dot(k_ref[0, pl.ds(k0, tk), g * HEAD_PAD:(g + 1) * HEAD_PAD], qt)

    def softmax_pv(j, s_ref, g, diag):
        k0 = pl.multiple_of(j * tk, tk)
        vt = vt_ref[0, g * V_ROWS:(g + 1) * V_ROWS, pl.ds(k0, tk)]
        for c in range(tq // ATT_TQC):
            if diag is not None and c < diag:
                continue
            cols = slice(c * ATT_TQC, (c + 1) * ATT_TQC)
            s = s_ref[g, :, cols]
            if diag is not None and c == diag:
                krow = lax.broadcasted_iota(jnp.int32, (tk, ATT_TQC), 0)
                qcol = lax.broadcasted_iota(jnp.int32, (tk, ATT_TQC), 1)
                s = jnp.where(krow <= qcol, s, NEG)
            m_old = m_ref[g, :, cols]
            m_new = jnp.maximum(m_old, jnp.max(s, axis=0, keepdims=True))
            alpha = jnp.exp2(m_old - m_new)
            p = jnp.exp2((s - m_new).astype(BF16))
            acc_ref[g, :, cols] = alpha * acc_ref[g, :, cols] + _dot(vt, p)
            m_ref[g, :, cols] = m_new

    def stage(j_next, s_next, j, s_cur, diag=None, next_col0=0):
        for g in range(ATT_HG):
            if j_next is not None:
                scores(j_next, s_next, g, next_col0)
            softmax_pv(j, s_cur, g, diag)

    for g in range(ATT_HG):
        scores(0, sa_ref, g)

    def pair(j):
        stage(j + 1, sb_ref, j, sa_ref)
        stage(j + 2, sa_ref, j + 1, sb_ref)

    def body(t, c):
        pair(4 * t)
        pair(4 * t + 2)
        return c

    lax.fori_loop(0, i // 2, body, 0)

    @pl.when(i % 2 == 1)
    def _():
        pair(2 * i - 2)

    n_full = 2 * i
    stage(n_full + 1, sb_ref, n_full, sa_ref, diag=0, next_col0=ATT_TQC)
    stage(None, None, n_full + 1, sb_ref, diag=1)
    for g in range(ATT_HG):
        o_ref[0, g * V_HEAD_DIM:(g + 1) * V_HEAD_DIM, :] = (
            acc_ref[g, 0:V_HEAD_DIM, :] / acc_ref[g, V_HEAD_DIM:V_HEAD_DIM + 1, :]).astype(o_ref.dtype)
    zero_copy.wait()


def _attention(qt, k, vt, min_zero_rows):
    B, _, S = qt.shape
    hg = ATT_HG
    grid = (B, N_HEADS // hg, S // TQ)
    n_steps = grid[0] * grid[1] * grid[2]
    zrows = -(-min_zero_rows // (8 * n_steps)) * 8
    return pl.pallas_call(
        _attn_kernel,
        grid=grid,
        in_specs=[
            pl.BlockSpec((1, hg * HEAD_PAD, TQ), lambda b, h, i: (b, h, i)),
            pl.BlockSpec((1, S, hg * HEAD_PAD), lambda b, h, i: (b, 0, h)),
            pl.BlockSpec((1, hg * V_ROWS, S), lambda b, h, i: (b, h, 0)),
        ],
        out_specs=[pl.BlockSpec((1, hg * V_HEAD_DIM, TQ), lambda b, h, i: (b, h, i)),
                   pl.BlockSpec(memory_space=pl.ANY)],
        out_shape=[jax.ShapeDtypeStruct((B, MLA_WIDTH, S), BF16),
                   jax.ShapeDtypeStruct((n_steps * zrows, D_MODEL // 2), jnp.uint32)],
        scratch_shapes=[pltpu.VMEM((hg, 1, TQ), F32), pltpu.VMEM((hg, V_ROWS, TQ), F32),
                        pltpu.VMEM((hg, TK, TQ), F32), pltpu.VMEM((hg, TK, TQ), F32),
                        pltpu.VMEM((zrows, D_MODEL // 2), jnp.uint32), pltpu.SemaphoreType.DMA],
        compiler_params=pltpu.CompilerParams(
            dimension_semantics=("arbitrary", "arbitrary", "arbitrary"),
            vmem_limit_bytes=VMEM_LIMIT),
        name="attn",
    )(qt, k, vt)


def _route_tile(xn, wrhl_ref, br_ref, carry_ref):
    tm = xn.shape[0]
    xh = xn.astype(BF16)
    xl = (xn - xh.astype(F32)).astype(BF16)
    hl = _dot_nt(wrhl_ref[...], xh)
    logits = (hl[0:ROUTER_ROWS] + hl[ROUTER_ROWS:2 * ROUTER_ROWS]
              + _dot_nt(wrhl_ref[0:ROUTER_ROWS, :], xl) + br_ref[...])
    big = jnp.int32(1 << 20)

    grow = lax.broadcasted_iota(jnp.int32, (ROUTER_E0, tm), 0)
    glog = jnp.where(grow < N_GROUPS, logits[0:ROUTER_E0], NEG)
    gmax = jnp.max(glog, axis=0, keepdims=True)
    gsum = jnp.sum(jnp.exp(glog - gmax), axis=0, keepdims=True)
    g_p = 1.0 / gsum
    g_idx = jnp.min(jnp.where(glog == gmax, grow, big), axis=0, keepdims=True)

    erow = lax.broadcasted_iota(jnp.int32, (N_EXPERTS, tm), 0)
    e_lo = g_idx * EXPERTS_PER_GROUP
    in_group = (erow >= e_lo) & (erow < e_lo + EXPERTS_PER_GROUP)
    elog = jnp.where(in_group, logits[ROUTER_E0:ROUTER_E0 + N_EXPERTS], NEG)
    emax = jnp.max(elog, axis=0, keepdims=True)
    esum = jnp.sum(jnp.exp(elog - emax), axis=0, keepdims=True)
    e1 = jnp.min(jnp.where(elog == emax, erow, big), axis=0, keepdims=True)
    elog2 = jnp.where(erow == e1, NEG, elog)
    emax2 = jnp.max(elog2, axis=0, keepdims=True)
    e2 = jnp.min(jnp.where(elog2 == emax2, erow, big), axis=0, keepdims=True)
    p1 = 1.0 / esum
    p2 = jnp.exp(emax2 - emax) / esum
    psum = p1 + p2
    w1 = g_p * (p1 / psum)
    w2 = g_p * (p2 / psum)

    oh1 = erow == e1
    oh2 = erow == e2
    oh = jnp.where(oh1 | oh2, 1.0, 0.0)
    srow = lax.broadcasted_iota(jnp.int32, (tm, tm), 0)
    scol = lax.broadcasted_iota(jnp.int32, (tm, tm), 1)
    earlier = jnp.where(srow < scol, 1.0, 0.0).astype(BF16)
    cum = _dot(oh.astype(BF16), earlier) + carry_ref[:, 0:1]
    r1 = jnp.sum(jnp.where(oh1, cum, 0.0), axis=0, keepdims=True)
    r2 = jnp.sum(jnp.where(oh2, cum, 0.0), axis=0, keepdims=True)
    return oh, oh1, oh2, r1, r2, w1, w2


def _outproj_kernel(x_ref, ot_ref, oc_ref, wo_ref, g_ref, wrhl_ref, br_ref, xbz_ref,
                    h1_ref, pos_ref, mw_ref, nb_ref, xb_ref,
                    carry_ref, last_ref, free_ref, xs0_ref, xs1_ref, dv0_ref, dv1_ref,
                    ds0_ref, ds1_ref, rsem, ssem):
    del xbz_ref
    tm = TM
    step = pl.program_id(0) * pl.num_programs(1) + pl.program_id(1)
    last_step = pl.num_programs(0) * pl.num_programs(1) - 1
    xs = (xs0_ref, xs1_ref)
    dv = (dv0_ref, dv1_ref)
    ds = (ds0_ref, ds1_ref)
    spare_row0 = xb_ref.shape[0] - 2 * tm

    def meta_copy(slot):
        return pltpu.make_async_copy(dv[slot], ds[slot], ssem.at[slot])

    def issue_rows(slot):
        for r in range(tm):
            for k in range(2):
                pltpu.make_async_copy(xs[slot].at[pl.ds(r, 1)],
                                      xb_ref.at[pl.ds(ds[slot][k, r], 1)], rsem.at[slot]
                                      ).start(priority=k)

    def wait_rows(slot):
        for _ in range(2):
            pltpu.make_async_copy(xs[slot], xb_ref.at[pl.ds(0, tm)], rsem.at[slot]).wait()

    def project(slot):
        rows = slice(slot * tm, (slot + 1) * tm)
        attn = (_dot_tn(ot_ref[0, :, rows], wo_ref[0:MLA_WIDTH, :])
                + _dot(oc_ref[0, rows, :], wo_ref[MLA_WIDTH:MLA_WIDTH + CONV_WIDTH, :]))
        h1 = x_ref[0, rows, :] + attn
        h1_ref[0, rows, :] = h1
        xn = _rms(h1, g_ref[...])
        xs[slot][...] = _pack_bf16_pairs(xn)
        return xn

    def route(slot, xn):
        rows = slice(slot * tm, (slot + 1) * tm)
        oh, oh1, oh2, r1, r2, w1, w2 = _route_tile(xn, wrhl_ref, br_ref, carry_ref)

        inv_blk = 1.0 / MOE_BLK
        cnt = carry_ref[:, 0:1]
        tile_cnt = jnp.sum(oh, axis=1, keepdims=True)
        nb_before = jnp.floor((cnt + (MOE_BLK - 1)) * inv_blk)
        nb_after = jnp.floor((cnt + tile_cnt + (MOE_BLK - 1)) * inv_blk)
        new = nb_after - nb_before
        erow = lax.broadcasted_iota(jnp.int32, (N_EXPERTS, N_EXPERTS), 0)
        ecol = lax.broadcasted_iota(jnp.int32, (N_EXPERTS, N_EXPERTS), 1)
        lower = jnp.where(ecol < erow, 1.0, 0.0).astype(BF16)
        new_rep = jnp.broadcast_to(new, (N_EXPERTS, HEAD_PAD))
        new_id = free_ref[0:1, :] + _dot(lower, new_rep.astype(BF16))
        free_ref[...] = free_ref[...] + jnp.sum(new, axis=0, keepdims=True)
        last_before = last_ref[...]
        is_new = new_rep > 0.0
        last_ref[...] = jnp.where(is_new, new_id, last_before)
        nb_ref[slot] = jnp.where(is_new, new_id, -1.0).astype(jnp.int32)
        carry_ref[...] = carry_ref[...] + tile_cnt

        def place(ohk, rk):
            bi = jnp.floor(rk * inv_blk)
            blk = jnp.where(bi == nb_before - 1.0, last_before[:, 0:1], new_id[:, 0:1])
            phys = jnp.sum(jnp.where(ohk, blk, 0.0), axis=0, keepdims=True)
            return (phys * MOE_BLK + (rk - bi * MOE_BLK)).astype(jnp.int32)

        mrow = lax.broadcasted_iota(jnp.int32, (8, tm), 0)
        dest = jnp.where(mrow == 0, place(oh1, r1), jnp.where(mrow == 1, place(oh2, r2), 0))
        dv[slot][...] = dest
        pos_ref[0, :, rows] = dest
        mw_ref[0, :, rows] = jnp.where(mrow == 0, w1, jnp.where(mrow == 1, w2, 0.0))
        meta_copy(slot).start()

    @pl.when(step == 0)
    def _():
        carry_ref[...] = jnp.zeros_like(carry_ref)
        free_ref[...] = jnp.zeros_like(free_ref)
        last_ref[...] = jnp.full_like(last_ref, -1.0)
        xs1_ref[...] = jnp.zeros_like(xs1_ref)
        mrow = lax.broadcasted_iota(jnp.int32, (8, tm), 0)
        lane = lax.broadcasted_iota(jnp.int32, (8, tm), 1)
        dv1_ref[...] = jnp.where(mrow < 2, spare_row0 + 2 * lane + mrow, 0)
        meta_copy(1).start()

    def tile(slot):
        issue_rows(1 - slot)
        xn = project(slot)

        @pl.when(step >= 0)
        def _():
            route(slot, xn)

    meta_copy(1).wait()
    tile(0)
    wait_rows(1)
    meta_copy(0).wait()
    tile(1)
    wait_rows(0)

    @pl.when(step == last_step)
    def _():
        meta_copy(1).wait()
        issue_rows(1)
        wait_rows(1)


def _outproj(x, ot, oc, wo, g, wrhl, br, xbz):
    B, S, D = x.shape
    tm2 = 2 * TM
    n_s = S // tm2
    full = lambda a: pl.BlockSpec(a.shape, lambda b, i: (0,) * a.ndim)
    tile = lambda w: pl.BlockSpec((1, tm2, w), lambda b, i: (b, i, 0))
    meta = pl.BlockSpec((1, 8, tm2), lambda b, i: (b, 0, i))
    any_space = pl.BlockSpec(memory_space=pl.ANY)
    return pl.pallas_call(
        _outproj_kernel,
        grid=(B, n_s),
        in_specs=[
            tile(D),
            pl.BlockSpec((1, MLA_WIDTH, tm2), lambda b, i: (b, 0, i)),
            tile(CONV_WIDTH),
            full(wo), full(g), full(wrhl), full(br), any_space,
        ],
        out_specs=[tile(D), meta, meta,
                   pl.BlockSpec((2, N_EXPERTS, HEAD_PAD), lambda b, i: (b * n_s + i, 0, 0)),
                   any_space],
        out_shape=[
            jax.ShapeDtypeStruct((B, S, D), F32),
            jax.ShapeDtypeStruct((B, 8, S), jnp.int32),
            jax.ShapeDtypeStruct((B, 8, S), F32),
            jax.ShapeDtypeStruct((B * S // TM, N_EXPERTS, HEAD_PAD), jnp.int32),
            jax.ShapeDtypeStruct(xbz.shape, xbz.dtype),
        ],
        input_output_aliases={7: 4},
        scratch_shapes=[
            pltpu.VMEM((N_EXPERTS, HEAD_PAD), F32), pltpu.VMEM((N_EXPERTS, HEAD_PAD), F32),
            pltpu.VMEM((8, HEAD_PAD), F32),
            pltpu.VMEM((TM, D // 2), jnp.uint32), pltpu.VMEM((TM, D // 2), jnp.uint32),
            pltpu.VMEM((8, TM), jnp.int32), pltpu.VMEM((8, TM), jnp.int32),
            pltpu.SMEM((8, TM), jnp.int32), pltpu.SMEM((8, TM), jnp.int32),
            pltpu.SemaphoreType.DMA((2,)), pltpu.SemaphoreType.DMA((2,)),
        ],
        compiler_params=pltpu.CompilerParams(
            dimension_semantics=("arbitrary", "arbitrary"), vmem_limit_bytes=VMEM_LIMIT),
        name="outproj",
    )(x, ot, oc, wo, g, wrhl, br, xbz)


def _expert_kernel(order_ref, blke_ref, nused_ref, xb_ref, wg_ref, wu_ref, wd_ref, yb_ref,
                   wgs_ref, wus_ref, wds_ref):
    del order_ref
    i = pl.program_id(0)
    used = i < nused_ref[0]
    changed = (i == 0) | (blke_ref[i] != blke_ref[jnp.maximum(i - 1, 0)])

    @pl.when(used & changed)
    def _():
        wgs_ref[...] = wg_ref[0].astype(BF16)
        wus_ref[...] = wu_ref[0].astype(BF16)
        wds_ref[...] = wd_ref[0].astype(BF16)

    @pl.when(used)
    def _():
        x_lo, x_hi = _unpack_bf16_pairs(xb_ref[...])
        dh = x_lo.shape[1]
        half = EXPERT_FF // 2

        def xdot(w_ref, cols):
            return _dot(x_lo, w_ref[0:dh, cols]) + _dot(x_hi, w_ref[dh:2 * dh, cols])

        g0 = xdot(wgs_ref, slice(0, half))
        u0 = xdot(wus_ref, slice(0, half))
        g1 = xdot(wgs_ref, slice(half, EXPERT_FF))
        u1 = xdot(wus_ref, slice(half, EXPERT_FF))
        h0 = ((g0 * jax.nn.sigmoid(g0)) * u0).astype(BF16)
        y = _dot(h0, wds_ref[:half, :])
        h1 = ((g1 * jax.nn.sigmoid(g1)) * u1).astype(BF16)
        yb_ref[...] = _pack_bf16_pairs(y + _dot(h1, wds_ref[half:, :]))

    @pl.when(jnp.logical_not(used))
    def _():
        yb_ref[...] = jnp.zeros_like(yb_ref)


def _experts(order, blk_e, n_used, xb, wg, wu, wd):
    D = 2 * xb.shape[1]
    blk = MOE_BLK
    n_blocks = order.shape[0]
    return pl.pallas_call(
        _expert_kernel,
        grid_spec=pltpu.PrefetchScalarGridSpec(
            num_scalar_prefetch=3,
            grid=(n_blocks,),
            in_specs=[
                pl.BlockSpec((blk, D // 2), lambda i, o, be, nu: (o[jnp.minimum(i, nu[0] - 1)], 0)),
                pl.BlockSpec((1, D, EXPERT_FF), lambda i, o, be, nu: (be[i], 0, 0)),
                pl.BlockSpec((1, D, EXPERT_FF), lambda i, o, be, nu: (be[i], 0, 0)),
                pl.BlockSpec((1, EXPERT_FF, D), lambda i, o, be, nu: (be[i], 0, 0)),
            ],
            out_specs=pl.BlockSpec((blk, D // 2), lambda i, o, be, nu: (o[i], 0)),
            scratch_shapes=[pltpu.VMEM((D, EXPERT_FF), BF16), pltpu.VMEM((D, EXPERT_FF), BF16),
                            pltpu.VMEM((EXPERT_FF, D), BF16)],
        ),
        out_shape=jax.ShapeDtypeStruct((n_blocks * blk, D // 2), jnp.uint32),
        compiler_params=pltpu.CompilerParams(
            dimension_semantics=("arbitrary",), vmem_limit_bytes=VMEM_LIMIT),
        name="experts",
    )(order, blk_e, n_used, xb, wg, wu, wd)


def _final_kernel(dest_ref, h1_ref, p_ref, wt_ref, yb_ref, gp_ref, wg_ref, bg_ref, wp_ref,
                  gf_ref, out_ref, buf0_ref, buf1_ref, buf2_ref, buf3_ref, sem):
    tm = h1_ref.shape[0] // FIN_TILES
    i = pl.program_id(0)
    bufs = (buf0_ref, buf1_ref, buf2_ref, buf3_ref)

    def row_copy(slot, k, r, d):
        return pltpu.make_async_copy(yb_ref.at[pl.ds(d, 1)], bufs[slot].at[k, pl.ds(r, 1)],
                                     sem.at[slot])

    def issue(tile, slot):
        for r in range(tm):
            t = (tile * tm + r) * 2
            row_copy(slot, 0, r, dest_ref[t]).start(priority=0)
            row_copy(slot, 1, r, dest_ref[t + 1]).start(priority=1)

    def wait(slot):
        for k in range(2):
            pltpu.make_async_copy(yb_ref.at[pl.ds(0, tm)], bufs[slot].at[k], sem.at[slot]).wait()

    def compute(slot):
        rows = slice(slot * tm, (slot + 1) * tm)
        pe = _dot(p_ref[rows, :].astype(BF16), wp_ref[...])
        wt = wt_ref[rows, :]
        h2 = (h1_ref[rows, :] + wt[:, 0:1] * _unpack_f32_pairs(bufs[slot][0])
              + wt[:, 1:2] * _unpack_f32_pairs(bufs[slot][1]))
        n = _rms(h2, gp_ref[...]).astype(BF16)
        gate = jax.nn.sigmoid(_dot(n, wg_ref[...]) + bg_ref[...])
        h3 = h2 + gate * pe
        out_ref[rows, :] = _rms(h3, gf_ref[...])

    @pl.when(i == 0)
    def _():
        for j in range(FIN_AHEAD):
            issue(j, j)

    last_tile = FIN_TILES * pl.num_programs(0) - 1
    for j in range(FIN_TILES):
        issue(jnp.minimum(FIN_TILES * i + j + FIN_AHEAD, last_tile), (j + FIN_AHEAD) % FIN_TILES)
        wait(j)
        compute(j)

    @pl.when(i == pl.num_programs(0) - 1)
    def _():
        for j in range(FIN_AHEAD):
            wait(j)


def _final(dest, h1, p, wt, yb, gp, wg, bg, wp, gf):
    N, D = h1.shape
    tm = TM
    full = lambda a: pl.BlockSpec(a.shape, lambda i, dest: (0,) * a.ndim)
    return pl.pallas_call(
        _final_kernel,
        grid_spec=pltpu.PrefetchScalarGridSpec(
            num_scalar_prefetch=1,
            grid=(N // (FIN_TILES * tm),),
            in_specs=[
                pl.BlockSpec((FIN_TILES * tm, D), lambda i, dest: (i, 0)),
                pl.BlockSpec((FIN_TILES * tm, PLE_DIM), lambda i, dest: (i, 0)),
                pl.BlockSpec((FIN_TILES * tm, 2), lambda i, dest: (i, 0)),
                pl.BlockSpec(memory_space=pl.ANY),
                full(gp), full(wg), full(bg), full(wp), full(gf),
            ],
            out_specs=pl.BlockSpec((FIN_TILES * tm, D), lambda i, dest: (i, 0)),
            scratch_shapes=[pltpu.VMEM((2, tm, D // 2), jnp.uint32) for _ in range(FIN_TILES)]
            + [pltpu.SemaphoreType.DMA((FIN_TILES,))],
        ),
        out_shape=jax.ShapeDtypeStruct((N, D), F32),
        compiler_params=pltpu.CompilerParams(
            dimension_semantics=("arbitrary",), vmem_limit_bytes=VMEM_LIMIT),
        name="final",
    )(dest, h1, p, wt, yb, gp, wg, bg, wp, gf)


def _prep_weights(w_in, w_uq, w_ukv):
    cq, ckv, kr, bg, cg, u = jnp.split(
        w_in, [Q_LORA, Q_LORA + KV_LORA, Q_LORA + KV_LORA + QK_ROPE_DIM,
               Q_LORA + KV_LORA + QK_ROPE_DIM + CONV_WIDTH,
               Q_LORA + KV_LORA + QK_ROPE_DIM + 2 * CONV_WIDTH], axis=1)
    kr_rot = jnp.concatenate([-kr[:, HALF_ROPE:], kr[:, :HALF_ROPE]], axis=1)
    pad = jnp.zeros((D_MODEL, HEAD_PAD - 2 * QK_ROPE_DIM), w_in.dtype)
    win = jnp.concatenate([cq, ckv, bg, cg, u, kr, kr_rot, pad], axis=1).astype(BF16)

    zq = jnp.zeros((Q_LORA, N_HEADS, HEAD_PAD - QK_DIM), w_uq.dtype)
    wqa = jnp.concatenate([w_uq, zq], axis=2)
    rope = w_uq[:, :, QK_NOPE_DIM:]
    wqb = jnp.concatenate([-rope[:, :, HALF_ROPE:], rope[:, :, :HALF_ROPE]], axis=2)
    wqa = wqa.reshape(Q_LORA, N_HEADS * HEAD_PAD).T.astype(BF16)
    wqb = wqb.reshape(Q_LORA, N_HEADS * QK_ROPE_DIM).T.astype(BF16)

    zk = jnp.zeros((KV_LORA, N_HEADS, HEAD_PAD - QK_NOPE_DIM), w_ukv.dtype)
    wuk = jnp.concatenate([w_ukv[:, :, :QK_NOPE_DIM], zk], axis=2)
    wuk = wuk.reshape(KV_LORA, N_HEADS * HEAD_PAD).astype(BF16)
    wuvt = w_ukv[:, :, QK_NOPE_DIM:].reshape(KV_LORA, MLA_WIDTH).T.astype(BF16)

    src = jnp.arange(QK_ROPE_DIM)[:, None]
    dst = jnp.arange(N_HEADS * HEAD_PAD)[None, :]
    pk = (dst % HEAD_PAD - QK_NOPE_DIM == src).astype(BF16)
    return win, wqa, wqb, wuk, wuvt, pk


def kernel(x, p, positions, attn_norm_g, w_in, q_norm_g, w_uq, kv_norm_g, w_ukv, conv_w, w_out,
           moe_norm_g, w_group_router, b_group_router, w_expert_router, b_expert_router,
           w_gate, w_up, w_down, ple_norm_g, w_ple_gate, b_ple_gate, w_ple_proj, final_norm_g):
    B, S, D = x.shape
    N = B * S
    assert w_in.shape[0] == 1, "single-layer trunk: the final norm is fused into the layer"
    pos = positions.astype(F32).reshape(B, 1, S)
    invf = (ROPE_BASE ** (-jnp.arange(0, QK_ROPE_DIM, 2, dtype=F32) / QK_ROPE_DIM)).reshape(-1, 1)
    row = lambda v: v.reshape(1, -1)
    h = x
    for i in range(1):
        win, wqa, wqb, wuk, wuvt, pk = _prep_weights(w_in[i], w_uq[i], w_ukv[i])
        qt, k, vt, oc = _inproj(h, pos, invf, row(attn_norm_g[i]), win, row(q_norm_g[i]), wqa,
                                wqb, row(kv_norm_g[i]), wuk, pk, wuvt,
                                conv_w[i].reshape(CONV_K, CONV_WIDTH))
        n_blocks = (2 * N) // MOE_BLK + N_EXPERTS
        ot, xbz = _attention(qt, k, vt, n_blocks * MOE_BLK + 2 * TM)

        zrow = lambda n: jnp.zeros((n, D), F32)
        wr = jnp.concatenate(
            [w_group_router[i].T, zrow(ROUTER_E0 - N_GROUPS), w_expert_router[i].T,
             zrow(ROUTER_ROWS - ROUTER_E0 - N_EXPERTS)], axis=0)
        wrh = wr.astype(BF16)
        wrhl = jnp.concatenate([wrh, (wr - wrh.astype(F32)).astype(BF16)], axis=0)
        br = jnp.concatenate(
            [b_group_router[i], jnp.zeros((ROUTER_E0 - N_GROUPS,), F32), b_expert_router[i],
             jnp.zeros((ROUTER_ROWS - ROUTER_E0 - N_EXPERTS,), F32)]).reshape(-1, 1)
        h1, pos, meta_w, opened, xb = _outproj(h, ot, oc, w_out[i].astype(BF16),
                                               row(moe_norm_g[i]), wrhl, br, xbz)

        opened = opened[:, :, 0]
        blk_ids = jnp.arange(n_blocks, dtype=jnp.int32)
        hit = opened[None, :, :] == blk_ids[:, None, None]
        used = jnp.any(hit, axis=(1, 2))
        e_of_blk = jnp.sum(jnp.where(hit, jnp.arange(N_EXPERTS, dtype=jnp.int32), 0), axis=(1, 2))
        n_used = jnp.sum(used.astype(jnp.int32)).reshape(1)
        key = jnp.where(used, e_of_blk, N_EXPERTS) * n_blocks + blk_ids
        slot_of_blk = jnp.sum((key[None, :] < key[:, None]).astype(jnp.int32), axis=1)
        at_slot = slot_of_blk[None, :] == blk_ids[:, None]
        order = jnp.sum(jnp.where(at_slot, blk_ids[None, :], 0), axis=1)
        blk_e = jnp.sum(jnp.where(at_slot, e_of_blk[None, :], 0), axis=1)
        blk_e = jnp.where(blk_ids < n_used, blk_e, jnp.max(jnp.where(used, e_of_blk, 0)))
        dest = jnp.swapaxes(pos[:, 0:2, :], 1, 2).reshape(2 * N)
        wt = jnp.swapaxes(meta_w[:, 0:2, :], 1, 2).reshape(N, 2)

        yb = _experts(order, blk_e, n_used, xb, w_gate[i], w_up[i], w_down[i])
        out = _final(dest, h1.reshape(N, D), p[i].reshape(N, PLE_DIM), wt, yb,
                     row(ple_norm_g[i]), w_ple_gate[i].astype(BF16), row(b_ple_gate[i]),
                     w_ple_proj[i].astype(BF16), row(final_norm_g))
        h = out.reshape(B, S, D)
    return h
```

```python
import math

import jax
import jax.numpy as jnp
from jax import lax
from jax.experimental import pallas as pl
from jax.experimental.pallas import tpu as pltpu

D_MODEL = 1024
PLE_DIM = 256
MLA_WIDTH = 512
CONV_WIDTH = 512
N_HEADS = 8
V_HEAD_DIM = 64
QK_NOPE_DIM = 64
QK_ROPE_DIM = 32
Q_LORA = 384
KV_LORA = 256
CONV_K = 3
N_GROUPS = 4
EXPERTS_PER_GROUP = 8
N_EXPERTS = N_GROUPS * EXPERTS_PER_GROUP
EXPERT_FF = 512
ROPE_BASE = 10000.0
EPS = 1e-6

HEAD_PAD = 128
V_ROWS = 80
QK_DIM = QK_NOPE_DIM + QK_ROPE_DIM
HALF_ROPE = QK_ROPE_DIM // 2

_C_Q = 0
_C_KV = _C_Q + Q_LORA
_C_BG = _C_KV + KV_LORA
_C_CG = _C_BG + CONV_WIDTH
_C_U = _C_CG + CONV_WIDTH
_C_KR = _C_U + CONV_WIDTH
IN_COLS_PAD = _C_KR + HEAD_PAD

TM = 256
TM_IN = 512
FIN_TILES = 4
FIN_AHEAD = 2
TQ = 512
TK = 256
ATT_HG = 4
ATT_TQC = 256
MOE_BLK = 512
SUBLANES = 8
BF16_SUBLANES = 16
ROUTER_E0 = SUBLANES
ROUTER_ROWS = -(-(ROUTER_E0 + N_EXPERTS) // BF16_SUBLANES) * BF16_SUBLANES
NEG = -1e30
Q_SCALE = (QK_DIM ** -0.5) * math.log2(math.e)
VMEM_LIMIT = 48 * 1024 * 1024
assert N_GROUPS <= ROUTER_E0 and MOE_BLK > TM

F32 = jnp.float32
BF16 = jnp.bfloat16


def _rms(x, g):
    return x * lax.rsqrt(jnp.mean(x * x, axis=-1, keepdims=True) + EPS) * g


def _dot(a, b):
    return jnp.dot(a, b, preferred_element_type=F32)


def _dot_nt(a, b):
    return lax.dot_general(a, b, (((1,), (1,)), ((), ())), preferred_element_type=F32)


def _pack_bf16_pairs(x):
    w = x.shape[1] // 2
    lo = pltpu.bitcast(x[:, :w].astype(BF16).astype(F32), jnp.uint32)
    hi = pltpu.bitcast(x[:, w:].astype(BF16).astype(F32), jnp.uint32)
    return (lo >> 16) | hi


def _unpack_bf16_pairs(p):
    lo = pltpu.bitcast(p << 16, F32).astype(BF16)
    hi = pltpu.bitcast(p & jnp.uint32(0xFFFF0000), F32).astype(BF16)
    return lo, hi


def _unpack_f32_pairs(p):
    lo = pltpu.bitcast(p << 16, F32)
    hi = pltpu.bitcast(p & jnp.uint32(0xFFFF0000), F32)
    return jnp.concatenate([lo, hi], axis=1)


def _dot_tn(a, b):
    return lax.dot_general(a, b, (((0,), (0,)), ((), ())), preferred_element_type=F32)


def _inproj_kernel(x_ref, pos_ref, invf_ref, g_ref, win_ref, qg_ref, wqa_ref, wqb_ref, kvg_ref,
                   wuk_ref, pk_ref, wuvt_ref, convw_ref,
                   qt_ref, k_ref, vt_ref, oc_ref, carry_ref, ext_ref):
    tm = x_ref.shape[1]
    x = x_ref[0]
    xn = _rms(x, g_ref[...])
    z = _dot(xn.astype(BF16), win_ref[...])

    ang = invf_ref[...] * pos_ref[0]
    cos = jnp.cos(ang)
    sin = jnp.sin(ang)
    cos2 = jnp.concatenate([cos, cos], axis=0)
    sin2 = jnp.concatenate([sin, sin], axis=0)

    cqn = _rms(z[:, _C_Q:_C_Q + Q_LORA], qg_ref[...]).astype(BF16)
    qa = _dot_nt(wqa_ref[...], cqn)
    qb = _dot_nt(wqb_ref[...], cqn)
    for h in range(N_HEADS):
        r0 = h * HEAD_PAD
        nope = qa[r0:r0 + QK_NOPE_DIM]
        rope = (qa[r0 + QK_NOPE_DIM:r0 + QK_DIM] * cos2
                + qb[h * QK_ROPE_DIM:(h + 1) * QK_ROPE_DIM] * sin2)
        qh = jnp.concatenate([nope, rope, qa[r0 + QK_DIM:r0 + HEAD_PAD]], axis=0) * Q_SCALE
        qt_ref[0, r0:r0 + HEAD_PAD, :] = qh.astype(BF16)

    kvn = _rms(z[:, _C_KV:_C_KV + KV_LORA], kvg_ref[...]).astype(BF16)
    krt = z[:, _C_KR:_C_KR + HEAD_PAD].T
    krot = krt[0:QK_ROPE_DIM] * cos2 + krt[QK_ROPE_DIM:2 * QK_ROPE_DIM] * sin2
    k_ref[0] = (_dot(kvn, wuk_ref[...]) + _dot_tn(krot.astype(BF16), pk_ref[...])).astype(BF16)
    vt = _dot_nt(wuvt_ref[...], kvn).astype(BF16)
    ones_row = jnp.where(lax.broadcasted_iota(jnp.int32, (V_ROWS - V_HEAD_DIM, tm), 0) == 0,
                         1.0, 0.0).astype(BF16)
    for h in range(N_HEADS):
        vt_ref[0, h * V_ROWS:h * V_ROWS + V_HEAD_DIM, :] = vt[h * V_HEAD_DIM:(h + 1) * V_HEAD_DIM]
        vt_ref[0, h * V_ROWS + V_HEAD_DIM:(h + 1) * V_ROWS, :] = ones_row

    @pl.when(pl.program_id(1) == 0)
    def _():
        carry_ref[...] = jnp.zeros_like(carry_ref)

    cu = z[:, _C_CG:_C_CG + CONV_WIDTH] * z[:, _C_U:_C_U + CONV_WIDTH]
    ext_ref[0:8, :] = carry_ref[...]
    ext_ref[8:8 + tm, :] = cu
    cu1 = ext_ref[7:7 + tm, :]
    cu2 = ext_ref[6:6 + tm, :]
    w = convw_ref[...]
    y = w[0:1] * cu2 + w[1:2] * cu1 + w[2:3] * cu
    oc_ref[0] = (z[:, _C_BG:_C_BG + CONV_WIDTH] * y).astype(BF16)
    carry_ref[...] = ext_ref[tm:tm + 8, :]


def _inproj(x, pos, invf, g, win, qg, wqa, wqb, kvg, wuk, pk, wuvt, convw):
    B, S, D = x.shape
    tm = TM_IN
    full = lambda a: pl.BlockSpec(a.shape, lambda b, i: (0,) * a.ndim)
    return pl.pallas_call(
        _inproj_kernel,
        grid=(B, S // tm),
        in_specs=[
            pl.BlockSpec((1, tm, D), lambda b, i: (b, i, 0)),
            pl.BlockSpec((1, 1, tm), lambda b, i: (b, 0, i)),
            full(invf), full(g), full(win), full(qg), full(wqa), full(wqb), full(kvg), full(wuk),
            full(pk), full(wuvt), full(convw),
        ],
        out_specs=[
            pl.BlockSpec((1, N_HEADS * HEAD_PAD, tm), lambda b, i: (b, 0, i)),
            pl.BlockSpec((1, tm, N_HEADS * HEAD_PAD), lambda b, i: (b, i, 0)),
            pl.BlockSpec((1, N_HEADS * V_ROWS, tm), lambda b, i: (b, 0, i)),
            pl.BlockSpec((1, tm, CONV_WIDTH), lambda b, i: (b, i, 0)),
        ],
        out_shape=[
            jax.ShapeDtypeStruct((B, N_HEADS * HEAD_PAD, S), BF16),
            jax.ShapeDtypeStruct((B, S, N_HEADS * HEAD_PAD), BF16),
            jax.ShapeDtypeStruct((B, N_HEADS * V_ROWS, S), BF16),
            jax.ShapeDtypeStruct((B, S, CONV_WIDTH), BF16),
        ],
        scratch_shapes=[pltpu.VMEM((8, CONV_WIDTH), F32), pltpu.VMEM((tm + 8, CONV_WIDTH), F32)],
        compiler_params=pltpu.CompilerParams(
            dimension_semantics=("arbitrary", "arbitrary"), vmem_limit_bytes=VMEM_LIMIT),
        name="inproj",
    )(x, pos, invf, g, win, qg, wqa, wqb, kvg, wuk, pk, wuvt, convw)


def _attn_kernel(qt_ref, k_ref, vt_ref, o_ref, z_ref, m_ref, acc_ref, sa_ref, sb_ref, zero_ref,
                 zsem):
    tq = qt_ref.shape[2]
    tk = TK
    assert tq == 2 * tk and ATT_TQC == tk
    i = pl.program_id(2)
    m_ref[...] = jnp.full_like(m_ref, NEG)
    acc_ref[...] = jnp.zeros_like(acc_ref)

    step = ((pl.program_id(0) * pl.num_programs(1) + pl.program_id(1)) * pl.num_programs(2) + i)
    zrows = zero_ref.shape[0]

    @pl.when(step == 0)
    def _():
        zero_ref[...] = jnp.zeros_like(zero_ref)

    zero_copy = pltpu.make_async_copy(
        zero_ref, z_ref.at[pl.ds(pl.multiple_of(step * zrows, 8), zrows)], zsem)
    zero_copy.start()

    def scores(j, s_ref, g, col0=0):
        k0 = pl.multiple_of(j * tk, tk)
        qt = qt_ref[0, g * HEAD_PAD:(g + 1) * HEAD_PAD, col0:]
        s_ref[g, :, col0:] = _dot(k_ref[0, pl.ds(k0, tk), g * HEAD_PAD:(g + 1) * HEAD_PAD], qt)

    def softmax_pv(j, s_ref, g, diag):
        k0 = pl.multiple_of(j * tk, tk)
        vt = vt_ref[0, g * V_ROWS:(g + 1) * V_ROWS, pl.ds(k0, tk)]
        for c in range(tq // ATT_TQC):
            if diag is not None and c < diag:
                continue
            cols = slice(c * ATT_TQC, (c + 1) * ATT_TQC)
            s = s_ref[g, :, cols]
            if diag is not None and c == diag:
                krow = lax.broadcasted_iota(jnp.int32, (tk, ATT_TQC), 0)
                qcol = lax.broadcasted_iota(jnp.int32, (tk, ATT_TQC), 1)
                s = jnp.where(krow <= qcol, s, NEG)
            m_old = m_ref[g, :, cols]
            m_new = jnp.maximum(m_old, jnp.max(s, axis=0, keepdims=True))
            alpha = jnp.exp2(m_old - m_new)
            p = jnp.exp2((s - m_new).astype(BF16))
            acc_ref[g, :, cols] = alpha * acc_ref[g, :, cols] + _dot(vt, p)
            m_ref[g, :, cols] = m_new

    def stage(j_next, s_next, j, s_cur, diag=None, next_col0=0):
        for g in range(ATT_HG):
            if j_next is not None:
                scores(j_next, s_next, g, next_col0)
            softmax_pv(j, s_cur, g, diag)

    for g in range(ATT_HG):
        scores(0, sa_ref, g)

    def pair(j):
        stage(j + 1, sb_ref, j, sa_ref)
        stage(j + 2, sa_ref, j + 1, sb_ref)

    def body(t, c):
        pair(4 * t)
        pair(4 * t + 2)
        return c

    lax.fori_loop(0, i // 2, body, 0)

    @pl.when(i % 2 == 1)
    def _():
        pair(2 * i - 2)

    n_full = 2 * i
    stage(n_full + 1, sb_ref, n_full, sa_ref, diag=0, next_col0=ATT_TQC)
    stage(None, None, n_full + 1, sb_ref, diag=1)
    for g in range(ATT_HG):
        o_ref[0, g * V_HEAD_DIM:(g + 1) * V_HEAD_DIM, :] = (
            acc_ref[g, 0:V_HEAD_DIM, :] / acc_ref[g, V_HEAD_DIM:V_HEAD_DIM + 1, :]).astype(o_ref.dtype)
    zero_copy.wait()


def _attention(qt, k, vt, min_zero_rows):
    B, _, S = qt.shape
    hg = ATT_HG
    grid = (B, N_HEADS // hg, S // TQ)
    n_steps = grid[0] * grid[1] * grid[2]
    zrows = -(-min_zero_rows // (SUBLANES * n_steps)) * SUBLANES
    return pl.pallas_call(
        _attn_kernel,
        grid=grid,
        in_specs=[
            pl.BlockSpec((1, hg * HEAD_PAD, TQ), lambda b, h, i: (b, h, i)),
            pl.BlockSpec((1, S, hg * HEAD_PAD), lambda b, h, i: (b, 0, h)),
            pl.BlockSpec((1, hg * V_ROWS, S), lambda b, h, i: (b, h, 0)),
        ],
        out_specs=[pl.BlockSpec((1, hg * V_HEAD_DIM, TQ), lambda b, h, i: (b, h, i)),
                   pl.BlockSpec(memory_space=pl.ANY)],
        out_shape=[jax.ShapeDtypeStruct((B, MLA_WIDTH, S), BF16),
                   jax.ShapeDtypeStruct((n_steps * zrows, D_MODEL // 2), jnp.uint32)],
        scratch_shapes=[pltpu.VMEM((hg, 1, TQ), F32), pltpu.VMEM((hg, V_ROWS, TQ), F32),
                        pltpu.VMEM((hg, TK, TQ), F32), pltpu.VMEM((hg, TK, TQ), F32),
                        pltpu.VMEM((zrows, D_MODEL // 2), jnp.uint32), pltpu.SemaphoreType.DMA],
        compiler_params=pltpu.CompilerParams(
            dimension_semantics=("arbitrary", "arbitrary", "arbitrary"),
            vmem_limit_bytes=VMEM_LIMIT),
        name="attn",
    )(qt, k, vt)


def _route_tile(xn, wrhl_ref, br_ref, carry_ref):
    tm = xn.shape[0]
    xh = xn.astype(BF16)
    xl = (xn - xh.astype(F32)).astype(BF16)
    hl = _dot_nt(wrhl_ref[...], xh)
    logits = (hl[0:ROUTER_ROWS] + hl[ROUTER_ROWS:2 * ROUTER_ROWS]
              + _dot_nt(wrhl_ref[0:ROUTER_ROWS, :], xl) + br_ref[...])
    big = jnp.int32(1 << 20)

    grow = lax.broadcasted_iota(jnp.int32, (ROUTER_E0, tm), 0)
    glog = jnp.where(grow < N_GROUPS, logits[0:ROUTER_E0], NEG)
    gmax = jnp.max(glog, axis=0, keepdims=True)
    gsum = jnp.sum(jnp.exp(glog - gmax), axis=0, keepdims=True)
    g_p = 1.0 / gsum
    g_idx = jnp.min(jnp.where(glog == gmax, grow, big), axis=0, keepdims=True)

    erow = lax.broadcasted_iota(jnp.int32, (N_EXPERTS, tm), 0)
    e_lo = g_idx * EXPERTS_PER_GROUP
    in_group = (erow >= e_lo) & (erow < e_lo + EXPERTS_PER_GROUP)
    elog = jnp.where(in_group, logits[ROUTER_E0:ROUTER_E0 + N_EXPERTS], NEG)
    emax = jnp.max(elog, axis=0, keepdims=True)
    esum = jnp.sum(jnp.exp(elog - emax), axis=0, keepdims=True)
    e1 = jnp.min(jnp.where(elog == emax, erow, big), axis=0, keepdims=True)
    elog2 = jnp.where(erow == e1, NEG, elog)
    emax2 = jnp.max(elog2, axis=0, keepdims=True)
    e2 = jnp.min(jnp.where(elog2 == emax2, erow, big), axis=0, keepdims=True)
    p1 = 1.0 / esum
    p2 = jnp.exp(emax2 - emax) / esum
    psum = p1 + p2
    w1 = g_p * (p1 / psum)
    w2 = g_p * (p2 / psum)

    oh1 = erow == e1
    oh2 = erow == e2
    oh = jnp.where(oh1 | oh2, 1.0, 0.0)
    srow = lax.broadcasted_iota(jnp.int32, (tm, tm), 0)
    scol = lax.broadcasted_iota(jnp.int32, (tm, tm), 1)
    earlier = jnp.where(srow < scol, 1.0, 0.0).astype(BF16)
    cum = _dot(oh.astype(BF16), earlier) + carry_ref[:, 0:1]
    r1 = jnp.sum(jnp.where(oh1, cum, 0.0), axis=0, keepdims=True)
    r2 = jnp.sum(jnp.where(oh2, cum, 0.0), axis=0, keepdims=True)
    return oh, oh1, oh2, r1, r2, w1, w2


def _outproj_kernel(x_ref, ot_ref, oc_ref, wo_ref, g_ref, wrhl_ref, br_ref, xbz_ref,
                    h1_ref, pos_ref, mw_ref, nb_ref, xb_ref,
                    carry_ref, last_ref, free_ref, xs0_ref, xs1_ref, dv0_ref, dv1_ref,
                    ds0_ref, ds1_ref, rsem, ssem):
    del xbz_ref
    tm = TM
    step = pl.program_id(0) * pl.num_programs(1) + pl.program_id(1)
    last_step = pl.num_programs(0) * pl.num_programs(1) - 1
    xs = (xs0_ref, xs1_ref)
    dv = (dv0_ref, dv1_ref)
    ds = (ds0_ref, ds1_ref)
    spare_row0 = xb_ref.shape[0] - 2 * tm

    def meta_copy(slot):
        return pltpu.make_async_copy(dv[slot], ds[slot], ssem.at[slot])

    def issue_rows(slot):
        for r in range(tm):
            for k in range(2):
                pltpu.make_async_copy(xs[slot].at[pl.ds(r, 1)],
                                      xb_ref.at[pl.ds(ds[slot][k, r], 1)], rsem.at[slot]
                                      ).start(priority=k)

    def wait_rows(slot):
        for _ in range(2):
            pltpu.make_async_copy(xs[slot], xb_ref.at[pl.ds(0, tm)], rsem.at[slot]).wait()

    def project(slot):
        rows = slice(slot * tm, (slot + 1) * tm)
        attn = (_dot_tn(ot_ref[0, :, rows], wo_ref[0:MLA_WIDTH, :])
                + _dot(oc_ref[0, rows, :], wo_ref[MLA_WIDTH:MLA_WIDTH + CONV_WIDTH, :]))
        h1 = x_ref[0, rows, :] + attn
        h1_ref[0, rows, :] = h1
        xn = _rms(h1, g_ref[...])
        xs[slot][...] = _pack_bf16_pairs(xn)
        return xn

    def route(slot, xn):
        rows = slice(slot * tm, (slot + 1) * tm)
        oh, oh1, oh2, r1, r2, w1, w2 = _route_tile(xn, wrhl_ref, br_ref, carry_ref)

        inv_blk = 1.0 / MOE_BLK
        cnt = carry_ref[:, 0:1]
        tile_cnt = jnp.sum(oh, axis=1, keepdims=True)
        nb_before = jnp.floor((cnt + (MOE_BLK - 1)) * inv_blk)
        nb_after = jnp.floor((cnt + tile_cnt + (MOE_BLK - 1)) * inv_blk)
        new = nb_after - nb_before
        erow = lax.broadcasted_iota(jnp.int32, (N_EXPERTS, N_EXPERTS), 0)
        ecol = lax.broadcasted_iota(jnp.int32, (N_EXPERTS, N_EXPERTS), 1)
        lower = jnp.where(ecol < erow, 1.0, 0.0).astype(BF16)
        new_rep = jnp.broadcast_to(new, (N_EXPERTS, HEAD_PAD))
        new_id = free_ref[0:1, :] + _dot(lower, new_rep.astype(BF16))
        free_ref[...] = free_ref[...] + jnp.sum(new, axis=0, keepdims=True)
        last_before = last_ref[...]
        is_new = new_rep > 0.0
        last_ref[...] = jnp.where(is_new, new_id, last_before)
        nb_ref[slot] = jnp.where(is_new, new_id, -1.0).astype(jnp.int32)
        carry_ref[...] = carry_ref[...] + tile_cnt

        def place(ohk, rk):
            bi = jnp.floor(rk * inv_blk)
            blk = jnp.where(bi == nb_before - 1.0, last_before[:, 0:1], new_id[:, 0:1])
            phys = jnp.sum(jnp.where(ohk, blk, 0.0), axis=0, keepdims=True)
            return (phys * MOE_BLK + (rk - bi * MOE_BLK)).astype(jnp.int32)

        mrow = lax.broadcasted_iota(jnp.int32, (SUBLANES, tm), 0)
        dest = jnp.where(mrow == 0, place(oh1, r1), jnp.where(mrow == 1, place(oh2, r2), 0))
        dv[slot][...] = dest
        pos_ref[0, :, rows] = dest
        mw_ref[0, :, rows] = jnp.where(mrow == 0, w1, jnp.where(mrow == 1, w2, 0.0))
        meta_copy(slot).start()

    @pl.when(step == 0)
    def _():
        carry_ref[...] = jnp.zeros_like(carry_ref)
        free_ref[...] = jnp.zeros_like(free_ref)
        last_ref[...] = jnp.full_like(last_ref, -1.0)
        xs1_ref[...] = jnp.zeros_like(xs1_ref)
        mrow = lax.broadcasted_iota(jnp.int32, (SUBLANES, tm), 0)
        lane = lax.broadcasted_iota(jnp.int32, (SUBLANES, tm), 1)
        dv1_ref[...] = jnp.where(mrow < 2, spare_row0 + 2 * lane + mrow, 0)
        meta_copy(1).start()

    def tile(slot):
        issue_rows(1 - slot)
        xn = project(slot)

        @pl.when(step >= 0)
        def _():
            route(slot, xn)

    meta_copy(1).wait()
    tile(0)
    wait_rows(1)
    meta_copy(0).wait()
    tile(1)
    wait_rows(0)

    @pl.when(step == last_step)
    def _():
        meta_copy(1).wait()
        issue_rows(1)
        wait_rows(1)


def _outproj(x, ot, oc, wo, g, wrhl, br, xbz):
    B, S, D = x.shape
    tm2 = 2 * TM
    n_s = S // tm2
    full = lambda a: pl.BlockSpec(a.shape, lambda b, i: (0,) * a.ndim)
    tile = lambda w: pl.BlockSpec((1, tm2, w), lambda b, i: (b, i, 0))
    meta = pl.BlockSpec((1, SUBLANES, tm2), lambda b, i: (b, 0, i))
    any_space = pl.BlockSpec(memory_space=pl.ANY)
    return pl.pallas_call(
        _outproj_kernel,
        grid=(B, n_s),
        in_specs=[
            tile(D),
            pl.BlockSpec((1, MLA_WIDTH, tm2), lambda b, i: (b, 0, i)),
            tile(CONV_WIDTH),
            full(wo), full(g), full(wrhl), full(br), any_space,
        ],
        out_specs=[tile(D), meta, meta,
                   pl.BlockSpec((2, N_EXPERTS, HEAD_PAD), lambda b, i: (b * n_s + i, 0, 0)),
                   any_space],
        out_shape=[
            jax.ShapeDtypeStruct((B, S, D), F32),
            jax.ShapeDtypeStruct((B, SUBLANES, S), jnp.int32),
            jax.ShapeDtypeStruct((B, SUBLANES, S), F32),
            jax.ShapeDtypeStruct((B * S // TM, N_EXPERTS, HEAD_PAD), jnp.int32),
            jax.ShapeDtypeStruct(xbz.shape, xbz.dtype),
        ],
        input_output_aliases={7: 4},
        scratch_shapes=[
            pltpu.VMEM((N_EXPERTS, HEAD_PAD), F32), pltpu.VMEM((N_EXPERTS, HEAD_PAD), F32),
            pltpu.VMEM((SUBLANES, HEAD_PAD), F32),
            pltpu.VMEM((TM, D // 2), jnp.uint32), pltpu.VMEM((TM, D // 2), jnp.uint32),
            pltpu.VMEM((SUBLANES, TM), jnp.int32), pltpu.VMEM((SUBLANES, TM), jnp.int32),
            pltpu.SMEM((SUBLANES, TM), jnp.int32), pltpu.SMEM((SUBLANES, TM), jnp.int32),
            pltpu.SemaphoreType.DMA((2,)), pltpu.SemaphoreType.DMA((2,)),
        ],
        compiler_params=pltpu.CompilerParams(
            dimension_semantics=("arbitrary", "arbitrary"), vmem_limit_bytes=VMEM_LIMIT),
        name="outproj",
    )(x, ot, oc, wo, g, wrhl, br, xbz)


def _expert_kernel(order_ref, blke_ref, nused_ref, xb_ref, wg_ref, wu_ref, wd_ref, yb_ref,
                   wgs_ref, wus_ref, wds_ref):
    del order_ref
    i = pl.program_id(0)
    used = i < nused_ref[0]
    changed = (i == 0) | (blke_ref[i] != blke_ref[jnp.maximum(i - 1, 0)])

    @pl.when(used & changed)
    def _():
        wgs_ref[...] = wg_ref[0].astype(BF16)
        wus_ref[...] = wu_ref[0].astype(BF16)
        wds_ref[...] = wd_ref[0].astype(BF16)

    @pl.when(used)
    def _():
        x_lo, x_hi = _unpack_bf16_pairs(xb_ref[...])
        dh = x_lo.shape[1]
        half = EXPERT_FF // 2

        def xdot(w_ref, cols):
            return _dot(x_lo, w_ref[0:dh, cols]) + _dot(x_hi, w_ref[dh:2 * dh, cols])

        g0 = xdot(wgs_ref, slice(0, half))
        u0 = xdot(wus_ref, slice(0, half))
        g1 = xdot(wgs_ref, slice(half, EXPERT_FF))
        u1 = xdot(wus_ref, slice(half, EXPERT_FF))
        h0 = ((g0 * jax.nn.sigmoid(g0)) * u0).astype(BF16)
        y = _dot(h0, wds_ref[:half, :])
        h1 = ((g1 * jax.nn.sigmoid(g1)) * u1).astype(BF16)
        yb_ref[...] = _pack_bf16_pairs(y + _dot(h1, wds_ref[half:, :]))

    @pl.when(jnp.logical_not(used))
    def _():
        yb_ref[...] = jnp.zeros_like(yb_ref)


def _experts(order, blk_e, n_used, xb, wg, wu, wd):
    D = 2 * xb.shape[1]
    blk = MOE_BLK
    n_blocks = order.shape[0]
    return pl.pallas_call(
        _expert_kernel,
        grid_spec=pltpu.PrefetchScalarGridSpec(
            num_scalar_prefetch=3,
            grid=(n_blocks,),
            in_specs=[
                pl.BlockSpec((blk, D // 2), lambda i, o, be, nu: (o[jnp.minimum(i, nu[0] - 1)], 0)),
                pl.BlockSpec((1, D, EXPERT_FF), lambda i, o, be, nu: (be[i], 0, 0)),
                pl.BlockSpec((1, D, EXPERT_FF), lambda i, o, be, nu: (be[i], 0, 0)),
                pl.BlockSpec((1, EXPERT_FF, D), lambda i, o, be, nu: (be[i], 0, 0)),
            ],
            out_specs=pl.BlockSpec((blk, D // 2), lambda i, o, be, nu: (o[i], 0)),
            scratch_shapes=[pltpu.VMEM((D, EXPERT_FF), BF16), pltpu.VMEM((D, EXPERT_FF), BF16),
                            pltpu.VMEM((EXPERT_FF, D), BF16)],
        ),
        out_shape=jax.ShapeDtypeStruct((n_blocks * blk, D // 2), jnp.uint32),
        compiler_params=pltpu.CompilerParams(
            dimension_semantics=("arbitrary",), vmem_limit_bytes=VMEM_LIMIT),
        name="experts",
    )(order, blk_e, n_used, xb, wg, wu, wd)


def _final_kernel(dest_ref, h1_ref, p_ref, wt_ref, yb_ref, gp_ref, wg_ref, bg_ref, wp_ref,
                  gf_ref, out_ref, buf0_ref, buf1_ref, buf2_ref, buf3_ref, sem):
    tm = h1_ref.shape[0] // FIN_TILES
    i = pl.program_id(0)
    bufs = (buf0_ref, buf1_ref, buf2_ref, buf3_ref)

    def row_copy(slot, k, r, d):
        return pltpu.make_async_copy(yb_ref.at[pl.ds(d, 1)], bufs[slot].at[k, pl.ds(r, 1)],
                                     sem.at[slot])

    def issue(tile, slot):
        for r in range(tm):
            t = (tile * tm + r) * 2
            row_copy(slot, 0, r, dest_ref[t]).start(priority=0)
            row_copy(slot, 1, r, dest_ref[t + 1]).start(priority=1)

    def wait(slot):
        for k in range(2):
            pltpu.make_async_copy(yb_ref.at[pl.ds(0, tm)], bufs[slot].at[k], sem.at[slot]).wait()

    def compute(slot):
        rows = slice(slot * tm, (slot + 1) * tm)
        pe = _dot(p_ref[rows, :].astype(BF16), wp_ref[...])
        wt = wt_ref[rows, :]
        h2 = (h1_ref[rows, :] + wt[:, 0:1] * _unpack_f32_pairs(bufs[slot][0])
              + wt[:, 1:2] * _unpack_f32_pairs(bufs[slot][1]))
        n = _rms(h2, gp_ref[...]).astype(BF16)
        gate = jax.nn.sigmoid(_dot(n, wg_ref[...]) + bg_ref[...])
        h3 = h2 + gate * pe
        out_ref[rows, :] = _rms(h3, gf_ref[...])

    @pl.when(i == 0)
    def _():
        for j in range(FIN_AHEAD):
            issue(j, j)

    last_tile = FIN_TILES * pl.num_programs(0) - 1
    for j in range(FIN_TILES):
        issue(jnp.minimum(FIN_TILES * i + j + FIN_AHEAD, last_tile), (j + FIN_AHEAD) % FIN_TILES)
        wait(j)
        compute(j)

    @pl.when(i == pl.num_programs(0) - 1)
    def _():
        for j in range(FIN_AHEAD):
            wait(j)


def _final(dest, h1, p, wt, yb, gp, wg, bg, wp, gf):
    N, D = h1.shape
    tm = TM
    full = lambda a: pl.BlockSpec(a.shape, lambda i, dest: (0,) * a.ndim)
    return pl.pallas_call(
        _final_kernel,
        grid_spec=pltpu.PrefetchScalarGridSpec(
            num_scalar_prefetch=1,
            grid=(N // (FIN_TILES * tm),),
            in_specs=[
                pl.BlockSpec((FIN_TILES * tm, D), lambda i, dest: (i, 0)),
                pl.BlockSpec((FIN_TILES * tm, PLE_DIM), lambda i, dest: (i, 0)),
                pl.BlockSpec((FIN_TILES * tm, 2), lambda i, dest: (i, 0)),
                pl.BlockSpec(memory_space=pl.ANY),
                full(gp), full(wg), full(bg), full(wp), full(gf),
            ],
            out_specs=pl.BlockSpec((FIN_TILES * tm, D), lambda i, dest: (i, 0)),
            scratch_shapes=[pltpu.VMEM((2, tm, D // 2), jnp.uint32) for _ in range(FIN_TILES)]
            + [pltpu.SemaphoreType.DMA((FIN_TILES,))],
        ),
        out_shape=jax.ShapeDtypeStruct((N, D), F32),
        compiler_params=pltpu.CompilerParams(
            dimension_semantics=("arbitrary",), vmem_limit_bytes=VMEM_LIMIT),
        name="final",
    )(dest, h1, p, wt, yb, gp, wg, bg, wp, gf)


def _prep_weights(w_in, w_uq, w_ukv):
    cq, ckv, kr, bg, cg, u = jnp.split(
        w_in, [Q_LORA, Q_LORA + KV_LORA, Q_LORA + KV_LORA + QK_ROPE_DIM,
               Q_LORA + KV_LORA + QK_ROPE_DIM + CONV_WIDTH,
               Q_LORA + KV_LORA + QK_ROPE_DIM + 2 * CONV_WIDTH], axis=1)
    kr_rot = jnp.concatenate([-kr[:, HALF_ROPE:], kr[:, :HALF_ROPE]], axis=1)
    pad = jnp.zeros((D_MODEL, HEAD_PAD - 2 * QK_ROPE_DIM), w_in.dtype)
    win = jnp.concatenate([cq, ckv, bg, cg, u, kr, kr_rot, pad], axis=1).astype(BF16)

    zq = jnp.zeros((Q_LORA, N_HEADS, HEAD_PAD - QK_DIM), w_uq.dtype)
    wqa = jnp.concatenate([w_uq, zq], axis=2)
    rope = w_uq[:, :, QK_NOPE_DIM:]
    wqb = jnp.concatenate([-rope[:, :, HALF_ROPE:], rope[:, :, :HALF_ROPE]], axis=2)
    wqa = wqa.reshape(Q_LORA, N_HEADS * HEAD_PAD).T.astype(BF16)
    wqb = wqb.reshape(Q_LORA, N_HEADS * QK_ROPE_DIM).T.astype(BF16)

    zk = jnp.zeros((KV_LORA, N_HEADS, HEAD_PAD - QK_NOPE_DIM), w_ukv.dtype)
    wuk = jnp.concatenate([w_ukv[:, :, :QK_NOPE_DIM], zk], axis=2)
    wuk = wuk.reshape(KV_LORA, N_HEADS * HEAD_PAD).astype(BF16)
    wuvt = w_ukv[:, :, QK_NOPE_DIM:].reshape(KV_LORA, MLA_WIDTH).T.astype(BF16)

    src = jnp.arange(QK_ROPE_DIM)[:, None]
    dst = jnp.arange(N_HEADS * HEAD_PAD)[None, :]
    pk = (dst % HEAD_PAD - QK_NOPE_DIM == src).astype(BF16)
    return win, wqa, wqb, wuk, wuvt, pk


def kernel(x, p, positions, attn_norm_g, w_in, q_norm_g, w_uq, kv_norm_g, w_ukv, conv_w, w_out,
           moe_norm_g, w_group_router, b_group_router, w_expert_router, b_expert_router,
           w_gate, w_up, w_down, ple_norm_g, w_ple_gate, b_ple_gate, w_ple_proj, final_norm_g):
    B, S, D = x.shape
    N = B * S
    assert w_in.shape[0] == 1, "single-layer trunk: the final norm is fused into the layer"
    pos = positions.astype(F32).reshape(B, 1, S)
    invf = (ROPE_BASE ** (-jnp.arange(0, QK_ROPE_DIM, 2, dtype=F32) / QK_ROPE_DIM)).reshape(-1, 1)
    row = lambda v: v.reshape(1, -1)
    h = x
    for i in range(1):
        win, wqa, wqb, wuk, wuvt, pk = _prep_weights(w_in[i], w_uq[i], w_ukv[i])
        qt, k, vt, oc = _inproj(h, pos, invf, row(attn_norm_g[i]), win, row(q_norm_g[i]), wqa,
                                wqb, row(kv_norm_g[i]), wuk, pk, wuvt,
                                conv_w[i].reshape(CONV_K, CONV_WIDTH))
        n_blocks = (2 * N) // MOE_BLK + N_EXPERTS
        ot, xbz = _attention(qt, k, vt, n_blocks * MOE_BLK + 2 * TM)

        zrow = lambda n: jnp.zeros((n, D), F32)
        wr = jnp.concatenate(
            [w_group_router[i].T, zrow(ROUTER_E0 - N_GROUPS), w_expert_router[i].T,
             zrow(ROUTER_ROWS - ROUTER_E0 - N_EXPERTS)], axis=0)
        wrh = wr.astype(BF16)
        wrhl = jnp.concatenate([wrh, (wr - wrh.astype(F32)).astype(BF16)], axis=0)
        br = jnp.concatenate(
            [b_group_router[i], jnp.zeros((ROUTER_E0 - N_GROUPS,), F32), b_expert_router[i],
             jnp.zeros((ROUTER_ROWS - ROUTER_E0 - N_EXPERTS,), F32)]).reshape(-1, 1)
        h1, pos, meta_w, opened, xb = _outproj(h, ot, oc, w_out[i].astype(BF16),
                                               row(moe_norm_g[i]), wrhl, br, xbz)

        opened = opened[:, :, 0]
        blk_ids = jnp.arange(n_blocks, dtype=jnp.int32)
        hit = opened[None, :, :] == blk_ids[:, None, None]
        used = jnp.any(hit, axis=(1, 2))
        e_of_blk = jnp.sum(jnp.where(hit, jnp.arange(N_EXPERTS, dtype=jnp.int32), 0), axis=(1, 2))
        n_used = jnp.sum(used.astype(jnp.int32)).reshape(1)
        key = jnp.where(used, e_of_blk, N_EXPERTS) * n_blocks + blk_ids
        slot_of_blk = jnp.sum((key[None, :] < key[:, None]).astype(jnp.int32), axis=1)
        at_slot = slot_of_blk[None, :] == blk_ids[:, None]
        order = jnp.sum(jnp.where(at_slot, blk_ids[None, :], 0), axis=1)
        blk_e = jnp.sum(jnp.where(at_slot, e_of_blk[None, :], 0), axis=1)
        blk_e = jnp.where(blk_ids < n_used, blk_e, jnp.max(jnp.where(used, e_of_blk, 0)))
        dest = jnp.swapaxes(pos[:, 0:2, :], 1, 2).reshape(2 * N)
        wt = jnp.swapaxes(meta_w[:, 0:2, :], 1, 2).reshape(N, 2)

        yb = _experts(order, blk_e, n_used, xb, w_gate[i], w_up[i], w_down[i])
        out = _final(dest, h1.reshape(N, D), p[i].reshape(N, PLE_DIM), wt, yb,
                     row(ple_norm_g[i]), w_ple_gate[i].astype(BF16), row(b_ple_gate[i]),
                     w_ple_proj[i].astype(BF16), row(final_norm_g))
        h = out.reshape(B, S, D)
    return h
```

```python
import math

import jax
import jax.numpy as jnp
from jax import lax
from jax.experimental import pallas as pl
from jax.experimental.pallas import tpu as pltpu

D_MODEL = 1024
PLE_DIM = 256
MLA_WIDTH = 512
CONV_WIDTH = 512
N_HEADS = 8
V_HEAD_DIM = 64
QK_NOPE_DIM = 64
QK_ROPE_DIM = 32
Q_LORA = 384
KV_LORA = 256
CONV_K = 3
N_GROUPS = 4
EXPERTS_PER_GROUP = 8
N_EXPERTS = N_GROUPS * EXPERTS_PER_GROUP
EXPERT_FF = 512
ROPE_BASE = 10000.0
EPS = 1e-6

HEAD_PAD = 128
V_ROWS = 80
QK_DIM = QK_NOPE_DIM + QK_ROPE_DIM
HALF_ROPE = QK_ROPE_DIM // 2

_C_Q = 0
_C_KV = _C_Q + Q_LORA
_C_BG = _C_KV + KV_LORA
_C_CG = _C_BG + CONV_WIDTH
_C_U = _C_CG + CONV_WIDTH
_C_KR = _C_U + CONV_WIDTH
IN_COLS_PAD = _C_KR + HEAD_PAD

TM = 256
TM_IN = 512
FIN_TILES = 4
FIN_AHEAD = 3
TQ = 512
TK = 256
ATT_HG = 4
ATT_TQC = 256
MOE_BLK = 512
SUBLANES = 8
BF16_SUBLANES = 16
ROUTER_E0 = SUBLANES
ROUTER_ROWS = -(-(ROUTER_E0 + N_EXPERTS) // BF16_SUBLANES) * BF16_SUBLANES
NEG = -1e30
Q_SCALE = (QK_DIM ** -0.5) * math.log2(math.e)
VMEM_LIMIT = 48 * 1024 * 1024
assert N_GROUPS <= ROUTER_E0 and MOE_BLK > TM

F32 = jnp.float32
BF16 = jnp.bfloat16


def _rms(x, g):
    return x * lax.rsqrt(jnp.mean(x * x, axis=-1, keepdims=True) + EPS) * g


def _dot(a, b):
    return jnp.dot(a, b, preferred_element_type=F32)


def _dot_nt(a, b):
    return lax.dot_general(a, b, (((1,), (1,)), ((), ())), preferred_element_type=F32)


def _pack_bf16_pairs(x):
    w = x.shape[1] // 2
    lo = pltpu.bitcast(x[:, :w].astype(BF16).astype(F32), jnp.uint32)
    hi = pltpu.bitcast(x[:, w:].astype(BF16).astype(F32), jnp.uint32)
    return (lo >> 16) | hi


def _unpack_bf16_pairs(p):
    lo = pltpu.bitcast(p << 16, F32).astype(BF16)
    hi = pltpu.bitcast(p & jnp.uint32(0xFFFF0000), F32).astype(BF16)
    return lo, hi


def _unpack_f32_pairs(p):
    lo = pltpu.bitcast(p << 16, F32)
    hi = pltpu.bitcast(p & jnp.uint32(0xFFFF0000), F32)
    return jnp.concatenate([lo, hi], axis=1)


def _dot_tn(a, b):
    return lax.dot_general(a, b, (((0,), (0,)), ((), ())), preferred_element_type=F32)


def _inproj_kernel(x_ref, pos_ref, invf_ref, g_ref, win_ref, qg_ref, wqa_ref, wqb_ref, kvg_ref,
                   wuk_ref, pk_ref, wuvt_ref, convw_ref,
                   qt_ref, k_ref, vt_ref, oc_ref, carry_ref, ext_ref):
    tm = x_ref.shape[1]
    x = x_ref[0]
    xn = _rms(x, g_ref[...])
    z = _dot(xn.astype(BF16), win_ref[...])

    ang = invf_ref[...] * pos_ref[0]
    cos = jnp.cos(ang)
    sin = jnp.sin(ang)
    cos2 = jnp.concatenate([cos, cos], axis=0)
    sin2 = jnp.concatenate([sin, sin], axis=0)

    cqn = _rms(z[:, _C_Q:_C_Q + Q_LORA], qg_ref[...]).astype(BF16)
    qa = _dot_nt(wqa_ref[...], cqn)
    qb = _dot_nt(wqb_ref[...], cqn)
    for h in range(N_HEADS):
        r0 = h * HEAD_PAD
        nope = qa[r0:r0 + QK_NOPE_DIM]
        rope = (qa[r0 + QK_NOPE_DIM:r0 + QK_DIM] * cos2
                + qb[h * QK_ROPE_DIM:(h + 1) * QK_ROPE_DIM] * sin2)
        qh = jnp.concatenate([nope, rope, qa[r0 + QK_DIM:r0 + HEAD_PAD]], axis=0) * Q_SCALE
        qt_ref[0, r0:r0 + HEAD_PAD, :] = qh.astype(BF16)

    kvn = _rms(z[:, _C_KV:_C_KV + KV_LORA], kvg_ref[...]).astype(BF16)
    krt = z[:, _C_KR:_C_KR + HEAD_PAD].T
    krot = krt[0:QK_ROPE_DIM] * cos2 + krt[QK_ROPE_DIM:2 * QK_ROPE_DIM] * sin2
    k_ref[0] = (_dot(kvn, wuk_ref[...]) + _dot_tn(krot.astype(BF16), pk_ref[...])).astype(BF16)
    vt = _dot_nt(wuvt_ref[...], kvn).astype(BF16)
    ones_row = jnp.where(lax.broadcasted_iota(jnp.int32, (V_ROWS - V_HEAD_DIM, tm), 0) == 0,
                         1.0, 0.0).astype(BF16)
    for h in range(N_HEADS):
        vt_ref[0, h * V_ROWS:h * V_ROWS + V_HEAD_DIM, :] = vt[h * V_HEAD_DIM:(h + 1) * V_HEAD_DIM]
        vt_ref[0, h * V_ROWS + V_HEAD_DIM:(h + 1) * V_ROWS, :] = ones_row

    @pl.when(pl.program_id(1) == 0)
    def _():
        carry_ref[...] = jnp.zeros_like(carry_ref)

    cu = z[:, _C_CG:_C_CG + CONV_WIDTH] * z[:, _C_U:_C_U + CONV_WIDTH]
    ext_ref[0:8, :] = carry_ref[...]
    ext_ref[8:8 + tm, :] = cu
    cu1 = ext_ref[7:7 + tm, :]
    cu2 = ext_ref[6:6 + tm, :]
    w = convw_ref[...]
    y = w[0:1] * cu2 + w[1:2] * cu1 + w[2:3] * cu
    oc_ref[0] = (z[:, _C_BG:_C_BG + CONV_WIDTH] * y).astype(BF16)
    carry_ref[...] = ext_ref[tm:tm + 8, :]


def _inproj(x, pos, invf, g, win, qg, wqa, wqb, kvg, wuk, pk, wuvt, convw):
    B, S, D = x.shape
    tm = TM_IN
    full = lambda a: pl.BlockSpec(a.shape, lambda b, i: (0,) * a.ndim)
    return pl.pallas_call(
        _inproj_kernel,
        grid=(B, S // tm),
        in_specs=[
            pl.BlockSpec((1, tm, D), lambda b, i: (b, i, 0)),
            pl.BlockSpec((1, 1, tm), lambda b, i: (b, 0, i)),
            full(invf), full(g), full(win), full(qg), full(wqa), full(wqb), full(kvg), full(wuk),
            full(pk), full(wuvt), full(convw),
        ],
        out_specs=[
            pl.BlockSpec((1, N_HEADS * HEAD_PAD, tm), lambda b, i: (b, 0, i)),
            pl.BlockSpec((1, tm, N_HEADS * HEAD_PAD), lambda b, i: (b, i, 0)),
            pl.BlockSpec((1, N_HEADS * V_ROWS, tm), lambda b, i: (b, 0, i)),
            pl.BlockSpec((1, tm, CONV_WIDTH), lambda b, i: (b, i, 0)),
        ],
        out_shape=[
            jax.ShapeDtypeStruct((B, N_HEADS * HEAD_PAD, S), BF16),
            jax.ShapeDtypeStruct((B, S, N_HEADS * HEAD_PAD), BF16),
            jax.ShapeDtypeStruct((B, N_HEADS * V_ROWS, S), BF16),
            jax.ShapeDtypeStruct((B, S, CONV_WIDTH), BF16),
        ],
        scratch_shapes=[pltpu.VMEM((8, CONV_WIDTH), F32), pltpu.VMEM((tm + 8, CONV_WIDTH), F32)],
        compiler_params=pltpu.CompilerParams(
            dimension_semantics=("arbitrary", "arbitrary"), vmem_limit_bytes=VMEM_LIMIT),
        name="inproj",
    )(x, pos, invf, g, win, qg, wqa, wqb, kvg, wuk, pk, wuvt, convw)


def _attn_kernel(qt_ref, k_ref, vt_ref, o_ref, z_ref, m_ref, acc_ref, sa_ref, sb_ref, zero_ref,
                 zsem):
    tq = qt_ref.shape[2]
    tk = TK
    assert tq == 2 * tk and ATT_TQC == tk
    i = pl.program_id(2)
    m_ref[...] = jnp.full_like(m_ref, NEG)
    acc_ref[...] = jnp.zeros_like(acc_ref)

    step = ((pl.program_id(0) * pl.num_programs(1) + pl.program_id(1)) * pl.num_programs(2) + i)
    zrows = zero_ref.shape[0]

    @pl.when(step == 0)
    def _():
        zero_ref[...] = jnp.zeros_like(zero_ref)

    zero_copy = pltpu.make_async_copy(
        zero_ref, z_ref.at[pl.ds(pl.multiple_of(step * zrows, 8), zrows)], zsem)
    zero_copy.start()

    def scores(j, s_ref, g, col0=0):
        k0 = pl.multiple_of(j * tk, tk)
        qt = qt_ref[0, g * HEAD_PAD:(g + 1) * HEAD_PAD, col0:]
        s_ref[g, :, col0:] = _dot(k_ref[0, pl.ds(k0, tk), g * HEAD_PAD:(g + 1) * HEAD_PAD], qt)

    def softmax_pv(j, s_ref, g, diag):
        k0 = pl.multiple_of(j * tk, tk)
        vt = vt_ref[0, g * V_ROWS:(g + 1) * V_ROWS, pl.ds(k0, tk)]
        for c in range(tq // ATT_TQC):
            if diag is not None and c < diag:
                continue
            cols = slice(c * ATT_TQC, (c + 1) * ATT_TQC)
            s = s_ref[g, :, cols]
            if diag is not None and c == diag:
                krow = lax.broadcasted_iota(jnp.int32, (tk, ATT_TQC), 0)
                qcol = lax.broadcasted_iota(jnp.int32, (tk, ATT_TQC), 1)
                s = jnp.where(krow <= qcol, s, NEG)
            m_old = m_ref[g, :, cols]
            m_new = jnp.maximum(m_old, jnp.max(s, axis=0, keepdims=True))
            alpha = jnp.exp2(m_old - m_new)
            p = jnp.exp2((s - m_new).astype(BF16))
            acc_ref[g, :, cols] = alpha * acc_ref[g, :, cols] + _dot(vt, p)
            m_ref[g, :, cols] = m_new

    def stage(j_next, s_next, j, s_cur, diag=None, next_col0=0):
        for g in range(ATT_HG):
            if j_next is not None:
                scores(j_next, s_next, g, next_col0)
            softmax_pv(j, s_cur, g, diag)

    for g in range(ATT_HG):
        scores(0, sa_ref, g)

    def pair(j):
        stage(j + 1, sb_ref, j, sa_ref)
        stage(j + 2, sa_ref, j + 1, sb_ref)

    def body(t, c):
        pair(4 * t)
        pair(4 * t + 2)
        return c

    lax.fori_loop(0, i // 2, body, 0)

    @pl.when(i % 2 == 1)
    def _():
        pair(2 * i - 2)

    n_full = 2 * i
    stage(n_full + 1, sb_ref, n_full, sa_ref, diag=0, next_col0=ATT_TQC)
    stage(None, None, n_full + 1, sb_ref, diag=1)
    for g in range(ATT_HG):
        o_ref[0, g * V_HEAD_DIM:(g + 1) * V_HEAD_DIM, :] = (
            acc_ref[g, 0:V_HEAD_DIM, :] / acc_ref[g, V_HEAD_DIM:V_HEAD_DIM + 1, :]).astype(o_ref.dtype)
    zero_copy.wait()


def _attention(qt, k, vt, min_zero_rows):
    B, _, S = qt.shape
    hg = ATT_HG
    grid = (B, N_HEADS // hg, S // TQ)
    n_steps = grid[0] * grid[1] * grid[2]
    zrows = -(-min_zero_rows // (SUBLANES * n_steps)) * SUBLANES
    return pl.pallas_call(
        _attn_kernel,
        grid=grid,
        in_specs=[
            pl.BlockSpec((1, hg * HEAD_PAD, TQ), lambda b, h, i: (b, h, i)),
            pl.BlockSpec((1, S, hg * HEAD_PAD), lambda b, h, i: (b, 0, h)),
            pl.BlockSpec((1, hg * V_ROWS, S), lambda b, h, i: (b, h, 0)),
        ],
        out_specs=[pl.BlockSpec((1, hg * V_HEAD_DIM, TQ), lambda b, h, i: (b, h, i)),
                   pl.BlockSpec(memory_space=pl.ANY)],
        out_shape=[jax.ShapeDtypeStruct((B, MLA_WIDTH, S), BF16),
                   jax.ShapeDtypeStruct((n_steps * zrows, D_MODEL // 2), jnp.uint32)],
        scratch_shapes=[pltpu.VMEM((hg, 1, TQ), F32), pltpu.VMEM((hg, V_ROWS, TQ), F32),
                        pltpu.VMEM((hg, TK, TQ), F32), pltpu.VMEM((hg, TK, TQ), F32),
                        pltpu.VMEM((zrows, D_MODEL // 2), jnp.uint32), pltpu.SemaphoreType.DMA],
        compiler_params=pltpu.CompilerParams(
            dimension_semantics=("arbitrary", "arbitrary", "arbitrary"),
            vmem_limit_bytes=VMEM_LIMIT),
        name="attn",
    )(qt, k, vt)


def _route_tile(xn, wrhl_ref, br_ref, carry_ref):
    tm = xn.shape[0]
    xh = xn.astype(BF16)
    xl = (xn - xh.astype(F32)).astype(BF16)
    hl = _dot_nt(wrhl_ref[...], xh)
    logits = (hl[0:ROUTER_ROWS] + hl[ROUTER_ROWS:2 * ROUTER_ROWS]
              + _dot_nt(wrhl_ref[0:ROUTER_ROWS, :], xl) + br_ref[...])
    big = jnp.int32(1 << 20)

    grow = lax.broadcasted_iota(jnp.int32, (ROUTER_E0, tm), 0)
    glog = jnp.where(grow < N_GROUPS, logits[0:ROUTER_E0], NEG)
    gmax = jnp.max(glog, axis=0, keepdims=True)
    gsum = jnp.sum(jnp.exp(glog - gmax), axis=0, keepdims=True)
    g_p = 1.0 / gsum
    g_idx = jnp.min(jnp.where(glog == gmax, grow, big), axis=0, keepdims=True)

    erow = lax.broadcasted_iota(jnp.int32, (N_EXPERTS, tm), 0)
    e_lo = g_idx * EXPERTS_PER_GROUP
    in_group = (erow >= e_lo) & (erow < e_lo + EXPERTS_PER_GROUP)
    elog = jnp.where(in_group, logits[ROUTER_E0:ROUTER_E0 + N_EXPERTS], NEG)
    emax = jnp.max(elog, axis=0, keepdims=True)
    esum = jnp.sum(jnp.exp(elog - emax), axis=0, keepdims=True)
    e1 = jnp.min(jnp.where(elog == emax, erow, big), axis=0, keepdims=True)
    elog2 = jnp.where(erow == e1, NEG, elog)
    emax2 = jnp.max(elog2, axis=0, keepdims=True)
    e2 = jnp.min(jnp.where(elog2 == emax2, erow, big), axis=0, keepdims=True)
    p1 = 1.0 / esum
    p2 = jnp.exp(emax2 - emax) / esum
    psum = p1 + p2
    w1 = g_p * (p1 / psum)
    w2 = g_p * (p2 / psum)

    oh1 = erow == e1
    oh2 = erow == e2
    oh = jnp.where(oh1 | oh2, 1.0, 0.0)
    srow = lax.broadcasted_iota(jnp.int32, (tm, tm), 0)
    scol = lax.broadcasted_iota(jnp.int32, (tm, tm), 1)
    earlier = jnp.where(srow < scol, 1.0, 0.0).astype(BF16)
    cum = _dot(oh.astype(BF16), earlier) + carry_ref[:, 0:1]
    r1 = jnp.sum(jnp.where(oh1, cum, 0.0), axis=0, keepdims=True)
    r2 = jnp.sum(jnp.where(oh2, cum, 0.0), axis=0, keepdims=True)
    return oh, oh1, oh2, r1, r2, w1, w2


def _outproj_kernel(x_ref, ot_ref, oc_ref, wo_ref, g_ref, wrhl_ref, br_ref, xbz_ref,
                    h1_ref, pos_ref, mw_ref, nb_ref, xb_ref,
                    carry_ref, last_ref, free_ref, xs0_ref, xs1_ref, dv0_ref, dv1_ref,
                    ds0_ref, ds1_ref, rsem, ssem):
    del xbz_ref
    tm = TM
    step = pl.program_id(0) * pl.num_programs(1) + pl.program_id(1)
    last_step = pl.num_programs(0) * pl.num_programs(1) - 1
    xs = (xs0_ref, xs1_ref)
    dv = (dv0_ref, dv1_ref)
    ds = (ds0_ref, ds1_ref)
    spare_row0 = xb_ref.shape[0] - 2 * tm

    def meta_copy(slot):
        return pltpu.make_async_copy(dv[slot], ds[slot], ssem.at[slot])

    def issue_rows(slot):
        for r in range(tm):
            for k in range(2):
                pltpu.make_async_copy(xs[slot].at[pl.ds(r, 1)],
                                      xb_ref.at[pl.ds(ds[slot][k, r], 1)], rsem.at[slot]
                                      ).start(priority=k)

    def wait_rows(slot):
        for _ in range(2):
            pltpu.make_async_copy(xs[slot], xb_ref.at[pl.ds(0, tm)], rsem.at[slot]).wait()

    def project(slot):
        rows = slice(slot * tm, (slot + 1) * tm)
        attn = (_dot_tn(ot_ref[0, :, rows], wo_ref[0:MLA_WIDTH, :])
                + _dot(oc_ref[0, rows, :], wo_ref[MLA_WIDTH:MLA_WIDTH + CONV_WIDTH, :]))
        h1 = x_ref[0, rows, :] + attn
        h1_ref[0, rows, :] = h1
        xn = _rms(h1, g_ref[...])
        xs[slot][...] = _pack_bf16_pairs(xn)
        return xn

    def route(slot, xn):
        rows = slice(slot * tm, (slot + 1) * tm)
        oh, oh1, oh2, r1, r2, w1, w2 = _route_tile(xn, wrhl_ref, br_ref, carry_ref)

        inv_blk = 1.0 / MOE_BLK
        cnt = carry_ref[:, 0:1]
        tile_cnt = jnp.sum(oh, axis=1, keepdims=True)
        nb_before = jnp.floor((cnt + (MOE_BLK - 1)) * inv_blk)
        nb_after = jnp.floor((cnt + tile_cnt + (MOE_BLK - 1)) * inv_blk)
        new = nb_after - nb_before
        erow = lax.broadcasted_iota(jnp.int32, (N_EXPERTS, N_EXPERTS), 0)
        ecol = lax.broadcasted_iota(jnp.int32, (N_EXPERTS, N_EXPERTS), 1)
        lower = jnp.where(ecol < erow, 1.0, 0.0).astype(BF16)
        new_rep = jnp.broadcast_to(new, (N_EXPERTS, HEAD_PAD))
        new_id = free_ref[0:1, :] + _dot(lower, new_rep.astype(BF16))
        free_ref[...] = free_ref[...] + jnp.sum(new, axis=0, keepdims=True)
        last_before = last_ref[...]
        is_new = new_rep > 0.0
        last_ref[...] = jnp.where(is_new, new_id, last_before)
        nb_ref[slot] = jnp.where(is_new, new_id, -1.0).astype(jnp.int32)
        carry_ref[...] = carry_ref[...] + tile_cnt

        def place(ohk, rk):
            bi = jnp.floor(rk * inv_blk)
            blk = jnp.where(bi == nb_before - 1.0, last_before[:, 0:1], new_id[:, 0:1])
            phys = jnp.sum(jnp.where(ohk, blk, 0.0), axis=0, keepdims=True)
            return (phys * MOE_BLK + (rk - bi * MOE_BLK)).astype(jnp.int32)

        mrow = lax.broadcasted_iota(jnp.int32, (SUBLANES, tm), 0)
        dest = jnp.where(mrow == 0, place(oh1, r1), jnp.where(mrow == 1, place(oh2, r2), 0))
        dv[slot][...] = dest
        pos_ref[0, :, rows] = dest
        mw_ref[0, :, rows] = jnp.where(mrow == 0, w1, jnp.where(mrow == 1, w2, 0.0))
        meta_copy(slot).start()

    @pl.when(step == 0)
    def _():
        carry_ref[...] = jnp.zeros_like(carry_ref)
        free_ref[...] = jnp.zeros_like(free_ref)
        last_ref[...] = jnp.full_like(last_ref, -1.0)
        xs1_ref[...] = jnp.zeros_like(xs1_ref)
        mrow = lax.broadcasted_iota(jnp.int32, (SUBLANES, tm), 0)
        lane = lax.broadcasted_iota(jnp.int32, (SUBLANES, tm), 1)
        dv1_ref[...] = jnp.where(mrow < 2, spare_row0 + 2 * lane + mrow, 0)
        meta_copy(1).start()

    def tile(slot):
        issue_rows(1 - slot)
        xn = project(slot)

        @pl.when(step >= 0)
        def _():
            route(slot, xn)

    meta_copy(1).wait()
    tile(0)
    wait_rows(1)
    meta_copy(0).wait()
    tile(1)
    wait_rows(0)

    @pl.when(step == last_step)
    def _():
        meta_copy(1).wait()
        issue_rows(1)
        wait_rows(1)


def _outproj(x, ot, oc, wo, g, wrhl, br, xbz):
    B, S, D = x.shape
    tm2 = 2 * TM
    n_s = S // tm2
    full = lambda a: pl.BlockSpec(a.shape, lambda b, i: (0,) * a.ndim)
    tile = lambda w: pl.BlockSpec((1, tm2, w), lambda b, i: (b, i, 0))
    meta = pl.BlockSpec((1, SUBLANES, tm2), lambda b, i: (b, 0, i))
    any_space = pl.BlockSpec(memory_space=pl.ANY)
    return pl.pallas_call(
        _outproj_kernel,
        grid=(B, n_s),
        in_specs=[
            tile(D),
            pl.BlockSpec((1, MLA_WIDTH, tm2), lambda b, i: (b, 0, i)),
            tile(CONV_WIDTH),
            full(wo), full(g), full(wrhl), full(br), any_space,
        ],
        out_specs=[tile(D), meta, meta,
                   pl.BlockSpec((2, N_EXPERTS, HEAD_PAD), lambda b, i: (b * n_s + i, 0, 0)),
                   any_space],
        out_shape=[
            jax.ShapeDtypeStruct((B, S, D), F32),
            jax.ShapeDtypeStruct((B, SUBLANES, S), jnp.int32),
            jax.ShapeDtypeStruct((B, SUBLANES, S), F32),
            jax.ShapeDtypeStruct((B * S // TM, N_EXPERTS, HEAD_PAD), jnp.int32),
            jax.ShapeDtypeStruct(xbz.shape, xbz.dtype),
        ],
        input_output_aliases={7: 4},
        scratch_shapes=[
            pltpu.VMEM((N_EXPERTS, HEAD_PAD), F32), pltpu.VMEM((N_EXPERTS, HEAD_PAD), F32),
            pltpu.VMEM((SUBLANES, HEAD_PAD), F32),
            pltpu.VMEM((TM, D // 2), jnp.uint32), pltpu.VMEM((TM, D // 2), jnp.uint32),
            pltpu.VMEM((SUBLANES, TM), jnp.int32), pltpu.VMEM((SUBLANES, TM), jnp.int32),
            pltpu.SMEM((SUBLANES, TM), jnp.int32), pltpu.SMEM((SUBLANES, TM), jnp.int32),
            pltpu.SemaphoreType.DMA((2,)), pltpu.SemaphoreType.DMA((2,)),
        ],
        compiler_params=pltpu.CompilerParams(
            dimension_semantics=("arbitrary", "arbitrary"), vmem_limit_bytes=VMEM_LIMIT),
        name="outproj",
    )(x, ot, oc, wo, g, wrhl, br, xbz)


def _expert_kernel(order_ref, blke_ref, nused_ref, xb_ref, wg_ref, wu_ref, wd_ref, yb_ref,
                   wgs_ref, wus_ref, wds_ref):
    del order_ref
    i = pl.program_id(0)
    used = i < nused_ref[0]
    changed = (i == 0) | (blke_ref[i] != blke_ref[jnp.maximum(i - 1, 0)])

    @pl.when(used & changed)
    def _():
        wgs_ref[...] = wg_ref[0].astype(BF16)
        wus_ref[...] = wu_ref[0].astype(BF16)
        wds_ref[...] = wd_ref[0].astype(BF16)

    @pl.when(used)
    def _():
        x_lo, x_hi = _unpack_bf16_pairs(xb_ref[...])
        dh = x_lo.shape[1]
        half = EXPERT_FF // 2

        def xdot(w_ref, cols):
            return _dot(x_lo, w_ref[0:dh, cols]) + _dot(x_hi, w_ref[dh:2 * dh, cols])

        g0 = xdot(wgs_ref, slice(0, half))
        u0 = xdot(wus_ref, slice(0, half))
        g1 = xdot(wgs_ref, slice(half, EXPERT_FF))
        u1 = xdot(wus_ref, slice(half, EXPERT_FF))
        h0 = ((g0 * jax.nn.sigmoid(g0)) * u0).astype(BF16)
        y = _dot(h0, wds_ref[:half, :])
        h1 = ((g1 * jax.nn.sigmoid(g1)) * u1).astype(BF16)
        yb_ref[...] = _pack_bf16_pairs(y + _dot(h1, wds_ref[half:, :]))

    @pl.when(jnp.logical_not(used))
    def _():
        yb_ref[...] = jnp.zeros_like(yb_ref)


def _experts(order, blk_e, n_used, xb, wg, wu, wd):
    D = 2 * xb.shape[1]
    blk = MOE_BLK
    n_blocks = order.shape[0]
    return pl.pallas_call(
        _expert_kernel,
        grid_spec=pltpu.PrefetchScalarGridSpec(
            num_scalar_prefetch=3,
            grid=(n_blocks,),
            in_specs=[
                pl.BlockSpec((blk, D // 2), lambda i, o, be, nu: (o[jnp.minimum(i, nu[0] - 1)], 0)),
                pl.BlockSpec((1, D, EXPERT_FF), lambda i, o, be, nu: (be[i], 0, 0)),
                pl.BlockSpec((1, D, EXPERT_FF), lambda i, o, be, nu: (be[i], 0, 0)),
                pl.BlockSpec((1, EXPERT_FF, D), lambda i, o, be, nu: (be[i], 0, 0)),
            ],
            out_specs=pl.BlockSpec((blk, D // 2), lambda i, o, be, nu: (o[i], 0)),
            scratch_shapes=[pltpu.VMEM((D, EXPERT_FF), BF16), pltpu.VMEM((D, EXPERT_FF), BF16),
                            pltpu.VMEM((EXPERT_FF, D), BF16)],
        ),
        out_shape=jax.ShapeDtypeStruct((n_blocks * blk, D // 2), jnp.uint32),
        compiler_params=pltpu.CompilerParams(
            dimension_semantics=("arbitrary",), vmem_limit_bytes=VMEM_LIMIT),
        name="experts",
    )(order, blk_e, n_used, xb, wg, wu, wd)


def _final_kernel(dest_ref, h1_ref, p_ref, wt_ref, yb_ref, gp_ref, wg_ref, bg_ref, wp_ref,
                  gf_ref, out_ref, buf0_ref, buf1_ref, buf2_ref, buf3_ref, sem):
    tm = h1_ref.shape[0] // FIN_TILES
    i = pl.program_id(0)
    bufs = (buf0_ref, buf1_ref, buf2_ref, buf3_ref)

    def row_copy(slot, k, r, d):
        return pltpu.make_async_copy(yb_ref.at[pl.ds(d, 1)], bufs[slot].at[k, pl.ds(r, 1)],
                                     sem.at[slot])

    def issue(tile, slot):
        for r in range(tm):
            t = (tile * tm + r) * 2
            row_copy(slot, 0, r, dest_ref[t]).start(priority=0)
            row_copy(slot, 1, r, dest_ref[t + 1]).start(priority=1)

    def wait(slot):
        for k in range(2):
            pltpu.make_async_copy(yb_ref.at[pl.ds(0, tm)], bufs[slot].at[k], sem.at[slot]).wait()

    def compute(slot):
        rows = slice(slot * tm, (slot + 1) * tm)
        pe = _dot(p_ref[rows, :].astype(BF16), wp_ref[...])
        wt = wt_ref[rows, :]
        h2 = (h1_ref[rows, :] + wt[:, 0:1] * _unpack_f32_pairs(bufs[slot][0])
              + wt[:, 1:2] * _unpack_f32_pairs(bufs[slot][1]))
        n = _rms(h2, gp_ref[...]).astype(BF16)
        gate = jax.nn.sigmoid(_dot(n, wg_ref[...]) + bg_ref[...])
        h3 = h2 + gate * pe
        out_ref[rows, :] = _rms(h3, gf_ref[...])

    @pl.when(i == 0)
    def _():
        for j in range(FIN_AHEAD):
            issue(j, j)

    last_tile = FIN_TILES * pl.num_programs(0) - 1
    for j in range(FIN_TILES):
        issue(jnp.minimum(FIN_TILES * i + j + FIN_AHEAD, last_tile), (j + FIN_AHEAD) % FIN_TILES)
        wait(j)
        compute(j)

    @pl.when(i == pl.num_programs(0) - 1)
    def _():
        for j in range(FIN_AHEAD):
            wait(j)


def _final(dest, h1, p, wt, yb, gp, wg, bg, wp, gf):
    N, D = h1.shape
    tm = TM
    full = lambda a: pl.BlockSpec(a.shape, lambda i, dest: (0,) * a.ndim)
    return pl.pallas_call(
        _final_kernel,
        grid_spec=pltpu.PrefetchScalarGridSpec(
            num_scalar_prefetch=1,
            grid=(N // (FIN_TILES * tm),),
            in_specs=[
                pl.BlockSpec((FIN_TILES * tm, D), lambda i, dest: (i, 0)),
                pl.BlockSpec((FIN_TILES * tm, PLE_DIM), lambda i, dest: (i, 0)),
                pl.BlockSpec((FIN_TILES * tm, 2), lambda i, dest: (i, 0)),
                pl.BlockSpec(memory_space=pl.ANY),
                full(gp), full(wg), full(bg), full(wp), full(gf),
            ],
            out_specs=pl.BlockSpec((FIN_TILES * tm, D), lambda i, dest: (i, 0)),
            scratch_shapes=[pltpu.VMEM((2, tm, D // 2), jnp.uint32) for _ in range(FIN_TILES)]
            + [pltpu.SemaphoreType.DMA((FIN_TILES,))],
        ),
        out_shape=jax.ShapeDtypeStruct((N, D), F32),
        compiler_params=pltpu.CompilerParams(
            dimension_semantics=("arbitrary",), vmem_limit_bytes=VMEM_LIMIT),
        name="final",
    )(dest, h1, p, wt, yb, gp, wg, bg, wp, gf)


def _prep_weights(w_in, w_uq, w_ukv):
    cq, ckv, kr, bg, cg, u = jnp.split(
        w_in, [Q_LORA, Q_LORA + KV_LORA, Q_LORA + KV_LORA + QK_ROPE_DIM,
               Q_LORA + KV_LORA + QK_ROPE_DIM + CONV_WIDTH,
               Q_LORA + KV_LORA + QK_ROPE_DIM + 2 * CONV_WIDTH], axis=1)
    kr_rot = jnp.concatenate([-kr[:, HALF_ROPE:], kr[:, :HALF_ROPE]], axis=1)
    pad = jnp.zeros((D_MODEL, HEAD_PAD - 2 * QK_ROPE_DIM), w_in.dtype)
    win = jnp.concatenate([cq, ckv, bg, cg, u, kr, kr_rot, pad], axis=1).astype(BF16)

    zq = jnp.zeros((Q_LORA, N_HEADS, HEAD_PAD - QK_DIM), w_uq.dtype)
    wqa = jnp.concatenate([w_uq, zq], axis=2)
    rope = w_uq[:, :, QK_NOPE_DIM:]
    wqb = jnp.concatenate([-rope[:, :, HALF_ROPE:], rope[:, :, :HALF_ROPE]], axis=2)
    wqa = wqa.reshape(Q_LORA, N_HEADS * HEAD_PAD).T.astype(BF16)
    wqb = wqb.reshape(Q_LORA, N_HEADS * QK_ROPE_DIM).T.astype(BF16)

    zk = jnp.zeros((KV_LORA, N_HEADS, HEAD_PAD - QK_NOPE_DIM), w_ukv.dtype)
    wuk = jnp.concatenate([w_ukv[:, :, :QK_NOPE_DIM], zk], axis=2)
    wuk = wuk.reshape(KV_LORA, N_HEADS * HEAD_PAD).astype(BF16)
    wuvt = w_ukv[:, :, QK_NOPE_DIM:].reshape(KV_LORA, MLA_WIDTH).T.astype(BF16)

    src = jnp.arange(QK_ROPE_DIM)[:, None]
    dst = jnp.arange(N_HEADS * HEAD_PAD)[None, :]
    pk = (dst % HEAD_PAD - QK_NOPE_DIM == src).astype(BF16)
    return win, wqa, wqb, wuk, wuvt, pk


def kernel(x, p, positions, attn_norm_g, w_in, q_norm_g, w_uq, kv_norm_g, w_ukv, conv_w, w_out,
           moe_norm_g, w_group_router, b_group_router, w_expert_router, b_expert_router,
           w_gate, w_up, w_down, ple_norm_g, w_ple_gate, b_ple_gate, w_ple_proj, final_norm_g):
    B, S, D = x.shape
    N = B * S
    assert w_in.shape[0] == 1, "single-layer trunk: the final norm is fused into the layer"
    pos = positions.astype(F32).reshape(B, 1, S)
    invf = (ROPE_BASE ** (-jnp.arange(0, QK_ROPE_DIM, 2, dtype=F32) / QK_ROPE_DIM)).reshape(-1, 1)
    row = lambda v: v.reshape(1, -1)
    h = x
    for i in range(1):
        win, wqa, wqb, wuk, wuvt, pk = _prep_weights(w_in[i], w_uq[i], w_ukv[i])
        qt, k, vt, oc = _inproj(h, pos, invf, row(attn_norm_g[i]), win, row(q_norm_g[i]), wqa,
                                wqb, row(kv_norm_g[i]), wuk, pk, wuvt,
                                conv_w[i].reshape(CONV_K, CONV_WIDTH))
        n_blocks = (2 * N) // MOE_BLK + N_EXPERTS
        ot, xbz = _attention(qt, k, vt, n_blocks * MOE_BLK + 2 * TM)

        zrow = lambda n: jnp.zeros((n, D), F32)
        wr = jnp.concatenate(
            [w_group_router[i].T, zrow(ROUTER_E0 - N_GROUPS), w_expert_router[i].T,
             zrow(ROUTER_ROWS - ROUTER_E0 - N_EXPERTS)], axis=0)
        wrh = wr.astype(BF16)
        wrhl = jnp.concatenate([wrh, (wr - wrh.astype(F32)).astype(BF16)], axis=0)
        br = jnp.concatenate(
            [b_group_router[i], jnp.zeros((ROUTER_E0 - N_GROUPS,), F32), b_expert_router[i],
             jnp.zeros((ROUTER_ROWS - ROUTER_E0 - N_EXPERTS,), F32)]).reshape(-1, 1)
        h1, pos, meta_w, opened, xb = _outproj(h, ot, oc, w_out[i].astype(BF16),
                                               row(moe_norm_g[i]), wrhl, br, xbz)

        opened = opened[:, :, 0]
        blk_ids = jnp.arange(n_blocks, dtype=jnp.int32)
        hit = opened[None, :, :] == blk_ids[:, None, None]
        used = jnp.any(hit, axis=(1, 2))
        e_of_blk = jnp.sum(jnp.where(hit, jnp.arange(N_EXPERTS, dtype=jnp.int32), 0), axis=(1, 2))
        n_used = jnp.sum(used.astype(jnp.int32)).reshape(1)
        key = jnp.where(used, e_of_blk, N_EXPERTS) * n_blocks + blk_ids
        slot_of_blk = jnp.sum((key[None, :] < key[:, None]).astype(jnp.int32), axis=1)
        at_slot = slot_of_blk[None, :] == blk_ids[:, None]
        order = jnp.sum(jnp.where(at_slot, blk_ids[None, :], 0), axis=1)
        blk_e = jnp.sum(jnp.where(at_slot, e_of_blk[None, :], 0), axis=1)
        blk_e = jnp.where(blk_ids < n_used, blk_e, jnp.max(jnp.where(used, e_of_blk, 0)))
        dest = jnp.swapaxes(pos[:, 0:2, :], 1, 2).reshape(2 * N)
        wt = jnp.swapaxes(meta_w[:, 0:2, :], 1, 2).reshape(N, 2)

        yb = _experts(order, blk_e, n_used, xb, w_gate[i], w_up[i], w_down[i])
        out = _final(dest, h1.reshape(N, D), p[i].reshape(N, PLE_DIM), wt, yb,
                     row(ple_norm_g[i]), w_ple_gate[i].astype(BF16), row(b_ple_gate[i]),
                     w_ple_proj[i].astype(BF16), row(final_norm_g))
        h = out.reshape(B, S, D)
    return h
```

```python
import functools
import math

import jax
import jax.numpy as jnp
from jax import lax
from jax.experimental import pallas as pl
from jax.experimental.pallas import tpu as pltpu

D_MODEL = 1024
PLE_DIM = 256
MLA_WIDTH = 512
CONV_WIDTH = 512
N_HEADS = 8
V_HEAD_DIM = 64
QK_NOPE_DIM = 64
QK_ROPE_DIM = 32
Q_LORA = 384
KV_LORA = 256
CONV_K = 3
N_GROUPS = 4
EXPERTS_PER_GROUP = 8
N_EXPERTS = N_GROUPS * EXPERTS_PER_GROUP
EXPERT_FF = 512
ROPE_BASE = 10000.0
EPS = 1e-6

HEAD_PAD = 128
V_ROWS = 80
QK_DIM = QK_NOPE_DIM + QK_ROPE_DIM
HALF_ROPE = QK_ROPE_DIM // 2

_C_Q = 0
_C_KV = _C_Q + Q_LORA
_C_BG = _C_KV + KV_LORA
_C_CG = _C_BG + CONV_WIDTH
_C_U = _C_CG + CONV_WIDTH
_C_KR = _C_U + CONV_WIDTH
IN_COLS_PAD = _C_KR + HEAD_PAD

TM = 256
TM_IN = 512
FIN_TILES = 4
FIN_AHEAD = 2
TQ = 512
TK = 256
ATT_HG = 4
ATT_TQC = 256
MOE_BLK = 512
SUBLANES = 8
BF16_SUBLANES = 16
ROUTER_E0 = SUBLANES
ROUTER_ROWS = -(-(ROUTER_E0 + N_EXPERTS) // BF16_SUBLANES) * BF16_SUBLANES
NEG = -1e30
Q_SCALE = (QK_DIM ** -0.5) * math.log2(math.e)
VMEM_LIMIT = 48 * 1024 * 1024
assert N_GROUPS <= ROUTER_E0 and MOE_BLK > TM

F32 = jnp.float32
BF16 = jnp.bfloat16


def _rms(x, g):
    return x * lax.rsqrt(jnp.mean(x * x, axis=-1, keepdims=True) + EPS) * g


def _dot(a, b):
    return jnp.dot(a, b, preferred_element_type=F32)


def _dot_nt(a, b):
    return lax.dot_general(a, b, (((1,), (1,)), ((), ())), preferred_element_type=F32)


def _pack_bf16_pairs(x):
    w = x.shape[1] // 2
    lo = pltpu.bitcast(x[:, :w].astype(BF16).astype(F32), jnp.uint32)
    hi = pltpu.bitcast(x[:, w:].astype(BF16).astype(F32), jnp.uint32)
    return (lo >> 16) | hi


def _unpack_bf16_pairs(p):
    lo = pltpu.bitcast(p << 16, F32).astype(BF16)
    hi = pltpu.bitcast(p & jnp.uint32(0xFFFF0000), F32).astype(BF16)
    return lo, hi


def _unpack_f32_pairs(p):
    lo = pltpu.bitcast(p << 16, F32)
    hi = pltpu.bitcast(p & jnp.uint32(0xFFFF0000), F32)
    return jnp.concatenate([lo, hi], axis=1)


def _dot_tn(a, b):
    return lax.dot_general(a, b, (((0,), (0,)), ((), ())), preferred_element_type=F32)


def _inproj_kernel(x_ref, pos_ref, invf_ref, g_ref, win_ref, qg_ref, wqa_ref, wqb_ref, kvg_ref,
                   wuk_ref, pk_ref, wuvt_ref, convw_ref,
                   qt_ref, k_ref, vt_ref, oc_ref, carry_ref, ext_ref):
    tm = x_ref.shape[1]
    x = x_ref[0]
    xn = _rms(x, g_ref[...])
    z = _dot(xn.astype(BF16), win_ref[...])

    ang = invf_ref[...] * pos_ref[0]
    cos = jnp.cos(ang)
    sin = jnp.sin(ang)
    cos2 = jnp.concatenate([cos, cos], axis=0)
    sin2 = jnp.concatenate([sin, sin], axis=0)

    cqn = _rms(z[:, _C_Q:_C_Q + Q_LORA], qg_ref[...]).astype(BF16)
    qa = _dot_nt(wqa_ref[...], cqn)
    qb = _dot_nt(wqb_ref[...], cqn)
    for h in range(N_HEADS):
        r0 = h * HEAD_PAD
        nope = qa[r0:r0 + QK_NOPE_DIM]
        rope = (qa[r0 + QK_NOPE_DIM:r0 + QK_DIM] * cos2
                + qb[h * QK_ROPE_DIM:(h + 1) * QK_ROPE_DIM] * sin2)
        qh = jnp.concatenate([nope, rope, qa[r0 + QK_DIM:r0 + HEAD_PAD]], axis=0) * Q_SCALE
        qt_ref[0, r0:r0 + HEAD_PAD, :] = qh.astype(BF16)

    kvn = _rms(z[:, _C_KV:_C_KV + KV_LORA], kvg_ref[...]).astype(BF16)
    krt = z[:, _C_KR:_C_KR + HEAD_PAD].T
    krot = krt[0:QK_ROPE_DIM] * cos2 + krt[QK_ROPE_DIM:2 * QK_ROPE_DIM] * sin2
    k_ref[0] = (_dot(kvn, wuk_ref[...]) + _dot_tn(krot.astype(BF16), pk_ref[...])).astype(BF16)
    vt = _dot_nt(wuvt_ref[...], kvn).astype(BF16)
    ones_row = jnp.where(lax.broadcasted_iota(jnp.int32, (V_ROWS - V_HEAD_DIM, tm), 0) == 0,
                         1.0, 0.0).astype(BF16)
    for h in range(N_HEADS):
        vt_ref[0, h * V_ROWS:h * V_ROWS + V_HEAD_DIM, :] = vt[h * V_HEAD_DIM:(h + 1) * V_HEAD_DIM]
        vt_ref[0, h * V_ROWS + V_HEAD_DIM:(h + 1) * V_ROWS, :] = ones_row

    @pl.when(pl.program_id(1) == 0)
    def _():
        carry_ref[...] = jnp.zeros_like(carry_ref)

    cu = z[:, _C_CG:_C_CG + CONV_WIDTH] * z[:, _C_U:_C_U + CONV_WIDTH]
    ext_ref[0:8, :] = carry_ref[...]
    ext_ref[8:8 + tm, :] = cu
    cu1 = ext_ref[7:7 + tm, :]
    cu2 = ext_ref[6:6 + tm, :]
    w = convw_ref[...]
    y = w[0:1] * cu2 + w[1:2] * cu1 + w[2:3] * cu
    oc_ref[0] = (z[:, _C_BG:_C_BG + CONV_WIDTH] * y).astype(BF16)
    carry_ref[...] = ext_ref[tm:tm + 8, :]


def _inproj(x, pos, invf, g, win, qg, wqa, wqb, kvg, wuk, pk, wuvt, convw):
    B, S, D = x.shape
    tm = TM_IN
    full = lambda a: pl.BlockSpec(a.shape, lambda b, i: (0,) * a.ndim)
    return pl.pallas_call(
        _inproj_kernel,
        grid=(B, S // tm),
        in_specs=[
            pl.BlockSpec((1, tm, D), lambda b, i: (b, i, 0)),
            pl.BlockSpec((1, 1, tm), lambda b, i: (b, 0, i)),
            full(invf), full(g), full(win), full(qg), full(wqa), full(wqb), full(kvg), full(wuk),
            full(pk), full(wuvt), full(convw),
        ],
        out_specs=[
            pl.BlockSpec((1, N_HEADS * HEAD_PAD, tm), lambda b, i: (b, 0, i)),
            pl.BlockSpec((1, tm, N_HEADS * HEAD_PAD), lambda b, i: (b, i, 0)),
            pl.BlockSpec((1, N_HEADS * V_ROWS, tm), lambda b, i: (b, 0, i)),
            pl.BlockSpec((1, tm, CONV_WIDTH), lambda b, i: (b, i, 0)),
        ],
        out_shape=[
            jax.ShapeDtypeStruct((B, N_HEADS * HEAD_PAD, S), BF16),
            jax.ShapeDtypeStruct((B, S, N_HEADS * HEAD_PAD), BF16),
            jax.ShapeDtypeStruct((B, N_HEADS * V_ROWS, S), BF16),
            jax.ShapeDtypeStruct((B, S, CONV_WIDTH), BF16),
        ],
        scratch_shapes=[pltpu.VMEM((8, CONV_WIDTH), F32), pltpu.VMEM((tm + 8, CONV_WIDTH), F32)],
        compiler_params=pltpu.CompilerParams(
            dimension_semantics=("arbitrary", "arbitrary"), vmem_limit_bytes=VMEM_LIMIT),
        name="inproj",
    )(x, pos, invf, g, win, qg, wqa, wqb, kvg, wuk, pk, wuvt, convw)


def _attn_kernel(qt_ref, k_ref, vt_ref, o_ref, z_ref, m_ref, acc_ref, sa_ref, sb_ref, zero_ref,
                 zsem):
    tq = qt_ref.shape[2]
    tk = TK
    assert tq == 2 * tk and ATT_TQC == tk
    i = pl.program_id(2)
    m_ref[...] = jnp.full_like(m_ref, NEG)
    acc_ref[...] = jnp.zeros_like(acc_ref)

    step = ((pl.program_id(0) * pl.num_programs(1) + pl.program_id(1)) * pl.num_programs(2) + i)
    zrows = zero_ref.shape[0]

    @pl.when(step == 0)
    def _():
        zero_ref[...] = jnp.zeros_like(zero_ref)

    zero_copy = pltpu.make_async_copy(
        zero_ref, z_ref.at[pl.ds(pl.multiple_of(step * zrows, 8), zrows)], zsem)
    zero_copy.start()

    def scores(j, s_ref, g, col0=0):
        k0 = pl.multiple_of(j * tk, tk)
        qt = qt_ref[0, g * HEAD_PAD:(g + 1) * HEAD_PAD, col0:]
        s_ref[g, :, col0:] = _dot(k_ref[0, pl.ds(k0, tk), g * HEAD_PAD:(g + 1) * HEAD_PAD], qt)

    def softmax_pv(j, s_ref, g, diag):
        k0 = pl.multiple_of(j * tk, tk)
        vt = vt_ref[0, g * V_ROWS:(g + 1) * V_ROWS, pl.ds(k0, tk)]
        for c in range(tq // ATT_TQC):
            if diag is not None and c < diag:
                continue
            cols = slice(c * ATT_TQC, (c + 1) * ATT_TQC)
            s = s_ref[g, :, cols]
            if diag is not None and c == diag:
                krow = lax.broadcasted_iota(jnp.int32, (tk, ATT_TQC), 0)
                qcol = lax.broadcasted_iota(jnp.int32, (tk, ATT_TQC), 1)
                s = jnp.where(krow <= qcol, s, NEG)
            m_old = m_ref[g, :, cols]
            m_new = jnp.maximum(m_old, jnp.max(s, axis=0, keepdims=True))
            alpha = jnp.exp2(m_old - m_new)
            p = jnp.exp2((s - m_new).astype(BF16))
            acc_ref[g, :, cols] = alpha * acc_ref[g, :, cols] + _dot(vt, p)
            m_ref[g, :, cols] = m_new

    def stage(j_next, s_next, j, s_cur, diag=None, next_col0=0):
        for g in range(ATT_HG):
            if j_next is not None:
                scores(j_next, s_next, g, next_col0)
            softmax_pv(j, s_cur, g, diag)

    for g in range(ATT_HG):
        scores(0, sa_ref, g)

    def pair(j):
        stage(j + 1, sb_ref, j, sa_ref)
        stage(j + 2, sa_ref, j + 1, sb_ref)

    def body(t, c):
        pair(4 * t)
        pair(4 * t + 2)
        return c

    lax.fori_loop(0, i // 2, body, 0)

    @pl.when(i % 2 == 1)
    def _():
        pair(2 * i - 2)

    n_full = 2 * i
    stage(n_full + 1, sb_ref, n_full, sa_ref, diag=0, next_col0=ATT_TQC)
    stage(None, None, n_full + 1, sb_ref, diag=1)
    for g in range(ATT_HG):
        o_ref[0, g * V_HEAD_DIM:(g + 1) * V_HEAD_DIM, :] = (
            acc_ref[g, 0:V_HEAD_DIM, :] / acc_ref[g, V_HEAD_DIM:V_HEAD_DIM + 1, :]).astype(o_ref.dtype)
    zero_copy.wait()


def _attention(qt, k, vt, min_zero_rows):
    B, _, S = qt.shape
    hg = ATT_HG
    grid = (B, N_HEADS // hg, S // TQ)
    n_steps = grid[0] * grid[1] * grid[2]
    zrows = -(-min_zero_rows // (SUBLANES * n_steps)) * SUBLANES
    return pl.pallas_call(
        _attn_kernel,
        grid=grid,
        in_specs=[
            pl.BlockSpec((1, hg * HEAD_PAD, TQ), lambda b, h, i: (b, h, i)),
            pl.BlockSpec((1, S, hg * HEAD_PAD), lambda b, h, i: (b, 0, h)),
            pl.BlockSpec((1, hg * V_ROWS, S), lambda b, h, i: (b, h, 0)),
        ],
        out_specs=[pl.BlockSpec((1, hg * V_HEAD_DIM, TQ), lambda b, h, i: (b, h, i)),
                   pl.BlockSpec(memory_space=pl.ANY)],
        out_shape=[jax.ShapeDtypeStruct((B, MLA_WIDTH, S), BF16),
                   jax.ShapeDtypeStruct((n_steps * zrows, D_MODEL // 2), jnp.uint32)],
        scratch_shapes=[pltpu.VMEM((hg, 1, TQ), F32), pltpu.VMEM((hg, V_ROWS, TQ), F32),
                        pltpu.VMEM((hg, TK, TQ), F32), pltpu.VMEM((hg, TK, TQ), F32),
                        pltpu.VMEM((zrows, D_MODEL // 2), jnp.uint32), pltpu.SemaphoreType.DMA],
        compiler_params=pltpu.CompilerParams(
            dimension_semantics=("arbitrary", "arbitrary", "arbitrary"),
            vmem_limit_bytes=VMEM_LIMIT),
        name="attn",
    )(qt, k, vt)


def _route_tile(xn, wrhl_ref, br_ref, carry_ref):
    tm = xn.shape[0]
    xh = xn.astype(BF16)
    xl = (xn - xh.astype(F32)).astype(BF16)
    hl = _dot_nt(wrhl_ref[...], xh)
    logits = (hl[0:ROUTER_ROWS] + hl[ROUTER_ROWS:2 * ROUTER_ROWS]
              + _dot_nt(wrhl_ref[0:ROUTER_ROWS, :], xl) + br_ref[...])
    big = jnp.int32(1 << 20)

    grow = lax.broadcasted_iota(jnp.int32, (ROUTER_E0, tm), 0)
    glog = jnp.where(grow < N_GROUPS, logits[0:ROUTER_E0], NEG)
    gmax = jnp.max(glog, axis=0, keepdims=True)
    gsum = jnp.sum(jnp.exp(glog - gmax), axis=0, keepdims=True)
    g_p = 1.0 / gsum
    g_idx = jnp.min(jnp.where(glog == gmax, grow, big), axis=0, keepdims=True)

    erow = lax.broadcasted_iota(jnp.int32, (N_EXPERTS, tm), 0)
    e_lo = g_idx * EXPERTS_PER_GROUP
    in_group = (erow >= e_lo) & (erow < e_lo + EXPERTS_PER_GROUP)
    elog = jnp.where(in_group, logits[ROUTER_E0:ROUTER_E0 + N_EXPERTS], NEG)
    emax = jnp.max(elog, axis=0, keepdims=True)
    esum = jnp.sum(jnp.exp(elog - emax), axis=0, keepdims=True)
    e1 = jnp.min(jnp.where(elog == emax, erow, big), axis=0, keepdims=True)
    elog2 = jnp.where(erow == e1, NEG, elog)
    emax2 = jnp.max(elog2, axis=0, keepdims=True)
    e2 = jnp.min(jnp.where(elog2 == emax2, erow, big), axis=0, keepdims=True)
    p1 = 1.0 / esum
    p2 = jnp.exp(emax2 - emax) / esum
    psum = p1 + p2
    w1 = g_p * (p1 / psum)
    w2 = g_p * (p2 / psum)

    oh1 = erow == e1
    oh2 = erow == e2
    oh = jnp.where(oh1 | oh2, 1.0, 0.0)
    srow = lax.broadcasted_iota(jnp.int32, (tm, tm), 0)
    scol = lax.broadcasted_iota(jnp.int32, (tm, tm), 1)
    earlier = jnp.where(srow < scol, 1.0, 0.0).astype(BF16)
    cum = _dot(oh.astype(BF16), earlier) + carry_ref[:, 0:1]
    r1 = jnp.sum(jnp.where(oh1, cum, 0.0), axis=0, keepdims=True)
    r2 = jnp.sum(jnp.where(oh2, cum, 0.0), axis=0, keepdims=True)
    return oh, oh1, oh2, r1, r2, w1, w2


def _outproj_kernel(x_ref, ot_ref, oc_ref, wo_ref, g_ref, wrhl_ref, br_ref, xbz_ref,
                    h1_ref, pos_ref, mw_ref, nb_ref, xb_ref,
                    carry_ref, last_ref, free_ref, xs0_ref, xs1_ref, dv0_ref, dv1_ref,
                    ds0_ref, ds1_ref, rsem, ssem):
    del xbz_ref
    tm = TM
    step = pl.program_id(0) * pl.num_programs(1) + pl.program_id(1)
    last_step = pl.num_programs(0) * pl.num_programs(1) - 1
    xs = (xs0_ref, xs1_ref)
    dv = (dv0_ref, dv1_ref)
    ds = (ds0_ref, ds1_ref)
    spare_row0 = xb_ref.shape[0] - 2 * tm

    def meta_copy(slot):
        return pltpu.make_async_copy(dv[slot], ds[slot], ssem.at[slot])

    def issue_rows(slot):
        for r in range(tm):
            for k in range(2):
                pltpu.make_async_copy(xs[slot].at[pl.ds(r, 1)],
                                      xb_ref.at[pl.ds(ds[slot][k, r], 1)], rsem.at[slot]
                                      ).start(priority=k)

    def wait_rows(slot):
        for _ in range(2):
            pltpu.make_async_copy(xs[slot], xb_ref.at[pl.ds(0, tm)], rsem.at[slot]).wait()

    def project(slot):
        rows = slice(slot * tm, (slot + 1) * tm)
        attn = (_dot_tn(ot_ref[0, :, rows], wo_ref[0:MLA_WIDTH, :])
                + _dot(oc_ref[0, rows, :], wo_ref[MLA_WIDTH:MLA_WIDTH + CONV_WIDTH, :]))
        h1 = x_ref[0, rows, :] + attn
        h1_ref[0, rows, :] = h1
        xn = _rms(h1, g_ref[...])
        xs[slot][...] = _pack_bf16_pairs(xn)
        return xn

    def route(slot, xn):
        rows = slice(slot * tm, (slot + 1) * tm)
        oh, oh1, oh2, r1, r2, w1, w2 = _route_tile(xn, wrhl_ref, br_ref, carry_ref)

        inv_blk = 1.0 / MOE_BLK
        cnt = carry_ref[:, 0:1]
        tile_cnt = jnp.sum(oh, axis=1, keepdims=True)
        nb_before = jnp.floor((cnt + (MOE_BLK - 1)) * inv_blk)
        nb_after = jnp.floor((cnt + tile_cnt + (MOE_BLK - 1)) * inv_blk)
        new = nb_after - nb_before
        erow = lax.broadcasted_iota(jnp.int32, (N_EXPERTS, N_EXPERTS), 0)
        ecol = lax.broadcasted_iota(jnp.int32, (N_EXPERTS, N_EXPERTS), 1)
        lower = jnp.where(ecol < erow, 1.0, 0.0).astype(BF16)
        new_rep = jnp.broadcast_to(new, (N_EXPERTS, HEAD_PAD))
        new_id = free_ref[0:1, :] + _dot(lower, new_rep.astype(BF16))
        free_ref[...] = free_ref[...] + jnp.sum(new, axis=0, keepdims=True)
        last_before = last_ref[...]
        is_new = new_rep > 0.0
        last_ref[...] = jnp.where(is_new, new_id, last_before)
        nb_ref[slot] = jnp.where(is_new, new_id, -1.0).astype(jnp.int32)
        carry_ref[...] = carry_ref[...] + tile_cnt

        def place(ohk, rk):
            bi = jnp.floor(rk * inv_blk)
            blk = jnp.where(bi == nb_before - 1.0, last_before[:, 0:1], new_id[:, 0:1])
            phys = jnp.sum(jnp.where(ohk, blk, 0.0), axis=0, keepdims=True)
            return (phys * MOE_BLK + (rk - bi * MOE_BLK)).astype(jnp.int32)

        mrow = lax.broadcasted_iota(jnp.int32, (SUBLANES, tm), 0)
        dest = jnp.where(mrow == 0, place(oh1, r1), jnp.where(mrow == 1, place(oh2, r2), 0))
        dv[slot][...] = dest
        pos_ref[0, :, rows] = dest
        mw_ref[0, :, rows] = jnp.where(mrow == 0, w1, jnp.where(mrow == 1, w2, 0.0))
        meta_copy(slot).start()

    @pl.when(step == 0)
    def _():
        carry_ref[...] = jnp.zeros_like(carry_ref)
        free_ref[...] = jnp.zeros_like(free_ref)
        last_ref[...] = jnp.full_like(last_ref, -1.0)
        xs1_ref[...] = jnp.zeros_like(xs1_ref)
        mrow = lax.broadcasted_iota(jnp.int32, (SUBLANES, tm), 0)
        lane = lax.broadcasted_iota(jnp.int32, (SUBLANES, tm), 1)
        dv1_ref[...] = jnp.where(mrow < 2, spare_row0 + 2 * lane + mrow, 0)
        meta_copy(1).start()

    def tile(slot):
        issue_rows(1 - slot)
        xn = project(slot)

        @pl.when(step >= 0)
        def _():
            route(slot, xn)

    meta_copy(1).wait()
    tile(0)
    wait_rows(1)
    meta_copy(0).wait()
    tile(1)
    wait_rows(0)

    @pl.when(step == last_step)
    def _():
        meta_copy(1).wait()
        issue_rows(1)
        wait_rows(1)


def _outproj(x, ot, oc, wo, g, wrhl, br, xbz):
    B, S, D = x.shape
    tm2 = 2 * TM
    n_s = S // tm2
    full = lambda a: pl.BlockSpec(a.shape, lambda b, i: (0,) * a.ndim)
    tile = lambda w: pl.BlockSpec((1, tm2, w), lambda b, i: (b, i, 0))
    meta = pl.BlockSpec((1, SUBLANES, tm2), lambda b, i: (b, 0, i))
    any_space = pl.BlockSpec(memory_space=pl.ANY)
    return pl.pallas_call(
        _outproj_kernel,
        grid=(B, n_s),
        in_specs=[
            tile(D),
            pl.BlockSpec((1, MLA_WIDTH, tm2), lambda b, i: (b, 0, i)),
            tile(CONV_WIDTH),
            full(wo), full(g), full(wrhl), full(br), any_space,
        ],
        out_specs=[tile(D), meta, meta,
                   pl.BlockSpec((2, N_EXPERTS, HEAD_PAD), lambda b, i: (b * n_s + i, 0, 0)),
                   any_space],
        out_shape=[
            jax.ShapeDtypeStruct((B, S, D), F32),
            jax.ShapeDtypeStruct((B, SUBLANES, S), jnp.int32),
            jax.ShapeDtypeStruct((B, SUBLANES, S), F32),
            jax.ShapeDtypeStruct((B * S // TM, N_EXPERTS, HEAD_PAD), jnp.int32),
            jax.ShapeDtypeStruct(xbz.shape, xbz.dtype),
        ],
        input_output_aliases={7: 4},
        scratch_shapes=[
            pltpu.VMEM((N_EXPERTS, HEAD_PAD), F32), pltpu.VMEM((N_EXPERTS, HEAD_PAD), F32),
            pltpu.VMEM((SUBLANES, HEAD_PAD), F32),
            pltpu.VMEM((TM, D // 2), jnp.uint32), pltpu.VMEM((TM, D // 2), jnp.uint32),
            pltpu.VMEM((SUBLANES, TM), jnp.int32), pltpu.VMEM((SUBLANES, TM), jnp.int32),
            pltpu.SMEM((SUBLANES, TM), jnp.int32), pltpu.SMEM((SUBLANES, TM), jnp.int32),
            pltpu.SemaphoreType.DMA((2,)), pltpu.SemaphoreType.DMA((2,)),
        ],
        compiler_params=pltpu.CompilerParams(
            dimension_semantics=("arbitrary", "arbitrary"), vmem_limit_bytes=VMEM_LIMIT),
        name="outproj",
    )(x, ot, oc, wo, g, wrhl, br, xbz)


def _expert_kernel(order_ref, blke_ref, nused_ref, xb_ref, wg_ref, wu_ref, wd_ref, yb_ref,
                   wgs_ref, wus_ref, wds_ref):
    del order_ref
    i = pl.program_id(0)
    used = i < nused_ref[0]
    changed = (i == 0) | (blke_ref[i] != blke_ref[jnp.maximum(i - 1, 0)])

    @pl.when(used & changed)
    def _():
        wgs_ref[...] = wg_ref[0].astype(BF16)
        wus_ref[...] = wu_ref[0].astype(BF16)
        wds_ref[...] = wd_ref[0].astype(BF16)

    @pl.when(used)
    def _():
        x_lo, x_hi = _unpack_bf16_pairs(xb_ref[...])
        dh = x_lo.shape[1]
        half = EXPERT_FF // 2

        def xdot(w_ref, cols):
            return _dot(x_lo, w_ref[0:dh, cols]) + _dot(x_hi, w_ref[dh:2 * dh, cols])

        g0 = xdot(wgs_ref, slice(0, half))
        u0 = xdot(wus_ref, slice(0, half))
        g1 = xdot(wgs_ref, slice(half, EXPERT_FF))
        u1 = xdot(wus_ref, slice(half, EXPERT_FF))
        h0 = ((g0 * jax.nn.sigmoid(g0)) * u0).astype(BF16)
        y = _dot(h0, wds_ref[:half, :])
        h1 = ((g1 * jax.nn.sigmoid(g1)) * u1).astype(BF16)
        yb_ref[...] = _pack_bf16_pairs(y + _dot(h1, wds_ref[half:, :]))

    @pl.when(jnp.logical_not(used))
    def _():
        yb_ref[...] = jnp.zeros_like(yb_ref)


def _experts(order, blk_e, n_used, xb, wg, wu, wd):
    D = 2 * xb.shape[1]
    blk = MOE_BLK
    n_blocks = order.shape[0]
    return pl.pallas_call(
        _expert_kernel,
        grid_spec=pltpu.PrefetchScalarGridSpec(
            num_scalar_prefetch=3,
            grid=(n_blocks,),
            in_specs=[
                pl.BlockSpec((blk, D // 2), lambda i, o, be, nu: (o[jnp.minimum(i, nu[0] - 1)], 0)),
                pl.BlockSpec((1, D, EXPERT_FF), lambda i, o, be, nu: (be[i], 0, 0)),
                pl.BlockSpec((1, D, EXPERT_FF), lambda i, o, be, nu: (be[i], 0, 0)),
                pl.BlockSpec((1, EXPERT_FF, D), lambda i, o, be, nu: (be[i], 0, 0)),
            ],
            out_specs=pl.BlockSpec((blk, D // 2), lambda i, o, be, nu: (o[i], 0)),
            scratch_shapes=[pltpu.VMEM((D, EXPERT_FF), BF16), pltpu.VMEM((D, EXPERT_FF), BF16),
                            pltpu.VMEM((EXPERT_FF, D), BF16)],
        ),
        out_shape=jax.ShapeDtypeStruct((n_blocks * blk, D // 2), jnp.uint32),
        compiler_params=pltpu.CompilerParams(
            dimension_semantics=("arbitrary",), vmem_limit_bytes=VMEM_LIMIT),
        name="experts",
    )(order, blk_e, n_used, xb, wg, wu, wd)


def _final_kernel(dest_ref, h1_ref, p_ref, wt_ref, yb_ref, gp_ref, wg_ref, bg_ref, wp_ref,
                  gf_ref, out_ref, buf0_ref, buf1_ref, buf2_ref, buf3_ref, sem, *, seq):
    tm = h1_ref.shape[0] // FIN_TILES
    i = pl.program_id(0)
    bufs = (buf0_ref, buf1_ref, buf2_ref, buf3_ref)

    def row_copy(slot, k, r, d):
        return pltpu.make_async_copy(yb_ref.at[pl.ds(d, 1)], bufs[slot].at[k, pl.ds(r, 1)],
                                     sem.at[slot])

    def issue(tile, slot):
        t0 = tile * tm
        b = t0 // seq
        base = (2 * b) * seq + (t0 - b * seq)
        for r in range(tm):
            row_copy(slot, 0, r, dest_ref[base + r]).start(priority=0)
            row_copy(slot, 1, r, dest_ref[base + seq + r]).start(priority=1)

    def wait(slot):
        for k in range(2):
            pltpu.make_async_copy(yb_ref.at[pl.ds(0, tm)], bufs[slot].at[k], sem.at[slot]).wait()

    def compute(slot):
        rows = slice(slot * tm, (slot + 1) * tm)
        pe = _dot(p_ref[rows, :].astype(BF16), wp_ref[...])
        wt = wt_ref[rows, :]
        h2 = (h1_ref[rows, :] + wt[:, 0:1] * _unpack_f32_pairs(bufs[slot][0])
              + wt[:, 1:2] * _unpack_f32_pairs(bufs[slot][1]))
        n = _rms(h2, gp_ref[...]).astype(BF16)
        gate = jax.nn.sigmoid(_dot(n, wg_ref[...]) + bg_ref[...])
        h3 = h2 + gate * pe
        out_ref[rows, :] = _rms(h3, gf_ref[...])

    @pl.when(i == 0)
    def _():
        for j in range(FIN_AHEAD):
            issue(j, j)

    last_tile = FIN_TILES * pl.num_programs(0) - 1
    for j in range(FIN_TILES):
        issue(jnp.minimum(FIN_TILES * i + j + FIN_AHEAD, last_tile), (j + FIN_AHEAD) % FIN_TILES)
        wait(j)
        compute(j)

    @pl.when(i == pl.num_programs(0) - 1)
    def _():
        for j in range(FIN_AHEAD):
            wait(j)


def _final(dest, h1, p, wt, yb, gp, wg, bg, wp, gf, seq):
    N, D = h1.shape
    tm = TM
    assert seq % tm == 0
    full = lambda a: pl.BlockSpec(a.shape, lambda i, dest: (0,) * a.ndim)
    return pl.pallas_call(
        functools.partial(_final_kernel, seq=seq),
        grid_spec=pltpu.PrefetchScalarGridSpec(
            num_scalar_prefetch=1,
            grid=(N // (FIN_TILES * tm),),
            in_specs=[
                pl.BlockSpec((FIN_TILES * tm, D), lambda i, dest: (i, 0)),
                pl.BlockSpec((FIN_TILES * tm, PLE_DIM), lambda i, dest: (i, 0)),
                pl.BlockSpec((FIN_TILES * tm, 2), lambda i, dest: (i, 0)),
                pl.BlockSpec(memory_space=pl.ANY),
                full(gp), full(wg), full(bg), full(wp), full(gf),
            ],
            out_specs=pl.BlockSpec((FIN_TILES * tm, D), lambda i, dest: (i, 0)),
            scratch_shapes=[pltpu.VMEM((2, tm, D // 2), jnp.uint32) for _ in range(FIN_TILES)]
            + [pltpu.SemaphoreType.DMA((FIN_TILES,))],
        ),
        out_shape=jax.ShapeDtypeStruct((N, D), F32),
        compiler_params=pltpu.CompilerParams(
            dimension_semantics=("arbitrary",), vmem_limit_bytes=VMEM_LIMIT),
        name="final",
    )(dest, h1, p, wt, yb, gp, wg, bg, wp, gf)


def _prep_weights(w_in, w_uq, w_ukv):
    cq, ckv, kr, bg, cg, u = jnp.split(
        w_in, [Q_LORA, Q_LORA + KV_LORA, Q_LORA + KV_LORA + QK_ROPE_DIM,
               Q_LORA + KV_LORA + QK_ROPE_DIM + CONV_WIDTH,
               Q_LORA + KV_LORA + QK_ROPE_DIM + 2 * CONV_WIDTH], axis=1)
    kr_rot = jnp.concatenate([-kr[:, HALF_ROPE:], kr[:, :HALF_ROPE]], axis=1)
    pad = jnp.zeros((D_MODEL, HEAD_PAD - 2 * QK_ROPE_DIM), w_in.dtype)
    win = jnp.concatenate([cq, ckv, bg, cg, u, kr, kr_rot, pad], axis=1).astype(BF16)

    zq = jnp.zeros((Q_LORA, N_HEADS, HEAD_PAD - QK_DIM), w_uq.dtype)
    wqa = jnp.concatenate([w_uq, zq], axis=2)
    rope = w_uq[:, :, QK_NOPE_DIM:]
    wqb = jnp.concatenate([-rope[:, :, HALF_ROPE:], rope[:, :, :HALF_ROPE]], axis=2)
    wqa = wqa.reshape(Q_LORA, N_HEADS * HEAD_PAD).T.astype(BF16)
    wqb = wqb.reshape(Q_LORA, N_HEADS * QK_ROPE_DIM).T.astype(BF16)

    zk = jnp.zeros((KV_LORA, N_HEADS, HEAD_PAD - QK_NOPE_DIM), w_ukv.dtype)
    wuk = jnp.concatenate([w_ukv[:, :, :QK_NOPE_DIM], zk], axis=2)
    wuk = wuk.reshape(KV_LORA, N_HEADS * HEAD_PAD).astype(BF16)
    wuvt = w_ukv[:, :, QK_NOPE_DIM:].reshape(KV_LORA, MLA_WIDTH).T.astype(BF16)

    src = jnp.arange(QK_ROPE_DIM)[:, None]
    dst = jnp.arange(N_HEADS * HEAD_PAD)[None, :]
    pk = (dst % HEAD_PAD - QK_NOPE_DIM == src).astype(BF16)
    return win, wqa, wqb, wuk, wuvt, pk


def kernel(x, p, positions, attn_norm_g, w_in, q_norm_g, w_uq, kv_norm_g, w_ukv, conv_w, w_out,
           moe_norm_g, w_group_router, b_group_router, w_expert_router, b_expert_router,
           w_gate, w_up, w_down, ple_norm_g, w_ple_gate, b_ple_gate, w_ple_proj, final_norm_g):
    B, S, D = x.shape
    N = B * S
    assert w_in.shape[0] == 1, "single-layer trunk: the final norm is fused into the layer"
    pos = positions.astype(F32).reshape(B, 1, S)
    invf = (ROPE_BASE ** (-jnp.arange(0, QK_ROPE_DIM, 2, dtype=F32) / QK_ROPE_DIM)).reshape(-1, 1)
    row = lambda v: v.reshape(1, -1)
    h = x
    for i in range(1):
        win, wqa, wqb, wuk, wuvt, pk = _prep_weights(w_in[i], w_uq[i], w_ukv[i])
        qt, k, vt, oc = _inproj(h, pos, invf, row(attn_norm_g[i]), win, row(q_norm_g[i]), wqa,
                                wqb, row(kv_norm_g[i]), wuk, pk, wuvt,
                                conv_w[i].reshape(CONV_K, CONV_WIDTH))
        n_blocks = (2 * N) // MOE_BLK + N_EXPERTS
        ot, xbz = _attention(qt, k, vt, n_blocks * MOE_BLK + 2 * TM)

        zrow = lambda n: jnp.zeros((n, D), F32)
        wr = jnp.concatenate(
            [w_group_router[i].T, zrow(ROUTER_E0 - N_GROUPS), w_expert_router[i].T,
             zrow(ROUTER_ROWS - ROUTER_E0 - N_EXPERTS)], axis=0)
        wrh = wr.astype(BF16)
        wrhl = jnp.concatenate([wrh, (wr - wrh.astype(F32)).astype(BF16)], axis=0)
        br = jnp.concatenate(
            [b_group_router[i], jnp.zeros((ROUTER_E0 - N_GROUPS,), F32), b_expert_router[i],
             jnp.zeros((ROUTER_ROWS - ROUTER_E0 - N_EXPERTS,), F32)]).reshape(-1, 1)
        h1, pos, meta_w, opened, xb = _outproj(h, ot, oc, w_out[i].astype(BF16),
                                               row(moe_norm_g[i]), wrhl, br, xbz)

        opened = opened[:, :, 0]
        blk_ids = jnp.arange(n_blocks, dtype=jnp.int32)
        hit = opened[None, :, :] == blk_ids[:, None, None]
        used = jnp.any(hit, axis=(1, 2))
        e_of_blk = jnp.sum(jnp.where(hit, jnp.arange(N_EXPERTS, dtype=jnp.int32), 0), axis=(1, 2))
        n_used = jnp.sum(used.astype(jnp.int32)).reshape(1)
        key = jnp.where(used, e_of_blk, N_EXPERTS) * n_blocks + blk_ids
        slot_of_blk = jnp.sum((key[None, :] < key[:, None]).astype(jnp.int32), axis=1)
        at_slot = slot_of_blk[None, :] == blk_ids[:, None]
        order = jnp.sum(jnp.where(at_slot, blk_ids[None, :], 0), axis=1)
        blk_e = jnp.sum(jnp.where(at_slot, e_of_blk[None, :], 0), axis=1)
        blk_e = jnp.where(blk_ids < n_used, blk_e, jnp.max(jnp.where(used, e_of_blk, 0)))
        dest = pos[:, 0:2, :].reshape(2 * N)
        wt = jnp.swapaxes(meta_w[:, 0:2, :], 1, 2).reshape(N, 2)

        yb = _experts(order, blk_e, n_used, xb, w_gate[i], w_up[i], w_down[i])
        out = _final(dest, h1.reshape(N, D), p[i].reshape(N, PLE_DIM), wt, yb,
                     row(ple_norm_g[i]), w_ple_gate[i].astype(BF16), row(b_ple_gate[i]),
                     w_ple_proj[i].astype(BF16), row(final_norm_g), S)
        h = out.reshape(B, S, D)
    return h
```
